```python
import jax, jax.numpy as jnp
from jax import lax
import numpy as np

D_MODEL = 1024
BATCH = 16
SEQ = 2048
DEPTH = 2

CTX_LEN = 256
GRID_W = 64
ROPE_THETA = 10000.0
NORM_EPS = 1e-6
ATTN_QBLK = 128

MLA_HEADS = 8
MLA_Q_RANK = 384
MLA_KV_RANK = 256
MLA_NOPE = 64
MLA_ROPE = 32
MLA_V = 64
SWA_HEADS = 8
SWA_KV_HEADS = 2
SWA_GROUP = SWA_HEADS // SWA_KV_HEADS
SWA_HEAD_DIM = 64
SWA_WINDOW = 128
SWA_BLK = SWA_WINDOW
DN_HEADS = 8
DN_HEAD_DIM = 64
DN_CONV = 5
DN_CHUNK = 64
N_BRANCH = 3
BRANCH_W = MLA_HEADS * MLA_V
N_EXPERTS = 32
TOP_K = 4
D_EXPERT = 1024
SWIGLU_LIMIT = 7.0
SWIGLU_ALPHA = 1.702
MOE_BLK = 256

IN_SIZES = (
    MLA_Q_RANK,
    MLA_KV_RANK,
    MLA_ROPE,
    SWA_HEADS * SWA_HEAD_DIM,
    SWA_KV_HEADS * SWA_HEAD_DIM,
    SWA_KV_HEADS * SWA_HEAD_DIM,
    3 * DN_HEADS * DN_HEAD_DIM,
    DN_HEADS * DN_HEAD_DIM,
    2 * DN_HEADS,
    2 * DN_HEADS,
    N_BRANCH * D_MODEL,
)
D_IN = sum(IN_SIZES)

kernel_name = "hybrid_mla_swa_deltanet_moe_dit"


def _rmsnorm(x, g):
    xf = x.astype(jnp.float32)
    y = xf * lax.rsqrt(jnp.mean(xf * xf, axis=-1, keepdims=True) + NORM_EPS)
    return (y * g.astype(jnp.float32)).astype(x.dtype)


def _l2norm(x):
    xf = x.astype(jnp.float32)
    return (xf * lax.rsqrt(jnp.sum(xf * xf, axis=-1, keepdims=True) + NORM_EPS)).astype(x.dtype)


def _modulate(h, shift, scale):
    return h * (1 + scale) + shift


def _split(x, sizes):
    return jnp.split(x, np.cumsum(sizes)[:-1].tolist(), axis=-1)


def _axial_rope_tables(n_tok, rot_dim):
    rows = n_tok // GRID_W
    row = jnp.broadcast_to(jnp.arange(rows)[:, None], (rows, GRID_W)).reshape(-1).astype(jnp.float32)
    col = jnp.broadcast_to(jnp.arange(GRID_W)[None, :], (rows, GRID_W)).reshape(-1).astype(jnp.float32)
    n_freq = rot_dim // 4
    inv_freq = ROPE_THETA ** (-jnp.arange(n_freq, dtype=jnp.float32) / n_freq)
    ang = jnp.concatenate([row[:, None] * inv_freq, col[:, None] * inv_freq], axis=-1)
    return jnp.cos(ang), jnp.sin(ang)


def _apply_rope(x, cos, sin):
    xf = x.astype(jnp.float32)
    x1, x2 = jnp.split(xf, 2, axis=-1)
    cs, sn = cos[None, :, None, :], sin[None, :, None, :]
    return jnp.concatenate([x1 * cs - x2 * sn, x2 * cs + x1 * sn], axis=-1).astype(x.dtype)


def _dense_block_attention(q, k, v, scale):
    B, S, H, dk = q.shape
    nb = S // ATTN_QBLK
    qb = jnp.moveaxis(q.reshape(B, nb, ATTN_QBLK, H, dk), 1, 0)

    def one(qblk):
        s = jnp.einsum('bqhd,bthd->bhqt', qblk, k).astype(jnp.float32) * scale
        p = jax.nn.softmax(s, axis=-1).astype(v.dtype)
        return jnp.einsum('bhqt,bthe->bqhe', p, v)

    o = lax.map(one, qb)
    return jnp.moveaxis(o, 0, 1).reshape(B, S, H, v.shape[-1])


def _mla_branch(cq, ckv, kr, cq_c, ckv_c, kr_c, q_norm_g, w_q_up, kv_norm_g, w_kv_up, cos, sin, ctx_out):
    scale = (MLA_NOPE + MLA_ROPE) ** -0.5

    def queries(cqx):
        B, T = cqx.shape[:2]
        return (_rmsnorm(cqx, q_norm_g) @ w_q_up).reshape(B, T, MLA_HEADS, MLA_NOPE + MLA_ROPE)

    def keys_values(ckvx, krx):
        B, T = ckvx.shape[:2]
        kv = (_rmsnorm(ckvx, kv_norm_g) @ w_kv_up).reshape(B, T, MLA_HEADS, MLA_NOPE + MLA_V)
        k_nope, v = kv[..., :MLA_NOPE], kv[..., MLA_NOPE:]
        k_rope = jnp.broadcast_to(krx, (B, T, MLA_HEADS, MLA_ROPE))
        return jnp.concatenate([k_nope, k_rope], axis=-1), v

    B, S = cq.shape[:2]
    q = queries(cq)
    q = jnp.concatenate([q[..., :MLA_NOPE], _apply_rope(q[..., MLA_NOPE:], cos, sin)], axis=-1)
    k, v = keys_values(ckv, _apply_rope(kr[:, :, None, :], cos, sin))
    kc, vc = keys_values(ckv_c, kr_c[:, :, None, :])
    o = _dense_block_attention(q, jnp.concatenate([kc, k], axis=1), jnp.concatenate([vc, v], axis=1), scale)
    o = o.reshape(B, S, BRANCH_W)
    if not ctx_out:
        return o, None
    oc = _dense_block_attention(queries(cq_c), kc, vc, scale)
    return o, oc.reshape(cq_c.shape[0], cq_c.shape[1], BRANCH_W)


def _sink_attend(q, segs, sink, scale):
    scores = []
    for k, _, m in segs:
        s = jnp.einsum('bqngd,btnd->bngqt', q, k).astype(jnp.float32) * scale
        if m is not None:
            s = jnp.where(m, s, -jnp.inf)
        scores.append(s)
    B, N, G, Q = scores[0].shape[:4]
    sink_col = jnp.broadcast_to(sink[None, :, :, None, None], (B, N, G, Q, 1))
    p = jax.nn.softmax(jnp.concatenate(scores + [sink_col], axis=-1), axis=-1)
    out = None
    off = 0
    for (k, v, _), s in zip(segs, scores):
        t = s.shape[-1]
        o = jnp.einsum('bngqt,btnd->bqngd', p[..., off:off + t].astype(v.dtype), v)
        out = o if out is None else out + o
        off += t
    return out


def _swa_branch(sq, sk, sv, sq_c, sk_c, sv_c, sink, cos, sin, ctx_out):
    B, S = sq.shape[:2]
    T = sq_c.shape[1]
    dh = SWA_HEAD_DIM
    scale = dh ** -0.5
    sink_g = sink.astype(jnp.float32).reshape(SWA_KV_HEADS, SWA_GROUP)
    q = _apply_rope(sq.reshape(B, S, SWA_HEADS, dh), cos, sin).reshape(B, S, SWA_KV_HEADS, SWA_GROUP, dh)
    k = _apply_rope(sk.reshape(B, S, SWA_KV_HEADS, dh), cos, sin)
    v = sv.reshape(B, S, SWA_KV_HEADS, dh)
    kc = sk_c.reshape(B, T, SWA_KV_HEADS, dh)
    vc = sv_c.reshape(B, T, SWA_KV_HEADS, dh)
    pad = ((0, 0), (SWA_WINDOW, SWA_WINDOW), (0, 0), (0, 0))
    kpad, vpad = jnp.pad(k, pad), jnp.pad(v, pad)
    rel = (jnp.arange(3 * SWA_BLK) - SWA_BLK)[None, :] - jnp.arange(SWA_BLK)[:, None]
    band = jnp.abs(rel) <= SWA_WINDOW

    def block(i):
        start = i * SWA_BLK
        qb = lax.dynamic_slice_in_dim(q, start, SWA_BLK, 1)
        kb = lax.dynamic_slice_in_dim(kpad, start, 3 * SWA_BLK, 1)
        vb = lax.dynamic_slice_in_dim(vpad, start, 3 * SWA_BLK, 1)
        kpos = start - SWA_BLK + jnp.arange(3 * SWA_BLK)
        mask = band & ((kpos >= 0) & (kpos < S))[None, :]
        return _sink_attend(qb, [(kc, vc, None), (kb, vb, mask)], sink_g, scale)

    o = lax.map(block, jnp.arange(S // SWA_BLK))
    o = jnp.moveaxis(o, 0, 1).reshape(B, S, BRANCH_W)
    if not ctx_out:
        return o, None
    qc = sq_c.reshape(B, T, SWA_KV_HEADS, SWA_GROUP, dh)
    oc = _sink_attend(qc, [(kc, vc, None)], sink_g, scale).reshape(B, T, BRANCH_W)
    return o, oc


def _short_conv(u, w):
    C = u.shape[-1]
    y = lax.conv_general_dilated(u, w[:, None, :].astype(u.dtype), window_strides=(1,),
                                 padding=[(DN_CONV // 2, DN_CONV // 2)],
                                 dimension_numbers=('NWC', 'WIO', 'NWC'), feature_group_count=C)
    return jax.nn.silu(y)


def _gated_delta_chunked(q, k, v, g, beta, s0, with_output):
    out_dtype = v.dtype
    B, T, H, dk = q.shape
    dv = v.shape[-1]
    C = DN_CHUNK
    n = T // C

    def chunks(x):
        x = x.astype(jnp.float32).reshape((B, n, C, H) + x.shape[3:])
        return jnp.moveaxis(x, 3, 2)

    q, k, v, g, beta = chunks(q), chunks(k), chunks(v), chunks(g), chunks(beta)
    gcum = jnp.cumsum(g, axis=-1)
    causal = jnp.tril(jnp.ones((C, C), bool))
    strict = jnp.tril(jnp.ones((C, C), bool), -1)
    diff = gcum[..., :, None] - gcum[..., None, :]
    decay = jnp.where(causal, jnp.exp(jnp.where(causal, diff, 0.0)), 0.0)
    kb = k * beta[..., None]
    lmat = jnp.where(strict, jnp.einsum('bnhid,bnhjd->bnhij', kb, k) * decay, 0.0)
    amat = lmat + jnp.eye(C, dtype=jnp.float32)
    rhs = jnp.concatenate([v * beta[..., None], kb * jnp.exp(gcum)[..., None]], axis=-1)
    sol = lax.linalg.triangular_solve(amat, rhs, left_side=True, lower=True, unit_diagonal=True)
    u, w = sol[..., :dv], sol[..., dv:]
    k_dec = k * jnp.exp(gcum[..., -1:] - gcum)[..., None]
    g_last = jnp.exp(gcum[..., -1])
    to_scan = lambda x: jnp.moveaxis(x, 1, 0)

    def state_update(S, w_i, u_i, kd_i, gl_i):
        v_new = u_i - jnp.einsum('bhck,bhkv->bhcv', w_i, S)
        S_next = S * gl_i[..., None, None] + jnp.einsum('bhck,bhcv->bhkv', kd_i, v_new)
        return S_next, v_new

    if not with_output:
        def step_state(S, xs):
            S_next, _ = state_update(S, *xs)
            return S_next, None
        s_fin, _ = lax.scan(step_state, s0, tuple(map(to_scan, (w, u, k_dec, g_last))))
        return s_fin, None

    a_intra = jnp.einsum('bnhid,bnhjd->bnhij', q, k) * decay
    q_dec = q * jnp.exp(gcum)[..., None]

    def step(S, xs):
        w_i, u_i, kd_i, gl_i, qd_i, a_i = xs
        S_next, v_new = state_update(S, w_i, u_i, kd_i, gl_i)
        o = jnp.einsum('bhck,bhkv->bhcv', qd_i, S) + jnp.einsum('bhij,bhjv->bhiv', a_i, v_new)
        return S_next, o

    s_fin, o = lax.scan(step, s0, tuple(map(to_scan, (w, u, k_dec, g_last, q_dec, a_intra))))
    o = jnp.moveaxis(jnp.moveaxis(o, 0, 1), 2, 3).reshape(B, T, H, dv).astype(out_dtype)
    return s_fin, o


def _deltanet_branch(dqkv, dz, dbeta, da, dqkv_c, dz_c, dbeta_c, da_c, conv_w, a_log, dt_bias, norm_g, ctx_out):
    H, dk = DN_HEADS, DN_HEAD_DIM

    def heads(u):
        B, T = u.shape[:2]
        q, k, v = jnp.split(_short_conv(u, conv_w), 3, axis=-1)
        q = _l2norm(q.reshape(B, T, H, dk)) * (dk ** -0.5)
        k = _l2norm(k.reshape(B, T, H, dk))
        return q, k, v.reshape(B, T, H, dk)

    def gates(braw, araw, d):
        beta = jax.nn.sigmoid(braw[..., d * H:(d + 1) * H].astype(jnp.float32))
        g = -jnp.exp(a_log[d].astype(jnp.float32)) * jax.nn.softplus(
            araw[..., d * H:(d + 1) * H].astype(jnp.float32) + dt_bias[d].astype(jnp.float32))
        return g, beta

    def out_gate(o, z):
        B, T = o.shape[:2]
        return (_rmsnorm(o, norm_g) * jax.nn.silu(z.reshape(B, T, H, dk))).reshape(B, T, H * dk)

    q, k, v = heads(dqkv)
    qc, kc, vc = heads(dqkv_c)
    s0 = jnp.zeros((q.shape[0], H, dk, dk), jnp.float32)
    o, oc = None, None
    for d in range(2):
        g, beta = gates(dbeta, da, d)
        g_c, beta_c = gates(dbeta_c, da_c, d)
        seq, cseq = (q, k, v, g, beta), (qc, kc, vc, g_c, beta_c)
        if d == 1:
            seq = tuple(jnp.flip(t, axis=1) for t in seq)
            cseq = tuple(jnp.flip(t, axis=1) for t in cseq)
        s_ctx, oc_d = _gated_delta_chunked(*cseq, s0, with_output=ctx_out)
        _, o_d = _gated_delta_chunked(*seq, s_ctx, with_output=True)
        if d == 1:
            o_d = jnp.flip(o_d, axis=1)
            oc_d = jnp.flip(oc_d, axis=1) if ctx_out else None
        o = o_d if o is None else o + o_d
        if ctx_out:
            oc = oc_d if oc is None else oc + oc_d
    o = out_gate(o, dz)
    return o, (out_gate(oc, dz_c) if ctx_out else None)


def _merge(branches, gate_raw, w_branch, w_out):
    gate_parts = jnp.split(gate_raw, N_BRANCH, axis=-1)
    y = None
    for i in range(N_BRANCH):
        yi = jax.nn.sigmoid(gate_parts[i]) * (branches[i] @ w_branch[i])
        y = yi if y is None else y + yi
    return y @ w_out


def _hybrid_mixer(h, hc, w_in, mla_q_norm_g, mla_w_q_up, mla_kv_norm_g, mla_w_kv_up, swa_sink,
                  dn_conv_w, dn_a_log, dn_dt_bias, dn_norm_g, w_branch, w_out, ctx_out):
    S = h.shape[1]
    cos_m, sin_m = _axial_rope_tables(S, MLA_ROPE)
    cos_s, sin_s = _axial_rope_tables(S, SWA_HEAD_DIM)
    cq, ckv, kr, sq, sk, sv, dqkv, dz, dbeta, da, gate_raw = _split(h @ w_in, IN_SIZES)
    cq_c, ckv_c, kr_c, sq_c, sk_c, sv_c, dqkv_c, dz_c, dbeta_c, da_c, gate_raw_c = _split(hc @ w_in, IN_SIZES)
    o_a, oc_a = _mla_branch(cq, ckv, kr, cq_c, ckv_c, kr_c, mla_q_norm_g, mla_w_q_up, mla_kv_norm_g,
                            mla_w_kv_up, cos_m, sin_m, ctx_out)
    o_b, oc_b = _swa_branch(sq, sk, sv, sq_c, sk_c, sv_c, swa_sink, cos_s, sin_s, ctx_out)
    o_c, oc_c = _deltanet_branch(dqkv, dz, dbeta, da, dqkv_c, dz_c, dbeta_c, da_c, dn_conv_w, dn_a_log,
                                 dn_dt_bias, dn_norm_g, ctx_out)
    y = _merge((o_a, o_b, o_c), gate_raw, w_branch, w_out)
    yc = _merge((oc_a, oc_b, oc_c), gate_raw_c, w_branch, w_out) if ctx_out else None
    return y, yc


def _moe_ffn(h, router_w, router_b, w_gu, b_gu, w_dn, b_dn):
    N, D = h.shape
    logits = (h @ router_w).astype(jnp.float32) + router_b.astype(jnp.float32)
    top_v, top_e = lax.top_k(logits, TOP_K)
    gate = jax.nn.softmax(top_v, axis=-1)
    A = N * TOP_K
    flat_e = top_e.reshape(A)
    order = jnp.argsort(flat_e)
    e_sorted = flat_e[order]
    tok_sorted = (order // TOP_K).astype(jnp.int32)
    gate_sorted = gate.reshape(A)[order]
    counts = jnp.zeros((N_EXPERTS,), jnp.int32).at[flat_e].add(1)
    padded = (counts + MOE_BLK - 1) // MOE_BLK * MOE_BLK
    pad_end = jnp.cumsum(padded)
    pad_start = pad_end - padded
    cnt_start = jnp.cumsum(counts) - counts
    slot = pad_start[e_sorted] + jnp.arange(A, dtype=jnp.int32) - cnt_start[e_sorted]
    n_blocks = -(-A // MOE_BLK) + N_EXPERTS
    P = n_blocks * MOE_BLK
    slot_tok = jnp.full((P,), N, jnp.int32).at[slot].set(tok_sorted)
    slot_gate = jnp.zeros((P,), jnp.float32).at[slot].set(gate_sorted)
    block_e = jnp.minimum(jnp.searchsorted(pad_end, jnp.arange(n_blocks, dtype=jnp.int32) * MOE_BLK, side='right'),
                          N_EXPERTS - 1)
    h_pad = jnp.concatenate([h, jnp.zeros((1, D), h.dtype)], axis=0)

    def run(args):
        idx, e = args
        xb = h_pad[idx]
        gu = xb @ w_gu[e] + b_gu[e]
        g_, u_ = gu[:, 0::2], gu[:, 1::2]
        g_ = jnp.minimum(g_, SWIGLU_LIMIT)
        u_ = jnp.clip(u_, -SWIGLU_LIMIT, SWIGLU_LIMIT)
        act = (u_ + 1) * (g_ * jax.nn.sigmoid(SWIGLU_ALPHA * g_))
        return act @ w_dn[e] + b_dn[e]

    yb = lax.map(run, (slot_tok.reshape(n_blocks, MOE_BLK), block_e)).reshape(P, D)
    y = jnp.zeros((N + 1, D), h.dtype).at[slot_tok].add(yb * slot_gate[:, None].astype(yb.dtype))
    return y[:N]


def setup_inputs(seed: int = 0) -> dict:
    key = jax.random.key(seed)
    ks = jax.random.split(key, 32)
    f32 = jnp.float32
    L, D, E, F, H = DEPTH, D_MODEL, N_EXPERTS, D_EXPERT, DN_HEADS

    def nrm(i, shape, scale):
        return jax.random.normal(ks[i], shape, f32) * scale

    def gain(i, shape):
        return 1.0 + nrm(i, shape, 0.02)

    dt = jnp.exp(jax.random.uniform(ks[16], (L, 2, H), f32, float(np.log(1e-3)), float(np.log(1e-1))))
    return {
        "x": nrm(0, (BATCH, SEQ, D), 1.0),
        "c": nrm(1, (BATCH, D), 1.0),
        "ctx": nrm(2, (BATCH, CTX_LEN, D), 1.0),
        "c_ctx": nrm(3, (D,), 1.0),
        "w_mod": nrm(4, (L, D, 6 * D), 0.5 * D ** -0.5),
        "b_mod": nrm(5, (L, 6 * D), 0.01),
        "norm1_g": gain(6, (L, D)),
        "norm2_g": gain(7, (L, D)),
        "w_in": nrm(8, (L, D, D_IN), D ** -0.5),
        "mla_q_norm_g": gain(9, (L, MLA_Q_RANK)),
        "mla_w_q_up": nrm(10, (L, MLA_Q_RANK, MLA_HEADS * (MLA_NOPE + MLA_ROPE)), MLA_Q_RANK ** -0.5),
        "mla_kv_norm_g": gain(11, (L, MLA_KV_RANK)),
        "mla_w_kv_up": nrm(12, (L, MLA_KV_RANK, MLA_HEADS * (MLA_NOPE + MLA_V)), MLA_KV_RANK ** -0.5),
        "swa_sink": nrm(13, (L, SWA_HEADS), 0.5),
        "dn_conv_w": nrm(14, (L, DN_CONV, 3 * DN_HEADS * DN_HEAD_DIM), DN_CONV ** -0.5),
        "dn_a_log": jnp.log(jax.random.uniform(ks[15], (L, 2, H), f32, 1.0, 8.0)),
        "dn_dt_bias": dt + jnp.log(-jnp.expm1(-dt)),
        "dn_norm_g": gain(17, (L, DN_HEAD_DIM)),
        "w_branch": nrm(18, (L, N_BRANCH, BRANCH_W, D), BRANCH_W ** -0.5),
        "w_out": nrm(19, (L, D, D), D ** -0.5),
        "router_w": nrm(20, (L, D, E), D ** -0.5),
        "router_b": nrm(21, (L, E), 0.01),
        "exp_w_gu": nrm(22, (L, E, D, 2 * F), D ** -0.5),
        "exp_b_gu": nrm(23, (L, E, 2 * F), 0.01),
        "exp_w_dn": nrm(24, (L, E, F, D), F ** -0.5),
        "exp_b_dn": nrm(25, (L, E, D), 0.01),
        "final_norm_g": gain(26, (D,)),
    }


def reference(x, c, ctx, c_ctx, w_mod, b_mod, norm1_g, norm2_g, w_in, mla_q_norm_g, mla_w_q_up,
              mla_kv_norm_g, mla_w_kv_up, swa_sink, dn_conv_w, dn_a_log, dn_dt_bias, dn_norm_g,
              w_branch, w_out, router_w, router_b, exp_w_gu, exp_b_gu, exp_w_dn, exp_b_dn, final_norm_g):
    B, S, D = x.shape
    T = ctx.shape[1]
    for l in range(DEPTH):
        last = l == DEPTH - 1
        mod = jax.nn.silu(c) @ w_mod[l] + b_mod[l]
        mod_c = jax.nn.silu(c_ctx) @ w_mod[l] + b_mod[l]
        sh1, sc1, g1, sh2, sc2, g2 = [m[:, None, :] for m in jnp.split(mod, 6, axis=-1)]
        sh1c, sc1c, g1c, sh2c, sc2c, g2c = jnp.split(mod_c, 6, axis=-1)

        h = _modulate(_rmsnorm(x, norm1_g[l]), sh1, sc1)
        hc = _modulate(_rmsnorm(ctx, norm1_g[l]), sh1c, sc1c)
        y, yc = _hybrid_mixer(h, hc, w_in[l], mla_q_norm_g[l], mla_w_q_up[l], mla_kv_norm_g[l], mla_w_kv_up[l],
                              swa_sink[l], dn_conv_w[l], dn_a_log[l], dn_dt_bias[l], dn_norm_g[l],
                              w_branch[l], w_out[l], ctx_out=not last)
        x = x + g1 * y
        h = _modulate(_rmsnorm(x, norm2_g[l]), sh2, sc2)
        moe_p = (router_w[l], router_b[l], exp_w_gu[l], exp_b_gu[l], exp_w_dn[l], exp_b_dn[l])
        if last:
            x = x + g2 * _moe_ffn(h.reshape(B * S, D), *moe_p).reshape(B, S, D)
        else:
            ctx = ctx + g1c * yc
            hc = _modulate(_rmsnorm(ctx, norm2_g[l]), sh2c, sc2c)
            f = _moe_ffn(jnp.concatenate([h.reshape(B * S, D), hc.reshape(B * T, D)], axis=0), *moe_p)
            x = x + g2 * f[:B * S].reshape(B, S, D)
            ctx = ctx + g2c * f[B * S:].reshape(B, T, D)
    return _rmsnorm(x, final_norm_g)
```

```python
import functools

import jax
import jax.numpy as jnp
import numpy as np
from jax import lax
from jax.experimental import pallas as pl
from jax.experimental.pallas import tpu as pltpu

F32 = jnp.float32
BF16 = jnp.bfloat16

D_MODEL = 1024
GRID_W = 64
ROPE_THETA = 10000.0
NORM_EPS = 1e-6
MLA_HEADS = 8
MLA_Q_RANK = 384
MLA_KV_RANK = 256
MLA_NOPE = 64
MLA_ROPE = 32
MLA_V = 64
SWA_HEADS = 8
SWA_KV_HEADS = 2
SWA_HEAD_DIM = 64
SWA_WINDOW = 128
DN_HEADS = 8
DN_HEAD_DIM = 64
DN_CONV = 5
DN_CHUNK = 64
N_BRANCH = 3
BRANCH_W = 512
N_EXPERTS = 32
TOP_K = 4
D_EXPERT = 1024
SWIGLU_LIMIT = 7.0
SWIGLU_ALPHA = 1.702
MOE_BLK = 256

LANES = 128
TM = 256
VMEM_LIMIT = 56 * 1024 * 1024

_C_CQ = (0, 384)
_C_CKV = (384, 640)
_C_SQ = (640, 1152)
_C_SK = (1152, 1280)
_C_SV = (1280, 1408)
_C_DQKV = (1408, 2944)
_C_DZ = (2944, 3456)
_C_KRT = (3456, 3584)
_C_DG = (3584, 3712)
_N_PROJ = 3712


def _cparams(n_axes=1):
    return pltpu.CompilerParams(dimension_semantics=("arbitrary",) * n_axes, vmem_limit_bytes=VMEM_LIMIT)


def _dot(a, b):
    return jnp.dot(a, b, preferred_element_type=F32)


def _dot_nt(a, b):
    return lax.dot_general(a, b, (((1,), (1,)), ((), ())), preferred_element_type=F32)


def _split_bf16(a):
    hi = a.astype(BF16)
    lo = (a - hi.astype(F32)).astype(BF16)
    return hi, lo


def _dot3(a, b):
    ah, al = _split_bf16(a)
    bh, bl = _split_bf16(b)
    return _dot(ah, bh) + (_dot(ah, bl) + _dot(al, bh))


def _dot3_nt(a, b):
    ah, al = _split_bf16(a)
    bh, bl = _split_bf16(b)
    return _dot_nt(ah, bh) + (_dot_nt(ah, bl) + _dot_nt(al, bh))


def _rms(x, g):
    return x * lax.rsqrt(jnp.mean(x * x, axis=-1, keepdims=True) + NORM_EPS) * g


def _rope_cols(x, tab_ref, half):
    c, sa, sb = tab_ref[0], tab_ref[1], tab_ref[2]
    return x * c + pltpu.roll(x, LANES - half, axis=1) * sa + pltpu.roll(x, half, axis=1) * sb


def _mod_body(c_ref, w_ref, b_ref, o_ref):
    c = c_ref[...]
    a = c * jax.nn.sigmoid(c)
    o_ref[...] = _dot3(a, w_ref[...]) + b_ref[...]


def _modulation(cvec, w_mod, b_mod):
    m = cvec.shape[0]
    n = w_mod.shape[1]
    tn = 512
    return pl.pallas_call(
        _mod_body,
        out_shape=jax.ShapeDtypeStruct((m, n), F32),
        grid=(n // tn,),
        in_specs=[pl.BlockSpec((m, D_MODEL), lambda j: (0, 0)),
                  pl.BlockSpec((D_MODEL, tn), lambda j: (0, j)),
                  pl.BlockSpec((1, tn), lambda j: (0, j))],
        out_specs=pl.BlockSpec((m, tn), lambda j: (0, j)),
        compiler_params=_cparams(),
        name="modulation",
    )(cvec, w_mod, b_mod.reshape(1, n))


def _inproj_body(x_ref, mod_ref, g_ref, w_ref, rope_ref,
                 cq_ref, ckv_ref, krt_ref, sq_ref, sk_ref, sv_ref, dqkv_ref, dz_ref, dg_ref):
    x = x_ref[...]
    h = (_rms(x, g_ref[...]) * (1.0 + mod_ref[0, 0, 1:2, :]) + mod_ref[0, 0, 0:1, :]).astype(BF16)

    def proj(cols):
        return _dot(h, w_ref[:, cols[0]:cols[1]])

    cq_ref[...] = proj(_C_CQ)
    ckv_ref[...] = proj(_C_CKV)
    krt_ref[...] = proj(_C_KRT)
    dqkv_ref[...] = proj(_C_DQKV)
    dz_ref[...] = proj(_C_DZ)
    dg_ref[...] = proj(_C_DG)
    sv_ref[...] = proj(_C_SV).astype(BF16)
    sk_ref[...] = _rope_cols(proj(_C_SK), rope_ref, SWA_HEAD_DIM // 2).astype(BF16)
    sq = proj(_C_SQ)
    lane = lax.broadcasted_iota(jnp.int32, (TM, LANES), 1)
    lo = lane < SWA_HEAD_DIM
    for c in range(4):
        xr = _rope_cols(sq[:, c * LANES:(c + 1) * LANES], rope_ref, SWA_HEAD_DIM // 2)
        xs = pltpu.roll(xr, SWA_HEAD_DIM, axis=1)
        if c < 2:
            a, b = jnp.where(lo, xr, 0.0), jnp.where(lo, xs, 0.0)
        else:
            a, b = jnp.where(lo, 0.0, xs), jnp.where(lo, 0.0, xr)
        sq_ref[:, (2 * c) * LANES:(2 * c + 1) * LANES] = a.astype(BF16)
        sq_ref[:, (2 * c + 1) * LANES:(2 * c + 2) * LANES] = b.astype(BF16)


def _inproj(x, modtab, g, w, rope_s, nb_per_batch):
    n = x.shape[0]
    row = lambda i: (i, 0)
    widths = [(384, F32), (256, F32), (128, F32), (1024, BF16), (128, BF16), (128, BF16),
              (1536, F32), (512, F32), (128, F32)]
    return pl.pallas_call(
        _inproj_body,
        out_shape=[jax.ShapeDtypeStruct((n, wd), dt) for wd, dt in widths],
        grid=(n // TM,),
        in_specs=[pl.BlockSpec((TM, D_MODEL), row),
                  pl.BlockSpec((1, 1, 6, D_MODEL),
                               lambda i: (i // nb_per_batch, jnp.minimum(i % nb_per_batch, 1), 0, 0)),
                  pl.BlockSpec((1, D_MODEL), lambda i: (0, 0)),
                  pl.BlockSpec((D_MODEL, _N_PROJ), lambda i: (0, 0)),
                  pl.BlockSpec((3, TM, LANES), lambda i: (0, i % nb_per_batch, 0))],
        out_specs=[pl.BlockSpec((TM, wd), row) for wd, _ in widths],
        compiler_params=_cparams(),
        name="inproj",
    )(x, modtab, g, w, rope_s)


def _mla_prep_body(cq_ref, ckv_ref, krt_ref, qg_ref, kvg_ref, wq_ref, wkv_ref, rope_ref,
                   q_ref, k_ref, v_ref):
    half = MLA_ROPE // 2
    qn = _rms(cq_ref[...], qg_ref[...]).astype(BF16)
    q = _dot(qn, wq_ref[...])
    q_ref[:, 0:512] = q[:, 0:512].astype(BF16)
    for c in range(2):
        lo_, hi_ = 512 + c * LANES, 512 + (c + 1) * LANES
        q_ref[:, lo_:hi_] = _rope_cols(q[:, lo_:hi_], rope_ref, half).astype(BF16)
    kvn = _rms(ckv_ref[...], kvg_ref[...]).astype(BF16)
    kv = _dot(kvn, wkv_ref[...])
    kr = _rope_cols(krt_ref[...], rope_ref, half).astype(BF16)
    for c in range(4):
        k_ref[:, (2 * c) * LANES:(2 * c + 1) * LANES] = kv[:, c * LANES:(c + 1) * LANES].astype(BF16)
        k_ref[:, (2 * c + 1) * LANES:(2 * c + 2) * LANES] = kr
    v_ref[...] = kv[:, 512:1024].astype(BF16)


def _mla_prep(cq, ckv, krt, qg, kvg, wq, wkv, rope_m, nb_per_batch):
    n = cq.shape[0]
    row = lambda i: (i, 0)
    full = lambda i: (0, 0)
    return pl.pallas_call(
        _mla_prep_body,
        out_shape=[jax.ShapeDtypeStruct((n, 768), BF16), jax.ShapeDtypeStruct((n, 1024), BF16),
                   jax.ShapeDtypeStruct((n, 512), BF16)],
        grid=(n // TM,),
        in_specs=[pl.BlockSpec((TM, 384), row), pl.BlockSpec((TM, 256), row), pl.BlockSpec((TM, 128), row),
                  pl.BlockSpec((1, 384), full), pl.BlockSpec((1, 256), full),
                  pl.BlockSpec((384, 768), full), pl.BlockSpec((256, 1024), full),
                  pl.BlockSpec((3, TM, LANES), lambda i: (0, i % nb_per_batch, 0))],
        out_specs=[pl.BlockSpec((TM, 768), row), pl.BlockSpec((TM, 1024), row), pl.BlockSpec((TM, 512), row)],
        compiler_params=_cparams(),
        name="mla_prep",
    )(cq, ckv, krt, qg, kvg, wq, wkv, rope_m)


def _mla_attn_body(q_ref, k_ref, v_ref, o_ref, *, t_ctx):
    scale = (MLA_NOPE + MLA_ROPE) ** -0.5
    lane = lax.broadcasted_iota(jnp.int32, (TM, LANES), 1)
    zero = jnp.zeros((TM, LANES), BF16)

    def attend(nk):
        for c in range(4):
            kc = k_ref[0:nk, c * 2 * LANES:(c + 1) * 2 * LANES]
            vc = v_ref[0:nk, c * LANES:(c + 1) * LANES]
            qn = q_ref[:, c * LANES:(c + 1) * LANES]
            qr = q_ref[:, 512 + (c // 2) * LANES:512 + (c // 2 + 1) * LANES]
            outs = []
            for s in range(2):
                h4 = (2 * c + s) % 4
                qa = jnp.concatenate(
                    [jnp.where((lane >= 64 * s) & (lane < 64 * (s + 1)), qn, zero),
                     jnp.where((lane >= 32 * h4) & (lane < 32 * (h4 + 1)), qr, zero)], axis=1)
                sc = _dot_nt(qa, kc) * scale
                m = jnp.max(sc, axis=-1, keepdims=True)
                e = jnp.exp(sc - m)
                l = jnp.sum(e, axis=-1, keepdims=True)
                outs.append(_dot(e.astype(BF16), vc) / l)
            o_ref[:, c * LANES:(c + 1) * LANES] = jnp.where(lane < 64, outs[0], outs[1]).astype(BF16)

    j = pl.program_id(1)

    @pl.when(j == 0)
    def _():
        attend(t_ctx)

    @pl.when(j > 0)
    def _():
        attend(k_ref.shape[0])


def _mla_attention(q, k, v, n_batch, rows):
    nbp = rows // TM
    return pl.pallas_call(
        functools.partial(_mla_attn_body, t_ctx=TM),
        out_shape=jax.ShapeDtypeStruct((n_batch * rows, BRANCH_W), BF16),
        grid=(n_batch, nbp),
        in_specs=[pl.BlockSpec((TM, 768), lambda b, j: (b * nbp + j, 0)),
                  pl.BlockSpec((rows, 1024), lambda b, j: (b, 0)),
                  pl.BlockSpec((rows, 512), lambda b, j: (b, 0))],
        out_specs=pl.BlockSpec((TM, BRANCH_W), lambda b, j: (b * nbp + j, 0)),
        compiler_params=_cparams(2),
        name="mla_attention",
    )(q, k, v)


SWA_QB = 128


def _swa_body(q_ref, k_ref, v_ref, sink_ref, o_ref, *, t_ctx, seq):
    scale = SWA_HEAD_DIM ** -0.5
    j = pl.program_id(1)
    n_ctx_blk = t_ctx // SWA_QB
    lane = lax.broadcasted_iota(jnp.int32, (SWA_QB, LANES), 1)
    lo = lane < SWA_HEAD_DIM
    win = 3 * SWA_QB

    def finish(parts):
        for c in range(4):
            g = c // 2
            r0 = (2 * (c % 2)) * SWA_QB
            a = parts[g][r0:r0 + SWA_QB]
            b = parts[g][r0 + SWA_QB:r0 + 2 * SWA_QB]
            if g == 0:
                col = jnp.where(lo, a, pltpu.roll(b, SWA_HEAD_DIM, axis=1))
            else:
                col = jnp.where(lo, pltpu.roll(a, SWA_HEAD_DIM, axis=1), b)
            o_ref[:, c * LANES:(c + 1) * LANES] = col.astype(BF16)

    def group_q(g):
        return jnp.concatenate([q_ref[:, (4 * g + i) * LANES:(4 * g + i + 1) * LANES] for i in range(4)], axis=0)

    def group_sink(g):
        return jnp.concatenate(
            [jnp.broadcast_to(sink_ref[4 * g + i:4 * g + i + 1, 0:1], (SWA_QB, 1)) for i in range(4)], axis=0)

    @pl.when(j < n_ctx_blk)
    def _():
        kc, vc = k_ref[0:t_ctx, :], v_ref[0:t_ctx, :]
        parts = []
        for g in range(SWA_KV_HEADS):
            sk = group_sink(g)
            s_c = _dot_nt(group_q(g), kc) * scale
            m = jnp.maximum(jnp.max(s_c, axis=-1, keepdims=True), sk)
            e_c = jnp.exp(s_c - m)
            l = jnp.sum(e_c, axis=-1, keepdims=True) + jnp.exp(sk - m)
            parts.append(_dot(e_c.astype(BF16), vc) / l)
        finish(parts)

    @pl.when(j >= n_ctx_blk)
    def _():
        i = j - n_ctx_blk
        start = jnp.clip((i - 1) * SWA_QB, 0, seq - win)
        rs = pl.multiple_of(t_ctx + start, SWA_QB)
        kc, vc = k_ref[0:t_ctx, :], v_ref[0:t_ctx, :]
        kw, vw = k_ref[pl.ds(rs, win), :], v_ref[pl.ds(rs, win), :]
        qpos = i * SWA_QB + lax.broadcasted_iota(jnp.int32, (SWA_QB, win), 0)
        kpos = start + lax.broadcasted_iota(jnp.int32, (SWA_QB, win), 1)
        band1 = jnp.abs(kpos - qpos) <= SWA_WINDOW
        band = jnp.concatenate([band1] * 4, axis=0)
        parts = []
        for g in range(SWA_KV_HEADS):
            qg = group_q(g)
            sk = group_sink(g)
            s_c = _dot_nt(qg, kc) * scale
            s_w = jnp.where(band, _dot_nt(qg, kw) * scale, -jnp.inf)
            m = jnp.maximum(jnp.maximum(jnp.max(s_c, axis=-1, keepdims=True),
                                        jnp.max(s_w, axis=-1, keepdims=True)), sk)
            e_c = jnp.exp(s_c - m)
            e_w = jnp.exp(s_w - m)
            l = (jnp.sum(e_c, axis=-1, keepdims=True) + jnp.sum(e_w, axis=-1, keepdims=True)
                 + jnp.exp(sk - m))
            parts.append((_dot(e_c.astype(BF16), vc) + _dot(e_w.astype(BF16), vw)) / l)
        finish(parts)


def _swa_attention(q, k, v, sink, n_batch, t_ctx, seq):
    rows = t_ctx + seq
    nqb = rows // SWA_QB
    sink_tab = jnp.broadcast_to(sink.astype(F32)[:, None], (SWA_HEADS, LANES))
    return pl.pallas_call(
        functools.partial(_swa_body, t_ctx=t_ctx, seq=seq),
        out_shape=jax.ShapeDtypeStruct((n_batch * rows, BRANCH_W), BF16),
        grid=(n_batch, nqb),
        in_specs=[pl.BlockSpec((SWA_QB, 1024), lambda b, j: (b * nqb + j, 0)),
                  pl.BlockSpec((rows, LANES), lambda b, j: (b, 0)),
                  pl.BlockSpec((rows, LANES), lambda b, j: (b, 0)),
                  pl.BlockSpec((SWA_HEADS, LANES), lambda b, j: (0, 0))],
        out_specs=pl.BlockSpec((SWA_QB, BRANCH_W), lambda b, j: (b * nqb + j, 0)),
        compiler_params=_cparams(2),
        name="swa_attention",
    )(q, k, v, sink_tab)


def _merge_body(x_ref, mod_ref, g1_ref, g2_ref, wg_ref, oa_ref, ob_ref, oc_ref, wb_ref, wo_ref,
                rw_ref, rb_ref, xo_ref, h2_ref, lg_ref):
    x = x_ref[...]
    h = (_rms(x, g1_ref[...]) * (1.0 + mod_ref[0, 0, 1:2, :]) + mod_ref[0, 0, 0:1, :]).astype(BF16)
    y = None
    for i, o_ref in enumerate((oa_ref, ob_ref, oc_ref)):
        gate = jax.nn.sigmoid(_dot(h, wg_ref[:, i * D_MODEL:(i + 1) * D_MODEL]))
        yi = gate * _dot(o_ref[...], wb_ref[i])
        y = yi if y is None else y + yi
    xn = x + mod_ref[0, 0, 2:3, :] * _dot(y.astype(BF16), wo_ref[...])
    xo_ref[...] = xn
    h2 = _rms(xn, g2_ref[...]) * (1.0 + mod_ref[0, 0, 4:5, :]) + mod_ref[0, 0, 3:4, :]
    h2_ref[...] = h2.astype(BF16)
    lg_ref[...] = _dot3_nt(rw_ref[...], h2) + rb_ref[...]


def _merge(x, modtab, g1, g2, wg, oa, ob, oc, wb, wo, rwt, rb, nb_per_batch):
    n = x.shape[0]
    row = lambda i: (i, 0)
    full2 = lambda i: (0, 0)
    return pl.pallas_call(
        _merge_body,
        out_shape=[jax.ShapeDtypeStruct((n, D_MODEL), F32), jax.ShapeDtypeStruct((n, D_MODEL), BF16),
                   jax.ShapeDtypeStruct((N_EXPERTS, n), F32)],
        grid=(n // TM,),
        in_specs=[pl.BlockSpec((TM, D_MODEL), row),
                  pl.BlockSpec((1, 1, 6, D_MODEL),
                               lambda i: (i // nb_per_batch, jnp.minimum(i % nb_per_batch, 1), 0, 0)),
                  pl.BlockSpec((1, D_MODEL), full2), pl.BlockSpec((1, D_MODEL), full2),
                  pl.BlockSpec((D_MODEL, N_BRANCH * D_MODEL), full2),
                  pl.BlockSpec((TM, BRANCH_W), row), pl.BlockSpec((TM, BRANCH_W), row),
                  pl.BlockSpec((TM, BRANCH_W), row),
                  pl.BlockSpec((N_BRANCH, BRANCH_W, D_MODEL), lambda i: (0, 0, 0)),
                  pl.BlockSpec((D_MODEL, D_MODEL), full2),
                  pl.BlockSpec((N_EXPERTS, D_MODEL), full2), pl.BlockSpec((N_EXPERTS, 1), full2)],
        out_specs=[pl.BlockSpec((TM, D_MODEL), row), pl.BlockSpec((TM, D_MODEL), row),
                   pl.BlockSpec((N_EXPERTS, TM), lambda i: (0, i))],
        compiler_params=_cparams(),
        name="merge",
    )(x, modtab, g1, g2, wg, oa, ob, oc, wb, wo, rwt, rb)


ROUTE_TB = 1024


def _router_body(lg_ref, tri_ref, e_ref, gate_ref, pos_ref, cnt_ref, run_ref):
    @pl.when(pl.program_id(0) == 0)
    def _():
        run_ref[...] = jnp.zeros_like(run_ref)

    lg = lg_ref[...]
    eid = lax.broadcasted_iota(jnp.int32, lg.shape, 0)
    work = lg
    vals, idxs = [], []
    sel = jnp.zeros(lg.shape, F32)
    for _ in range(TOP_K):
        m = jnp.max(work, axis=0, keepdims=True)
        idx = jnp.min(jnp.where(work == m, eid, N_EXPERTS), axis=0, keepdims=True)
        hit = eid == idx
        sel = jnp.where(hit, 1.0, sel)
        work = jnp.where(hit, -jnp.inf, work)
        vals.append(m)
        idxs.append(idx)
    ex = [jnp.exp(v - vals[0]) for v in vals]
    den = ex[0] + ex[1] + ex[2] + ex[3]
    before = _dot(sel.astype(BF16), tri_ref[...]) + run_ref[:, 0:1]
    for k in range(TOP_K):
        e_ref[k:k + 1, :] = idxs[k]
        gate_ref[k:k + 1, :] = ex[k] / den
        pos_ref[k:k + 1, :] = jnp.sum(jnp.where(eid == idxs[k], before, 0.0), axis=0,
                                      keepdims=True).astype(jnp.int32)
    run_ref[...] = run_ref[...] + jnp.sum(sel, axis=1, keepdims=True)
    cnt_ref[...] = run_ref[...]


def _router(logits_t):
    n = logits_t.shape[1]
    tri = (jnp.arange(ROUTE_TB)[:, None] < jnp.arange(ROUTE_TB)[None, :]).astype(BF16)
    blk = lambda i: (0, i)
    return pl.pallas_call(
        _router_body,
        out_shape=[jax.ShapeDtypeStruct((TOP_K, n), jnp.int32), jax.ShapeDtypeStruct((TOP_K, n), F32),
                   jax.ShapeDtypeStruct((TOP_K, n), jnp.int32), jax.ShapeDtypeStruct((N_EXPERTS, LANES), F32)],
        grid=(n // ROUTE_TB,),
        in_specs=[pl.BlockSpec((N_EXPERTS, ROUTE_TB), blk), pl.BlockSpec((ROUTE_TB, ROUTE_TB), lambda i: (0, 0))],
        out_specs=[pl.BlockSpec((TOP_K, ROUTE_TB), blk), pl.BlockSpec((TOP_K, ROUTE_TB), blk),
                   pl.BlockSpec((TOP_K, ROUTE_TB), blk), pl.BlockSpec((N_EXPERTS, LANES), lambda i: (0, 0))],
        scratch_shapes=[pltpu.VMEM((N_EXPERTS, LANES), F32)],
        compiler_params=_cparams(),
        name="router",
    )(logits_t, tri)


def _expert_body(be_ref, nv_ref, x_ref, wgu_ref, bgu_ref, wdn_ref, bdn_ref, o_ref):
    @pl.when(pl.program_id(0) < nv_ref[0])
    def _():
        gu = _dot(x_ref[...], wgu_ref[0]) + bgu_ref[0]
        g_ = jnp.minimum(gu[:, :D_EXPERT], SWIGLU_LIMIT)
        u_ = jnp.clip(gu[:, D_EXPERT:], -SWIGLU_LIMIT, SWIGLU_LIMIT)
        act = (u_ + 1.0) * (g_ * jax.nn.sigmoid(SWIGLU_ALPHA * g_))
        o_ref[...] = _dot(act.astype(BF16), wdn_ref[0]) + bdn_ref[0]

    @pl.when(pl.program_id(0) >= nv_ref[0])
    def _():
        o_ref[...] = jnp.zeros_like(o_ref)


def _experts(block_e, n_valid, xs, wgu, bgu, wdn, bdn):
    p = xs.shape[0]
    n_blocks = p // MOE_BLK
    return pl.pallas_call(
        _expert_body,
        out_shape=jax.ShapeDtypeStruct((p, D_MODEL), F32),
        grid_spec=pltpu.PrefetchScalarGridSpec(
            num_scalar_prefetch=2,
            grid=(n_blocks,),
            in_specs=[pl.BlockSpec((MOE_BLK, D_MODEL), lambda i, be, nv: (i, 0)),
                      pl.BlockSpec((1, D_MODEL, 2 * D_EXPERT), lambda i, be, nv: (be[i], 0, 0)),
                      pl.BlockSpec((1, 1, 2 * D_EXPERT), lambda i, be, nv: (be[i], 0, 0)),
                      pl.BlockSpec((1, D_EXPERT, D_MODEL), lambda i, be, nv: (be[i], 0, 0)),
                      pl.BlockSpec((1, 1, D_MODEL), lambda i, be, nv: (be[i], 0, 0))],
            out_specs=pl.BlockSpec((MOE_BLK, D_MODEL), lambda i, be, nv: (i, 0))),
        compiler_params=_cparams(),
        name="experts",
    )(block_e, n_valid, xs, wgu, bgu, wdn, bdn)


def _moe(h2, logits_t, wgu, bgu, wdn, bdn):
    n = h2.shape[0]
    a = n * TOP_K
    top_e, gate, pos, cnt = _router(logits_t)
    counts = cnt[:, 0].astype(jnp.int32)
    padded = (counts + MOE_BLK - 1) // MOE_BLK * MOE_BLK
    pad_end = jnp.cumsum(padded)
    pad_start = pad_end - padded
    n_blocks = -(-a // MOE_BLK) + N_EXPERTS
    p = n_blocks * MOE_BLK
    slot = pad_start[top_e] + pos
    tok = jnp.broadcast_to(jnp.arange(n, dtype=jnp.int32)[None, :], (TOP_K, n))
    slot_tok = jnp.zeros((p,), jnp.int32).at[slot.reshape(-1)].set(tok.reshape(-1))
    block_e = jnp.minimum(
        jnp.searchsorted(pad_end, jnp.arange(n_blocks, dtype=jnp.int32) * MOE_BLK, side='right'),
        N_EXPERTS - 1).astype(jnp.int32)
    n_valid = (pad_end[-1] // MOE_BLK).astype(jnp.int32).reshape(1)
    xs = jnp.take(h2, slot_tok, axis=0)
    yb = _experts(block_e, n_valid, xs, wgu, bgu, wdn, bdn)
    y = None
    for k in range(TOP_K):
        yk = jnp.take(yb, slot[k], axis=0) * gate[k][:, None]
        y = yk if y is None else y + yk
    return y


def _gated_delta_chunked(q, k, v, g, beta, s0, with_output):
    B, T, H, dk = q.shape
    dv = v.shape[-1]
    C = DN_CHUNK
    n = T // C

    def chunks(x):
        x = x.astype(F32).reshape((B, n, C, H) + x.shape[3:])
        return jnp.moveaxis(x, 3, 2)

    q, k, v, g, beta = chunks(q), chunks(k), chunks(v), chunks(g), chunks(beta)
    gcum = jnp.cumsum(g, axis=-1)
    causal = jnp.tril(jnp.ones((C, C), bool))
    strict = jnp.tril(jnp.ones((C, C), bool), -1)
    diff = gcum[..., :, None] - gcum[..., None, :]
    decay = jnp.where(causal, jnp.exp(jnp.where(causal, diff, 0.0)), 0.0)
    kb = k * beta[..., None]
    lmat = jnp.where(strict, jnp.einsum('bnhid,bnhjd->bnhij', kb, k) * decay, 0.0)
    amat = lmat + jnp.eye(C, dtype=F32)
    rhs = jnp.concatenate([v * beta[..., None], kb * jnp.exp(gcum)[..., None]], axis=-1)
    sol = lax.linalg.triangular_solve(amat, rhs, left_side=True, lower=True, unit_diagonal=True)
    u, w = sol[..., :dv], sol[..., dv:]
    k_dec = k * jnp.exp(gcum[..., -1:] - gcum)[..., None]
    g_last = jnp.exp(gcum[..., -1])
    to_scan = lambda x: jnp.moveaxis(x, 1, 0)

    def state_update(S, w_i, u_i, kd_i, gl_i):
        v_new = u_i - jnp.einsum('bhck,bhkv->bhcv', w_i, S)
        S_next = S * gl_i[..., None, None] + jnp.einsum('bhck,bhcv->bhkv', kd_i, v_new)
        return S_next, v_new

    if not with_output:
        def step_state(S, xs):
            S_next, _ = state_update(S, *xs)
            return S_next, None
        s_fin, _ = lax.scan(step_state, s0, tuple(map(to_scan, (w, u, k_dec, g_last))))
        return s_fin, None

    a_intra = jnp.einsum('bnhid,bnhjd->bnhij', q, k) * decay
    q_dec = q * jnp.exp(gcum)[..., None]

    def step(S, xs):
        w_i, u_i, kd_i, gl_i, qd_i, a_i = xs
        S_next, v_new = state_update(S, w_i, u_i, kd_i, gl_i)
        o = jnp.einsum('bhck,bhkv->bhcv', qd_i, S) + jnp.einsum('bhij,bhjv->bhiv', a_i, v_new)
        return S_next, o

    s_fin, o = lax.scan(step, s0, tuple(map(to_scan, (w, u, k_dec, g_last, q_dec, a_intra))))
    o = jnp.moveaxis(jnp.moveaxis(o, 0, 1), 2, 3).reshape(B, T, H, dv)
    return s_fin, o


def _deltanet(dqkv, dz, dgraw, conv_w, a_log, dt_bias, norm_g, t_ctx, ctx_out):
    H, dk = DN_HEADS, DN_HEAD_DIM

    def short_conv(u):
        C = u.shape[-1]
        y = lax.conv_general_dilated(u, conv_w[:, None, :], window_strides=(1,),
                                     padding=[(DN_CONV // 2, DN_CONV // 2)],
                                     dimension_numbers=('NWC', 'WIO', 'NWC'), feature_group_count=C)
        return jax.nn.silu(y)

    def l2norm(x):
        return x * lax.rsqrt(jnp.sum(x * x, axis=-1, keepdims=True) + NORM_EPS)

    def heads(u):
        B, T = u.shape[:2]
        q, k, v = jnp.split(short_conv(u), 3, axis=-1)
        q = l2norm(q.reshape(B, T, H, dk)) * (dk ** -0.5)
        k = l2norm(k.reshape(B, T, H, dk))
        return q, k, v.reshape(B, T, H, dk)

    def gates(raw, d):
        beta = jax.nn.sigmoid(raw[..., d * H:(d + 1) * H])
        g = -jnp.exp(a_log[d]) * jax.nn.softplus(raw[..., 2 * H + d * H:2 * H + (d + 1) * H] + dt_bias[d])
        return g, beta

    def out_gate(o, z):
        B, T = o.shape[:2]
        on = o * lax.rsqrt(jnp.mean(o * o, axis=-1, keepdims=True) + NORM_EPS) * norm_g
        return (on * jax.nn.silu(z.reshape(B, T, H, dk))).reshape(B, T, H * dk)

    qc, kc, vc = heads(dqkv[:, :t_ctx])
    q, k, v = heads(dqkv[:, t_ctx:])
    s0 = jnp.zeros((q.shape[0], H, dk, dk), F32)
    o, oc = None, None
    for d in range(2):
        g, beta = gates(dgraw[:, t_ctx:], d)
        g_c, beta_c = gates(dgraw[:, :t_ctx], d)
        seq, cseq = (q, k, v, g, beta), (qc, kc, vc, g_c, beta_c)
        if d == 1:
            seq = tuple(jnp.flip(t, axis=1) for t in seq)
            cseq = tuple(jnp.flip(t, axis=1) for t in cseq)
        s_ctx, oc_d = _gated_delta_chunked(*cseq, s0, with_output=ctx_out)
        _, o_d = _gated_delta_chunked(*seq, s_ctx, with_output=True)
        if d == 1:
            o_d = jnp.flip(o_d, axis=1)
            oc_d = jnp.flip(oc_d, axis=1) if ctx_out else None
        o = o_d if o is None else o + o_d
        if ctx_out:
            oc = oc_d if oc is None else oc + oc_d
    o = out_gate(o, dz[:, t_ctx:])
    if ctx_out:
        oc = out_gate(oc, dz[:, :t_ctx])
    else:
        oc = jnp.zeros((o.shape[0], t_ctx, H * dk), o.dtype)
    return jnp.concatenate([oc, o], axis=1)


def _final_norm_body(x_ref, g_ref, o_ref):
    o_ref[...] = _rms(x_ref[...], g_ref[...])


def _final_norm(x, g):
    n = x.shape[0]
    return pl.pallas_call(
        _final_norm_body,
        out_shape=jax.ShapeDtypeStruct(x.shape, F32),
        grid=(n // TM,),
        in_specs=[pl.BlockSpec((TM, D_MODEL), lambda i: (i, 0)), pl.BlockSpec((1, D_MODEL), lambda i: (0, 0))],
        out_specs=pl.BlockSpec((TM, D_MODEL), lambda i: (i, 0)),
        compiler_params=_cparams(),
        name="final_norm",
    )(x, g)


def _rope_tables(seq, t_ctx, rot_dim):
    rows = seq // GRID_W
    row = jnp.broadcast_to(jnp.arange(rows)[:, None], (rows, GRID_W)).reshape(-1).astype(F32)
    col = jnp.broadcast_to(jnp.arange(GRID_W)[None, :], (rows, GRID_W)).reshape(-1).astype(F32)
    n_freq = rot_dim // 4
    inv_freq = ROPE_THETA ** (-jnp.arange(n_freq, dtype=F32) / n_freq)
    ang = jnp.concatenate([row[:, None] * inv_freq, col[:, None] * inv_freq], axis=-1)
    cos, sin = jnp.cos(ang), jnp.sin(ang)
    zero = jnp.zeros_like(sin)
    c = jnp.concatenate([cos, cos], axis=-1)
    sa = jnp.concatenate([-sin, zero], axis=-1)
    sb = jnp.concatenate([zero, sin], axis=-1)
    ident = jnp.stack([jnp.ones((t_ctx, rot_dim), F32), jnp.zeros((t_ctx, rot_dim), F32),
                       jnp.zeros((t_ctx, rot_dim), F32)])
    tab = jnp.concatenate([ident, jnp.stack([c, sa, sb])], axis=1)
    return jnp.tile(tab, (1, 1, LANES // rot_dim))


def _layer_weights(w_in, w_q_up, w_kv_up, w_branch, w_out, router_w, router_b, w_gu, b_gu, w_dn, b_dn):
    sizes = (MLA_Q_RANK, MLA_KV_RANK, MLA_ROPE, 512, 128, 128, 1536, 512, 16, 16, N_BRANCH * D_MODEL)
    offs = np.cumsum((0,) + sizes)
    part = lambda i: w_in[:, offs[i]:offs[i + 1]]
    pad = jnp.zeros((D_MODEL, LANES - 32), F32)
    w_proj = jnp.concatenate(
        [part(0), part(1), part(3), part(4), part(5), part(6), part(7),
         jnp.tile(part(2), (1, LANES // MLA_ROPE)), part(8), part(9), pad], axis=1).astype(BF16)
    w_gate = part(10).astype(BF16)
    hq = MLA_NOPE + MLA_ROPE
    wq = w_q_up.reshape(MLA_Q_RANK, MLA_HEADS, hq)
    wq = jnp.concatenate([wq[:, :, :MLA_NOPE].reshape(MLA_Q_RANK, -1),
                          wq[:, :, MLA_NOPE:].reshape(MLA_Q_RANK, -1)], axis=1).astype(BF16)
    wkv = w_kv_up.reshape(MLA_KV_RANK, MLA_HEADS, MLA_NOPE + MLA_V)
    wkv = jnp.concatenate([wkv[:, :, :MLA_NOPE].reshape(MLA_KV_RANK, -1),
                           wkv[:, :, MLA_NOPE:].reshape(MLA_KV_RANK, -1)], axis=1).astype(BF16)
    wgu = jnp.concatenate([w_gu[:, :, 0::2], w_gu[:, :, 1::2]], axis=-1).astype(BF16)
    bgu = jnp.concatenate([b_gu[:, 0::2], b_gu[:, 1::2]], axis=-1).reshape(N_EXPERTS, 1, 2 * D_EXPERT)
    return dict(w_proj=w_proj, w_gate=w_gate, wq=wq, wkv=wkv, wb=w_branch.astype(BF16),
                wo=w_out.astype(BF16), rwt=router_w.T, rb=router_b.reshape(N_EXPERTS, 1),
                wgu=wgu, bgu=bgu, wdn=w_dn.astype(BF16), bdn=b_dn.reshape(N_EXPERTS, 1, D_MODEL))


def kernel(x, c, ctx, c_ctx, w_mod, b_mod, norm1_g, norm2_g, w_in, mla_q_norm_g, mla_w_q_up,
           mla_kv_norm_g, mla_w_kv_up, swa_sink, dn_conv_w, dn_a_log, dn_dt_bias, dn_norm_g,
           w_branch, w_out, router_w, router_b, exp_w_gu, exp_b_gu, exp_w_dn, exp_b_dn, final_norm_g):
    B, S, D = x.shape
    T = ctx.shape[1]
    R = T + S
    depth = w_mod.shape[0]
    assert D == D_MODEL and T == TM and S % TM == 0 and (B * R) % ROUTE_TB == 0
    nbp = R // TM
    xa = jnp.concatenate([ctx, x], axis=1).reshape(B * R, D)
    rope_s = _rope_tables(S, T, SWA_HEAD_DIM)
    rope_m = _rope_tables(S, T, MLA_ROPE)
    cvec = jnp.concatenate([c, c_ctx[None, :]], axis=0)
    for l in range(depth):
        last = l == depth - 1
        w = _layer_weights(w_in[l], mla_w_q_up[l], mla_w_kv_up[l], w_branch[l], w_out[l], router_w[l],
                           router_b[l], exp_w_gu[l], exp_b_gu[l], exp_w_dn[l], exp_b_dn[l])
        mod = _modulation(cvec, w_mod[l], b_mod[l]).reshape(B + 1, 6, D)
        modtab = jnp.stack([jnp.broadcast_to(mod[B][None], (B, 6, D)), mod[:B]], axis=1)
        g1 = norm1_g[l].reshape(1, D)
        g2 = norm2_g[l].reshape(1, D)
        cq, ckv, krt, sq, sk, sv, dqkv, dz, dg = _inproj(xa, modtab, g1, w["w_proj"], rope_s, nbp)
        q_m, k_m, v_m = _mla_prep(cq, ckv, krt, mla_q_norm_g[l].reshape(1, -1), mla_kv_norm_g[l].reshape(1, -1),
                                  w["wq"], w["wkv"], rope_m, nbp)
        o_a = _mla_attention(q_m, k_m, v_m, B, R)
        o_b = _swa_attention(sq, sk, sv, swa_sink[l], B, T, S)
        o_c = _deltanet(dqkv.reshape(B, R, -1), dz.reshape(B, R, -1), dg.reshape(B, R, -1), dn_conv_w[l],
                        dn_a_log[l], dn_dt_bias[l], dn_norm_g[l], T, not last)
        o_c = o_c.reshape(B * R, BRANCH_W).astype(BF16)
        xn, h2, logits_t = _merge(xa, modtab, g1, g2, w["w_gate"], o_a, o_b, o_c, w["wb"], w["wo"],
                                  w["rwt"], w["rb"], nbp)
        y = _moe(h2, logits_t, w["wgu"], w["bgu"], w["wdn"], w["bdn"])
        gate2 = jnp.broadcast_to(modtab[:, :, 5][:, :, None, :], (B, 2, 1, D))
        g2rows = jnp.concatenate([jnp.broadcast_to(gate2[:, 0], (B, T, D)),
                                  jnp.broadcast_to(gate2[:, 1], (B, S, D))], axis=1).reshape(B * R, D)
        xa = xn + g2rows * y
    out = _final_norm(xa, final_norm_g.reshape(1, D)).reshape(B, R, D)
    return out[:, T:]
```

```python
import functools

import jax
import jax.numpy as jnp
import numpy as np
from jax import lax
from jax.experimental import pallas as pl
from jax.experimental.pallas import tpu as pltpu

F32 = jnp.float32
BF16 = jnp.bfloat16

D_MODEL = 1024
GRID_W = 64
ROPE_THETA = 10000.0
NORM_EPS = 1e-6
MLA_HEADS = 8
MLA_Q_RANK = 384
MLA_KV_RANK = 256
MLA_NOPE = 64
MLA_ROPE = 32
MLA_V = 64
SWA_HEADS = 8
SWA_KV_HEADS = 2
SWA_HEAD_DIM = 64
SWA_WINDOW = 128
DN_HEADS = 8
DN_HEAD_DIM = 64
DN_CONV = 5
DN_CHUNK = 64
N_BRANCH = 3
BRANCH_W = 512
N_EXPERTS = 32
TOP_K = 4
D_EXPERT = 1024
SWIGLU_LIMIT = 7.0
SWIGLU_ALPHA = 1.702
MOE_BLK = 256

LANES = 128
TM = 256
VMEM_LIMIT = 56 * 1024 * 1024

_C_CQ = (0, 384)
_C_CKV = (384, 640)
_C_SQ = (640, 1152)
_C_SK = (1152, 1280)
_C_SV = (1280, 1408)
_C_DQKV = (1408, 2944)
_C_DZ = (2944, 3456)
_C_KRT = (3456, 3584)
_C_DG = (3584, 3712)
_N_PROJ = 3712


def _cparams(n_axes=1):
    return pltpu.CompilerParams(dimension_semantics=("arbitrary",) * n_axes, vmem_limit_bytes=VMEM_LIMIT)


def _dot(a, b):
    return jnp.dot(a, b, preferred_element_type=F32)


def _dot_nt(a, b):
    return lax.dot_general(a, b, (((1,), (1,)), ((), ())), preferred_element_type=F32)


def _split_bf16(a):
    hi = a.astype(BF16)
    lo = (a - hi.astype(F32)).astype(BF16)
    return hi, lo


def _dot3(a, b):
    ah, al = _split_bf16(a)
    bh, bl = _split_bf16(b)
    return _dot(ah, bh) + (_dot(ah, bl) + _dot(al, bh))


def _dot3_nt(a, b):
    ah, al = _split_bf16(a)
    bh, bl = _split_bf16(b)
    return _dot_nt(ah, bh) + (_dot_nt(ah, bl) + _dot_nt(al, bh))


def _rms(x, g):
    return x * lax.rsqrt(jnp.mean(x * x, axis=-1, keepdims=True) + NORM_EPS) * g


def _rope_cols(x, tab_ref, half):
    c, sa, sb = tab_ref[0], tab_ref[1], tab_ref[2]
    return x * c + pltpu.roll(x, LANES - half, axis=1) * sa + pltpu.roll(x, half, axis=1) * sb


def _mod_body(c_ref, w_ref, b_ref, o_ref):
    c = c_ref[...]
    a = c * jax.nn.sigmoid(c)
    o_ref[...] = _dot3(a, w_ref[...]) + b_ref[...]


def _modulation(cvec, w_mod, b_mod):
    m = cvec.shape[0]
    n = w_mod.shape[1]
    tn = 512
    return pl.pallas_call(
        _mod_body,
        out_shape=jax.ShapeDtypeStruct((m, n), F32),
        grid=(n // tn,),
        in_specs=[pl.BlockSpec((m, D_MODEL), lambda j: (0, 0)),
                  pl.BlockSpec((D_MODEL, tn), lambda j: (0, j)),
                  pl.BlockSpec((1, tn), lambda j: (0, j))],
        out_specs=pl.BlockSpec((m, tn), lambda j: (0, j)),
        compiler_params=_cparams(),
        name="modulation",
    )(cvec, w_mod, b_mod.reshape(1, n))


def _inproj_body(x_ref, mod_ref, g_ref, w_ref, rope_ref,
                 cq_ref, ckv_ref, krt_ref, sq_ref, sk_ref, sv_ref, dqkv_ref, dz_ref, dg_ref):
    x = x_ref[...]
    h = (_rms(x, g_ref[...]) * (1.0 + mod_ref[0, 0, 1:2, :]) + mod_ref[0, 0, 0:1, :]).astype(BF16)

    def proj(cols):
        return _dot(h, w_ref[:, cols[0]:cols[1]])

    cq_ref[...] = proj(_C_CQ)
    ckv_ref[...] = proj(_C_CKV)
    krt_ref[...] = proj(_C_KRT)
    dqkv_ref[...] = proj(_C_DQKV)
    dz_ref[...] = proj(_C_DZ)
    dg_ref[...] = proj(_C_DG)
    sv_ref[...] = proj(_C_SV).astype(BF16)
    sk_ref[...] = _rope_cols(proj(_C_SK), rope_ref, SWA_HEAD_DIM // 2).astype(BF16)
    sq = proj(_C_SQ)
    lane = lax.broadcasted_iota(jnp.int32, (TM, LANES), 1)
    lo = lane < SWA_HEAD_DIM
    for c in range(4):
        xr = _rope_cols(sq[:, c * LANES:(c + 1) * LANES], rope_ref, SWA_HEAD_DIM // 2)
        xs = pltpu.roll(xr, SWA_HEAD_DIM, axis=1)
        if c < 2:
            a, b = jnp.where(lo, xr, 0.0), jnp.where(lo, xs, 0.0)
        else:
            a, b = jnp.where(lo, 0.0, xs), jnp.where(lo, 0.0, xr)
        sq_ref[:, (2 * c) * LANES:(2 * c + 1) * LANES] = a.astype(BF16)
        sq_ref[:, (2 * c + 1) * LANES:(2 * c + 2) * LANES] = b.astype(BF16)


def _inproj(x, modtab, g, w, rope_s, nb_per_batch):
    n = x.shape[0]
    row = lambda i: (i, 0)
    widths = [(384, F32), (256, F32), (128, F32), (1024, BF16), (128, BF16), (128, BF16),
              (1536, F32), (512, F32), (128, F32)]
    return pl.pallas_call(
        _inproj_body,
        out_shape=[jax.ShapeDtypeStruct((n, wd), dt) for wd, dt in widths],
        grid=(n // TM,),
        in_specs=[pl.BlockSpec((TM, D_MODEL), row),
                  pl.BlockSpec((1, 1, 6, D_MODEL),
                               lambda i: (i // nb_per_batch, jnp.minimum(i % nb_per_batch, 1), 0, 0)),
                  pl.BlockSpec((1, D_MODEL), lambda i: (0, 0)),
                  pl.BlockSpec((D_MODEL, _N_PROJ), lambda i: (0, 0)),
                  pl.BlockSpec((3, TM, LANES), lambda i: (0, i % nb_per_batch, 0))],
        out_specs=[pl.BlockSpec((TM, wd), row) for wd, _ in widths],
        compiler_params=_cparams(),
        name="inproj",
    )(x, modtab, g, w, rope_s)


def _mla_prep_body(cq_ref, ckv_ref, krt_ref, qg_ref, kvg_ref, wq_ref, wkv_ref, rope_ref,
                   q_ref, k_ref, v_ref):
    half = MLA_ROPE // 2
    qn = _rms(cq_ref[...], qg_ref[...]).astype(BF16)
    q = _dot(qn, wq_ref[...])
    q_ref[:, 0:512] = q[:, 0:512].astype(BF16)
    for c in range(2):
        lo_, hi_ = 512 + c * LANES, 512 + (c + 1) * LANES
        q_ref[:, lo_:hi_] = _rope_cols(q[:, lo_:hi_], rope_ref, half).astype(BF16)
    kvn = _rms(ckv_ref[...], kvg_ref[...]).astype(BF16)
    kv = _dot(kvn, wkv_ref[...])
    kr = _rope_cols(krt_ref[...], rope_ref, half).astype(BF16)
    for c in range(4):
        k_ref[:, (2 * c) * LANES:(2 * c + 1) * LANES] = kv[:, c * LANES:(c + 1) * LANES].astype(BF16)
        k_ref[:, (2 * c + 1) * LANES:(2 * c + 2) * LANES] = kr
    v_ref[...] = kv[:, 512:1024].astype(BF16)


def _mla_prep(cq, ckv, krt, qg, kvg, wq, wkv, rope_m, nb_per_batch):
    n = cq.shape[0]
    row = lambda i: (i, 0)
    full = lambda i: (0, 0)
    return pl.pallas_call(
        _mla_prep_body,
        out_shape=[jax.ShapeDtypeStruct((n, 768), BF16), jax.ShapeDtypeStruct((n, 1024), BF16),
                   jax.ShapeDtypeStruct((n, 512), BF16)],
        grid=(n // TM,),
        in_specs=[pl.BlockSpec((TM, 384), row), pl.BlockSpec((TM, 256), row), pl.BlockSpec((TM, 128), row),
                  pl.BlockSpec((1, 384), full), pl.BlockSpec((1, 256), full),
                  pl.BlockSpec((384, 768), full), pl.BlockSpec((256, 1024), full),
                  pl.BlockSpec((3, TM, LANES), lambda i: (0, i % nb_per_batch, 0))],
        out_specs=[pl.BlockSpec((TM, 768), row), pl.BlockSpec((TM, 1024), row), pl.BlockSpec((TM, 512), row)],
        compiler_params=_cparams(),
        name="mla_prep",
    )(cq, ckv, krt, qg, kvg, wq, wkv, rope_m)


def _mla_attn_body(q_ref, k_ref, v_ref, o_ref, *, t_ctx):
    scale = (MLA_NOPE + MLA_ROPE) ** -0.5
    lane = lax.broadcasted_iota(jnp.int32, (TM, LANES), 1)
    zero = jnp.zeros((TM, LANES), BF16)

    def attend(nk):
        for c in range(4):
            kc = k_ref[0:nk, c * 2 * LANES:(c + 1) * 2 * LANES]
            vc = v_ref[0:nk, c * LANES:(c + 1) * LANES]
            qn = q_ref[:, c * LANES:(c + 1) * LANES]
            qr = q_ref[:, 512 + (c // 2) * LANES:512 + (c // 2 + 1) * LANES]
            outs = []
            for s in range(2):
                h4 = (2 * c + s) % 4
                qa = jnp.concatenate(
                    [jnp.where((lane >= 64 * s) & (lane < 64 * (s + 1)), qn, zero),
                     jnp.where((lane >= 32 * h4) & (lane < 32 * (h4 + 1)), qr, zero)], axis=1)
                sc = _dot_nt(qa, kc) * scale
                m = jnp.max(sc, axis=-1, keepdims=True)
                e = jnp.exp(sc - m)
                l = jnp.sum(e, axis=-1, keepdims=True)
                outs.append(_dot(e.astype(BF16), vc) / l)
            o_ref[:, c * LANES:(c + 1) * LANES] = jnp.where(lane < 64, outs[0], outs[1]).astype(BF16)

    j = pl.program_id(1)

    @pl.when(j == 0)
    def _():
        attend(t_ctx)

    @pl.when(j > 0)
    def _():
        attend(k_ref.shape[0])


def _mla_attention(q, k, v, n_batch, rows):
    nbp = rows // TM
    return pl.pallas_call(
        functools.partial(_mla_attn_body, t_ctx=TM),
        out_shape=jax.ShapeDtypeStruct((n_batch * rows, BRANCH_W), BF16),
        grid=(n_batch, nbp),
        in_specs=[pl.BlockSpec((TM, 768), lambda b, j: (b * nbp + j, 0)),
                  pl.BlockSpec((rows, 1024), lambda b, j: (b, 0)),
                  pl.BlockSpec((rows, 512), lambda b, j: (b, 0))],
        out_specs=pl.BlockSpec((TM, BRANCH_W), lambda b, j: (b * nbp + j, 0)),
        compiler_params=_cparams(2),
        name="mla_attention",
    )(q, k, v)


SWA_QB = 128


def _swa_body(q_ref, k_ref, v_ref, sink_ref, o_ref, *, t_ctx, seq):
    scale = SWA_HEAD_DIM ** -0.5
    j = pl.program_id(1)
    n_ctx_blk = t_ctx // SWA_QB
    lane = lax.broadcasted_iota(jnp.int32, (SWA_QB, LANES), 1)
    lo = lane < SWA_HEAD_DIM
    win = 3 * SWA_QB

    def finish(parts):
        for c in range(4):
            g = c // 2
            r0 = (2 * (c % 2)) * SWA_QB
            a = parts[g][r0:r0 + SWA_QB]
            b = parts[g][r0 + SWA_QB:r0 + 2 * SWA_QB]
            if g == 0:
                col = jnp.where(lo, a, pltpu.roll(b, SWA_HEAD_DIM, axis=1))
            else:
                col = jnp.where(lo, pltpu.roll(a, SWA_HEAD_DIM, axis=1), b)
            o_ref[:, c * LANES:(c + 1) * LANES] = col.astype(BF16)

    def group_q(g):
        return jnp.concatenate([q_ref[:, (4 * g + i) * LANES:(4 * g + i + 1) * LANES] for i in range(4)], axis=0)

    def group_sink(g):
        return jnp.concatenate(
            [jnp.broadcast_to(sink_ref[4 * g + i:4 * g + i + 1, 0:1], (SWA_QB, 1)) for i in range(4)], axis=0)

    @pl.when(j < n_ctx_blk)
    def _():
        kc, vc = k_ref[0:t_ctx, :], v_ref[0:t_ctx, :]
        parts = []
        for g in range(SWA_KV_HEADS):
            sk = group_sink(g)
            s_c = _dot_nt(group_q(g), kc) * scale
            m = jnp.maximum(jnp.max(s_c, axis=-1, keepdims=True), sk)
            e_c = jnp.exp(s_c - m)
            l = jnp.sum(e_c, axis=-1, keepdims=True) + jnp.exp(sk - m)
            parts.append(_dot(e_c.astype(BF16), vc) / l)
        finish(parts)

    @pl.when(j >= n_ctx_blk)
    def _():
        i = j - n_ctx_blk
        start = jnp.clip((i - 1) * SWA_QB, 0, seq - win)
        rs = pl.multiple_of(t_ctx + start, SWA_QB)
        kc, vc = k_ref[0:t_ctx, :], v_ref[0:t_ctx, :]
        kw, vw = k_ref[pl.ds(rs, win), :], v_ref[pl.ds(rs, win), :]
        qpos = i * SWA_QB + lax.broadcasted_iota(jnp.int32, (SWA_QB, win), 0)
        kpos = start + lax.broadcasted_iota(jnp.int32, (SWA_QB, win), 1)
        band1 = jnp.abs(kpos - qpos) <= SWA_WINDOW
        band = jnp.concatenate([band1] * 4, axis=0)
        parts = []
        for g in range(SWA_KV_HEADS):
            qg = group_q(g)
            sk = group_sink(g)
            s_c = _dot_nt(qg, kc) * scale
            s_w = jnp.where(band, _dot_nt(qg, kw) * scale, -jnp.inf)
            m = jnp.maximum(jnp.maximum(jnp.max(s_c, axis=-1, keepdims=True),
                                        jnp.max(s_w, axis=-1, keepdims=True)), sk)
            e_c = jnp.exp(s_c - m)
            e_w = jnp.exp(s_w - m)
            l = (jnp.sum(e_c, axis=-1, keepdims=True) + jnp.sum(e_w, axis=-1, keepdims=True)
                 + jnp.exp(sk - m))
            parts.append((_dot(e_c.astype(BF16), vc) + _dot(e_w.astype(BF16), vw)) / l)
        finish(parts)


def _swa_attention(q, k, v, sink, n_batch, t_ctx, seq):
    rows = t_ctx + seq
    nqb = rows // SWA_QB
    sink_tab = jnp.broadcast_to(sink.astype(F32)[:, None], (SWA_HEADS, LANES))
    return pl.pallas_call(
        functools.partial(_swa_body, t_ctx=t_ctx, seq=seq),
        out_shape=jax.ShapeDtypeStruct((n_batch * rows, BRANCH_W), BF16),
        grid=(n_batch, nqb),
        in_specs=[pl.BlockSpec((SWA_QB, 1024), lambda b, j: (b * nqb + j, 0)),
                  pl.BlockSpec((rows, LANES), lambda b, j: (b, 0)),
                  pl.BlockSpec((rows, LANES), lambda b, j: (b, 0)),
                  pl.BlockSpec((SWA_HEADS, LANES), lambda b, j: (0, 0))],
        out_specs=pl.BlockSpec((SWA_QB, BRANCH_W), lambda b, j: (b * nqb + j, 0)),
        compiler_params=_cparams(2),
        name="swa_attention",
    )(q, k, v, sink_tab)


def _merge_body(x_ref, mod_ref, g1_ref, g2_ref, wg_ref, oa_ref, ob_ref, oc_ref, wb_ref, wo_ref,
                rw_ref, rb_ref, xo_ref, h2_ref, lg_ref):
    x = x_ref[...]
    h = (_rms(x, g1_ref[...]) * (1.0 + mod_ref[0, 0, 1:2, :]) + mod_ref[0, 0, 0:1, :]).astype(BF16)
    y = None
    for i, o_ref in enumerate((oa_ref, ob_ref, oc_ref)):
        gate = jax.nn.sigmoid(_dot(h, wg_ref[:, i * D_MODEL:(i + 1) * D_MODEL]))
        yi = gate * _dot(o_ref[...], wb_ref[i])
        y = yi if y is None else y + yi
    xn = x + mod_ref[0, 0, 2:3, :] * _dot(y.astype(BF16), wo_ref[...])
    xo_ref[...] = xn
    h2 = _rms(xn, g2_ref[...]) * (1.0 + mod_ref[0, 0, 4:5, :]) + mod_ref[0, 0, 3:4, :]
    h2_ref[...] = h2.astype(BF16)
    lg_ref[...] = _dot3_nt(rw_ref[...], h2) + rb_ref[...]


def _merge(x, modtab, g1, g2, wg, oa, ob, oc, wb, wo, rwt, rb, nb_per_batch):
    n = x.shape[0]
    row = lambda i: (i, 0)
    full2 = lambda i: (0, 0)
    return pl.pallas_call(
        _merge_body,
        out_shape=[jax.ShapeDtypeStruct((n, D_MODEL), F32), jax.ShapeDtypeStruct((n, D_MODEL), BF16),
                   jax.ShapeDtypeStruct((N_EXPERTS, n), F32)],
        grid=(n // TM,),
        in_specs=[pl.BlockSpec((TM, D_MODEL), row),
                  pl.BlockSpec((1, 1, 6, D_MODEL),
                               lambda i: (i // nb_per_batch, jnp.minimum(i % nb_per_batch, 1), 0, 0)),
                  pl.BlockSpec((1, D_MODEL), full2), pl.BlockSpec((1, D_MODEL), full2),
                  pl.BlockSpec((D_MODEL, N_BRANCH * D_MODEL), full2),
                  pl.BlockSpec((TM, BRANCH_W), row), pl.BlockSpec((TM, BRANCH_W), row),
                  pl.BlockSpec((TM, BRANCH_W), row),
                  pl.BlockSpec((N_BRANCH, BRANCH_W, D_MODEL), lambda i: (0, 0, 0)),
                  pl.BlockSpec((D_MODEL, D_MODEL), full2),
                  pl.BlockSpec((N_EXPERTS, D_MODEL), full2), pl.BlockSpec((N_EXPERTS, 1), full2)],
        out_specs=[pl.BlockSpec((TM, D_MODEL), row), pl.BlockSpec((TM, D_MODEL), row),
                   pl.BlockSpec((N_EXPERTS, TM), lambda i: (0, i))],
        compiler_params=_cparams(),
        name="merge",
    )(x, modtab, g1, g2, wg, oa, ob, oc, wb, wo, rwt, rb)


ROUTE_TB = 1024


def _router_body(lg_ref, tri_ref, e_ref, gate_ref, pos_ref, cnt_ref, run_ref):
    @pl.when(pl.program_id(0) == 0)
    def _():
        run_ref[...] = jnp.zeros_like(run_ref)

    lg = lg_ref[...]
    eid = lax.broadcasted_iota(jnp.int32, lg.shape, 0)
    work = lg
    vals, idxs = [], []
    sel = jnp.zeros(lg.shape, F32)
    for _ in range(TOP_K):
        m = jnp.max(work, axis=0, keepdims=True)
        idx = jnp.min(jnp.where(work == m, eid, N_EXPERTS), axis=0, keepdims=True)
        hit = eid == idx
        sel = jnp.where(hit, 1.0, sel)
        work = jnp.where(hit, -jnp.inf, work)
        vals.append(m)
        idxs.append(idx)
    ex = [jnp.exp(v - vals[0]) for v in vals]
    den = ex[0] + ex[1] + ex[2] + ex[3]
    before = _dot(sel.astype(BF16), tri_ref[...]) + run_ref[:, 0:1]
    for k in range(TOP_K):
        e_ref[k:k + 1, :] = idxs[k]
        gate_ref[k:k + 1, :] = ex[k] / den
        pos_ref[k:k + 1, :] = jnp.sum(jnp.where(eid == idxs[k], before, 0.0), axis=0,
                                      keepdims=True).astype(jnp.int32)
    run_ref[...] = run_ref[...] + jnp.sum(sel, axis=1, keepdims=True)
    cnt_ref[...] = run_ref[...]


def _router(logits_t):
    n = logits_t.shape[1]
    tri = (jnp.arange(ROUTE_TB)[:, None] < jnp.arange(ROUTE_TB)[None, :]).astype(BF16)
    blk = lambda i: (0, i)
    return pl.pallas_call(
        _router_body,
        out_shape=[jax.ShapeDtypeStruct((TOP_K, n), jnp.int32), jax.ShapeDtypeStruct((TOP_K, n), F32),
                   jax.ShapeDtypeStruct((TOP_K, n), jnp.int32), jax.ShapeDtypeStruct((N_EXPERTS, LANES), F32)],
        grid=(n // ROUTE_TB,),
        in_specs=[pl.BlockSpec((N_EXPERTS, ROUTE_TB), blk), pl.BlockSpec((ROUTE_TB, ROUTE_TB), lambda i: (0, 0))],
        out_specs=[pl.BlockSpec((TOP_K, ROUTE_TB), blk), pl.BlockSpec((TOP_K, ROUTE_TB), blk),
                   pl.BlockSpec((TOP_K, ROUTE_TB), blk), pl.BlockSpec((N_EXPERTS, LANES), lambda i: (0, 0))],
        scratch_shapes=[pltpu.VMEM((N_EXPERTS, LANES), F32)],
        compiler_params=_cparams(),
        name="router",
    )(logits_t, tri)


def _expert_body(be_ref, nv_ref, x_ref, wgu_ref, bgu_ref, wdn_ref, bdn_ref, o_ref):
    @pl.when(pl.program_id(0) < nv_ref[0])
    def _():
        gu = _dot(x_ref[...], wgu_ref[0]) + bgu_ref[0]
        g_ = jnp.minimum(gu[:, :D_EXPERT], SWIGLU_LIMIT)
        u_ = jnp.clip(gu[:, D_EXPERT:], -SWIGLU_LIMIT, SWIGLU_LIMIT)
        act = (u_ + 1.0) * (g_ * jax.nn.sigmoid(SWIGLU_ALPHA * g_))
        o_ref[...] = _dot(act.astype(BF16), wdn_ref[0]) + bdn_ref[0]

    @pl.when(pl.program_id(0) >= nv_ref[0])
    def _():
        o_ref[...] = jnp.zeros_like(o_ref)


def _experts(block_e, n_valid, xs, wgu, bgu, wdn, bdn):
    p = xs.shape[0]
    n_blocks = p // MOE_BLK
    return pl.pallas_call(
        _expert_body,
        out_shape=jax.ShapeDtypeStruct((p, D_MODEL), F32),
        grid_spec=pltpu.PrefetchScalarGridSpec(
            num_scalar_prefetch=2,
            grid=(n_blocks,),
            in_specs=[pl.BlockSpec((MOE_BLK, D_MODEL), lambda i, be, nv: (i, 0)),
                      pl.BlockSpec((1, D_MODEL, 2 * D_EXPERT), lambda i, be, nv: (be[i], 0, 0)),
                      pl.BlockSpec((1, 1, 2 * D_EXPERT), lambda i, be, nv: (be[i], 0, 0)),
                      pl.BlockSpec((1, D_EXPERT, D_MODEL), lambda i, be, nv: (be[i], 0, 0)),
                      pl.BlockSpec((1, 1, D_MODEL), lambda i, be, nv: (be[i], 0, 0))],
            out_specs=pl.BlockSpec((MOE_BLK, D_MODEL), lambda i, be, nv: (i, 0))),
        compiler_params=_cparams(),
        name="experts",
    )(block_e, n_valid, xs, wgu, bgu, wdn, bdn)


def _moe(h2, logits_t, wgu, bgu, wdn, bdn):
    n = h2.shape[0]
    a = n * TOP_K
    top_e, gate, pos, cnt = _router(logits_t)
    counts = cnt[:, 0].astype(jnp.int32)
    padded = (counts + MOE_BLK - 1) // MOE_BLK * MOE_BLK
    pad_end = jnp.cumsum(padded)
    pad_start = pad_end - padded
    n_blocks = -(-a // MOE_BLK) + N_EXPERTS
    p = n_blocks * MOE_BLK
    slot = pad_start[top_e] + pos
    tok = jnp.broadcast_to(jnp.arange(n, dtype=jnp.int32)[None, :], (TOP_K, n))
    slot_tok = jnp.zeros((p,), jnp.int32).at[slot.reshape(-1)].set(tok.reshape(-1))
    block_e = jnp.minimum(
        jnp.searchsorted(pad_end, jnp.arange(n_blocks, dtype=jnp.int32) * MOE_BLK, side='right'),
        N_EXPERTS - 1).astype(jnp.int32)
    n_valid = (pad_end[-1] // MOE_BLK).astype(jnp.int32).reshape(1)
    xs = jnp.take(h2, slot_tok, axis=0)
    yb = _experts(block_e, n_valid, xs, wgu, bgu, wdn, bdn)
    y = None
    for k in range(TOP_K):
        yk = jnp.take(yb, slot[k], axis=0) * gate[k][:, None]
        y = yk if y is None else y + yk
    return y


DN_GW = 256
HALO = 8


def _split3(a):
    hi = a.astype(BF16)
    r = a - hi.astype(F32)
    lo = r.astype(BF16)
    lo2 = (r - lo.astype(F32)).astype(BF16)
    return hi, lo, lo2


def _bd(x):
    xb = x.astype(BF16)
    t = jnp.concatenate([xb, xb, xb, xb], axis=0)
    r = lax.broadcasted_iota(jnp.int32, (DN_GW, DN_GW), 0) // DN_HEAD_DIM
    c = lax.broadcasted_iota(jnp.int32, (DN_GW, DN_GW), 1) // DN_HEAD_DIM
    return jnp.where(r == c, t, jnp.zeros_like(t))


def _mm(a, wbd):
    return _dot(a.astype(BF16), wbd)


def _tri_inverse(l_mat, eye, blk16, blk32):
    ld = jnp.where(blk16, l_mat, 0.0)
    lo32 = jnp.where(blk32 & jnp.logical_not(blk16), l_mat, 0.0)
    lo64 = jnp.where(blk32, 0.0, l_mat)
    p2 = _mm(ld, _bd(ld))
    x = eye - ld
    r = _mm(jnp.concatenate([p2, x], axis=0), _bd(p2))
    p4, x = r[0:64], x + r[64:128]
    r = _mm(jnp.concatenate([p4, x], axis=0), _bd(p4))
    p8, x = r[0:64], x + r[64:128]
    x = x + _mm(x, _bd(p8))
    x = x - _mm(_mm(x, _bd(lo32)), _bd(x))
    x = x - _mm(_mm(x, _bd(lo64)), _bd(x))
    return x


def _dn_prep_body(cur_ref, prev_ref, next_ref, cw_ref, dg_ref, gc_ref, g512_ref, e_ref, tri_ref,
                  u_ref, w_ref, kd_ref, qd_ref, aq_ref, gl_ref,
                  ext_ref, q_s, k_s, v_s, gx_s, bx_s):
    j = pl.program_id(1)
    nbp = pl.num_programs(1)
    C = DN_CHUNK
    ext_ref[HALO:HALO + TM, :] = cur_ref[...]
    ext_ref[0:HALO, :] = jnp.where(j >= 2, prev_ref[...], 0.0)
    ext_ref[HALO + TM:2 * HALO + TM, :] = jnp.where((j >= 1) & (j < nbp - 1), next_ref[...], 0.0)
    y = None
    for t in range(DN_CONV):
        o = HALO - DN_CONV // 2 + t
        term = cw_ref[t:t + 1, :] * ext_ref[o:o + TM, :]
        y = term if y is None else y + term
    y = y * jax.nn.sigmoid(y)

    g512 = g512_ref[...]

    def head_sumsq(x):
        hi, lo = _split_bf16(x * x)
        return _dot(hi, g512) + _dot(lo, g512)

    q = y[:, 0:512]
    k = y[:, 512:1024]
    q_s[...] = q * lax.rsqrt(head_sumsq(q) + NORM_EPS) * (DN_HEAD_DIM ** -0.5)
    k_s[...] = k * lax.rsqrt(head_sumsq(k) + NORM_EPS)
    v_s[...] = y[:, 1024:1536]

    dg = dg_ref[...]
    beta_all = jax.nn.sigmoid(dg)
    z = dg + gc_ref[1:2, :]
    g_all = gc_ref[0:1, :] * (jnp.maximum(z, 0.0) + jnp.log(1.0 + jnp.exp(-jnp.abs(z))))
    bh, bl = _split_bf16(beta_all)
    bcat = jnp.concatenate([bh, bl], axis=0)
    gparts = jnp.concatenate(_split3(g_all), axis=1)
    for d in range(2):
        cs = _dot(tri_ref[d], gparts)
        gcum = cs[:, 0:128] + cs[:, 128:256] + cs[:, 256:384]
        gcat = jnp.concatenate(_split3(gcum), axis=0)
        for grp in range(2):
            eg = _dot(gcat, e_ref[4 + 2 * d + grp])
            gx_s[d, grp] = eg[0:TM] + eg[TM:2 * TM] + eg[2 * TM:3 * TM]
            eb = _dot(bcat, e_ref[2 * d + grp])
            bx_s[d, grp] = eb[0:TM] + eb[TM:2 * TM]

    row = lax.broadcasted_iota(jnp.int32, (C, DN_GW), 0)
    col = lax.broadcasted_iota(jnp.int32, (C, DN_GW), 1) % C
    eye_b = row == col
    eye = jnp.where(eye_b, 1.0, 0.0)
    blk16 = (row // 16) == (col // 16)
    blk32 = (row // 32) == (col // 32)
    incl = (col <= row, col >= row)
    strict = (col < row, col > row)

    def chunk(cc, carry):
        r0 = pl.multiple_of(cc * C, C)
        rows = pl.ds(r0, C)
        for grp in range(2):
            cols = slice(grp * DN_GW, (grp + 1) * DN_GW)
            kg, qg, vg = k_s[rows, cols], q_s[rows, cols], v_s[rows, cols]
            kb = [kg * bx_s[d, grp, rows, :] for d in range(2)]
            raw = _dot_nt(jnp.concatenate([kb[0], kb[1], qg], axis=0).astype(BF16), _bd(kg))
            for d in range(2):
                gx = gx_s[d, grp, rows, :]
                bx = bx_s[d, grp, rows, :]
                rvec = jnp.sum(jnp.where(eye_b, gx, 0.0), axis=0, keepdims=True)
                dec = jnp.where(incl[d], jnp.exp(jnp.where(incl[d], gx - rvec, 0.0)), 0.0)
                l_mat = jnp.where(strict[d], raw[C * d:C * (d + 1)] * dec, 0.0)
                tinv = _tri_inverse(l_mat, eye, blk16, blk32)
                eg = jnp.exp(gx)
                glast = gx[C - 1:C, :] if d == 0 else gx[0:1, :]
                u_ref[d, rows, cols] = _mm(tinv, _bd(vg * bx))
                w_ref[d, rows, cols] = _mm(tinv, _bd(kb[d] * eg)).astype(BF16)
                kd_ref[d, rows, cols] = (kg * jnp.exp(glast - gx)).astype(BF16)
                qd_ref[d, rows, cols] = (qg * eg).astype(BF16)
                aq_ref[d, rows, cols] = (raw[2 * C:3 * C] * dec).astype(BF16)
                gl_ref[d, pl.ds(cc, 1), :, cols] = jnp.exp(glast).reshape(1, 1, DN_GW)
        return carry

    lax.fori_loop(0, TM // C, chunk, 0)


def _dn_prep(dqkv, dg, conv_w, a_log, dt_bias, n_batch, rows):
    n = dqkv.shape[0]
    nbp = rows // TM
    hb = TM // HALO
    H = DN_HEADS
    cw = jnp.zeros((8, 1536), F32).at[0:DN_CONV].set(conv_w)
    gc = jnp.zeros((8, LANES), F32)
    gc = gc.at[0, 2 * H:4 * H].set(-jnp.exp(a_log.reshape(-1))).at[1, 2 * H:4 * H].set(dt_bias.reshape(-1))
    hid = np.arange(512) // DN_HEAD_DIM
    g512 = jnp.asarray(hid[:, None] == hid[None, :], BF16)
    e = np.zeros((8, LANES, DN_GW), np.float32)
    for kind in range(2):
        for d in range(2):
            for grp in range(2):
                for h in range(4):
                    e[4 * kind + 2 * d + grp, 16 * kind + 8 * d + 4 * grp + h, 64 * h:64 * (h + 1)] = 1.0
    e = jnp.asarray(e, BF16)
    t = np.arange(TM)
    same = (t[:, None] // DN_CHUNK) == (t[None, :] // DN_CHUNK)
    tri = jnp.asarray(np.stack([same & (t[None, :] <= t[:, None]), same & (t[None, :] >= t[:, None])]), BF16)
    blk = lambda b, j: (b * nbp + j, 0)
    full2 = lambda b, j: (0, 0)
    full3 = lambda b, j: (0, 0, 0)
    dblk = lambda b, j: (0, b * nbp + j, 0)
    nlast = n // HALO - 1
    outs = pl.pallas_call(
        _dn_prep_body,
        out_shape=[jax.ShapeDtypeStruct((2, n, 512), F32)] + [jax.ShapeDtypeStruct((2, n, 512), BF16)] * 4
        + [jax.ShapeDtypeStruct((2, n // DN_CHUNK, 1, 512), F32)],
        grid=(n_batch, nbp),
        in_specs=[pl.BlockSpec((TM, 1536), blk),
                  pl.BlockSpec((HALO, 1536), lambda b, j: (jnp.maximum((b * nbp + j) * hb - 1, 0), 0)),
                  pl.BlockSpec((HALO, 1536), lambda b, j: (jnp.minimum((b * nbp + j + 1) * hb, nlast), 0)),
                  pl.BlockSpec((8, 1536), full2), pl.BlockSpec((TM, LANES), blk), pl.BlockSpec((8, LANES), full2),
                  pl.BlockSpec((512, 512), full2), pl.BlockSpec((8, LANES, DN_GW), full3),
                  pl.BlockSpec((2, TM, TM), full3)],
        out_specs=[pl.BlockSpec((2, TM, 512), dblk)] * 5
        + [pl.BlockSpec((2, TM // DN_CHUNK, 1, 512), lambda b, j: (0, b * nbp + j, 0, 0))],
        scratch_shapes=[pltpu.VMEM((TM + 2 * HALO, 1536), F32), pltpu.VMEM((TM, 512), F32),
                        pltpu.VMEM((TM, 512), F32), pltpu.VMEM((TM, 512), F32),
                        pltpu.VMEM((2, 2, TM, DN_GW), F32), pltpu.VMEM((2, 2, TM, DN_GW), F32)],
        compiler_params=_cparams(2),
        name="dn_prep",
    )(dqkv, dqkv, dqkv, cw, dg, gc, g512, e, tri)
    return outs, g512


def _dn_scan_body(uf, wf, kf, qf, af, gf, ub, wb, kb, qb, ab, gb, of_ref, ob_ref, s_ref):
    @pl.when(pl.program_id(1) == 0)
    def _():
        s_ref[...] = jnp.zeros_like(s_ref)

    r = lax.broadcasted_iota(jnp.int32, (DN_GW, DN_GW), 0) // DN_HEAD_DIM
    c = lax.broadcasted_iota(jnp.int32, (DN_GW, DN_GW), 1) // DN_HEAD_DIM
    same = r == c
    for d, (u_r, w_r, k_r, q_r, a_r, g_r, o_r) in enumerate(((uf, wf, kf, qf, af, gf, of_ref),
                                                             (ub, wb, kb, qb, ab, gb, ob_ref))):
        for grp in range(2):
            cols = slice(grp * DN_GW, (grp + 1) * DN_GW)
            s = s_ref[d, grp]
            wq = jnp.concatenate([w_r[0, :, cols], q_r[0, :, cols]], axis=0)
            sq = _dot(wq, s.astype(BF16))
            vnew = u_r[0, :, cols] - sq[0:DN_CHUNK]
            vb = vnew.astype(BF16)
            o_r[:, cols] = sq[DN_CHUNK:2 * DN_CHUNK] + _dot(a_r[0, :, cols], _bd(vnew))
            upd = lax.dot_general(k_r[0, :, cols], vb, (((0,), (0,)), ((), ())), preferred_element_type=F32)
            s_ref[d, grp] = s * g_r[0, 0, :, cols] + jnp.where(same, upd, 0.0)


def _dn_scan(u, w, kd, qd, aq, gl, n_batch, rows, t_ctx):
    n = u.shape[1]
    nch = rows // DN_CHUNK
    nctx = t_ctx // DN_CHUNK

    def cb(c):
        return jnp.where(c < nctx, nctx - 1 - c, nch - 1 - (c - nctx))

    fwd = lambda b, c: (0, b * nch + c, 0)
    bwd = lambda b, c: (1, b * nch + cb(c), 0)
    fwd4 = lambda b, c: (0, b * nch + c, 0, 0)
    bwd4 = lambda b, c: (1, b * nch + cb(c), 0, 0)
    blk = (1, DN_CHUNK, 512)
    specs = []
    for im, im4 in ((fwd, fwd4), (bwd, bwd4)):
        specs += [pl.BlockSpec(blk, im)] * 5 + [pl.BlockSpec((1, 1, 1, 512), im4)]
    return pl.pallas_call(
        _dn_scan_body,
        out_shape=[jax.ShapeDtypeStruct((n, 512), F32)] * 2,
        grid=(n_batch, nch),
        in_specs=specs,
        out_specs=[pl.BlockSpec((DN_CHUNK, 512), lambda b, c: (b * nch + c, 0)),
                   pl.BlockSpec((DN_CHUNK, 512), lambda b, c: (b * nch + cb(c), 0))],
        scratch_shapes=[pltpu.VMEM((2, 2, DN_GW, DN_GW), F32)],
        compiler_params=_cparams(2),
        name="dn_scan",
    )(u, w, kd, qd, aq, gl, u, w, kd, qd, aq, gl)


def _dn_out_body(of_ref, ob_ref, z_ref, g_ref, g512_ref, o_ref):
    o = of_ref[...] + ob_ref[...]
    hi, lo = _split_bf16(o * o)
    ms = (_dot(hi, g512_ref[...]) + _dot(lo, g512_ref[...])) * (1.0 / DN_HEAD_DIM)
    z = z_ref[...]
    o_ref[...] = (o * lax.rsqrt(ms + NORM_EPS) * g_ref[...] * (z * jax.nn.sigmoid(z))).astype(BF16)


def _dn_out(o_f, o_b, dz, norm_g, g512):
    n = o_f.shape[0]
    row = lambda i: (i, 0)
    g = jnp.tile(norm_g, DN_HEADS).reshape(1, 512)
    return pl.pallas_call(
        _dn_out_body,
        out_shape=jax.ShapeDtypeStruct((n, BRANCH_W), BF16),
        grid=(n // TM,),
        in_specs=[pl.BlockSpec((TM, 512), row)] * 3 + [pl.BlockSpec((1, 512), lambda i: (0, 0)),
                                                       pl.BlockSpec((512, 512), lambda i: (0, 0))],
        out_specs=pl.BlockSpec((TM, BRANCH_W), row),
        compiler_params=_cparams(),
        name="dn_out",
    )(o_f, o_b, dz, g, g512)


def _deltanet(dqkv, dz, dg, conv_w, a_log, dt_bias, norm_g, n_batch, rows, t_ctx):
    (u, w, kd, qd, aq, gl), g512 = _dn_prep(dqkv, dg, conv_w, a_log, dt_bias, n_batch, rows)
    o_f, o_b = _dn_scan(u, w, kd, qd, aq, gl, n_batch, rows, t_ctx)
    return _dn_out(o_f, o_b, dz, norm_g, g512)


def _final_norm_body(x_ref, g_ref, o_ref):
    o_ref[...] = _rms(x_ref[...], g_ref[...])


def _final_norm(x, g):
    n = x.shape[0]
    return pl.pallas_call(
        _final_norm_body,
        out_shape=jax.ShapeDtypeStruct(x.shape, F32),
        grid=(n // TM,),
        in_specs=[pl.BlockSpec((TM, D_MODEL), lambda i: (i, 0)), pl.BlockSpec((1, D_MODEL), lambda i: (0, 0))],
        out_specs=pl.BlockSpec((TM, D_MODEL), lambda i: (i, 0)),
        compiler_params=_cparams(),
        name="final_norm",
    )(x, g)


def _rope_tables(seq, t_ctx, rot_dim):
    rows = seq // GRID_W
    row = jnp.broadcast_to(jnp.arange(rows)[:, None], (rows, GRID_W)).reshape(-1).astype(F32)
    col = jnp.broadcast_to(jnp.arange(GRID_W)[None, :], (rows, GRID_W)).reshape(-1).astype(F32)
    n_freq = rot_dim // 4
    inv_freq = ROPE_THETA ** (-jnp.arange(n_freq, dtype=F32) / n_freq)
    ang = jnp.concatenate([row[:, None] * inv_freq, col[:, None] * inv_freq], axis=-1)
    cos, sin = jnp.cos(ang), jnp.sin(ang)
    zero = jnp.zeros_like(sin)
    c = jnp.concatenate([cos, cos], axis=-1)
    sa = jnp.concatenate([-sin, zero], axis=-1)
    sb = jnp.concatenate([zero, sin], axis=-1)
    ident = jnp.stack([jnp.ones((t_ctx, rot_dim), F32), jnp.zeros((t_ctx, rot_dim), F32),
                       jnp.zeros((t_ctx, rot_dim), F32)])
    tab = jnp.concatenate([ident, jnp.stack([c, sa, sb])], axis=1)
    return jnp.tile(tab, (1, 1, LANES // rot_dim))


def _layer_weights(w_in, w_q_up, w_kv_up, w_branch, w_out, router_w, router_b, w_gu, b_gu, w_dn, b_dn):
    sizes = (MLA_Q_RANK, MLA_KV_RANK, MLA_ROPE, 512, 128, 128, 1536, 512, 16, 16, N_BRANCH * D_MODEL)
    offs = np.cumsum((0,) + sizes)
    part = lambda i: w_in[:, offs[i]:offs[i + 1]]
    pad = jnp.zeros((D_MODEL, LANES - 32), F32)
    w_proj = jnp.concatenate(
        [part(0), part(1), part(3), part(4), part(5), part(6), part(7),
         jnp.tile(part(2), (1, LANES // MLA_ROPE)), part(8), part(9), pad], axis=1).astype(BF16)
    w_gate = part(10).astype(BF16)
    hq = MLA_NOPE + MLA_ROPE
    wq = w_q_up.reshape(MLA_Q_RANK, MLA_HEADS, hq)
    wq = jnp.concatenate([wq[:, :, :MLA_NOPE].reshape(MLA_Q_RANK, -1),
                          wq[:, :, MLA_NOPE:].reshape(MLA_Q_RANK, -1)], axis=1).astype(BF16)
    wkv = w_kv_up.reshape(MLA_KV_RANK, MLA_HEADS, MLA_NOPE + MLA_V)
    wkv = jnp.concatenate([wkv[:, :, :MLA_NOPE].reshape(MLA_KV_RANK, -1),
                           wkv[:, :, MLA_NOPE:].reshape(MLA_KV_RANK, -1)], axis=1).astype(BF16)
    wgu = jnp.concatenate([w_gu[:, :, 0::2], w_gu[:, :, 1::2]], axis=-1).astype(BF16)
    bgu = jnp.concatenate([b_gu[:, 0::2], b_gu[:, 1::2]], axis=-1).reshape(N_EXPERTS, 1, 2 * D_EXPERT)
    return dict(w_proj=w_proj, w_gate=w_gate, wq=wq, wkv=wkv, wb=w_branch.astype(BF16),
                wo=w_out.astype(BF16), rwt=router_w.T, rb=router_b.reshape(N_EXPERTS, 1),
                wgu=wgu, bgu=bgu, wdn=w_dn.astype(BF16), bdn=b_dn.reshape(N_EXPERTS, 1, D_MODEL))


def kernel(x, c, ctx, c_ctx, w_mod, b_mod, norm1_g, norm2_g, w_in, mla_q_norm_g, mla_w_q_up,
           mla_kv_norm_g, mla_w_kv_up, swa_sink, dn_conv_w, dn_a_log, dn_dt_bias, dn_norm_g,
           w_branch, w_out, router_w, router_b, exp_w_gu, exp_b_gu, exp_w_dn, exp_b_dn, final_norm_g):
    B, S, D = x.shape
    T = ctx.shape[1]
    R = T + S
    depth = w_mod.shape[0]
    assert D == D_MODEL and T == TM and S % TM == 0 and (B * R) % ROUTE_TB == 0
    nbp = R // TM
    xa = jnp.concatenate([ctx, x], axis=1).reshape(B * R, D)
    rope_s = _rope_tables(S, T, SWA_HEAD_DIM)
    rope_m = _rope_tables(S, T, MLA_ROPE)
    cvec = jnp.concatenate([c, c_ctx[None, :]], axis=0)
    for l in range(depth):
        last = l == depth - 1
        w = _layer_weights(w_in[l], mla_w_q_up[l], mla_w_kv_up[l], w_branch[l], w_out[l], router_w[l],
                           router_b[l], exp_w_gu[l], exp_b_gu[l], exp_w_dn[l], exp_b_dn[l])
        mod = _modulation(cvec, w_mod[l], b_mod[l]).reshape(B + 1, 6, D)
        modtab = jnp.stack([jnp.broadcast_to(mod[B][None], (B, 6, D)), mod[:B]], axis=1)
        g1 = norm1_g[l].reshape(1, D)
        g2 = norm2_g[l].reshape(1, D)
        cq, ckv, krt, sq, sk, sv, dqkv, dz, dg = _inproj(xa, modtab, g1, w["w_proj"], rope_s, nbp)
        q_m, k_m, v_m = _mla_prep(cq, ckv, krt, mla_q_norm_g[l].reshape(1, -1), mla_kv_norm_g[l].reshape(1, -1),
                                  w["wq"], w["wkv"], rope_m, nbp)
        o_a = _mla_attention(q_m, k_m, v_m, B, R)
        o_b = _swa_attention(sq, sk, sv, swa_sink[l], B, T, S)
        o_c = _deltanet(dqkv, dz, dg, dn_conv_w[l], dn_a_log[l], dn_dt_bias[l], dn_norm_g[l], B, R, T)
        xn, h2, logits_t = _merge(xa, modtab, g1, g2, w["w_gate"], o_a, o_b, o_c, w["wb"], w["wo"],
                                  w["rwt"], w["rb"], nbp)
        y = _moe(h2, logits_t, w["wgu"], w["bgu"], w["wdn"], w["bdn"])
        gate2 = jnp.broadcast_to(modtab[:, :, 5][:, :, None, :], (B, 2, 1, D))
        g2rows = jnp.concatenate([jnp.broadcast_to(gate2[:, 0], (B, T, D)),
                                  jnp.broadcast_to(gate2[:, 1], (B, S, D))], axis=1).reshape(B * R, D)
        xa = xn + g2rows * y
    out = _final_norm(xa, final_norm_g.reshape(1, D)).reshape(B, R, D)
    return out[:, T:]
```

```python
import functools

import jax
import jax.numpy as jnp
import numpy as np
from jax import lax
from jax.experimental import pallas as pl
from jax.experimental.pallas import tpu as pltpu

F32 = jnp.float32
BF16 = jnp.bfloat16

D_MODEL = 1024
GRID_W = 64
ROPE_THETA = 10000.0
NORM_EPS = 1e-6
MLA_HEADS = 8
MLA_Q_RANK = 384
MLA_KV_RANK = 256
MLA_NOPE = 64
MLA_ROPE = 32
MLA_V = 64
SWA_HEADS = 8
SWA_KV_HEADS = 2
SWA_HEAD_DIM = 64
SWA_WINDOW = 128
DN_HEADS = 8
DN_HEAD_DIM = 64
DN_CONV = 5
DN_CHUNK = 64
N_BRANCH = 3
BRANCH_W = 512
N_EXPERTS = 32
TOP_K = 4
D_EXPERT = 1024
SWIGLU_LIMIT = 7.0
SWIGLU_ALPHA = 1.702
MOE_BLK = 256

LANES = 128
TM = 256
VMEM_LIMIT = 56 * 1024 * 1024

_C_CQ = (0, 384)
_C_CKV = (384, 640)
_C_SQ = (640, 1152)
_C_SK = (1152, 1280)
_C_SV = (1280, 1408)
_C_DQKV = (1408, 2944)
_C_DZ = (2944, 3456)
_C_KRT = (3456, 3584)
_C_DG = (3584, 3712)
_N_PROJ = 3712


def _cparams(n_axes=1):
    return pltpu.CompilerParams(dimension_semantics=("arbitrary",) * n_axes, vmem_limit_bytes=VMEM_LIMIT)


def _dot(a, b):
    return jnp.dot(a, b, preferred_element_type=F32)


def _dot_nt(a, b):
    return lax.dot_general(a, b, (((1,), (1,)), ((), ())), preferred_element_type=F32)


def _split_bf16(a):
    hi = a.astype(BF16)
    lo = (a - hi.astype(F32)).astype(BF16)
    return hi, lo


def _dot3(a, b):
    ah, al = _split_bf16(a)
    bh, bl = _split_bf16(b)
    return _dot(ah, bh) + (_dot(ah, bl) + _dot(al, bh))


def _dot3_nt(a, b):
    ah, al = _split_bf16(a)
    bh, bl = _split_bf16(b)
    return _dot_nt(ah, bh) + (_dot_nt(ah, bl) + _dot_nt(al, bh))


def _rms(x, g):
    return x * lax.rsqrt(jnp.mean(x * x, axis=-1, keepdims=True) + NORM_EPS) * g


def _rope_cols(x, tab_ref, half):
    c, sa, sb = tab_ref[0], tab_ref[1], tab_ref[2]
    return x * c + pltpu.roll(x, LANES - half, axis=1) * sa + pltpu.roll(x, half, axis=1) * sb


def _mod_body(c_ref, w_ref, b_ref, o_ref):
    c = c_ref[...]
    a = c * jax.nn.sigmoid(c)
    o_ref[...] = _dot3(a, w_ref[...]) + b_ref[...]


def _modulation(cvec, w_mod, b_mod):
    m = cvec.shape[0]
    n = w_mod.shape[1]
    tn = 512
    return pl.pallas_call(
        _mod_body,
        out_shape=jax.ShapeDtypeStruct((m, n), F32),
        grid=(n // tn,),
        in_specs=[pl.BlockSpec((m, D_MODEL), lambda j: (0, 0)),
                  pl.BlockSpec((D_MODEL, tn), lambda j: (0, j)),
                  pl.BlockSpec((1, tn), lambda j: (0, j))],
        out_specs=pl.BlockSpec((m, tn), lambda j: (0, j)),
        compiler_params=_cparams(),
        name="modulation",
    )(cvec, w_mod, b_mod.reshape(1, n))


def _inproj_body(x_ref, mod_ref, g_ref, w_ref, rope_ref,
                 cq_ref, ckv_ref, krt_ref, sq_ref, sk_ref, sv_ref, dqkv_ref, dz_ref, dg_ref):
    x = x_ref[...]
    h = (_rms(x, g_ref[...]) * (1.0 + mod_ref[0, 0, 1:2, :]) + mod_ref[0, 0, 0:1, :]).astype(BF16)

    def proj(cols):
        return _dot(h, w_ref[:, cols[0]:cols[1]])

    cq_ref[...] = proj(_C_CQ)
    ckv_ref[...] = proj(_C_CKV)
    krt_ref[...] = proj(_C_KRT)
    dqkv_ref[...] = proj(_C_DQKV)
    dz_ref[...] = proj(_C_DZ)
    dg_ref[...] = proj(_C_DG)
    sv_ref[...] = proj(_C_SV).astype(BF16)
    sk_ref[...] = _rope_cols(proj(_C_SK), rope_ref, SWA_HEAD_DIM // 2).astype(BF16)
    sq = proj(_C_SQ)
    lane = lax.broadcasted_iota(jnp.int32, (TM, LANES), 1)
    lo = lane < SWA_HEAD_DIM
    for c in range(4):
        xr = _rope_cols(sq[:, c * LANES:(c + 1) * LANES], rope_ref, SWA_HEAD_DIM // 2)
        xs = pltpu.roll(xr, SWA_HEAD_DIM, axis=1)
        if c < 2:
            a, b = jnp.where(lo, xr, 0.0), jnp.where(lo, xs, 0.0)
        else:
            a, b = jnp.where(lo, 0.0, xs), jnp.where(lo, 0.0, xr)
        sq_ref[:, (2 * c) * LANES:(2 * c + 1) * LANES] = a.astype(BF16)
        sq_ref[:, (2 * c + 1) * LANES:(2 * c + 2) * LANES] = b.astype(BF16)


def _inproj(x, modtab, g, w, rope_s, nb_per_batch):
    n = x.shape[0]
    row = lambda i: (i, 0)
    widths = [(384, F32), (256, F32), (128, F32), (1024, BF16), (128, BF16), (128, BF16),
              (1536, F32), (512, F32), (128, F32)]
    return pl.pallas_call(
        _inproj_body,
        out_shape=[jax.ShapeDtypeStruct((n, wd), dt) for wd, dt in widths],
        grid=(n // TM,),
        in_specs=[pl.BlockSpec((TM, D_MODEL), row),
                  pl.BlockSpec((1, 1, 6, D_MODEL),
                               lambda i: (i // nb_per_batch, jnp.minimum(i % nb_per_batch, 1), 0, 0)),
                  pl.BlockSpec((1, D_MODEL), lambda i: (0, 0)),
                  pl.BlockSpec((D_MODEL, _N_PROJ), lambda i: (0, 0)),
                  pl.BlockSpec((3, TM, LANES), lambda i: (0, i % nb_per_batch, 0))],
        out_specs=[pl.BlockSpec((TM, wd), row) for wd, _ in widths],
        compiler_params=_cparams(),
        name="inproj",
    )(x, modtab, g, w, rope_s)


def _mla_prep_body(cq_ref, ckv_ref, krt_ref, qg_ref, kvg_ref, wq_ref, wkv_ref, rope_ref,
                   q_ref, k_ref, v_ref):
    half = MLA_ROPE // 2
    qn = _rms(cq_ref[...], qg_ref[...]).astype(BF16)
    q = _dot(qn, wq_ref[...])
    q_ref[:, 0:512] = q[:, 0:512].astype(BF16)
    for c in range(2):
        lo_, hi_ = 512 + c * LANES, 512 + (c + 1) * LANES
        q_ref[:, lo_:hi_] = _rope_cols(q[:, lo_:hi_], rope_ref, half).astype(BF16)
    kvn = _rms(ckv_ref[...], kvg_ref[...]).astype(BF16)
    kv = _dot(kvn, wkv_ref[...])
    kr = _rope_cols(krt_ref[...], rope_ref, half).astype(BF16)
    for c in range(4):
        k_ref[:, (2 * c) * LANES:(2 * c + 1) * LANES] = kv[:, c * LANES:(c + 1) * LANES].astype(BF16)
        k_ref[:, (2 * c + 1) * LANES:(2 * c + 2) * LANES] = kr
    v_ref[...] = kv[:, 512:1024].astype(BF16)


def _mla_prep(cq, ckv, krt, qg, kvg, wq, wkv, rope_m, nb_per_batch):
    n = cq.shape[0]
    row = lambda i: (i, 0)
    full = lambda i: (0, 0)
    return pl.pallas_call(
        _mla_prep_body,
        out_shape=[jax.ShapeDtypeStruct((n, 768), BF16), jax.ShapeDtypeStruct((n, 1024), BF16),
                   jax.ShapeDtypeStruct((n, 512), BF16)],
        grid=(n // TM,),
        in_specs=[pl.BlockSpec((TM, 384), row), pl.BlockSpec((TM, 256), row), pl.BlockSpec((TM, 128), row),
                  pl.BlockSpec((1, 384), full), pl.BlockSpec((1, 256), full),
                  pl.BlockSpec((384, 768), full), pl.BlockSpec((256, 1024), full),
                  pl.BlockSpec((3, TM, LANES), lambda i: (0, i % nb_per_batch, 0))],
        out_specs=[pl.BlockSpec((TM, 768), row), pl.BlockSpec((TM, 1024), row), pl.BlockSpec((TM, 512), row)],
        compiler_params=_cparams(),
        name="mla_prep",
    )(cq, ckv, krt, qg, kvg, wq, wkv, rope_m)


def _mla_attn_body(q_ref, k_ref, v_ref, o_ref, *, t_ctx):
    scale = (MLA_NOPE + MLA_ROPE) ** -0.5
    lane = lax.broadcasted_iota(jnp.int32, (TM, LANES), 1)
    zero = jnp.zeros((TM, LANES), BF16)

    def attend(nk):
        for c in range(4):
            kc = k_ref[0:nk, c * 2 * LANES:(c + 1) * 2 * LANES]
            vc = v_ref[0:nk, c * LANES:(c + 1) * LANES]
            qn = q_ref[:, c * LANES:(c + 1) * LANES]
            qr = q_ref[:, 512 + (c // 2) * LANES:512 + (c // 2 + 1) * LANES]
            outs = []
            for s in range(2):
                h4 = (2 * c + s) % 4
                qa = jnp.concatenate(
                    [jnp.where((lane >= 64 * s) & (lane < 64 * (s + 1)), qn, zero),
                     jnp.where((lane >= 32 * h4) & (lane < 32 * (h4 + 1)), qr, zero)], axis=1)
                sc = _dot_nt(qa, kc) * scale
                m = jnp.max(sc, axis=-1, keepdims=True)
                e = jnp.exp(sc - m)
                l = jnp.sum(e, axis=-1, keepdims=True)
                outs.append(_dot(e.astype(BF16), vc) / l)
            o_ref[:, c * LANES:(c + 1) * LANES] = jnp.where(lane < 64, outs[0], outs[1]).astype(BF16)

    j = pl.program_id(1)

    @pl.when(j == 0)
    def _():
        attend(t_ctx)

    @pl.when(j > 0)
    def _():
        attend(k_ref.shape[0])


def _mla_attention(q, k, v, n_batch, rows):
    nbp = rows // TM
    return pl.pallas_call(
        functools.partial(_mla_attn_body, t_ctx=TM),
        out_shape=jax.ShapeDtypeStruct((n_batch * rows, BRANCH_W), BF16),
        grid=(n_batch, nbp),
        in_specs=[pl.BlockSpec((TM, 768), lambda b, j: (b * nbp + j, 0)),
                  pl.BlockSpec((rows, 1024), lambda b, j: (b, 0)),
                  pl.BlockSpec((rows, 512), lambda b, j: (b, 0))],
        out_specs=pl.BlockSpec((TM, BRANCH_W), lambda b, j: (b * nbp + j, 0)),
        compiler_params=_cparams(2),
        name="mla_attention",
    )(q, k, v)


SWA_QB = 128


def _swa_body(q_ref, k_ref, v_ref, sink_ref, o_ref, *, t_ctx, seq):
    scale = SWA_HEAD_DIM ** -0.5
    j = pl.program_id(1)
    n_ctx_blk = t_ctx // SWA_QB
    lane = lax.broadcasted_iota(jnp.int32, (SWA_QB, LANES), 1)
    lo = lane < SWA_HEAD_DIM
    win = 3 * SWA_QB

    def finish(parts):
        for c in range(4):
            g = c // 2
            r0 = (2 * (c % 2)) * SWA_QB
            a = parts[g][r0:r0 + SWA_QB]
            b = parts[g][r0 + SWA_QB:r0 + 2 * SWA_QB]
            if g == 0:
                col = jnp.where(lo, a, pltpu.roll(b, SWA_HEAD_DIM, axis=1))
            else:
                col = jnp.where(lo, pltpu.roll(a, SWA_HEAD_DIM, axis=1), b)
            o_ref[:, c * LANES:(c + 1) * LANES] = col.astype(BF16)

    def group_q(g):
        return jnp.concatenate([q_ref[:, (4 * g + i) * LANES:(4 * g + i + 1) * LANES] for i in range(4)], axis=0)

    def group_sink(g):
        return jnp.concatenate(
            [jnp.broadcast_to(sink_ref[4 * g + i:4 * g + i + 1, 0:1], (SWA_QB, 1)) for i in range(4)], axis=0)

    @pl.when(j < n_ctx_blk)
    def _():
        kc, vc = k_ref[0:t_ctx, :], v_ref[0:t_ctx, :]
        parts = []
        for g in range(SWA_KV_HEADS):
            sk = group_sink(g)
            s_c = _dot_nt(group_q(g), kc) * scale
            m = jnp.maximum(jnp.max(s_c, axis=-1, keepdims=True), sk)
            e_c = jnp.exp(s_c - m)
            l = jnp.sum(e_c, axis=-1, keepdims=True) + jnp.exp(sk - m)
            parts.append(_dot(e_c.astype(BF16), vc) / l)
        finish(parts)

    @pl.when(j >= n_ctx_blk)
    def _():
        i = j - n_ctx_blk
        start = jnp.clip((i - 1) * SWA_QB, 0, seq - win)
        rs = pl.multiple_of(t_ctx + start, SWA_QB)
        kc, vc = k_ref[0:t_ctx, :], v_ref[0:t_ctx, :]
        kw, vw = k_ref[pl.ds(rs, win), :], v_ref[pl.ds(rs, win), :]
        qpos = i * SWA_QB + lax.broadcasted_iota(jnp.int32, (SWA_QB, win), 0)
        kpos = start + lax.broadcasted_iota(jnp.int32, (SWA_QB, win), 1)
        band1 = jnp.abs(kpos - qpos) <= SWA_WINDOW
        band = jnp.concatenate([band1] * 4, axis=0)
        parts = []
        for g in range(SWA_KV_HEADS):
            qg = group_q(g)
            sk = group_sink(g)
            s_c = _dot_nt(qg, kc) * scale
            s_w = jnp.where(band, _dot_nt(qg, kw) * scale, -jnp.inf)
            m = jnp.maximum(jnp.maximum(jnp.max(s_c, axis=-1, keepdims=True),
                                        jnp.max(s_w, axis=-1, keepdims=True)), sk)
            e_c = jnp.exp(s_c - m)
            e_w = jnp.exp(s_w - m)
            l = (jnp.sum(e_c, axis=-1, keepdims=True) + jnp.sum(e_w, axis=-1, keepdims=True)
                 + jnp.exp(sk - m))
            parts.append((_dot(e_c.astype(BF16), vc) + _dot(e_w.astype(BF16), vw)) / l)
        finish(parts)


def _swa_attention(q, k, v, sink, n_batch, t_ctx, seq):
    rows = t_ctx + seq
    nqb = rows // SWA_QB
    sink_tab = jnp.broadcast_to(sink.astype(F32)[:, None], (SWA_HEADS, LANES))
    return pl.pallas_call(
        functools.partial(_swa_body, t_ctx=t_ctx, seq=seq),
        out_shape=jax.ShapeDtypeStruct((n_batch * rows, BRANCH_W), BF16),
        grid=(n_batch, nqb),
        in_specs=[pl.BlockSpec((SWA_QB, 1024), lambda b, j: (b * nqb + j, 0)),
                  pl.BlockSpec((rows, LANES), lambda b, j: (b, 0)),
                  pl.BlockSpec((rows, LANES), lambda b, j: (b, 0)),
                  pl.BlockSpec((SWA_HEADS, LANES), lambda b, j: (0, 0))],
        out_specs=pl.BlockSpec((SWA_QB, BRANCH_W), lambda b, j: (b * nqb + j, 0)),
        compiler_params=_cparams(2),
        name="swa_attention",
    )(q, k, v, sink_tab)


def _merge_body(x_ref, mod_ref, g1_ref, g2_ref, wg_ref, oa_ref, ob_ref, oc_ref, wb_ref, wo_ref,
                rw_ref, rb_ref, xo_ref, h2_ref, lg_ref):
    x = x_ref[...]
    h = (_rms(x, g1_ref[...]) * (1.0 + mod_ref[0, 0, 1:2, :]) + mod_ref[0, 0, 0:1, :]).astype(BF16)
    y = None
    for i, o_ref in enumerate((oa_ref, ob_ref, oc_ref)):
        gate = jax.nn.sigmoid(_dot(h, wg_ref[:, i * D_MODEL:(i + 1) * D_MODEL]))
        yi = gate * _dot(o_ref[...], wb_ref[i])
        y = yi if y is None else y + yi
    xn = x + mod_ref[0, 0, 2:3, :] * _dot(y.astype(BF16), wo_ref[...])
    xo_ref[...] = xn
    h2 = _rms(xn, g2_ref[...]) * (1.0 + mod_ref[0, 0, 4:5, :]) + mod_ref[0, 0, 3:4, :]
    h2_ref[...] = h2.astype(BF16)
    lg_ref[...] = _dot3_nt(rw_ref[...], h2) + rb_ref[...]


def _merge(x, modtab, g1, g2, wg, oa, ob, oc, wb, wo, rwt, rb, nb_per_batch):
    n = x.shape[0]
    row = lambda i: (i, 0)
    full2 = lambda i: (0, 0)
    return pl.pallas_call(
        _merge_body,
        out_shape=[jax.ShapeDtypeStruct((n, D_MODEL), F32), jax.ShapeDtypeStruct((n, D_MODEL), BF16),
                   jax.ShapeDtypeStruct((N_EXPERTS, n), F32)],
        grid=(n // TM,),
        in_specs=[pl.BlockSpec((TM, D_MODEL), row),
                  pl.BlockSpec((1, 1, 6, D_MODEL),
                               lambda i: (i // nb_per_batch, jnp.minimum(i % nb_per_batch, 1), 0, 0)),
                  pl.BlockSpec((1, D_MODEL), full2), pl.BlockSpec((1, D_MODEL), full2),
                  pl.BlockSpec((D_MODEL, N_BRANCH * D_MODEL), full2),
                  pl.BlockSpec((TM, BRANCH_W), row), pl.BlockSpec((TM, BRANCH_W), row),
                  pl.BlockSpec((TM, BRANCH_W), row),
                  pl.BlockSpec((N_BRANCH, BRANCH_W, D_MODEL), lambda i: (0, 0, 0)),
                  pl.BlockSpec((D_MODEL, D_MODEL), full2),
                  pl.BlockSpec((N_EXPERTS, D_MODEL), full2), pl.BlockSpec((N_EXPERTS, 1), full2)],
        out_specs=[pl.BlockSpec((TM, D_MODEL), row), pl.BlockSpec((TM, D_MODEL), row),
                   pl.BlockSpec((N_EXPERTS, TM), lambda i: (0, i))],
        compiler_params=_cparams(),
        name="merge",
    )(x, modtab, g1, g2, wg, oa, ob, oc, wb, wo, rwt, rb)


ROUTE_TB = 1024


def _router_body(lg_ref, tri_ref, e_ref, gate_ref, pos_ref, cnt_ref, run_ref):
    @pl.when(pl.program_id(0) == 0)
    def _():
        run_ref[...] = jnp.zeros_like(run_ref)

    lg = lg_ref[...]
    eid = lax.broadcasted_iota(jnp.int32, lg.shape, 0)
    work = lg
    vals, idxs = [], []
    sel = jnp.zeros(lg.shape, F32)
    for _ in range(TOP_K):
        m = jnp.max(work, axis=0, keepdims=True)
        idx = jnp.min(jnp.where(work == m, eid, N_EXPERTS), axis=0, keepdims=True)
        hit = eid == idx
        sel = jnp.where(hit, 1.0, sel)
        work = jnp.where(hit, -jnp.inf, work)
        vals.append(m)
        idxs.append(idx)
    ex = [jnp.exp(v - vals[0]) for v in vals]
    den = ex[0] + ex[1] + ex[2] + ex[3]
    before = _dot(sel.astype(BF16), tri_ref[...]) + run_ref[:, 0:1]
    for k in range(TOP_K):
        e_ref[k:k + 1, :] = idxs[k]
        gate_ref[k:k + 1, :] = ex[k] / den
        pos_ref[k:k + 1, :] = jnp.sum(jnp.where(eid == idxs[k], before, 0.0), axis=0,
                                      keepdims=True).astype(jnp.int32)
    run_ref[...] = run_ref[...] + jnp.sum(sel, axis=1, keepdims=True)
    cnt_ref[...] = run_ref[...]


def _router(logits_t):
    n = logits_t.shape[1]
    tri = (jnp.arange(ROUTE_TB)[:, None] < jnp.arange(ROUTE_TB)[None, :]).astype(BF16)
    blk = lambda i: (0, i)
    return pl.pallas_call(
        _router_body,
        out_shape=[jax.ShapeDtypeStruct((TOP_K, n), jnp.int32), jax.ShapeDtypeStruct((TOP_K, n), F32),
                   jax.ShapeDtypeStruct((TOP_K, n), jnp.int32), jax.ShapeDtypeStruct((N_EXPERTS, LANES), F32)],
        grid=(n // ROUTE_TB,),
        in_specs=[pl.BlockSpec((N_EXPERTS, ROUTE_TB), blk), pl.BlockSpec((ROUTE_TB, ROUTE_TB), lambda i: (0, 0))],
        out_specs=[pl.BlockSpec((TOP_K, ROUTE_TB), blk), pl.BlockSpec((TOP_K, ROUTE_TB), blk),
                   pl.BlockSpec((TOP_K, ROUTE_TB), blk), pl.BlockSpec((N_EXPERTS, LANES), lambda i: (0, 0))],
        scratch_shapes=[pltpu.VMEM((N_EXPERTS, LANES), F32)],
        compiler_params=_cparams(),
        name="router",
    )(logits_t, tri)


def _expert_body(be_ref, nv_ref, x_ref, wgu_ref, bgu_ref, wdn_ref, bdn_ref, o_ref):
    @pl.when(pl.program_id(0) < nv_ref[0])
    def _():
        gu = _dot(x_ref[...], wgu_ref[0]) + bgu_ref[0]
        g_ = jnp.minimum(gu[:, :D_EXPERT], SWIGLU_LIMIT)
        u_ = jnp.clip(gu[:, D_EXPERT:], -SWIGLU_LIMIT, SWIGLU_LIMIT)
        act = (u_ + 1.0) * (g_ * jax.nn.sigmoid(SWIGLU_ALPHA * g_))
        o_ref[...] = _dot(act.astype(BF16), wdn_ref[0]) + bdn_ref[0]

    @pl.when(pl.program_id(0) >= nv_ref[0])
    def _():
        o_ref[...] = jnp.zeros_like(o_ref)


def _experts(block_e, n_valid, xs, wgu, bgu, wdn, bdn):
    p = xs.shape[0]
    n_blocks = p // MOE_BLK
    return pl.pallas_call(
        _expert_body,
        out_shape=jax.ShapeDtypeStruct((p, D_MODEL), F32),
        grid_spec=pltpu.PrefetchScalarGridSpec(
            num_scalar_prefetch=2,
            grid=(n_blocks,),
            in_specs=[pl.BlockSpec((MOE_BLK, D_MODEL), lambda i, be, nv: (i, 0)),
                      pl.BlockSpec((1, D_MODEL, 2 * D_EXPERT), lambda i, be, nv: (be[i], 0, 0)),
                      pl.BlockSpec((1, 1, 2 * D_EXPERT), lambda i, be, nv: (be[i], 0, 0)),
                      pl.BlockSpec((1, D_EXPERT, D_MODEL), lambda i, be, nv: (be[i], 0, 0)),
                      pl.BlockSpec((1, 1, D_MODEL), lambda i, be, nv: (be[i], 0, 0))],
            out_specs=pl.BlockSpec((MOE_BLK, D_MODEL), lambda i, be, nv: (i, 0))),
        compiler_params=_cparams(),
        name="experts",
    )(block_e, n_valid, xs, wgu, bgu, wdn, bdn)


def _moe(h2, logits_t, wgu, bgu, wdn, bdn):
    n = h2.shape[0]
    a = n * TOP_K
    top_e, gate, pos, cnt = _router(logits_t)
    counts = cnt[:, 0].astype(jnp.int32)
    padded = (counts + MOE_BLK - 1) // MOE_BLK * MOE_BLK
    pad_end = jnp.cumsum(padded)
    pad_start = pad_end - padded
    n_blocks = -(-a // MOE_BLK) + N_EXPERTS
    p = n_blocks * MOE_BLK
    eids = jnp.arange(N_EXPERTS, dtype=jnp.int32)[:, None, None]
    slot = pos + jnp.sum(jnp.where(top_e[None] == eids, pad_start[:, None, None], 0), axis=0)
    tok = jnp.broadcast_to(jnp.arange(n, dtype=jnp.int32)[None, :], (TOP_K, n))
    slot_tok = jnp.zeros((p,), jnp.int32).at[slot.reshape(-1)].set(
        tok.reshape(-1), unique_indices=True, indices_are_sorted=False, mode="promise_in_bounds")
    blk_start = jnp.arange(n_blocks, dtype=jnp.int32) * MOE_BLK
    block_e = jnp.minimum(jnp.sum((pad_end[None, :] <= blk_start[:, None]).astype(jnp.int32), axis=1),
                          N_EXPERTS - 1)
    n_valid = (pad_end[-1] // MOE_BLK).astype(jnp.int32).reshape(1)
    xs = h2.at[slot_tok].get(mode="promise_in_bounds")
    yb = _experts(block_e, n_valid, xs, wgu, bgu, wdn, bdn)
    y = None
    for k in range(TOP_K):
        yk = yb.at[slot[k]].get(mode="promise_in_bounds") * gate[k][:, None]
        y = yk if y is None else y + yk
    return y


DN_GW = 256
HALO = 8
DN_PAIR = 2


def _split3(a):
    hi = a.astype(BF16)
    r = a - hi.astype(F32)
    lo = r.astype(BF16)
    lo2 = (r - lo.astype(F32)).astype(BF16)
    return hi, lo, lo2


def _bd(x):
    xb = x.astype(BF16)
    t = jnp.concatenate([xb, xb, xb, xb], axis=0)
    r = lax.broadcasted_iota(jnp.int32, (DN_GW, DN_GW), 0) // DN_HEAD_DIM
    c = lax.broadcasted_iota(jnp.int32, (DN_GW, DN_GW), 1) // DN_HEAD_DIM
    return jnp.where(r == c, t, jnp.zeros_like(t))


def _mm(a, wbd):
    return _dot(a.astype(BF16), wbd)


def _tri_inverse(l_mats, eye, blk16, blk32):
    half = DN_CHUNK
    lds = [jnp.where(blk16, l, 0.0) for l in l_mats]
    ps = [_mm(ld, _bd(ld)) for ld in lds]
    xs = [eye - ld for ld in lds]
    for _ in range(2):
        rs = [_mm(jnp.concatenate([p, x], axis=0), _bd(p)) for p, x in zip(ps, xs)]
        ps = [r[0:half] for r in rs]
        xs = [x + r[half:2 * half] for x, r in zip(xs, rs)]
    xs = [x + _mm(x, _bd(p)) for x, p in zip(xs, ps)]
    ts = [_mm(x, _bd(jnp.where(blk32 & jnp.logical_not(blk16), l, 0.0))) for x, l in zip(xs, l_mats)]
    xs = [x - _mm(t, _bd(x)) for x, t in zip(xs, ts)]
    ts = [_mm(x, _bd(jnp.where(blk32, 0.0, l))) for x, l in zip(xs, l_mats)]
    return [x - _mm(t, _bd(x)) for x, t in zip(xs, ts)]


def _dn_prep_body(cur_ref, prev_ref, next_ref, cw_ref, dg_ref, gc_ref, g512_ref, e_ref, tri_ref,
                  u_ref, w_ref, kd_ref, qd_ref, aq_ref, gl_ref,
                  ext_ref, q_s, k_s, v_s, gx_s, bx_s):
    j = pl.program_id(1)
    nbp = pl.num_programs(1)
    C = DN_CHUNK
    ext_ref[HALO:HALO + TM, :] = cur_ref[...]
    ext_ref[0:HALO, :] = jnp.where(j >= 2, prev_ref[...], 0.0)
    ext_ref[HALO + TM:2 * HALO + TM, :] = jnp.where((j >= 1) & (j < nbp - 1), next_ref[...], 0.0)
    y = None
    for t in range(DN_CONV):
        o = HALO - DN_CONV // 2 + t
        term = cw_ref[t:t + 1, :] * ext_ref[o:o + TM, :]
        y = term if y is None else y + term
    y = y * jax.nn.sigmoid(y)

    g512 = g512_ref[...]

    def head_sumsq(x):
        hi, lo = _split_bf16(x * x)
        return _dot(hi, g512) + _dot(lo, g512)

    q = y[:, 0:512]
    k = y[:, 512:1024]
    q_s[...] = q * lax.rsqrt(head_sumsq(q) + NORM_EPS) * (DN_HEAD_DIM ** -0.5)
    k_s[...] = k * lax.rsqrt(head_sumsq(k) + NORM_EPS)
    v_s[...] = y[:, 1024:1536]

    dg = dg_ref[...]
    beta_all = jax.nn.sigmoid(dg)
    z = dg + gc_ref[1:2, :]
    g_all = gc_ref[0:1, :] * (jnp.maximum(z, 0.0) + jnp.log(1.0 + jnp.exp(-jnp.abs(z))))
    bh, bl = _split_bf16(beta_all)
    bcat = jnp.concatenate([bh, bl], axis=0)
    gparts = jnp.concatenate(_split3(g_all), axis=1)
    for d in range(2):
        cs = _dot(tri_ref[d], gparts)
        gcum = cs[:, 0:128] + cs[:, 128:256] + cs[:, 256:384]
        gcat = jnp.concatenate(_split3(gcum), axis=0)
        for grp in range(2):
            eg = _dot(gcat, e_ref[4 + 2 * d + grp])
            gx_s[d, grp] = eg[0:TM] + eg[TM:2 * TM] + eg[2 * TM:3 * TM]
            eb = _dot(bcat, e_ref[2 * d + grp])
            bx_s[d, grp] = eb[0:TM] + eb[TM:2 * TM]

    row = lax.broadcasted_iota(jnp.int32, (C, DN_GW), 0)
    col = lax.broadcasted_iota(jnp.int32, (C, DN_GW), 1) % C
    eye_b = row == col
    eye = jnp.where(eye_b, 1.0, 0.0)
    blk16 = (row // 16) == (col // 16)
    blk32 = (row // 32) == (col // 32)
    incl = (col <= row, col >= row)
    strict = (col < row, col > row)

    def chunk_pair(it, carry):
        units = []
        l_mats = []
        for ci in range(DN_PAIR):
            cc = it * DN_PAIR + ci
            rows = pl.ds(pl.multiple_of(cc * C, C), C)
            for grp in range(2):
                cols = slice(grp * DN_GW, (grp + 1) * DN_GW)
                kg, qg = k_s[rows, cols], q_s[rows, cols]
                kb = [kg * bx_s[d, grp, rows, :] for d in range(2)]
                raw = _dot_nt(jnp.concatenate([kb[0], kb[1], qg], axis=0).astype(BF16), _bd(kg))
                for d in range(2):
                    gx = gx_s[d, grp, rows, :]
                    rvec = jnp.sum(jnp.where(eye_b, gx, 0.0), axis=0, keepdims=True)
                    dec = jnp.where(incl[d], jnp.exp(jnp.where(incl[d], gx - rvec, 0.0)), 0.0)
                    l_mats.append(jnp.where(strict[d], raw[C * d:C * (d + 1)] * dec, 0.0))
                    glast = gx[C - 1:C, :] if d == 0 else gx[0:1, :]
                    kd_ref[d, rows, cols] = (kg * jnp.exp(glast - gx)).astype(BF16)
                    qd_ref[d, rows, cols] = (qg * jnp.exp(gx)).astype(BF16)
                    aq_ref[d, rows, cols] = (raw[2 * C:3 * C] * dec).astype(BF16)
                    gl_ref[d, pl.ds(cc, 1), :, cols] = jnp.exp(glast).reshape(1, 1, DN_GW)
                    units.append((rows, cols, d, grp))
        tinvs = _tri_inverse(l_mats, eye, blk16, blk32)
        for (rows, cols, d, grp), tinv in zip(units, tinvs):
            bx = bx_s[d, grp, rows, :]
            u_ref[d, rows, cols] = _mm(tinv, _bd(v_s[rows, cols] * bx))
            w_ref[d, rows, cols] = _mm(tinv, _bd(k_s[rows, cols] * bx * jnp.exp(gx_s[d, grp, rows, :]))
                                       ).astype(BF16)
        return carry

    lax.fori_loop(0, TM // C // DN_PAIR, chunk_pair, 0)


def _dn_prep(dqkv, dg, conv_w, a_log, dt_bias, n_batch, rows):
    n = dqkv.shape[0]
    nbp = rows // TM
    hb = TM // HALO
    H = DN_HEADS
    cw = jnp.zeros((8, 1536), F32).at[0:DN_CONV].set(conv_w)
    gc = jnp.zeros((8, LANES), F32)
    gc = gc.at[0, 2 * H:4 * H].set(-jnp.exp(a_log.reshape(-1))).at[1, 2 * H:4 * H].set(dt_bias.reshape(-1))
    hid = np.arange(512) // DN_HEAD_DIM
    g512 = jnp.asarray(hid[:, None] == hid[None, :], BF16)
    e = np.zeros((8, LANES, DN_GW), np.float32)
    for kind in range(2):
        for d in range(2):
            for grp in range(2):
                for h in range(4):
                    e[4 * kind + 2 * d + grp, 16 * kind + 8 * d + 4 * grp + h, 64 * h:64 * (h + 1)] = 1.0
    e = jnp.asarray(e, BF16)
    t = np.arange(TM)
    same = (t[:, None] // DN_CHUNK) == (t[None, :] // DN_CHUNK)
    tri = jnp.asarray(np.stack([same & (t[None, :] <= t[:, None]), same & (t[None, :] >= t[:, None])]), BF16)
    blk = lambda b, j: (b * nbp + j, 0)
    full2 = lambda b, j: (0, 0)
    full3 = lambda b, j: (0, 0, 0)
    dblk = lambda b, j: (0, b * nbp + j, 0)
    nlast = n // HALO - 1
    outs = pl.pallas_call(
        _dn_prep_body,
        out_shape=[jax.ShapeDtypeStruct((2, n, 512), F32)] + [jax.ShapeDtypeStruct((2, n, 512), BF16)] * 4
        + [jax.ShapeDtypeStruct((2, n // DN_CHUNK, 1, 512), F32)],
        grid=(n_batch, nbp),
        in_specs=[pl.BlockSpec((TM, 1536), blk),
                  pl.BlockSpec((HALO, 1536), lambda b, j: (jnp.maximum((b * nbp + j) * hb - 1, 0), 0)),
                  pl.BlockSpec((HALO, 1536), lambda b, j: (jnp.minimum((b * nbp + j + 1) * hb, nlast), 0)),
                  pl.BlockSpec((8, 1536), full2), pl.BlockSpec((TM, LANES), blk), pl.BlockSpec((8, LANES), full2),
                  pl.BlockSpec((512, 512), full2), pl.BlockSpec((8, LANES, DN_GW), full3),
                  pl.BlockSpec((2, TM, TM), full3)],
        out_specs=[pl.BlockSpec((2, TM, 512), dblk)] * 5
        + [pl.BlockSpec((2, TM // DN_CHUNK, 1, 512), lambda b, j: (0, b * nbp + j, 0, 0))],
        scratch_shapes=[pltpu.VMEM((TM + 2 * HALO, 1536), F32), pltpu.VMEM((TM, 512), F32),
                        pltpu.VMEM((TM, 512), F32), pltpu.VMEM((TM, 512), F32),
                        pltpu.VMEM((2, 2, TM, DN_GW), F32), pltpu.VMEM((2, 2, TM, DN_GW), F32)],
        compiler_params=_cparams(2),
        name="dn_prep",
    )(dqkv, dqkv, dqkv, cw, dg, gc, g512, e, tri)
    return outs, g512


def _dn_scan_body(uf, wf, kf, qf, af, gf, ub, wb, kb, qb, ab, gb, of_ref, ob_ref, s_ref):
    @pl.when(pl.program_id(1) == 0)
    def _():
        s_ref[...] = jnp.zeros_like(s_ref)

    r = lax.broadcasted_iota(jnp.int32, (DN_GW, DN_GW), 0) // DN_HEAD_DIM
    c = lax.broadcasted_iota(jnp.int32, (DN_GW, DN_GW), 1) // DN_HEAD_DIM
    same = r == c
    for d, (u_r, w_r, k_r, q_r, a_r, g_r, o_r) in enumerate(((uf, wf, kf, qf, af, gf, of_ref),
                                                             (ub, wb, kb, qb, ab, gb, ob_ref))):
        for grp in range(2):
            cols = slice(grp * DN_GW, (grp + 1) * DN_GW)
            s = s_ref[d, grp]
            wq = jnp.concatenate([w_r[0, :, cols], q_r[0, :, cols]], axis=0)
            sq = _dot(wq, s.astype(BF16))
            vnew = u_r[0, :, cols] - sq[0:DN_CHUNK]
            vb = vnew.astype(BF16)
            o_r[:, cols] = sq[DN_CHUNK:2 * DN_CHUNK] + _dot(a_r[0, :, cols], _bd(vnew))
            upd = lax.dot_general(k_r[0, :, cols], vb, (((0,), (0,)), ((), ())), preferred_element_type=F32)
            s_ref[d, grp] = s * g_r[0, 0, :, cols] + jnp.where(same, upd, 0.0)


def _dn_scan(u, w, kd, qd, aq, gl, n_batch, rows, t_ctx):
    n = u.shape[1]
    nch = rows // DN_CHUNK
    nctx = t_ctx // DN_CHUNK

    def cb(c):
        return jnp.where(c < nctx, nctx - 1 - c, nch - 1 - (c - nctx))

    fwd = lambda b, c: (0, b * nch + c, 0)
    bwd = lambda b, c: (1, b * nch + cb(c), 0)
    fwd4 = lambda b, c: (0, b * nch + c, 0, 0)
    bwd4 = lambda b, c: (1, b * nch + cb(c), 0, 0)
    blk = (1, DN_CHUNK, 512)
    specs = []
    for im, im4 in ((fwd, fwd4), (bwd, bwd4)):
        specs += [pl.BlockSpec(blk, im)] * 5 + [pl.BlockSpec((1, 1, 1, 512), im4)]
    return pl.pallas_call(
        _dn_scan_body,
        out_shape=[jax.ShapeDtypeStruct((n, 512), F32)] * 2,
        grid=(n_batch, nch),
        in_specs=specs,
        out_specs=[pl.BlockSpec((DN_CHUNK, 512), lambda b, c: (b * nch + c, 0)),
                   pl.BlockSpec((DN_CHUNK, 512), lambda b, c: (b * nch + cb(c), 0))],
        scratch_shapes=[pltpu.VMEM((2, 2, DN_GW, DN_GW), F32)],
        compiler_params=_cparams(2),
        name="dn_scan",
    )(u, w, kd, qd, aq, gl, u, w, kd, qd, aq, gl)


def _dn_out_body(of_ref, ob_ref, z_ref, g_ref, g512_ref, o_ref):
    o = of_ref[...] + ob_ref[...]
    hi, lo = _split_bf16(o * o)
    ms = (_dot(hi, g512_ref[...]) + _dot(lo, g512_ref[...])) * (1.0 / DN_HEAD_DIM)
    z = z_ref[...]
    o_ref[...] = (o * lax.rsqrt(ms + NORM_EPS) * g_ref[...] * (z * jax.nn.sigmoid(z))).astype(BF16)


def _dn_out(o_f, o_b, dz, norm_g, g512):
    n = o_f.shape[0]
    row = lambda i: (i, 0)
    g = jnp.tile(norm_g, DN_HEADS).reshape(1, 512)
    return pl.pallas_call(
        _dn_out_body,
        out_shape=jax.ShapeDtypeStruct((n, BRANCH_W), BF16),
        grid=(n // TM,),
        in_specs=[pl.BlockSpec((TM, 512), row)] * 3 + [pl.BlockSpec((1, 512), lambda i: (0, 0)),
                                                       pl.BlockSpec((512, 512), lambda i: (0, 0))],
        out_specs=pl.BlockSpec((TM, BRANCH_W), row),
        compiler_params=_cparams(),
        name="dn_out",
    )(o_f, o_b, dz, g, g512)


def _deltanet(dqkv, dz, dg, conv_w, a_log, dt_bias, norm_g, n_batch, rows, t_ctx):
    (u, w, kd, qd, aq, gl), g512 = _dn_prep(dqkv, dg, conv_w, a_log, dt_bias, n_batch, rows)
    o_f, o_b = _dn_scan(u, w, kd, qd, aq, gl, n_batch, rows, t_ctx)
    return _dn_out(o_f, o_b, dz, norm_g, g512)


def _final_norm_body(x_ref, g_ref, o_ref):
    o_ref[...] = _rms(x_ref[...], g_ref[...])


def _final_norm(x, g):
    n = x.shape[0]
    return pl.pallas_call(
        _final_norm_body,
        out_shape=jax.ShapeDtypeStruct(x.shape, F32),
        grid=(n // TM,),
        in_specs=[pl.BlockSpec((TM, D_MODEL), lambda i: (i, 0)), pl.BlockSpec((1, D_MODEL), lambda i: (0, 0))],
        out_specs=pl.BlockSpec((TM, D_MODEL), lambda i: (i, 0)),
        compiler_params=_cparams(),
        name="final_norm",
    )(x, g)


def _rope_tables(seq, t_ctx, rot_dim):
    rows = seq // GRID_W
    row = jnp.broadcast_to(jnp.arange(rows)[:, None], (rows, GRID_W)).reshape(-1).astype(F32)
    col = jnp.broadcast_to(jnp.arange(GRID_W)[None, :], (rows, GRID_W)).reshape(-1).astype(F32)
    n_freq = rot_dim // 4
    inv_freq = ROPE_THETA ** (-jnp.arange(n_freq, dtype=F32) / n_freq)
    ang = jnp.concatenate([row[:, None] * inv_freq, col[:, None] * inv_freq], axis=-1)
    cos, sin = jnp.cos(ang), jnp.sin(ang)
    zero = jnp.zeros_like(sin)
    c = jnp.concatenate([cos, cos], axis=-1)
    sa = jnp.concatenate([-sin, zero], axis=-1)
    sb = jnp.concatenate([zero, sin], axis=-1)
    ident = jnp.stack([jnp.ones((t_ctx, rot_dim), F32), jnp.zeros((t_ctx, rot_dim), F32),
                       jnp.zeros((t_ctx, rot_dim), F32)])
    tab = jnp.concatenate([ident, jnp.stack([c, sa, sb])], axis=1)
    return jnp.tile(tab, (1, 1, LANES // rot_dim))


WPREP_ROWS = 512


def _wgu_prep_body(w_ref, perm_ref, o_ref):
    for m in range(2 * D_EXPERT // (2 * LANES)):
        t = _dot(w_ref[0, :, m * 2 * LANES:(m + 1) * 2 * LANES].astype(BF16), perm_ref[...])
        o_ref[0, :, m * LANES:(m + 1) * LANES] = t[:, 0:LANES].astype(BF16)
        o_ref[0, :, D_EXPERT + m * LANES:D_EXPERT + (m + 1) * LANES] = t[:, LANES:2 * LANES].astype(BF16)


def _wgu_prep(w_gu):
    perm = np.zeros((2 * LANES, 2 * LANES), np.float32)
    j = np.arange(LANES)
    perm[2 * j, j] = 1.0
    perm[2 * j + 1, LANES + j] = 1.0
    blk = (1, WPREP_ROWS, 2 * D_EXPERT)
    return pl.pallas_call(
        _wgu_prep_body,
        out_shape=jax.ShapeDtypeStruct(w_gu.shape, BF16),
        grid=(N_EXPERTS, D_MODEL // WPREP_ROWS),
        in_specs=[pl.BlockSpec(blk, lambda e, r: (e, r, 0)),
                  pl.BlockSpec((2 * LANES, 2 * LANES), lambda e, r: (0, 0))],
        out_specs=pl.BlockSpec(blk, lambda e, r: (e, r, 0)),
        compiler_params=_cparams(2),
        name="wgu_prep",
    )(w_gu, jnp.asarray(perm, BF16))


def _layer_weights(w_in, w_q_up, w_kv_up, w_branch, w_out, router_w, router_b, w_gu, b_gu, w_dn, b_dn):
    sizes = (MLA_Q_RANK, MLA_KV_RANK, MLA_ROPE, 512, 128, 128, 1536, 512, 16, 16, N_BRANCH * D_MODEL)
    offs = np.cumsum((0,) + sizes)
    part = lambda i: w_in[:, offs[i]:offs[i + 1]]
    pad = jnp.zeros((D_MODEL, LANES - 32), F32)
    w_proj = jnp.concatenate(
        [part(0), part(1), part(3), part(4), part(5), part(6), part(7),
         jnp.tile(part(2), (1, LANES // MLA_ROPE)), part(8), part(9), pad], axis=1).astype(BF16)
    w_gate = part(10).astype(BF16)
    hq = MLA_NOPE + MLA_ROPE
    wq = w_q_up.reshape(MLA_Q_RANK, MLA_HEADS, hq)
    wq = jnp.concatenate([wq[:, :, :MLA_NOPE].reshape(MLA_Q_RANK, -1),
                          wq[:, :, MLA_NOPE:].reshape(MLA_Q_RANK, -1)], axis=1).astype(BF16)
    wkv = w_kv_up.reshape(MLA_KV_RANK, MLA_HEADS, MLA_NOPE + MLA_V)
    wkv = jnp.concatenate([wkv[:, :, :MLA_NOPE].reshape(MLA_KV_RANK, -1),
                           wkv[:, :, MLA_NOPE:].reshape(MLA_KV_RANK, -1)], axis=1).astype(BF16)
    wgu = _wgu_prep(w_gu)
    bgu = jnp.concatenate([b_gu[:, 0::2], b_gu[:, 1::2]], axis=-1).reshape(N_EXPERTS, 1, 2 * D_EXPERT)
    return dict(w_proj=w_proj, w_gate=w_gate, wq=wq, wkv=wkv, wb=w_branch.astype(BF16),
                wo=w_out.astype(BF16), rwt=router_w.T, rb=router_b.reshape(N_EXPERTS, 1),
                wgu=wgu, bgu=bgu, wdn=w_dn.astype(BF16), bdn=b_dn.reshape(N_EXPERTS, 1, D_MODEL))


def kernel(x, c, ctx, c_ctx, w_mod, b_mod, norm1_g, norm2_g, w_in, mla_q_norm_g, mla_w_q_up,
           mla_kv_norm_g, mla_w_kv_up, swa_sink, dn_conv_w, dn_a_log, dn_dt_bias, dn_norm_g,
           w_branch, w_out, router_w, router_b, exp_w_gu, exp_b_gu, exp_w_dn, exp_b_dn, final_norm_g):
    B, S, D = x.shape
    T = ctx.shape[1]
    R = T + S
    depth = w_mod.shape[0]
    assert D == D_MODEL and T == TM and S % TM == 0 and (B * R) % ROUTE_TB == 0
    nbp = R // TM
    xa = jnp.concatenate([ctx, x], axis=1).reshape(B * R, D)
    rope_s = _rope_tables(S, T, SWA_HEAD_DIM)
    rope_m = _rope_tables(S, T, MLA_ROPE)
    cvec = jnp.concatenate([c, c_ctx[None, :]], axis=0)
    for l in range(depth):
        w =_layer_weights(w_in[l], mla_w_q_up[l], mla_w_kv_up[l], w_branch[l], w_out[l], router_w[l],
                           router_b[l], exp_w_gu[l], exp_b_gu[l], exp_w_dn[l], exp_b_dn[l])
        mod = _modulation(cvec, w_mod[l], b_mod[l]).reshape(B + 1, 6, D)
        modtab = jnp.stack([jnp.broadcast_to(mod[B][None], (B, 6, D)), mod[:B]], axis=1)
        g1 = norm1_g[l].reshape(1, D)
        g2 = norm2_g[l].reshape(1, D)
        cq, ckv, krt, sq, sk, sv, dqkv, dz, dg = _inproj(xa, modtab, g1, w["w_proj"], rope_s, nbp)
        q_m, k_m, v_m = _mla_prep(cq, ckv, krt, mla_q_norm_g[l].reshape(1, -1), mla_kv_norm_g[l].reshape(1, -1),
                                  w["wq"], w["wkv"], rope_m, nbp)
        o_a = _mla_attention(q_m, k_m, v_m, B, R)
        o_b = _swa_attention(sq, sk, sv, swa_sink[l], B, T, S)
        o_c = _deltanet(dqkv, dz, dg, dn_conv_w[l], dn_a_log[l], dn_dt_bias[l], dn_norm_g[l], B, R, T)
        xn, h2, logits_t = _merge(xa, modtab, g1, g2, w["w_gate"], o_a, o_b, o_c, w["wb"], w["wo"],
                                  w["rwt"], w["rb"], nbp)
        y = _moe(h2, logits_t, w["wgu"], w["bgu"], w["wdn"], w["bdn"])
        is_ctx = lax.broadcasted_iota(jnp.int32, (1, R, 1), 1) < T
        g2rows = jnp.where(is_ctx, modtab[:, 0, 5][:, None, :], modtab[:, 1, 5][:, None, :])
        xa = (xn.reshape(B, R, D) + g2rows * y.reshape(B, R, D)).reshape(B * R, D)
    out = _final_norm(xa, final_norm_g.reshape(1, D)).reshape(B, R, D)
    return out[:, T:]
```

```python
import functools

import jax
import jax.numpy as jnp
import numpy as np
from jax import lax
from jax.experimental import pallas as pl
from jax.experimental.pallas import tpu as pltpu

F32 = jnp.float32
BF16 = jnp.bfloat16

D_MODEL = 1024
GRID_W = 64
ROPE_THETA = 10000.0
NORM_EPS = 1e-6
MLA_HEADS = 8
MLA_Q_RANK = 384
MLA_KV_RANK = 256
MLA_NOPE = 64
MLA_ROPE = 32
MLA_V = 64
SWA_HEADS = 8
SWA_KV_HEADS = 2
SWA_HEAD_DIM = 64
SWA_WINDOW = 128
DN_HEADS = 8
DN_HEAD_DIM = 64
DN_CONV = 5
DN_CHUNK = 64
N_BRANCH = 3
BRANCH_W = 512
N_EXPERTS = 32
TOP_K = 4
D_EXPERT = 1024
SWIGLU_LIMIT = 7.0
SWIGLU_ALPHA = 1.702
MOE_BLK = 256

LANES = 128
LOG2E = 1.4426950408889634
TM = 256
VMEM_LIMIT = 56 * 1024 * 1024

_C_CQ = (0, 384)
_C_CKV = (384, 640)
_C_SQ = (640, 1152)
_C_SK = (1152, 1280)
_C_SV = (1280, 1408)
_C_DQKV = (1408, 2944)
_C_DZ = (2944, 3456)
_C_KRT = (3456, 3584)
_C_DG = (3584, 3712)
_N_PROJ = 3712


def _cparams(n_axes=1):
    return pltpu.CompilerParams(dimension_semantics=("arbitrary",) * n_axes, vmem_limit_bytes=VMEM_LIMIT)


def _dot(a, b):
    return jnp.dot(a, b, preferred_element_type=F32)


def _dot_nt(a, b):
    return lax.dot_general(a, b, (((1,), (1,)), ((), ())), preferred_element_type=F32)


def _split_bf16(a):
    hi = a.astype(BF16)
    lo = (a - hi.astype(F32)).astype(BF16)
    return hi, lo


def _dot3(a, b):
    ah, al = _split_bf16(a)
    bh, bl = _split_bf16(b)
    return _dot(ah, bh) + (_dot(ah, bl) + _dot(al, bh))


def _dot3_nt(a, b):
    ah, al = _split_bf16(a)
    bh, bl = _split_bf16(b)
    return _dot_nt(ah, bh) + (_dot_nt(ah, bl) + _dot_nt(al, bh))


def _rms(x, g):
    return x * lax.rsqrt(jnp.mean(x * x, axis=-1, keepdims=True) + NORM_EPS) * g


def _rope_cols(x, tab_ref, half):
    c, sa, sb = tab_ref[0], tab_ref[1], tab_ref[2]
    return x * c + pltpu.roll(x, LANES - half, axis=1) * sa + pltpu.roll(x, half, axis=1) * sb


def _mod_body(c_ref, w_ref, b_ref, o_ref):
    c = c_ref[...]
    a = c * jax.nn.sigmoid(c)
    o_ref[...] = _dot3(a, w_ref[...]) + b_ref[...]


def _modulation(cvec, w_mod, b_mod):
    m = cvec.shape[0]
    n = w_mod.shape[1]
    tn = 512
    return pl.pallas_call(
        _mod_body,
        out_shape=jax.ShapeDtypeStruct((m, n), F32),
        grid=(n // tn,),
        in_specs=[pl.BlockSpec((m, D_MODEL), lambda j: (0, 0)),
                  pl.BlockSpec((D_MODEL, tn), lambda j: (0, j)),
                  pl.BlockSpec((1, tn), lambda j: (0, j))],
        out_specs=pl.BlockSpec((m, tn), lambda j: (0, j)),
        compiler_params=_cparams(),
        name="modulation",
    )(cvec, w_mod, b_mod.reshape(1, n))


def _inproj_body(x_ref, mod_ref, g_ref, w_ref, rope_ref,
                 cq_ref, ckv_ref, krt_ref, sq_ref, sk_ref, sv_ref, dqkv_ref, dz_ref, dg_ref):
    x = x_ref[...]
    h = (_rms(x, g_ref[...]) * (1.0 + mod_ref[0, 0, 1:2, :]) + mod_ref[0, 0, 0:1, :]).astype(BF16)

    def proj(cols):
        return _dot(h, w_ref[:, cols[0]:cols[1]])

    cq_ref[...] = proj(_C_CQ)
    ckv_ref[...] = proj(_C_CKV)
    krt_ref[...] = proj(_C_KRT)
    dqkv_ref[...] = proj(_C_DQKV)
    dz_ref[...] = proj(_C_DZ)
    dg_ref[...] = proj(_C_DG)
    sv_ref[...] = proj(_C_SV).astype(BF16)
    sk_ref[...] = _rope_cols(proj(_C_SK), rope_ref, SWA_HEAD_DIM // 2).astype(BF16)
    sq = proj(_C_SQ)
    lane = lax.broadcasted_iota(jnp.int32, (TM, LANES), 1)
    lo = lane < SWA_HEAD_DIM
    for c in range(4):
        xr = _rope_cols(sq[:, c * LANES:(c + 1) * LANES], rope_ref, SWA_HEAD_DIM // 2)
        xs = pltpu.roll(xr, SWA_HEAD_DIM, axis=1)
        if c < 2:
            a, b = jnp.where(lo, xr, 0.0), jnp.where(lo, xs, 0.0)
        else:
            a, b = jnp.where(lo, 0.0, xs), jnp.where(lo, 0.0, xr)
        sq_ref[:, (2 * c) * LANES:(2 * c + 1) * LANES] = a.astype(BF16)
        sq_ref[:, (2 * c + 1) * LANES:(2 * c + 2) * LANES] = b.astype(BF16)


def _inproj(x, modtab, g, w, rope_s, nb_per_batch):
    n = x.shape[0]
    row = lambda i: (i, 0)
    widths = [(384, F32), (256, F32), (128, F32), (1024, BF16), (128, BF16), (128, BF16),
              (1536, F32), (512, F32), (128, F32)]
    return pl.pallas_call(
        _inproj_body,
        out_shape=[jax.ShapeDtypeStruct((n, wd), dt) for wd, dt in widths],
        grid=(n // TM,),
        in_specs=[pl.BlockSpec((TM, D_MODEL), row),
                  pl.BlockSpec((1, 1, 6, D_MODEL),
                               lambda i: (i // nb_per_batch, jnp.minimum(i % nb_per_batch, 1), 0, 0)),
                  pl.BlockSpec((1, D_MODEL), lambda i: (0, 0)),
                  pl.BlockSpec((D_MODEL, _N_PROJ), lambda i: (0, 0)),
                  pl.BlockSpec((3, TM, LANES), lambda i: (0, i % nb_per_batch, 0))],
        out_specs=[pl.BlockSpec((TM, wd), row) for wd, _ in widths],
        compiler_params=_cparams(),
        name="inproj",
    )(x, modtab, g, w, rope_s)


def _mla_prep_body(cq_ref, ckv_ref, krt_ref, qg_ref, kvg_ref, wq_ref, wkv_ref, rope_ref,
                   q_ref, k_ref, v_ref):
    half = MLA_ROPE // 2
    qn = _rms(cq_ref[...], qg_ref[...]).astype(BF16)
    q = _dot(qn, wq_ref[...])
    q_ref[:, 0:512] = q[:, 0:512].astype(BF16)
    for c in range(2):
        lo_, hi_ = 512 + c * LANES, 512 + (c + 1) * LANES
        q_ref[:, lo_:hi_] = _rope_cols(q[:, lo_:hi_], rope_ref, half).astype(BF16)
    kvn = _rms(ckv_ref[...], kvg_ref[...]).astype(BF16)
    kv = _dot(kvn, wkv_ref[...])
    kr = _rope_cols(krt_ref[...], rope_ref, half).astype(BF16)
    for c in range(4):
        k_ref[:, (2 * c) * LANES:(2 * c + 1) * LANES] = kv[:, c * LANES:(c + 1) * LANES].astype(BF16)
        k_ref[:, (2 * c + 1) * LANES:(2 * c + 2) * LANES] = kr
    v_ref[...] = kv[:, 512:1024].astype(BF16)


def _mla_prep(cq, ckv, krt, qg, kvg, wq, wkv, rope_m, nb_per_batch):
    n = cq.shape[0]
    row = lambda i: (i, 0)
    full = lambda i: (0, 0)
    return pl.pallas_call(
        _mla_prep_body,
        out_shape=[jax.ShapeDtypeStruct((n, 768), BF16), jax.ShapeDtypeStruct((n, 1024), BF16),
                   jax.ShapeDtypeStruct((n, 512), BF16)],
        grid=(n // TM,),
        in_specs=[pl.BlockSpec((TM, 384), row), pl.BlockSpec((TM, 256), row), pl.BlockSpec((TM, 128), row),
                  pl.BlockSpec((1, 384), full), pl.BlockSpec((1, 256), full),
                  pl.BlockSpec((384, 768), full), pl.BlockSpec((256, 1024), full),
                  pl.BlockSpec((3, TM, LANES), lambda i: (0, i % nb_per_batch, 0))],
        out_specs=[pl.BlockSpec((TM, 768), row), pl.BlockSpec((TM, 1024), row), pl.BlockSpec((TM, 512), row)],
        compiler_params=_cparams(),
        name="mla_prep",
    )(cq, ckv, krt, qg, kvg, wq, wkv, rope_m)


def _mla_attn_body(q_ref, k_ref, v_ref, o_ref, *, t_ctx):
    c1 = (MLA_NOPE + MLA_ROPE) ** -0.5 * LOG2E
    lane = lax.broadcasted_iota(jnp.int32, (TM, LANES), 1)
    zero = jnp.zeros((TM, LANES), BF16)

    def attend(nk):
        for c in range(4):
            kc = k_ref[0:nk, c * 2 * LANES:(c + 1) * 2 * LANES]
            vc = v_ref[0:nk, c * LANES:(c + 1) * LANES]
            qn = q_ref[:, c * LANES:(c + 1) * LANES]
            qr = q_ref[:, 512 + (c // 2) * LANES:512 + (c // 2 + 1) * LANES]
            qa = [jnp.concatenate(
                [jnp.where((lane >= 64 * s) & (lane < 64 * (s + 1)), qn, zero),
                 jnp.where((lane >= 32 * ((2 * c + s) % 4)) & (lane < 32 * ((2 * c + s) % 4 + 1)), qr, zero)],
                axis=1) for s in range(2)]
            sc = [_dot_nt(q, kc) * c1 for q in qa]
            m = [jnp.max(t, axis=-1, keepdims=True) for t in sc]
            e = [jnp.exp2(t - mm) for t, mm in zip(sc, m)]
            l = [jnp.sum(t, axis=-1, keepdims=True) for t in e]
            outs = [_dot(t.astype(BF16), vc) / ll for t, ll in zip(e, l)]
            o_ref[:, c * LANES:(c + 1) * LANES] = jnp.where(lane < 64, outs[0], outs[1]).astype(BF16)

    j = pl.program_id(1)

    @pl.when(j == 0)
    def _():
        attend(t_ctx)

    @pl.when(j > 0)
    def _():
        attend(k_ref.shape[0])


def _mla_attention(q, k, v, n_batch, rows):
    nbp = rows // TM
    return pl.pallas_call(
        functools.partial(_mla_attn_body, t_ctx=TM),
        out_shape=jax.ShapeDtypeStruct((n_batch * rows, BRANCH_W), BF16),
        grid=(n_batch, nbp),
        in_specs=[pl.BlockSpec((TM, 768), lambda b, j: (b * nbp + j, 0)),
                  pl.BlockSpec((rows, 1024), lambda b, j: (b, 0)),
                  pl.BlockSpec((rows, 512), lambda b, j: (b, 0))],
        out_specs=pl.BlockSpec((TM, BRANCH_W), lambda b, j: (b * nbp + j, 0)),
        compiler_params=_cparams(2),
        name="mla_attention",
    )(q, k, v)


SWA_QB = 128
SWA_NQ = 2


def _swa_body(q_ref, k_ref, v_ref, sink_ref, o_ref, *, t_ctx, seq):
    c1 = SWA_HEAD_DIM ** -0.5 * LOG2E
    j = pl.program_id(1)
    n_ctx_steps = t_ctx // (SWA_QB * SWA_NQ)
    lane = lax.broadcasted_iota(jnp.int32, (SWA_QB, LANES), 1)
    lo = lane < SWA_HEAD_DIM
    win = 3 * SWA_QB

    def finish(rows, parts):
        for c in range(4):
            g = c // 2
            r0 = (2 * (c % 2)) * SWA_QB
            a = parts[g][r0:r0 + SWA_QB]
            b = parts[g][r0 + SWA_QB:r0 + 2 * SWA_QB]
            if g == 0:
                col = jnp.where(lo, a, pltpu.roll(b, SWA_HEAD_DIM, axis=1))
            else:
                col = jnp.where(lo, pltpu.roll(a, SWA_HEAD_DIM, axis=1), b)
            o_ref[rows, c * LANES:(c + 1) * LANES] = col.astype(BF16)

    def blocks(blks):
        kc, vc = k_ref[0:t_ctx, :], v_ref[0:t_ctx, :]
        windowed = blks[0] is not None
        kw, vw, band = [], [], []
        for blk in blks if windowed else ():
            start = jnp.clip((blk - 1) * SWA_QB, 0, seq - win)
            rs = pl.multiple_of(t_ctx + start, SWA_QB)
            kw.append(k_ref[pl.ds(rs, win), :])
            vw.append(v_ref[pl.ds(rs, win), :])
            qpos = blk * SWA_QB + lax.broadcasted_iota(jnp.int32, (SWA_QB, win), 0)
            kpos = start + lax.broadcasted_iota(jnp.int32, (SWA_QB, win), 1)
            band1 = jnp.abs(kpos - qpos) <= SWA_WINDOW
            band.append(jnp.concatenate([band1] * 4, axis=0))
        units = [(sub, g) for sub in range(SWA_NQ) for g in range(SWA_KV_HEADS)]
        rows = [slice(sub * SWA_QB, (sub + 1) * SWA_QB) for sub in range(SWA_NQ)]
        qg = [jnp.concatenate([q_ref[rows[sub], (4 * g + i) * LANES:(4 * g + i + 1) * LANES] for i in range(4)],
                              axis=0) for sub, g in units]
        sk = [c1 * jnp.concatenate(
            [jnp.broadcast_to(sink_ref[4 * g + i:4 * g + i + 1, 0:1], (SWA_QB, 1)) for i in range(4)], axis=0)
            for _, g in units]
        t_c = [_dot_nt(q, kc) * c1 for q in qg]
        m = [jnp.maximum(jnp.max(t, axis=-1, keepdims=True), s) for t, s in zip(t_c, sk)]
        if windowed:
            t_w = [jnp.where(band[sub], _dot_nt(q, kw[sub]) * c1, -jnp.inf) for (sub, _), q in zip(units, qg)]
            m = [jnp.maximum(mm, jnp.max(t, axis=-1, keepdims=True)) for mm, t in zip(m, t_w)]
        e_c = [jnp.exp2(t - mm) for t, mm in zip(t_c, m)]
        l = [jnp.sum(e, axis=-1, keepdims=True) + jnp.exp2(s - mm) for e, s, mm in zip(e_c, sk, m)]
        acc = [_dot(e.astype(BF16), vc) for e in e_c]
        if windowed:
            e_w = [jnp.exp2(t - mm) for t, mm in zip(t_w, m)]
            l = [ll + jnp.sum(e, axis=-1, keepdims=True) for ll, e in zip(l, e_w)]
            acc = [a + _dot(e.astype(BF16), vw[sub]) for a, e, (sub, _) in zip(acc, e_w, units)]
        parts = [a / ll for a, ll in zip(acc, l)]
        for sub in range(SWA_NQ):
            finish(rows[sub], parts[SWA_KV_HEADS * sub:SWA_KV_HEADS * (sub + 1)])

    @pl.when(j < n_ctx_steps)
    def _():
        blocks([None] * SWA_NQ)

    @pl.when(j >= n_ctx_steps)
    def _():
        blocks([(j - n_ctx_steps) * SWA_NQ + sub for sub in range(SWA_NQ)])


def _swa_attention(q, k, v, sink, n_batch, t_ctx, seq):
    rows = t_ctx + seq
    qrows = SWA_QB * SWA_NQ
    assert t_ctx % qrows == 0 and seq % qrows == 0
    nqb = rows // qrows
    sink_tab = jnp.broadcast_to(sink.astype(F32)[:, None], (SWA_HEADS, LANES))
    return pl.pallas_call(
        functools.partial(_swa_body, t_ctx=t_ctx, seq=seq),
        out_shape=jax.ShapeDtypeStruct((n_batch * rows, BRANCH_W), BF16),
        grid=(n_batch, nqb),
        in_specs=[pl.BlockSpec((qrows, 1024), lambda b, j: (b * nqb + j, 0)),
                  pl.BlockSpec((rows, LANES), lambda b, j: (b, 0)),
                  pl.BlockSpec((rows, LANES), lambda b, j: (b, 0)),
                  pl.BlockSpec((SWA_HEADS, LANES), lambda b, j: (0, 0))],
        out_specs=pl.BlockSpec((qrows, BRANCH_W), lambda b, j: (b * nqb + j, 0)),
        compiler_params=_cparams(2),
        name="swa_attention",
    )(q, k, v, sink_tab)


def _merge_body(x_ref, mod_ref, g1_ref, g2_ref, wg_ref, oa_ref, ob_ref, oc_ref, wb_ref, wo_ref,
                rw_ref, rb_ref, xo_ref, h2_ref, lg_ref):
    x = x_ref[...]
    h = (_rms(x, g1_ref[...]) * (1.0 + mod_ref[0, 0, 1:2, :]) + mod_ref[0, 0, 0:1, :]).astype(BF16)
    y = None
    for i, o_ref in enumerate((oa_ref, ob_ref, oc_ref)):
        gate = jax.nn.sigmoid(_dot(h, wg_ref[:, i * D_MODEL:(i + 1) * D_MODEL]))
        yi = gate * _dot(o_ref[...], wb_ref[i])
        y = yi if y is None else y + yi
    xn = x + mod_ref[0, 0, 2:3, :] * _dot(y.astype(BF16), wo_ref[...])
    xo_ref[...] = xn
    h2 = _rms(xn, g2_ref[...]) * (1.0 + mod_ref[0, 0, 4:5, :]) + mod_ref[0, 0, 3:4, :]
    h2_ref[...] = h2.astype(BF16)
    lg_ref[...] = _dot3_nt(rw_ref[...], h2) + rb_ref[...]


def _merge(x, modtab, g1, g2, wg, oa, ob, oc, wb, wo, rwt, rb, nb_per_batch):
    n = x.shape[0]
    row = lambda i: (i, 0)
    full2 = lambda i: (0, 0)
    return pl.pallas_call(
        _merge_body,
        out_shape=[jax.ShapeDtypeStruct((n, D_MODEL), F32), jax.ShapeDtypeStruct((n, D_MODEL), BF16),
                   jax.ShapeDtypeStruct((N_EXPERTS, n), F32)],
        grid=(n // TM,),
        in_specs=[pl.BlockSpec((TM, D_MODEL), row),
                  pl.BlockSpec((1, 1, 6, D_MODEL),
                               lambda i: (i // nb_per_batch, jnp.minimum(i % nb_per_batch, 1), 0, 0)),
                  pl.BlockSpec((1, D_MODEL), full2), pl.BlockSpec((1, D_MODEL), full2),
                  pl.BlockSpec((D_MODEL, N_BRANCH * D_MODEL), full2),
                  pl.BlockSpec((TM, BRANCH_W), row), pl.BlockSpec((TM, BRANCH_W), row),
                  pl.BlockSpec((TM, BRANCH_W), row),
                  pl.BlockSpec((N_BRANCH, BRANCH_W, D_MODEL), lambda i: (0, 0, 0)),
                  pl.BlockSpec((D_MODEL, D_MODEL), full2),
                  pl.BlockSpec((N_EXPERTS, D_MODEL), full2), pl.BlockSpec((N_EXPERTS, 1), full2)],
        out_specs=[pl.BlockSpec((TM, D_MODEL), row), pl.BlockSpec((TM, D_MODEL), row),
                   pl.BlockSpec((N_EXPERTS, TM), lambda i: (0, i))],
        compiler_params=_cparams(),
        name="merge",
    )(x, modtab, g1, g2, wg, oa, ob, oc, wb, wo, rwt, rb)


ROUTE_TB = 1024


def _router_body(lg_ref, tri_ref, e_ref, gate_ref, pos_ref, cnt_ref, run_ref):
    @pl.when(pl.program_id(0) == 0)
    def _():
        run_ref[...] = jnp.zeros_like(run_ref)

    lg = lg_ref[...]
    eid = lax.broadcasted_iota(jnp.int32, lg.shape, 0)
    work = lg
    vals, idxs = [], []
    sel = jnp.zeros(lg.shape, F32)
    for _ in range(TOP_K):
        m = jnp.max(work, axis=0, keepdims=True)
        idx = jnp.min(jnp.where(work == m, eid, N_EXPERTS), axis=0, keepdims=True)
        hit = eid == idx
        sel = jnp.where(hit, 1.0, sel)
        work = jnp.where(hit, -jnp.inf, work)
        vals.append(m)
        idxs.append(idx)
    ex = [jnp.exp(v - vals[0]) for v in vals]
    den = ex[0] + ex[1] + ex[2] + ex[3]
    before = _dot(sel.astype(BF16), tri_ref[...]) + run_ref[:, 0:1]
    for k in range(TOP_K):
        e_ref[k:k + 1, :] = idxs[k]
        gate_ref[k:k + 1, :] = ex[k] / den
        pos_ref[k:k + 1, :] = jnp.sum(jnp.where(eid == idxs[k], before, 0.0), axis=0,
                                      keepdims=True).astype(jnp.int32)
    run_ref[...] = run_ref[...] + jnp.sum(sel, axis=1, keepdims=True)
    cnt_ref[...] = run_ref[...]


def _router(logits_t):
    n = logits_t.shape[1]
    tri = (jnp.arange(ROUTE_TB)[:, None] < jnp.arange(ROUTE_TB)[None, :]).astype(BF16)
    blk = lambda i: (0, i)
    return pl.pallas_call(
        _router_body,
        out_shape=[jax.ShapeDtypeStruct((TOP_K, n), jnp.int32), jax.ShapeDtypeStruct((TOP_K, n), F32),
                   jax.ShapeDtypeStruct((TOP_K, n), jnp.int32), jax.ShapeDtypeStruct((N_EXPERTS, LANES), F32)],
        grid=(n // ROUTE_TB,),
        in_specs=[pl.BlockSpec((N_EXPERTS, ROUTE_TB), blk), pl.BlockSpec((ROUTE_TB, ROUTE_TB), lambda i: (0, 0))],
        out_specs=[pl.BlockSpec((TOP_K, ROUTE_TB), blk), pl.BlockSpec((TOP_K, ROUTE_TB), blk),
                   pl.BlockSpec((TOP_K, ROUTE_TB), blk), pl.BlockSpec((N_EXPERTS, LANES), lambda i: (0, 0))],
        scratch_shapes=[pltpu.VMEM((N_EXPERTS, LANES), F32)],
        compiler_params=_cparams(),
        name="router",
    )(logits_t, tri)


def _expert_body(be_ref, nv_ref, x_ref, wgu_ref, bgu_ref, wdn_ref, bdn_ref, perm_ref, o_ref, wgu_s, wdn_s):
    i = pl.program_id(0)

    @pl.when((i == 0) | (be_ref[i] != be_ref[jnp.maximum(i - 1, 0)]))
    def _():
        for m in range(D_EXPERT // LANES):
            t = _dot(wgu_ref[0, 0, :, m * 2 * LANES:(m + 1) * 2 * LANES].astype(BF16), perm_ref[...])
            wgu_s[:, m * LANES:(m + 1) * LANES] = t[:, 0:LANES].astype(BF16)
            wgu_s[:, D_EXPERT + m * LANES:D_EXPERT + (m + 1) * LANES] = t[:, LANES:2 * LANES].astype(BF16)
        wdn_s[...] = wdn_ref[0, 0].astype(BF16)

    @pl.when(i < nv_ref[0])
    def _():
        gu = _dot(x_ref[...], wgu_s[...]) + bgu_ref[0]
        g_ = jnp.minimum(gu[:, :D_EXPERT], SWIGLU_LIMIT)
        u_ = jnp.clip(gu[:, D_EXPERT:], -SWIGLU_LIMIT, SWIGLU_LIMIT)
        act = (u_ + 1.0) * (g_ * jax.nn.sigmoid(SWIGLU_ALPHA * g_))
        o_ref[...] = _dot(act.astype(BF16), wdn_s[...]) + bdn_ref[0]

    @pl.when(i >= nv_ref[0])
    def _():
        o_ref[...] = jnp.zeros_like(o_ref)


def _experts(block_e, n_valid, xs, w_gu, bgu, w_dn, bdn, layer):
    p = xs.shape[0]
    n_blocks = p // MOE_BLK
    perm = np.zeros((2 * LANES, 2 * LANES), np.float32)
    j = np.arange(LANES)
    perm[2 * j, j] = 1.0
    perm[2 * j + 1, LANES + j] = 1.0
    return pl.pallas_call(
        _expert_body,
        out_shape=jax.ShapeDtypeStruct((p, D_MODEL), F32),
        grid_spec=pltpu.PrefetchScalarGridSpec(
            num_scalar_prefetch=2,
            grid=(n_blocks,),
            in_specs=[pl.BlockSpec((MOE_BLK, D_MODEL), lambda i, be, nv: (i, 0)),
                      pl.BlockSpec((1, 1, D_MODEL, 2 * D_EXPERT), lambda i, be, nv: (layer, be[i], 0, 0)),
                      pl.BlockSpec((1, 1, 2 * D_EXPERT), lambda i, be, nv: (be[i], 0, 0)),
                      pl.BlockSpec((1, 1, D_EXPERT, D_MODEL), lambda i, be, nv: (layer, be[i], 0, 0)),
                      pl.BlockSpec((1, 1, D_MODEL), lambda i, be, nv: (be[i], 0, 0)),
                      pl.BlockSpec((2 * LANES, 2 * LANES), lambda i, be, nv: (0, 0))],
            out_specs=pl.BlockSpec((MOE_BLK, D_MODEL), lambda i, be, nv: (i, 0)),
            scratch_shapes=[pltpu.VMEM((D_MODEL, 2 * D_EXPERT), BF16), pltpu.VMEM((D_EXPERT, D_MODEL), BF16)]),
        compiler_params=_cparams(),
        name="experts",
    )(block_e, n_valid, xs, w_gu, bgu, w_dn, bdn, jnp.asarray(perm, BF16))


def _moe(h2, logits_t, w_gu, bgu, w_dn, bdn, layer):
    n = h2.shape[0]
    a = n * TOP_K
    top_e, gate, pos, cnt = _router(logits_t)
    counts = cnt[:, 0].astype(jnp.int32)
    padded = (counts + MOE_BLK - 1) // MOE_BLK * MOE_BLK
    pad_end = jnp.cumsum(padded)
    pad_start = pad_end - padded
    n_blocks = -(-a // MOE_BLK) + N_EXPERTS
    p = n_blocks * MOE_BLK
    eids = jnp.arange(N_EXPERTS, dtype=jnp.int32)[:, None, None]
    slot = pos + jnp.sum(jnp.where(top_e[None] == eids, pad_start[:, None, None], 0), axis=0)
    tok = jnp.broadcast_to(jnp.arange(n, dtype=jnp.int32)[None, :], (TOP_K, n))
    slot_tok = jnp.zeros((p,), jnp.int32).at[slot.reshape(-1)].set(
        tok.reshape(-1), unique_indices=True, indices_are_sorted=False, mode="promise_in_bounds")
    blk_start = jnp.arange(n_blocks, dtype=jnp.int32) * MOE_BLK
    block_e = jnp.minimum(jnp.sum((pad_end[None, :] <= blk_start[:, None]).astype(jnp.int32), axis=1),
                          N_EXPERTS - 1)
    n_valid = (pad_end[-1] // MOE_BLK).astype(jnp.int32).reshape(1)
    xs = h2.at[slot_tok].get(mode="promise_in_bounds")
    yb = _experts(block_e, n_valid, xs, w_gu, bgu, w_dn, bdn, layer)
    y = None
    for k in range(TOP_K):
        yk = yb.at[slot[k]].get(mode="promise_in_bounds") * gate[k][:, None]
        y = yk if y is None else y + yk
    return y


DN_GW = 256
HALO = 8
DN_PAIR = 2


def _split3(a):
    hi = a.astype(BF16)
    r = a - hi.astype(F32)
    lo = r.astype(BF16)
    lo2 = (r - lo.astype(F32)).astype(BF16)
    return hi, lo, lo2


def _bd(x):
    xb = x.astype(BF16)
    t = jnp.concatenate([xb, xb, xb, xb], axis=0)
    r = lax.broadcasted_iota(jnp.int32, (DN_GW, DN_GW), 0) // DN_HEAD_DIM
    c = lax.broadcasted_iota(jnp.int32, (DN_GW, DN_GW), 1) // DN_HEAD_DIM
    return jnp.where(r == c, t, jnp.zeros_like(t))


def _mm(a, wbd):
    return _dot(a.astype(BF16), wbd)


def _tri_inverse(l_mats, eye, blk16, blk32):
    half = DN_CHUNK
    lds = [jnp.where(blk16, l, 0.0) for l in l_mats]
    ps = [_mm(ld, _bd(ld)) for ld in lds]
    xs = [eye - ld for ld in lds]
    for _ in range(2):
        rs = [_mm(jnp.concatenate([p, x], axis=0), _bd(p)) for p, x in zip(ps, xs)]
        ps = [r[0:half] for r in rs]
        xs = [x + r[half:2 * half] for x, r in zip(xs, rs)]
    xs = [x + _mm(x, _bd(p)) for x, p in zip(xs, ps)]
    ts = [_mm(x, _bd(jnp.where(blk32 & jnp.logical_not(blk16), l, 0.0))) for x, l in zip(xs, l_mats)]
    xs = [x - _mm(t, _bd(x)) for x, t in zip(xs, ts)]
    ts = [_mm(x, _bd(jnp.where(blk32, 0.0, l))) for x, l in zip(xs, l_mats)]
    return [x - _mm(t, _bd(x)) for x, t in zip(xs, ts)]


def _dn_prep_body(cur_ref, prev_ref, next_ref, cw_ref, dg_ref, gc_ref, g512_ref, e_ref, tri_ref,
                  u_ref, w_ref, kd_ref, qd_ref, aq_ref, gl_ref,
                  ext_ref, q_s, k_s, v_s, gx_s, bx_s):
    j = pl.program_id(1)
    nbp = pl.num_programs(1)
    C = DN_CHUNK
    ext_ref[HALO:HALO + TM, :] = cur_ref[...]
    ext_ref[0:HALO, :] = jnp.where(j >= 2, prev_ref[...], 0.0)
    ext_ref[HALO + TM:2 * HALO + TM, :] = jnp.where((j >= 1) & (j < nbp - 1), next_ref[...], 0.0)
    y = None
    for t in range(DN_CONV):
        o = HALO - DN_CONV // 2 + t
        term = cw_ref[t:t + 1, :] * ext_ref[o:o + TM, :]
        y = term if y is None else y + term
    y = y * jax.nn.sigmoid(y)

    g512 = g512_ref[...]

    def head_sumsq(x):
        hi, lo = _split_bf16(x * x)
        return _dot(hi, g512) + _dot(lo, g512)

    q = y[:, 0:512]
    k = y[:, 512:1024]
    q_s[...] = q * lax.rsqrt(head_sumsq(q) + NORM_EPS) * (DN_HEAD_DIM ** -0.5)
    k_s[...] = k * lax.rsqrt(head_sumsq(k) + NORM_EPS)
    v_s[...] = y[:, 1024:1536]

    dg = dg_ref[...]
    beta_all = jax.nn.sigmoid(dg)
    z = dg + gc_ref[1:2, :]
    g_all = gc_ref[0:1, :] * (jnp.maximum(z, 0.0) + jnp.log(1.0 + jnp.exp(-jnp.abs(z))))
    bh, bl = _split_bf16(beta_all)
    bcat = jnp.concatenate([bh, bl], axis=0)
    gparts = jnp.concatenate(_split3(g_all), axis=1)
    for d in range(2):
        cs = _dot(tri_ref[d], gparts)
        gcum = cs[:, 0:128] + cs[:, 128:256] + cs[:, 256:384]
        gcat = jnp.concatenate(_split3(gcum), axis=0)
        for grp in range(2):
            eg = _dot(gcat, e_ref[4 + 2 * d + grp])
            gx_s[d, grp] = eg[0:TM] + eg[TM:2 * TM] + eg[2 * TM:3 * TM]
            eb = _dot(bcat, e_ref[2 * d + grp])
            bx_s[d, grp] = eb[0:TM] + eb[TM:2 * TM]

    row = lax.broadcasted_iota(jnp.int32, (C, DN_GW), 0)
    col = lax.broadcasted_iota(jnp.int32, (C, DN_GW), 1) % C
    eye_b = row == col
    eye = jnp.where(eye_b, 1.0, 0.0)
    blk16 = (row // 16) == (col // 16)
    blk32 = (row // 32) == (col // 32)
    incl = (col <= row, col >= row)
    strict = (col < row, col > row)

    def chunk_pair(it, carry):
        units = []
        l_mats = []
        for ci in range(DN_PAIR):
            cc = it * DN_PAIR + ci
            rows = pl.ds(pl.multiple_of(cc * C, C), C)
            for grp in range(2):
                cols = slice(grp * DN_GW, (grp + 1) * DN_GW)
                kg, qg = k_s[rows, cols], q_s[rows, cols]
                kb = [kg * bx_s[d, grp, rows, :] for d in range(2)]
                raw = _dot_nt(jnp.concatenate([kb[0], kb[1], qg], axis=0).astype(BF16), _bd(kg))
                for d in range(2):
                    gx = gx_s[d, grp, rows, :]
                    rvec = jnp.sum(jnp.where(eye_b, gx, 0.0), axis=0, keepdims=True)
                    dec = jnp.where(incl[d], jnp.exp(jnp.where(incl[d], gx - rvec, 0.0)), 0.0)
                    l_mats.append(jnp.where(strict[d], raw[C * d:C * (d + 1)] * dec, 0.0))
                    glast = gx[C - 1:C, :] if d == 0 else gx[0:1, :]
                    kd_ref[d, rows, cols] = (kg * jnp.exp(glast - gx)).astype(BF16)
                    qd_ref[d, rows, cols] = (qg * jnp.exp(gx)).astype(BF16)
                    aq_ref[d, rows, cols] = (raw[2 * C:3 * C] * dec).astype(BF16)
                    gl_ref[d, pl.ds(cc, 1), :, cols] = jnp.exp(glast).reshape(1, 1, DN_GW)
                    units.append((rows, cols, d, grp))
        tinvs = _tri_inverse(l_mats, eye, blk16, blk32)
        for (rows, cols, d, grp), tinv in zip(units, tinvs):
            bx = bx_s[d, grp, rows, :]
            u_ref[d, rows, cols] = _mm(tinv, _bd(v_s[rows, cols] * bx))
            w_ref[d, rows, cols] = _mm(tinv, _bd(k_s[rows, cols] * bx * jnp.exp(gx_s[d, grp, rows, :]))
                                       ).astype(BF16)
        return carry

    lax.fori_loop(0, TM // C // DN_PAIR, chunk_pair, 0)


def _dn_prep(dqkv, dg, conv_w, a_log, dt_bias, n_batch, rows):
    n = dqkv.shape[0]
    nbp = rows // TM
    hb = TM // HALO
    H = DN_HEADS
    cw = jnp.zeros((8, 1536), F32).at[0:DN_CONV].set(conv_w)
    gc = jnp.zeros((8, LANES), F32)
    gc = gc.at[0, 2 * H:4 * H].set(-jnp.exp(a_log.reshape(-1))).at[1, 2 * H:4 * H].set(dt_bias.reshape(-1))
    hid = np.arange(512) // DN_HEAD_DIM
    g512 = jnp.asarray(hid[:, None] == hid[None, :], BF16)
    e = np.zeros((8, LANES, DN_GW), np.float32)
    for kind in range(2):
        for d in range(2):
            for grp in range(2):
                for h in range(4):
                    e[4 * kind + 2 * d + grp, 16 * kind + 8 * d + 4 * grp + h, 64 * h:64 * (h + 1)] = 1.0
    e = jnp.asarray(e, BF16)
    t = np.arange(TM)
    same = (t[:, None] // DN_CHUNK) == (t[None, :] // DN_CHUNK)
    tri = jnp.asarray(np.stack([same & (t[None, :] <= t[:, None]), same & (t[None, :] >= t[:, None])]), BF16)
    blk = lambda b, j: (b * nbp + j, 0)
    full2 = lambda b, j: (0, 0)
    full3 = lambda b, j: (0, 0, 0)
    dblk = lambda b, j: (0, b * nbp + j, 0)
    nlast = n // HALO - 1
    outs = pl.pallas_call(
        _dn_prep_body,
        out_shape=[jax.ShapeDtypeStruct((2, n, 512), F32)] + [jax.ShapeDtypeStruct((2, n, 512), BF16)] * 4
        + [jax.ShapeDtypeStruct((2, n // DN_CHUNK, 1, 512), F32)],
        grid=(n_batch, nbp),
        in_specs=[pl.BlockSpec((TM, 1536), blk),
                  pl.BlockSpec((HALO, 1536), lambda b, j: (jnp.maximum((b * nbp + j) * hb - 1, 0), 0)),
                  pl.BlockSpec((HALO, 1536), lambda b, j: (jnp.minimum((b * nbp + j + 1) * hb, nlast), 0)),
                  pl.BlockSpec((8, 1536), full2), pl.BlockSpec((TM, LANES), blk), pl.BlockSpec((8, LANES), full2),
                  pl.BlockSpec((512, 512), full2), pl.BlockSpec((8, LANES, DN_GW), full3),
                  pl.BlockSpec((2, TM, TM), full3)],
        out_specs=[pl.BlockSpec((2, TM, 512), dblk)] * 5
        + [pl.BlockSpec((2, TM // DN_CHUNK, 1, 512), lambda b, j: (0, b * nbp + j, 0, 0))],
        scratch_shapes=[pltpu.VMEM((TM + 2 * HALO, 1536), F32), pltpu.VMEM((TM, 512), F32),
                        pltpu.VMEM((TM, 512), F32), pltpu.VMEM((TM, 512), F32),
                        pltpu.VMEM((2, 2, TM, DN_GW), F32), pltpu.VMEM((2, 2, TM, DN_GW), F32)],
        compiler_params=_cparams(2),
        name="dn_prep",
    )(dqkv, dqkv, dqkv, cw, dg, gc, g512, e, tri)
    return outs, g512


DN_SCAN_NB = 2


def _dn_scan_body(uf, wf, kf, qf, af, gf, ub, wb, kb, qb, ab, gb, of_ref, ob_ref, s_ref):
    @pl.when(pl.program_id(1) == 0)
    def _():
        s_ref[...] = jnp.zeros_like(s_ref)

    r = lax.broadcasted_iota(jnp.int32, (DN_GW, DN_GW), 0) // DN_HEAD_DIM
    c = lax.broadcasted_iota(jnp.int32, (DN_GW, DN_GW), 1) // DN_HEAD_DIM
    same = r == c
    dirs = ((uf, wf, kf, qf, af, gf, of_ref), (ub, wb, kb, qb, ab, gb, ob_ref))
    units = [(bb, d, grp) for bb in range(DN_SCAN_NB) for d in range(2) for grp in range(2)]
    cols = [slice(grp * DN_GW, (grp + 1) * DN_GW) for grp in range(2)]
    s = [s_ref[bb, d, grp] for bb, d, grp in units]
    sq = [_dot(jnp.concatenate([dirs[d][1][0, bb, :, cols[grp]], dirs[d][3][0, bb, :, cols[grp]]], axis=0),
               st.astype(BF16)) for (bb, d, grp), st in zip(units, s)]
    vnew = [dirs[d][0][0, bb, :, cols[grp]] - r[0:DN_CHUNK] for (bb, d, grp), r in zip(units, sq)]
    intra = [_dot(dirs[d][4][0, bb, :, cols[grp]], _bd(v)) for (bb, d, grp), v in zip(units, vnew)]
    upd = [lax.dot_general(dirs[d][2][0, bb, :, cols[grp]], v.astype(BF16), (((0,), (0,)), ((), ())),
                           preferred_element_type=F32) for (bb, d, grp), v in zip(units, vnew)]
    for (bb, d, grp), st, r, a, up in zip(units, s, sq, intra, upd):
        dirs[d][6][bb, :, cols[grp]] = r[DN_CHUNK:2 * DN_CHUNK] + a
        s_ref[bb, d, grp] = st * dirs[d][5][0, bb, 0, :, cols[grp]] + jnp.where(same, up, 0.0)


def _dn_scan(u, w, kd, qd, aq, gl, n_batch, rows, t_ctx):
    nch = rows // DN_CHUNK
    nctx = t_ctx // DN_CHUNK
    assert n_batch % DN_SCAN_NB == 0

    def cb(c):
        return jnp.where(c < nctx, nctx - 1 - c, nch - 1 - (c - nctx))

    per_batch = lambda a: a.reshape(2, n_batch, rows, 512)
    u, w, kd, qd, aq = map(per_batch, (u, w, kd, qd, aq))
    gl = gl.reshape(2, n_batch, nch, 1, 512)
    blk = (1, DN_SCAN_NB, DN_CHUNK, 512)
    gblk = (1, DN_SCAN_NB, 1, 1, 512)
    specs = []
    for d, ch in ((0, lambda c: c), (1, cb)):
        specs += [pl.BlockSpec(blk, lambda b, c, d=d, ch=ch: (d, b, ch(c), 0))] * 5
        specs += [pl.BlockSpec(gblk, lambda b, c, d=d, ch=ch: (d, b, ch(c), 0, 0))]
    o_f, o_b = pl.pallas_call(
        _dn_scan_body,
        out_shape=[jax.ShapeDtypeStruct((n_batch, rows, 512), F32)] * 2,
        grid=(n_batch // DN_SCAN_NB, nch),
        in_specs=specs,
        out_specs=[pl.BlockSpec((DN_SCAN_NB, DN_CHUNK, 512), lambda b, c: (b, c, 0)),
                   pl.BlockSpec((DN_SCAN_NB, DN_CHUNK, 512), lambda b, c: (b, cb(c), 0))],
        scratch_shapes=[pltpu.VMEM((DN_SCAN_NB, 2, 2, DN_GW, DN_GW), F32)],
        compiler_params=_cparams(2),
        name="dn_scan",
    )(u, w, kd, qd, aq, gl, u, w, kd, qd, aq, gl)
    return o_f.reshape(n_batch * rows, 512), o_b.reshape(n_batch * rows, 512)


def _dn_out_body(of_ref, ob_ref, z_ref, g_ref, g512_ref, o_ref):
    o = of_ref[...] + ob_ref[...]
    hi, lo = _split_bf16(o * o)
    ms = (_dot(hi, g512_ref[...]) + _dot(lo, g512_ref[...])) * (1.0 / DN_HEAD_DIM)
    z = z_ref[...]
    o_ref[...] = (o * lax.rsqrt(ms + NORM_EPS) * g_ref[...] * (z * jax.nn.sigmoid(z))).astype(BF16)


def _dn_out(o_f, o_b, dz, norm_g, g512):
    n = o_f.shape[0]
    row = lambda i: (i, 0)
    g = jnp.tile(norm_g, DN_HEADS).reshape(1, 512)
    return pl.pallas_call(
        _dn_out_body,
        out_shape=jax.ShapeDtypeStruct((n, BRANCH_W), BF16),
        grid=(n // TM,),
        in_specs=[pl.BlockSpec((TM, 512), row)] * 3 + [pl.BlockSpec((1, 512), lambda i: (0, 0)),
                                                       pl.BlockSpec((512, 512), lambda i: (0, 0))],
        out_specs=pl.BlockSpec((TM, BRANCH_W), row),
        compiler_params=_cparams(),
        name="dn_out",
    )(o_f, o_b, dz, g, g512)


def _deltanet(dqkv, dz, dg, conv_w, a_log, dt_bias, norm_g, n_batch, rows, t_ctx):
    (u, w, kd, qd, aq, gl), g512 = _dn_prep(dqkv, dg, conv_w, a_log, dt_bias, n_batch, rows)
    o_f, o_b = _dn_scan(u, w, kd, qd, aq, gl, n_batch, rows, t_ctx)
    return _dn_out(o_f, o_b, dz, norm_g, g512)


def _final_norm_body(x_ref, g_ref, o_ref):
    o_ref[...] = _rms(x_ref[...], g_ref[...])


def _final_norm(x, g):
    n = x.shape[0]
    return pl.pallas_call(
        _final_norm_body,
        out_shape=jax.ShapeDtypeStruct(x.shape, F32),
        grid=(n // TM,),
        in_specs=[pl.BlockSpec((TM, D_MODEL), lambda i: (i, 0)), pl.BlockSpec((1, D_MODEL), lambda i: (0, 0))],
        out_specs=pl.BlockSpec((TM, D_MODEL), lambda i: (i, 0)),
        compiler_params=_cparams(),
        name="final_norm",
    )(x, g)


def _rope_tables(seq, t_ctx, rot_dim):
    rows = seq // GRID_W
    row = jnp.broadcast_to(jnp.arange(rows)[:, None], (rows, GRID_W)).reshape(-1).astype(F32)
    col = jnp.broadcast_to(jnp.arange(GRID_W)[None, :], (rows, GRID_W)).reshape(-1).astype(F32)
    n_freq = rot_dim // 4
    inv_freq = ROPE_THETA ** (-jnp.arange(n_freq, dtype=F32) / n_freq)
    ang = jnp.concatenate([row[:, None] * inv_freq, col[:, None] * inv_freq], axis=-1)
    cos, sin = jnp.cos(ang), jnp.sin(ang)
    zero = jnp.zeros_like(sin)
    c = jnp.concatenate([cos, cos], axis=-1)
    sa = jnp.concatenate([-sin, zero], axis=-1)
    sb = jnp.concatenate([zero, sin], axis=-1)
    ident = jnp.stack([jnp.ones((t_ctx, rot_dim), F32), jnp.zeros((t_ctx, rot_dim), F32),
                       jnp.zeros((t_ctx, rot_dim), F32)])
    tab = jnp.concatenate([ident, jnp.stack([c, sa, sb])], axis=1)
    return jnp.tile(tab, (1, 1, LANES // rot_dim))


def _layer_weights(w_in, w_q_up, w_kv_up, w_branch, w_out, router_w, router_b, b_gu, b_dn):
    sizes = (MLA_Q_RANK, MLA_KV_RANK, MLA_ROPE, 512, 128, 128, 1536, 512, 16, 16, N_BRANCH * D_MODEL)
    offs = np.cumsum((0,) + sizes)
    part = lambda i: w_in[:, offs[i]:offs[i + 1]]
    pad = jnp.zeros((D_MODEL, LANES - 32), F32)
    w_proj = jnp.concatenate(
        [part(0), part(1), part(3), part(4), part(5), part(6), part(7),
         jnp.tile(part(2), (1, LANES // MLA_ROPE)), part(8), part(9), pad], axis=1).astype(BF16)
    w_gate = part(10).astype(BF16)
    hq = MLA_NOPE + MLA_ROPE
    wq = w_q_up.reshape(MLA_Q_RANK, MLA_HEADS, hq)
    wq = jnp.concatenate([wq[:, :, :MLA_NOPE].reshape(MLA_Q_RANK, -1),
                          wq[:, :, MLA_NOPE:].reshape(MLA_Q_RANK, -1)], axis=1).astype(BF16)
    wkv = w_kv_up.reshape(MLA_KV_RANK, MLA_HEADS, MLA_NOPE + MLA_V)
    wkv = jnp.concatenate([wkv[:, :, :MLA_NOPE].reshape(MLA_KV_RANK, -1),
                           wkv[:, :, MLA_NOPE:].reshape(MLA_KV_RANK, -1)], axis=1).astype(BF16)
    bgu = jnp.concatenate([b_gu[:, 0::2], b_gu[:, 1::2]], axis=-1).reshape(N_EXPERTS, 1, 2 * D_EXPERT)
    return dict(w_proj=w_proj, w_gate=w_gate, wq=wq, wkv=wkv, wb=w_branch.astype(BF16),
                wo=w_out.astype(BF16), rwt=router_w.T, rb=router_b.reshape(N_EXPERTS, 1),
                bgu=bgu, bdn=b_dn.reshape(N_EXPERTS, 1, D_MODEL))


def kernel(x, c, ctx, c_ctx, w_mod, b_mod, norm1_g, norm2_g, w_in, mla_q_norm_g, mla_w_q_up,
           mla_kv_norm_g, mla_w_kv_up, swa_sink, dn_conv_w, dn_a_log, dn_dt_bias, dn_norm_g,
           w_branch, w_out, router_w, router_b, exp_w_gu, exp_b_gu, exp_w_dn, exp_b_dn, final_norm_g):
    B, S, D = x.shape
    T = ctx.shape[1]
    R = T + S
    depth = w_mod.shape[0]
    assert D == D_MODEL and T == TM and S % TM == 0 and (B * R) % ROUTE_TB == 0
    nbp = R // TM
    xa = jnp.concatenate([ctx, x], axis=1).reshape(B * R, D)
    rope_s = _rope_tables(S, T, SWA_HEAD_DIM)
    rope_m = _rope_tables(S, T, MLA_ROPE)
    cvec = jnp.concatenate([c, c_ctx[None, :]], axis=0)
    for l in range(depth):
        w = _layer_weights(w_in[l], mla_w_q_up[l], mla_w_kv_up[l], w_branch[l], w_out[l], router_w[l],
                           router_b[l], exp_b_gu[l], exp_b_dn[l])
        mod = _modulation(cvec, w_mod[l], b_mod[l]).reshape(B + 1, 6, D)
        modtab = jnp.stack([jnp.broadcast_to(mod[B][None], (B, 6, D)), mod[:B]], axis=1)
        g1 = norm1_g[l].reshape(1, D)
        g2 = norm2_g[l].reshape(1, D)
        cq, ckv, krt, sq, sk, sv, dqkv, dz, dg = _inproj(xa, modtab, g1, w["w_proj"], rope_s, nbp)
        q_m, k_m, v_m = _mla_prep(cq, ckv, krt, mla_q_norm_g[l].reshape(1, -1), mla_kv_norm_g[l].reshape(1, -1),
                                  w["wq"], w["wkv"], rope_m, nbp)
        o_a = _mla_attention(q_m, k_m, v_m, B, R)
        o_b = _swa_attention(sq, sk, sv, swa_sink[l], B, T, S)
        o_c = _deltanet(dqkv, dz, dg, dn_conv_w[l], dn_a_log[l], dn_dt_bias[l], dn_norm_g[l], B, R, T)
        xn, h2, logits_t = _merge(xa, modtab, g1, g2, w["w_gate"], o_a, o_b, o_c, w["wb"], w["wo"],
                                  w["rwt"], w["rb"], nbp)
        y = _moe(h2, logits_t, exp_w_gu, w["bgu"], exp_w_dn, w["bdn"], l)
        is_ctx = lax.broadcasted_iota(jnp.int32, (1, R, 1), 1) < T
        g2rows = jnp.where(is_ctx, modtab[:, 0, 5][:, None, :], modtab[:, 1, 5][:, None, :])
        xa = (xn.reshape(B, R, D) + g2rows * y.reshape(B, R, D)).reshape(B * R, D)
    out = _final_norm(xa, final_norm_g.reshape(1, D)).reshape(B, R, D)
    return out[:, T:]
```

```python
import functools

import jax
import jax.numpy as jnp
import numpy as np
from jax import lax
from jax.experimental import pallas as pl
from jax.experimental.pallas import tpu as pltpu

F32 = jnp.float32
BF16 = jnp.bfloat16

D_MODEL = 1024
GRID_W = 64
ROPE_THETA = 10000.0
NORM_EPS = 1e-6
MLA_HEADS = 8
MLA_Q_RANK = 384
MLA_KV_RANK = 256
MLA_NOPE = 64
MLA_ROPE = 32
MLA_V = 64
SWA_HEADS = 8
SWA_KV_HEADS = 2
SWA_HEAD_DIM = 64
SWA_WINDOW = 128
DN_HEADS = 8
DN_HEAD_DIM = 64
DN_CONV = 5
DN_CHUNK = 64
N_BRANCH = 3
BRANCH_W = 512
N_EXPERTS = 32
TOP_K = 4
D_EXPERT = 1024
SWIGLU_LIMIT = 7.0
SWIGLU_ALPHA = 1.702
MOE_BLK = 512
MOE_CHUNKS = 4

LANES = 128
LOG2E = 1.4426950408889634
TM = 256
VMEM_LIMIT = 56 * 1024 * 1024

_C_CQ = (0, 384)
_C_CKV = (384, 640)
_C_SQ = (640, 1152)
_C_SK = (1152, 1280)
_C_SV = (1280, 1408)
_C_DQKV = (1408, 2944)
_C_DZ = (2944, 3456)
_C_KRT = (3456, 3584)
_C_DG = (3584, 3712)
_N_PROJ = 3712


def _cparams(n_axes=1):
    return pltpu.CompilerParams(dimension_semantics=("arbitrary",) * n_axes, vmem_limit_bytes=VMEM_LIMIT)


def _dot(a, b):
    return jnp.dot(a, b, preferred_element_type=F32)


def _dot_nt(a, b):
    return lax.dot_general(a, b, (((1,), (1,)), ((), ())), preferred_element_type=F32)


def _split_bf16(a):
    hi = a.astype(BF16)
    lo = (a - hi.astype(F32)).astype(BF16)
    return hi, lo


def _dot3(a, b):
    ah, al = _split_bf16(a)
    bh, bl = _split_bf16(b)
    return _dot(ah, bh) + (_dot(ah, bl) + _dot(al, bh))


def _dot3_nt(a, b):
    ah, al = _split_bf16(a)
    bh, bl = _split_bf16(b)
    return _dot_nt(ah, bh) + (_dot_nt(ah, bl) + _dot_nt(al, bh))


def _rms(x, g):
    return x * lax.rsqrt(jnp.mean(x * x, axis=-1, keepdims=True) + NORM_EPS) * g


def _rope_cols(x, tab_ref, half):
    c, sa, sb = tab_ref[0], tab_ref[1], tab_ref[2]
    return x * c + pltpu.roll(x, LANES - half, axis=1) * sa + pltpu.roll(x, half, axis=1) * sb


def _mod_body(c_ref, w_ref, b_ref, o_ref):
    c = c_ref[...]
    a = c * jax.nn.sigmoid(c)
    o_ref[...] = _dot3(a, w_ref[...]) + b_ref[...]


def _modulation(cvec, w_mod, b_mod):
    m = cvec.shape[0]
    n = w_mod.shape[1]
    tn = 512
    return pl.pallas_call(
        _mod_body,
        out_shape=jax.ShapeDtypeStruct((m, n), F32),
        grid=(n // tn,),
        in_specs=[pl.BlockSpec((m, D_MODEL), lambda j: (0, 0)),
                  pl.BlockSpec((D_MODEL, tn), lambda j: (0, j)),
                  pl.BlockSpec((1, tn), lambda j: (0, j))],
        out_specs=pl.BlockSpec((m, tn), lambda j: (0, j)),
        compiler_params=_cparams(),
        name="modulation",
    )(cvec, w_mod, b_mod.reshape(1, n))


def _inproj_body(x_ref, mod_ref, g_ref, w_ref, rope_ref,
                 cq_ref, ckv_ref, krt_ref, sq_ref, sk_ref, sv_ref, dqkv_ref, dz_ref, dg_ref):
    x = x_ref[...]
    h = (_rms(x, g_ref[...]) * (1.0 + mod_ref[0, 0, 1:2, :]) + mod_ref[0, 0, 0:1, :]).astype(BF16)

    def proj(cols):
        return _dot(h, w_ref[:, cols[0]:cols[1]])

    cq_ref[...] = proj(_C_CQ)
    ckv_ref[...] = proj(_C_CKV)
    krt_ref[...] = proj(_C_KRT)
    dqkv_ref[...] = proj(_C_DQKV)
    dz_ref[...] = proj(_C_DZ)
    dg_ref[...] = proj(_C_DG)
    sv_ref[...] = proj(_C_SV).astype(BF16)
    sk_ref[...] = _rope_cols(proj(_C_SK), rope_ref, SWA_HEAD_DIM // 2).astype(BF16)
    sq = proj(_C_SQ)
    lane = lax.broadcasted_iota(jnp.int32, (TM, LANES), 1)
    lo = lane < SWA_HEAD_DIM
    for c in range(4):
        xr = _rope_cols(sq[:, c * LANES:(c + 1) * LANES], rope_ref, SWA_HEAD_DIM // 2)
        xs = pltpu.roll(xr, SWA_HEAD_DIM, axis=1)
        if c < 2:
            a, b = jnp.where(lo, xr, 0.0), jnp.where(lo, xs, 0.0)
        else:
            a, b = jnp.where(lo, 0.0, xs), jnp.where(lo, 0.0, xr)
        sq_ref[:, (2 * c) * LANES:(2 * c + 1) * LANES] = a.astype(BF16)
        sq_ref[:, (2 * c + 1) * LANES:(2 * c + 2) * LANES] = b.astype(BF16)


def _inproj(x, modtab, g, w, rope_s, nb_per_batch):
    n = x.shape[0]
    row = lambda i: (i, 0)
    widths = [(384, F32), (256, F32), (128, F32), (1024, BF16), (128, BF16), (128, BF16),
              (1536, F32), (512, F32), (128, F32)]
    return pl.pallas_call(
        _inproj_body,
        out_shape=[jax.ShapeDtypeStruct((n, wd), dt) for wd, dt in widths],
        grid=(n // TM,),
        in_specs=[pl.BlockSpec((TM, D_MODEL), row),
                  pl.BlockSpec((1, 1, 6, D_MODEL),
                               lambda i: (i // nb_per_batch, jnp.minimum(i % nb_per_batch, 1), 0, 0)),
                  pl.BlockSpec((1, D_MODEL), lambda i: (0, 0)),
                  pl.BlockSpec((D_MODEL, _N_PROJ), lambda i: (0, 0)),
                  pl.BlockSpec((3, TM, LANES), lambda i: (0, i % nb_per_batch, 0))],
        out_specs=[pl.BlockSpec((TM, wd), row) for wd, _ in widths],
        compiler_params=_cparams(),
        name="inproj",
    )(x, modtab, g, w, rope_s)


def _mla_prep_body(cq_ref, ckv_ref, krt_ref, qg_ref, kvg_ref, wq_ref, wkv_ref, rope_ref,
                   q_ref, k_ref, v_ref):
    half = MLA_ROPE // 2
    qn = _rms(cq_ref[...], qg_ref[...]).astype(BF16)
    q = _dot(qn, wq_ref[...])
    q_ref[:, 0:512] = q[:, 0:512].astype(BF16)
    for c in range(2):
        lo_, hi_ = 512 + c * LANES, 512 + (c + 1) * LANES
        q_ref[:, lo_:hi_] = _rope_cols(q[:, lo_:hi_], rope_ref, half).astype(BF16)
    kvn = _rms(ckv_ref[...], kvg_ref[...]).astype(BF16)
    kv = _dot(kvn, wkv_ref[...])
    kr = _rope_cols(krt_ref[...], rope_ref, half).astype(BF16)
    for c in range(4):
        k_ref[:, (2 * c) * LANES:(2 * c + 1) * LANES] = kv[:, c * LANES:(c + 1) * LANES].astype(BF16)
        k_ref[:, (2 * c + 1) * LANES:(2 * c + 2) * LANES] = kr
    v_ref[...] = kv[:, 512:1024].astype(BF16)


def _mla_prep(cq, ckv, krt, qg, kvg, wq, wkv, rope_m, nb_per_batch):
    n = cq.shape[0]
    row = lambda i: (i, 0)
    full = lambda i: (0, 0)
    return pl.pallas_call(
        _mla_prep_body,
        out_shape=[jax.ShapeDtypeStruct((n, 768), BF16), jax.ShapeDtypeStruct((n, 1024), BF16),
                   jax.ShapeDtypeStruct((n, 512), BF16)],
        grid=(n // TM,),
        in_specs=[pl.BlockSpec((TM, 384), row), pl.BlockSpec((TM, 256), row), pl.BlockSpec((TM, 128), row),
                  pl.BlockSpec((1, 384), full), pl.BlockSpec((1, 256), full),
                  pl.BlockSpec((384, 768), full), pl.BlockSpec((256, 1024), full),
                  pl.BlockSpec((3, TM, LANES), lambda i: (0, i % nb_per_batch, 0))],
        out_specs=[pl.BlockSpec((TM, 768), row), pl.BlockSpec((TM, 1024), row), pl.BlockSpec((TM, 512), row)],
        compiler_params=_cparams(),
        name="mla_prep",
    )(cq, ckv, krt, qg, kvg, wq, wkv, rope_m)


def _mla_attn_body(q_ref, k_ref, v_ref, o_ref, *, t_ctx):
    c1 = (MLA_NOPE + MLA_ROPE) ** -0.5 * LOG2E
    lane = lax.broadcasted_iota(jnp.int32, (TM, LANES), 1)
    zero = jnp.zeros((TM, LANES), BF16)

    def attend(nk):
        for c in range(4):
            kc = k_ref[0:nk, c * 2 * LANES:(c + 1) * 2 * LANES]
            vc = v_ref[0:nk, c * LANES:(c + 1) * LANES]
            qn = q_ref[:, c * LANES:(c + 1) * LANES]
            qr = q_ref[:, 512 + (c // 2) * LANES:512 + (c // 2 + 1) * LANES]
            qa = [jnp.concatenate(
                [jnp.where((lane >= 64 * s) & (lane < 64 * (s + 1)), qn, zero),
                 jnp.where((lane >= 32 * ((2 * c + s) % 4)) & (lane < 32 * ((2 * c + s) % 4 + 1)), qr, zero)],
                axis=1) for s in range(2)]
            sc = [_dot_nt(q, kc) * c1 for q in qa]
            m = [jnp.max(t, axis=-1, keepdims=True) for t in sc]
            e = [jnp.exp2(t - mm) for t, mm in zip(sc, m)]
            l = [jnp.sum(t, axis=-1, keepdims=True) for t in e]
            outs = [_dot(t.astype(BF16), vc) / ll for t, ll in zip(e, l)]
            o_ref[:, c * LANES:(c + 1) * LANES] = jnp.where(lane < 64, outs[0], outs[1]).astype(BF16)

    j = pl.program_id(1)

    @pl.when(j == 0)
    def _():
        attend(t_ctx)

    @pl.when(j > 0)
    def _():
        attend(k_ref.shape[0])


def _mla_attention(q, k, v, n_batch, rows):
    nbp = rows // TM
    return pl.pallas_call(
        functools.partial(_mla_attn_body, t_ctx=TM),
        out_shape=jax.ShapeDtypeStruct((n_batch * rows, BRANCH_W), BF16),
        grid=(n_batch, nbp),
        in_specs=[pl.BlockSpec((TM, 768), lambda b, j: (b * nbp + j, 0)),
                  pl.BlockSpec((rows, 1024), lambda b, j: (b, 0)),
                  pl.BlockSpec((rows, 512), lambda b, j: (b, 0))],
        out_specs=pl.BlockSpec((TM, BRANCH_W), lambda b, j: (b * nbp + j, 0)),
        compiler_params=_cparams(2),
        name="mla_attention",
    )(q, k, v)


SWA_QB = 128
SWA_NQ = 2


def _swa_body(q_ref, k_ref, v_ref, sink_ref, o_ref, *, t_ctx, seq):
    c1 = SWA_HEAD_DIM ** -0.5 * LOG2E
    j = pl.program_id(1)
    n_ctx_steps = t_ctx // (SWA_QB * SWA_NQ)
    lane = lax.broadcasted_iota(jnp.int32, (SWA_QB, LANES), 1)
    lo = lane < SWA_HEAD_DIM
    win = 3 * SWA_QB

    def finish(rows, parts):
        for c in range(4):
            g = c // 2
            r0 = (2 * (c % 2)) * SWA_QB
            a = parts[g][r0:r0 + SWA_QB]
            b = parts[g][r0 + SWA_QB:r0 + 2 * SWA_QB]
            if g == 0:
                col = jnp.where(lo, a, pltpu.roll(b, SWA_HEAD_DIM, axis=1))
            else:
                col = jnp.where(lo, pltpu.roll(a, SWA_HEAD_DIM, axis=1), b)
            o_ref[rows, c * LANES:(c + 1) * LANES] = col.astype(BF16)

    def blocks(blks):
        kc, vc = k_ref[0:t_ctx, :], v_ref[0:t_ctx, :]
        windowed = blks[0] is not None
        kw, vw, band = [], [], []
        for blk in blks if windowed else ():
            start = jnp.clip((blk - 1) * SWA_QB, 0, seq - win)
            rs = pl.multiple_of(t_ctx + start, SWA_QB)
            kw.append(k_ref[pl.ds(rs, win), :])
            vw.append(v_ref[pl.ds(rs, win), :])
            qpos = blk * SWA_QB + lax.broadcasted_iota(jnp.int32, (SWA_QB, win), 0)
            kpos = start + lax.broadcasted_iota(jnp.int32, (SWA_QB, win), 1)
            band1 = jnp.abs(kpos - qpos) <= SWA_WINDOW
            band.append(jnp.concatenate([band1] * 4, axis=0))
        units = [(sub, g) for sub in range(SWA_NQ) for g in range(SWA_KV_HEADS)]
        rows = [slice(sub * SWA_QB, (sub + 1) * SWA_QB) for sub in range(SWA_NQ)]
        qg = [jnp.concatenate([q_ref[rows[sub], (4 * g + i) * LANES:(4 * g + i + 1) * LANES] for i in range(4)],
                              axis=0) for sub, g in units]
        sk = [c1 * jnp.concatenate(
            [jnp.broadcast_to(sink_ref[4 * g + i:4 * g + i + 1, 0:1], (SWA_QB, 1)) for i in range(4)], axis=0)
            for _, g in units]
        t_c = [_dot_nt(q, kc) * c1 for q in qg]
        m = [jnp.maximum(jnp.max(t, axis=-1, keepdims=True), s) for t, s in zip(t_c, sk)]
        if windowed:
            t_w = [jnp.where(band[sub], _dot_nt(q, kw[sub]) * c1, -jnp.inf) for (sub, _), q in zip(units, qg)]
            m = [jnp.maximum(mm, jnp.max(t, axis=-1, keepdims=True)) for mm, t in zip(m, t_w)]
        e_c = [jnp.exp2(t - mm) for t, mm in zip(t_c, m)]
        l = [jnp.sum(e, axis=-1, keepdims=True) + jnp.exp2(s - mm) for e, s, mm in zip(e_c, sk, m)]
        acc = [_dot(e.astype(BF16), vc) for e in e_c]
        if windowed:
            e_w = [jnp.exp2(t - mm) for t, mm in zip(t_w, m)]
            l = [ll + jnp.sum(e, axis=-1, keepdims=True) for ll, e in zip(l, e_w)]
            acc = [a + _dot(e.astype(BF16), vw[sub]) for a, e, (sub, _) in zip(acc, e_w, units)]
        parts = [a / ll for a, ll in zip(acc, l)]
        for sub in range(SWA_NQ):
            finish(rows[sub], parts[SWA_KV_HEADS * sub:SWA_KV_HEADS * (sub + 1)])

    @pl.when(j < n_ctx_steps)
    def _():
        blocks([None] * SWA_NQ)

    @pl.when(j >= n_ctx_steps)
    def _():
        blocks([(j - n_ctx_steps) * SWA_NQ + sub for sub in range(SWA_NQ)])


def _swa_attention(q, k, v, sink, n_batch, t_ctx, seq):
    rows = t_ctx + seq
    qrows = SWA_QB * SWA_NQ
    assert t_ctx % qrows == 0 and seq % qrows == 0
    nqb = rows // qrows
    sink_tab = jnp.broadcast_to(sink.astype(F32)[:, None], (SWA_HEADS, LANES))
    return pl.pallas_call(
        functools.partial(_swa_body, t_ctx=t_ctx, seq=seq),
        out_shape=jax.ShapeDtypeStruct((n_batch * rows, BRANCH_W), BF16),
        grid=(n_batch, nqb),
        in_specs=[pl.BlockSpec((qrows, 1024), lambda b, j: (b * nqb + j, 0)),
                  pl.BlockSpec((rows, LANES), lambda b, j: (b, 0)),
                  pl.BlockSpec((rows, LANES), lambda b, j: (b, 0)),
                  pl.BlockSpec((SWA_HEADS, LANES), lambda b, j: (0, 0))],
        out_specs=pl.BlockSpec((qrows, BRANCH_W), lambda b, j: (b * nqb + j, 0)),
        compiler_params=_cparams(2),
        name="swa_attention",
    )(q, k, v, sink_tab)


def _merge_body(x_ref, mod_ref, g1_ref, g2_ref, wg_ref, oa_ref, ob_ref, oc_ref, wb_ref, wo_ref,
                rw_ref, rb_ref, xo_ref, h2_ref, lg_ref):
    x = x_ref[...]
    h = (_rms(x, g1_ref[...]) * (1.0 + mod_ref[0, 0, 1:2, :]) + mod_ref[0, 0, 0:1, :]).astype(BF16)
    y = None
    for i, o_ref in enumerate((oa_ref, ob_ref, oc_ref)):
        gate = jax.nn.sigmoid(_dot(h, wg_ref[:, i * D_MODEL:(i + 1) * D_MODEL]))
        yi = gate * _dot(o_ref[...], wb_ref[i])
        y = yi if y is None else y + yi
    xn = x + mod_ref[0, 0, 2:3, :] * _dot(y.astype(BF16), wo_ref[...])
    xo_ref[...] = xn
    h2 = _rms(xn, g2_ref[...]) * (1.0 + mod_ref[0, 0, 4:5, :]) + mod_ref[0, 0, 3:4, :]
    h2_ref[...] = h2.astype(BF16)
    lg_ref[...] = _dot3_nt(rw_ref[...], h2) + rb_ref[...]


def _merge(x, modtab, g1, g2, wg, oa, ob, oc, wb, wo, rwt, rb, nb_per_batch):
    n = x.shape[0]
    row = lambda i: (i, 0)
    full2 = lambda i: (0, 0)
    return pl.pallas_call(
        _merge_body,
        out_shape=[jax.ShapeDtypeStruct((n, D_MODEL), F32), jax.ShapeDtypeStruct((n, D_MODEL), BF16),
                   jax.ShapeDtypeStruct((N_EXPERTS, n), F32)],
        grid=(n // TM,),
        in_specs=[pl.BlockSpec((TM, D_MODEL), row),
                  pl.BlockSpec((1, 1, 6, D_MODEL),
                               lambda i: (i // nb_per_batch, jnp.minimum(i % nb_per_batch, 1), 0, 0)),
                  pl.BlockSpec((1, D_MODEL), full2), pl.BlockSpec((1, D_MODEL), full2),
                  pl.BlockSpec((D_MODEL, N_BRANCH * D_MODEL), full2),
                  pl.BlockSpec((TM, BRANCH_W), row), pl.BlockSpec((TM, BRANCH_W), row),
                  pl.BlockSpec((TM, BRANCH_W), row),
                  pl.BlockSpec((N_BRANCH, BRANCH_W, D_MODEL), lambda i: (0, 0, 0)),
                  pl.BlockSpec((D_MODEL, D_MODEL), full2),
                  pl.BlockSpec((N_EXPERTS, D_MODEL), full2), pl.BlockSpec((N_EXPERTS, 1), full2)],
        out_specs=[pl.BlockSpec((TM, D_MODEL), row), pl.BlockSpec((TM, D_MODEL), row),
                   pl.BlockSpec((N_EXPERTS, TM), lambda i: (0, i))],
        compiler_params=_cparams(),
        name="merge",
    )(x, modtab, g1, g2, wg, oa, ob, oc, wb, wo, rwt, rb)


ROUTE_TB = 1024


def _router_body(lg_ref, tri_ref, e_ref, gate_ref, pos_ref, cnt_ref, run_ref):
    @pl.when(pl.program_id(0) == 0)
    def _():
        run_ref[...] = jnp.zeros_like(run_ref)

    lg = lg_ref[...]
    eid = lax.broadcasted_iota(jnp.int32, lg.shape, 0)
    work = lg
    vals, idxs = [], []
    sel = jnp.zeros(lg.shape, F32)
    for _ in range(TOP_K):
        m = jnp.max(work, axis=0, keepdims=True)
        idx = jnp.min(jnp.where(work == m, eid, N_EXPERTS), axis=0, keepdims=True)
        hit = eid == idx
        sel = jnp.where(hit, 1.0, sel)
        work = jnp.where(hit, -jnp.inf, work)
        vals.append(m)
        idxs.append(idx)
    ex = [jnp.exp(v - vals[0]) for v in vals]
    den = ex[0] + ex[1] + ex[2] + ex[3]
    before = _dot(sel.astype(BF16), tri_ref[...]) + run_ref[:, 0:1]
    for k in range(TOP_K):
        e_ref[k:k + 1, :] = idxs[k]
        gate_ref[k:k + 1, :] = ex[k] / den
        pos_ref[k:k + 1, :] = jnp.sum(jnp.where(eid == idxs[k], before, 0.0), axis=0,
                                      keepdims=True).astype(jnp.int32)
    run_ref[...] = run_ref[...] + jnp.sum(sel, axis=1, keepdims=True)
    cnt_ref[...] = run_ref[...]


def _router(logits_t):
    n = logits_t.shape[1]
    tri = (jnp.arange(ROUTE_TB)[:, None] < jnp.arange(ROUTE_TB)[None, :]).astype(BF16)
    blk = lambda i: (0, i)
    return pl.pallas_call(
        _router_body,
        out_shape=[jax.ShapeDtypeStruct((TOP_K, n), jnp.int32), jax.ShapeDtypeStruct((TOP_K, n), F32),
                   jax.ShapeDtypeStruct((TOP_K, n), jnp.int32), jax.ShapeDtypeStruct((N_EXPERTS, LANES), F32)],
        grid=(n // ROUTE_TB,),
        in_specs=[pl.BlockSpec((N_EXPERTS, ROUTE_TB), blk), pl.BlockSpec((ROUTE_TB, ROUTE_TB), lambda i: (0, 0))],
        out_specs=[pl.BlockSpec((TOP_K, ROUTE_TB), blk), pl.BlockSpec((TOP_K, ROUTE_TB), blk),
                   pl.BlockSpec((TOP_K, ROUTE_TB), blk), pl.BlockSpec((N_EXPERTS, LANES), lambda i: (0, 0))],
        scratch_shapes=[pltpu.VMEM((N_EXPERTS, LANES), F32)],
        compiler_params=_cparams(),
        name="router",
    )(logits_t, tri)


def _expert_body(ib_ref, ie_ref, lo_ref, hi_ref, x_ref, wgu_ref, bgu_ref, wdn_ref, bdn_ref, perm_ref, *rest):
    o_ref, wgu_s, wdn_s = rest[-3:]
    i = pl.program_id(0)
    prev = jnp.maximum(i - 1, 0)

    @pl.when((i == 0) | (ie_ref[i] != ie_ref[prev]))
    def _():
        for m in range(D_EXPERT // LANES):
            t = _dot(wgu_ref[0, 0, :, m * 2 * LANES:(m + 1) * 2 * LANES].astype(BF16), perm_ref[...])
            wgu_s[:, m * LANES:(m + 1) * LANES] = t[:, 0:LANES].astype(BF16)
            wgu_s[:, D_EXPERT + m * LANES:D_EXPERT + (m + 1) * LANES] = t[:, LANES:2 * LANES].astype(BF16)
        wdn_s[...] = wdn_ref[0, 0].astype(BF16)

    lo, hi = lo_ref[i], hi_ref[i]

    @pl.when(hi > lo)
    def _():
        gu = _dot(x_ref[...], wgu_s[...]) + bgu_ref[0]
        g_ = jnp.minimum(gu[:, :D_EXPERT], SWIGLU_LIMIT)
        u_ = jnp.clip(gu[:, D_EXPERT:], -SWIGLU_LIMIT, SWIGLU_LIMIT)
        act = (u_ + 1.0) * (g_ * jax.nn.sigmoid(SWIGLU_ALPHA * g_))
        y = _dot(act.astype(BF16), wdn_s[...]) + bdn_ref[0]
        first = (i == 0) | (ib_ref[i] != ib_ref[prev])
        row = lax.broadcasted_iota(jnp.int32, (MOE_BLK, 1), 0)
        o_ref[...] = jnp.where(first | ((row >= lo) & (row < hi)), y, o_ref[...])


def _experts(items, xs, w_gu, bgu, w_dn, bdn, layer, block0, n_rows, yb_prev):
    n_items = items[0].shape[0]
    perm = np.zeros((2 * LANES, 2 * LANES), np.float32)
    j = np.arange(LANES)
    perm[2 * j, j] = 1.0
    perm[2 * j + 1, LANES + j] = 1.0
    in_specs = [pl.BlockSpec((MOE_BLK, D_MODEL), lambda i, ib, ie, lo, hi: (ib[i] - block0, 0)),
                pl.BlockSpec((1, 1, D_MODEL, 2 * D_EXPERT), lambda i, ib, ie, lo, hi: (layer, ie[i], 0, 0)),
                pl.BlockSpec((1, 1, 2 * D_EXPERT), lambda i, ib, ie, lo, hi: (ie[i], 0, 0)),
                pl.BlockSpec((1, 1, D_EXPERT, D_MODEL), lambda i, ib, ie, lo, hi: (layer, ie[i], 0, 0)),
                pl.BlockSpec((1, 1, D_MODEL), lambda i, ib, ie, lo, hi: (ie[i], 0, 0)),
                pl.BlockSpec((2 * LANES, 2 * LANES), lambda i, ib, ie, lo, hi: (0, 0))]
    args = [*items, xs, w_gu, bgu, w_dn, bdn, jnp.asarray(perm, BF16)]
    aliases = {}
    if yb_prev is not None:
        in_specs.append(pl.BlockSpec(memory_space=pl.ANY))
        aliases = {len(args): 0}
        args.append(yb_prev)
    return pl.pallas_call(
        _expert_body,
        out_shape=jax.ShapeDtypeStruct((n_rows, D_MODEL), F32),
        grid_spec=pltpu.PrefetchScalarGridSpec(
            num_scalar_prefetch=4,
            grid=(n_items,),
            in_specs=in_specs,
            out_specs=pl.BlockSpec((MOE_BLK, D_MODEL), lambda i, ib, ie, lo, hi: (ib[i], 0)),
            scratch_shapes=[pltpu.VMEM((D_MODEL, 2 * D_EXPERT), BF16), pltpu.VMEM((D_EXPERT, D_MODEL), BF16)]),
        input_output_aliases=aliases,
        compiler_params=_cparams(),
        name="experts",
    )(*args)


def _lookup(table, idx):
    ids = jnp.arange(table.shape[0], dtype=jnp.int32)
    return jnp.sum(jnp.where(idx[..., None] == ids, table, 0), axis=-1)


def _work_items(cnt_start, cnt_end, block0, n_blk):
    n_items = n_blk + N_EXPERTS - 1
    b1 = block0 + n_blk
    first = jnp.maximum(cnt_start // MOE_BLK, block0)
    last = jnp.minimum((cnt_end - 1) // MOE_BLK, b1 - 1)
    n_e = jnp.where(cnt_end > cnt_start, jnp.maximum(last - first + 1, 0), 0)
    item_end = jnp.cumsum(n_e)
    item_start = item_end - n_e
    total = item_end[-1]
    ii = jnp.arange(n_items, dtype=jnp.int32)
    valid = ii < total
    e_i = jnp.sum((item_end[None, :] <= jnp.minimum(ii, total - 1)[:, None]).astype(jnp.int32), axis=1)
    blk = jnp.where(valid, _lookup(first, e_i) + ii - _lookup(item_start, e_i), b1 - 1)
    lo = jnp.clip(_lookup(cnt_start, e_i) - blk * MOE_BLK, 0, MOE_BLK)
    hi = jnp.clip(_lookup(cnt_end, e_i) - blk * MOE_BLK, 0, MOE_BLK)
    lo = jnp.where(valid, lo, 0)
    hi = jnp.where(valid, hi, 0)
    return blk.astype(jnp.int32), e_i.astype(jnp.int32), lo.astype(jnp.int32), hi.astype(jnp.int32)


def _moe(h2, logits_t, w_gu, bgu, w_dn, bdn, layer):
    n = h2.shape[0]
    a = n * TOP_K
    assert a % (MOE_BLK * MOE_CHUNKS) == 0
    top_e, gate, pos, cnt = _router(logits_t)
    counts = cnt[:, 0].astype(jnp.int32)
    cnt_end = jnp.cumsum(counts)
    cnt_start = cnt_end - counts
    slot = pos + _lookup(cnt_start, top_e)
    tok = jnp.broadcast_to(jnp.arange(n, dtype=jnp.int32)[None, :], (TOP_K, n))
    _, slot_tok = lax.sort_key_val(slot.reshape(-1), tok.reshape(-1))
    n_blk = a // MOE_BLK // MOE_CHUNKS
    rows_c = n_blk * MOE_BLK
    yb = None
    for c in range(MOE_CHUNKS):
        xs = h2.at[slot_tok[c * rows_c:(c + 1) * rows_c]].get(mode="promise_in_bounds")
        items = _work_items(cnt_start, cnt_end, c * n_blk, n_blk)
        yb = _experts(items, xs, w_gu, bgu, w_dn, bdn, layer, c * n_blk, a, yb)
    y = None
    for k in range(TOP_K):
        yk = yb.at[slot[k]].get(mode="promise_in_bounds") * gate[k][:, None]
        y = yk if y is None else y + yk
    return y


DN_GW = 256
HALO = 8
DN_PAIR = 2


def _split3(a):
    hi = a.astype(BF16)
    r = a - hi.astype(F32)
    lo = r.astype(BF16)
    lo2 = (r - lo.astype(F32)).astype(BF16)
    return hi, lo, lo2


def _bd(x):
    xb = x.astype(BF16)
    t = jnp.concatenate([xb, xb, xb, xb], axis=0)
    r = lax.broadcasted_iota(jnp.int32, (DN_GW, DN_GW), 0) // DN_HEAD_DIM
    c = lax.broadcasted_iota(jnp.int32, (DN_GW, DN_GW), 1) // DN_HEAD_DIM
    return jnp.where(r == c, t, jnp.zeros_like(t))


def _mm(a, wbd):
    return _dot(a.astype(BF16), wbd)


def _tri_inverse(l_mats, eye, blk16, blk32):
    half = DN_CHUNK
    lds = [jnp.where(blk16, l, 0.0) for l in l_mats]
    ps = [_mm(ld, _bd(ld)) for ld in lds]
    xs = [eye - ld for ld in lds]
    for _ in range(2):
        rs = [_mm(jnp.concatenate([p, x], axis=0), _bd(p)) for p, x in zip(ps, xs)]
        ps = [r[0:half] for r in rs]
        xs = [x + r[half:2 * half] for x, r in zip(xs, rs)]
    xs = [x + _mm(x, _bd(p)) for x, p in zip(xs, ps)]
    ts = [_mm(x, _bd(jnp.where(blk32 & jnp.logical_not(blk16), l, 0.0))) for x, l in zip(xs, l_mats)]
    xs = [x - _mm(t, _bd(x)) for x, t in zip(xs, ts)]
    ts = [_mm(x, _bd(jnp.where(blk32, 0.0, l))) for x, l in zip(xs, l_mats)]
    return [x - _mm(t, _bd(x)) for x, t in zip(xs, ts)]


def _dn_prep_body(cur_ref, prev_ref, next_ref, cw_ref, dg_ref, gc_ref, g512_ref, e_ref, tri_ref,
                  u_ref, w_ref, kd_ref, qd_ref, aq_ref, gl_ref,
                  ext_ref, q_s, k_s, v_s, gx_s, bx_s):
    j = pl.program_id(1)
    nbp = pl.num_programs(1)
    C = DN_CHUNK
    ext_ref[HALO:HALO + TM, :] = cur_ref[...]
    ext_ref[0:HALO, :] = jnp.where(j >= 2, prev_ref[...], 0.0)
    ext_ref[HALO + TM:2 * HALO + TM, :] = jnp.where((j >= 1) & (j < nbp - 1), next_ref[...], 0.0)
    y = None
    for t in range(DN_CONV):
        o = HALO - DN_CONV // 2 + t
        term = cw_ref[t:t + 1, :] * ext_ref[o:o + TM, :]
        y = term if y is None else y + term
    y = y * jax.nn.sigmoid(y)

    g512 = g512_ref[...]

    def head_sumsq(x):
        hi, lo = _split_bf16(x * x)
        return _dot(hi, g512) + _dot(lo, g512)

    q = y[:, 0:512]
    k = y[:, 512:1024]
    q_s[...] = q * lax.rsqrt(head_sumsq(q) + NORM_EPS) * (DN_HEAD_DIM ** -0.5)
    k_s[...] = k * lax.rsqrt(head_sumsq(k) + NORM_EPS)
    v_s[...] = y[:, 1024:1536]

    dg = dg_ref[...]
    beta_all = jax.nn.sigmoid(dg)
    z = dg + gc_ref[1:2, :]
    g_all = gc_ref[0:1, :] * (jnp.maximum(z, 0.0) + jnp.log(1.0 + jnp.exp(-jnp.abs(z))))
    bh, bl = _split_bf16(beta_all)
    bcat = jnp.concatenate([bh, bl], axis=0)
    gparts = jnp.concatenate(_split3(g_all), axis=1)
    for d in range(2):
        cs = _dot(tri_ref[d], gparts)
        gcum = cs[:, 0:128] + cs[:, 128:256] + cs[:, 256:384]
        gcat = jnp.concatenate(_split3(gcum), axis=0)
        for grp in range(2):
            eg = _dot(gcat, e_ref[4 + 2 * d + grp])
            gx_s[d, grp] = eg[0:TM] + eg[TM:2 * TM] + eg[2 * TM:3 * TM]
            eb = _dot(bcat, e_ref[2 * d + grp])
            bx_s[d, grp] = eb[0:TM] + eb[TM:2 * TM]

    row = lax.broadcasted_iota(jnp.int32, (C, DN_GW), 0)
    col = lax.broadcasted_iota(jnp.int32, (C, DN_GW), 1) % C
    eye_b = row == col
    eye = jnp.where(eye_b, 1.0, 0.0)
    blk16 = (row // 16) == (col // 16)
    blk32 = (row // 32) == (col // 32)
    incl = (col <= row, col >= row)
    strict = (col < row, col > row)

    def chunk_pair(it, carry):
        units = []
        l_mats = []
        for ci in range(DN_PAIR):
            cc = it * DN_PAIR + ci
            rows = pl.ds(pl.multiple_of(cc * C, C), C)
            for grp in range(2):
                cols = slice(grp * DN_GW, (grp + 1) * DN_GW)
                kg, qg = k_s[rows, cols], q_s[rows, cols]
                kb = [kg * bx_s[d, grp, rows, :] for d in range(2)]
                raw = _dot_nt(jnp.concatenate([kb[0], kb[1], qg], axis=0).astype(BF16), _bd(kg))
                for d in range(2):
                    gx = gx_s[d, grp, rows, :]
                    rvec = jnp.sum(jnp.where(eye_b, gx, 0.0), axis=0, keepdims=True)
                    dec = jnp.where(incl[d], jnp.exp(jnp.where(incl[d], gx - rvec, 0.0)), 0.0)
                    l_mats.append(jnp.where(strict[d], raw[C * d:C * (d + 1)] * dec, 0.0))
                    glast = gx[C - 1:C, :] if d == 0 else gx[0:1, :]
                    kd_ref[d, rows, cols] = (kg * jnp.exp(glast - gx)).astype(BF16)
                    qd_ref[d, rows, cols] = (qg * jnp.exp(gx)).astype(BF16)
                    aq_ref[d, rows, cols] = (raw[2 * C:3 * C] * dec).astype(BF16)
                    gl_ref[d, pl.ds(cc, 1), :, cols] = jnp.exp(glast).reshape(1, 1, DN_GW)
                    units.append((rows, cols, d, grp))
        tinvs = _tri_inverse(l_mats, eye, blk16, blk32)
        for (rows, cols, d, grp), tinv in zip(units, tinvs):
            bx = bx_s[d, grp, rows, :]
            u_ref[d, rows, cols] = _mm(tinv, _bd(v_s[rows, cols] * bx))
            w_ref[d, rows, cols] = _mm(tinv, _bd(k_s[rows, cols] * bx * jnp.exp(gx_s[d, grp, rows, :]))
                                       ).astype(BF16)
        return carry

    lax.fori_loop(0, TM // C // DN_PAIR, chunk_pair, 0)


def _dn_prep(dqkv, dg, conv_w, a_log, dt_bias, n_batch, rows):
    n = dqkv.shape[0]
    nbp = rows // TM
    hb = TM // HALO
    H = DN_HEADS
    cw = jnp.zeros((8, 1536), F32).at[0:DN_CONV].set(conv_w)
    gc = jnp.zeros((8, LANES), F32)
    gc = gc.at[0, 2 * H:4 * H].set(-jnp.exp(a_log.reshape(-1))).at[1, 2 * H:4 * H].set(dt_bias.reshape(-1))
    hid = np.arange(512) // DN_HEAD_DIM
    g512 = jnp.asarray(hid[:, None] == hid[None, :], BF16)
    e = np.zeros((8, LANES, DN_GW), np.float32)
    for kind in range(2):
        for d in range(2):
            for grp in range(2):
                for h in range(4):
                    e[4 * kind + 2 * d + grp, 16 * kind + 8 * d + 4 * grp + h, 64 * h:64 * (h + 1)] = 1.0
    e = jnp.asarray(e, BF16)
    t = np.arange(TM)
    same = (t[:, None] // DN_CHUNK) == (t[None, :] // DN_CHUNK)
    tri = jnp.asarray(np.stack([same & (t[None, :] <= t[:, None]), same & (t[None, :] >= t[:, None])]), BF16)
    blk = lambda b, j: (b * nbp + j, 0)
    full2 = lambda b, j: (0, 0)
    full3 = lambda b, j: (0, 0, 0)
    dblk = lambda b, j: (0, b * nbp + j, 0)
    nlast = n // HALO - 1
    outs = pl.pallas_call(
        _dn_prep_body,
        out_shape=[jax.ShapeDtypeStruct((2, n, 512), F32)] + [jax.ShapeDtypeStruct((2, n, 512), BF16)] * 4
        + [jax.ShapeDtypeStruct((2, n // DN_CHUNK, 1, 512), F32)],
        grid=(n_batch, nbp),
        in_specs=[pl.BlockSpec((TM, 1536), blk),
                  pl.BlockSpec((HALO, 1536), lambda b, j: (jnp.maximum((b * nbp + j) * hb - 1, 0), 0)),
                  pl.BlockSpec((HALO, 1536), lambda b, j: (jnp.minimum((b * nbp + j + 1) * hb, nlast), 0)),
                  pl.BlockSpec((8, 1536), full2), pl.BlockSpec((TM, LANES), blk), pl.BlockSpec((8, LANES), full2),
                  pl.BlockSpec((512, 512), full2), pl.BlockSpec((8, LANES, DN_GW), full3),
                  pl.BlockSpec((2, TM, TM), full3)],
        out_specs=[pl.BlockSpec((2, TM, 512), dblk)] * 5
        + [pl.BlockSpec((2, TM // DN_CHUNK, 1, 512), lambda b, j: (0, b * nbp + j, 0, 0))],
        scratch_shapes=[pltpu.VMEM((TM + 2 * HALO, 1536), F32), pltpu.VMEM((TM, 512), F32),
                        pltpu.VMEM((TM, 512), F32), pltpu.VMEM((TM, 512), F32),
                        pltpu.VMEM((2, 2, TM, DN_GW), F32), pltpu.VMEM((2, 2, TM, DN_GW), F32)],
        compiler_params=_cparams(2),
        name="dn_prep",
    )(dqkv, dqkv, dqkv, cw, dg, gc, g512, e, tri)
    return outs, g512


DN_SCAN_NB = 4


def _dn_scan_body(uf, wf, kf, qf, af, gf, ub, wb, kb, qb, ab, gb, of_ref, ob_ref, s_ref):
    @pl.when(pl.program_id(1) == 0)
    def _():
        s_ref[...] = jnp.zeros_like(s_ref)

    r = lax.broadcasted_iota(jnp.int32, (DN_GW, DN_GW), 0) // DN_HEAD_DIM
    c = lax.broadcasted_iota(jnp.int32, (DN_GW, DN_GW), 1) // DN_HEAD_DIM
    same = r == c
    dirs = ((uf, wf, kf, qf, af, gf, of_ref), (ub, wb, kb, qb, ab, gb, ob_ref))
    units = [(bb, d, grp) for bb in range(DN_SCAN_NB) for d in range(2) for grp in range(2)]
    cols = [slice(grp * DN_GW, (grp + 1) * DN_GW) for grp in range(2)]
    s = [s_ref[bb, d, grp] for bb, d, grp in units]
    sq = [_dot(jnp.concatenate([dirs[d][1][0, bb, :, cols[grp]], dirs[d][3][0, bb, :, cols[grp]]], axis=0),
               st.astype(BF16)) for (bb, d, grp), st in zip(units, s)]
    vnew = [dirs[d][0][0, bb, :, cols[grp]] - r[0:DN_CHUNK] for (bb, d, grp), r in zip(units, sq)]
    intra = [_dot(dirs[d][4][0, bb, :, cols[grp]], _bd(v)) for (bb, d, grp), v in zip(units, vnew)]
    upd = [lax.dot_general(dirs[d][2][0, bb, :, cols[grp]], v.astype(BF16), (((0,), (0,)), ((), ())),
                           preferred_element_type=F32) for (bb, d, grp), v in zip(units, vnew)]
    for (bb, d, grp), st, r, a, up in zip(units, s, sq, intra, upd):
        dirs[d][6][bb, :, cols[grp]] = r[DN_CHUNK:2 * DN_CHUNK] + a
        s_ref[bb, d, grp] = st * dirs[d][5][0, bb, 0, :, cols[grp]] + jnp.where(same, up, 0.0)


def _dn_scan(u, w, kd, qd, aq, gl, n_batch, rows, t_ctx):
    nch = rows // DN_CHUNK
    nctx = t_ctx // DN_CHUNK
    assert n_batch % DN_SCAN_NB == 0

    def cb(c):
        return jnp.where(c < nctx, nctx - 1 - c, nch - 1 - (c - nctx))

    per_batch = lambda a: a.reshape(2, n_batch, rows, 512)
    u, w, kd, qd, aq = map(per_batch, (u, w, kd, qd, aq))
    gl = gl.reshape(2, n_batch, nch, 1, 512)
    blk = (1, DN_SCAN_NB, DN_CHUNK, 512)
    gblk = (1, DN_SCAN_NB, 1, 1, 512)
    specs = []
    for d, ch in ((0, lambda c: c), (1, cb)):
        specs += [pl.BlockSpec(blk, lambda b, c, d=d, ch=ch: (d, b, ch(c), 0))] * 5
        specs += [pl.BlockSpec(gblk, lambda b, c, d=d, ch=ch: (d, b, ch(c), 0, 0))]
    o_f, o_b = pl.pallas_call(
        _dn_scan_body,
        out_shape=[jax.ShapeDtypeStruct((n_batch, rows, 512), F32)] * 2,
        grid=(n_batch // DN_SCAN_NB, nch),
        in_specs=specs,
        out_specs=[pl.BlockSpec((DN_SCAN_NB, DN_CHUNK, 512), lambda b, c: (b, c, 0)),
                   pl.BlockSpec((DN_SCAN_NB, DN_CHUNK, 512), lambda b, c: (b, cb(c), 0))],
        scratch_shapes=[pltpu.VMEM((DN_SCAN_NB, 2, 2, DN_GW, DN_GW), F32)],
        compiler_params=_cparams(2),
        name="dn_scan",
    )(u, w, kd, qd, aq, gl, u, w, kd, qd, aq, gl)
    return o_f.reshape(n_batch * rows, 512), o_b.reshape(n_batch * rows, 512)


def _dn_out_body(of_ref, ob_ref, z_ref, g_ref, g512_ref, o_ref):
    o = of_ref[...] + ob_ref[...]
    hi, lo = _split_bf16(o * o)
    ms = (_dot(hi, g512_ref[...]) + _dot(lo, g512_ref[...])) * (1.0 / DN_HEAD_DIM)
    z = z_ref[...]
    o_ref[...] = (o * lax.rsqrt(ms + NORM_EPS) * g_ref[...] * (z * jax.nn.sigmoid(z))).astype(BF16)


def _dn_out(o_f, o_b, dz, norm_g, g512):
    n = o_f.shape[0]
    row = lambda i: (i, 0)
    g = jnp.tile(norm_g, DN_HEADS).reshape(1, 512)
    return pl.pallas_call(
        _dn_out_body,
        out_shape=jax.ShapeDtypeStruct((n, BRANCH_W), BF16),
        grid=(n // TM,),
        in_specs=[pl.BlockSpec((TM, 512), row)] * 3 + [pl.BlockSpec((1, 512), lambda i: (0, 0)),
                                                       pl.BlockSpec((512, 512), lambda i: (0, 0))],
        out_specs=pl.BlockSpec((TM, BRANCH_W), row),
        compiler_params=_cparams(),
        name="dn_out",
    )(o_f, o_b, dz, g, g512)


def _deltanet(dqkv, dz, dg, conv_w, a_log, dt_bias, norm_g, n_batch, rows, t_ctx):
    (u, w, kd, qd, aq, gl), g512 = _dn_prep(dqkv, dg, conv_w, a_log, dt_bias, n_batch, rows)
    o_f, o_b = _dn_scan(u, w, kd, qd, aq, gl, n_batch, rows, t_ctx)
    return _dn_out(o_f, o_b, dz, norm_g, g512)


def _final_norm_body(x_ref, g_ref, o_ref):
    o_ref[...] = _rms(x_ref[...], g_ref[...])


def _final_norm(x, g):
    n = x.shape[0]
    return pl.pallas_call(
        _final_norm_body,
        out_shape=jax.ShapeDtypeStruct(x.shape, F32),
        grid=(n // TM,),
        in_specs=[pl.BlockSpec((TM, D_MODEL), lambda i: (i, 0)), pl.BlockSpec((1, D_MODEL), lambda i: (0, 0))],
        out_specs=pl.BlockSpec((TM, D_MODEL), lambda i: (i, 0)),
        compiler_params=_cparams(),
        name="final_norm",
    )(x, g)


def _rope_tables(seq, t_ctx, rot_dim):
    rows = seq // GRID_W
    row = jnp.broadcast_to(jnp.arange(rows)[:, None], (rows, GRID_W)).reshape(-1).astype(F32)
    col = jnp.broadcast_to(jnp.arange(GRID_W)[None, :], (rows, GRID_W)).reshape(-1).astype(F32)
    n_freq = rot_dim // 4
    inv_freq = ROPE_THETA ** (-jnp.arange(n_freq, dtype=F32) / n_freq)
    ang = jnp.concatenate([row[:, None] * inv_freq, col[:, None] * inv_freq], axis=-1)
    cos, sin = jnp.cos(ang), jnp.sin(ang)
    zero = jnp.zeros_like(sin)
    c = jnp.concatenate([cos, cos], axis=-1)
    sa = jnp.concatenate([-sin, zero], axis=-1)
    sb = jnp.concatenate([zero, sin], axis=-1)
    ident = jnp.stack([jnp.ones((t_ctx, rot_dim), F32), jnp.zeros((t_ctx, rot_dim), F32),
                       jnp.zeros((t_ctx, rot_dim), F32)])
    tab = jnp.concatenate([ident, jnp.stack([c, sa, sb])], axis=1)
    return jnp.tile(tab, (1, 1, LANES // rot_dim))


def _layer_weights(w_in, w_q_up, w_kv_up, w_branch, w_out, router_w, router_b, b_gu, b_dn):
    sizes = (MLA_Q_RANK, MLA_KV_RANK, MLA_ROPE, 512, 128, 128, 1536, 512, 16, 16, N_BRANCH * D_MODEL)
    offs = np.cumsum((0,) + sizes)
    part = lambda i: w_in[:, offs[i]:offs[i + 1]]
    pad = jnp.zeros((D_MODEL, LANES - 32), F32)
    w_proj = jnp.concatenate(
        [part(0), part(1), part(3), part(4), part(5), part(6), part(7),
         jnp.tile(part(2), (1, LANES // MLA_ROPE)), part(8), part(9), pad], axis=1).astype(BF16)
    w_gate = part(10).astype(BF16)
    hq = MLA_NOPE + MLA_ROPE
    wq = w_q_up.reshape(MLA_Q_RANK, MLA_HEADS, hq)
    wq = jnp.concatenate([wq[:, :, :MLA_NOPE].reshape(MLA_Q_RANK, -1),
                          wq[:, :, MLA_NOPE:].reshape(MLA_Q_RANK, -1)], axis=1).astype(BF16)
    wkv = w_kv_up.reshape(MLA_KV_RANK, MLA_HEADS, MLA_NOPE + MLA_V)
    wkv = jnp.concatenate([wkv[:, :, :MLA_NOPE].reshape(MLA_KV_RANK, -1),
                           wkv[:, :, MLA_NOPE:].reshape(MLA_KV_RANK, -1)], axis=1).astype(BF16)
    bgu = jnp.concatenate([b_gu[:, 0::2], b_gu[:, 1::2]], axis=-1).reshape(N_EXPERTS, 1, 2 * D_EXPERT)
    return dict(w_proj=w_proj, w_gate=w_gate, wq=wq, wkv=wkv, wb=w_branch.astype(BF16),
                wo=w_out.astype(BF16), rwt=router_w.T, rb=router_b.reshape(N_EXPERTS, 1),
                bgu=bgu, bdn=b_dn.reshape(N_EXPERTS, 1, D_MODEL))


def kernel(x, c, ctx, c_ctx, w_mod, b_mod, norm1_g, norm2_g, w_in, mla_q_norm_g, mla_w_q_up,
           mla_kv_norm_g, mla_w_kv_up, swa_sink, dn_conv_w, dn_a_log, dn_dt_bias, dn_norm_g,
           w_branch, w_out, router_w, router_b, exp_w_gu, exp_b_gu, exp_w_dn, exp_b_dn, final_norm_g):
    B, S, D = x.shape
    T = ctx.shape[1]
    R = T + S
    depth = w_mod.shape[0]
    assert D == D_MODEL and T == TM and S % TM == 0 and (B * R) % ROUTE_TB == 0
    nbp = R // TM
    xa = jnp.concatenate([ctx, x], axis=1).reshape(B * R, D)
    rope_s = _rope_tables(S, T, SWA_HEAD_DIM)
    rope_m = _rope_tables(S, T, MLA_ROPE)
    cvec = jnp.concatenate([c, c_ctx[None, :]], axis=0)
    for l in range(depth):
        w = _layer_weights(w_in[l], mla_w_q_up[l], mla_w_kv_up[l], w_branch[l], w_out[l], router_w[l],
                           router_b[l], exp_b_gu[l], exp_b_dn[l])
        mod = _modulation(cvec, w_mod[l], b_mod[l]).reshape(B + 1, 6, D)
        modtab = jnp.stack([jnp.broadcast_to(mod[B][None], (B, 6, D)), mod[:B]], axis=1)
        g1 = norm1_g[l].reshape(1, D)
        g2 = norm2_g[l].reshape(1, D)
        cq, ckv, krt, sq, sk, sv, dqkv, dz, dg = _inproj(xa, modtab, g1, w["w_proj"], rope_s, nbp)
        q_m, k_m, v_m = _mla_prep(cq, ckv, krt, mla_q_norm_g[l].reshape(1, -1), mla_kv_norm_g[l].reshape(1, -1),
                                  w["wq"], w["wkv"], rope_m, nbp)
        o_a = _mla_attention(q_m, k_m, v_m, B, R)
        o_b = _swa_attention(sq, sk, sv, swa_sink[l], B, T, S)
        o_c = _deltanet(dqkv, dz, dg, dn_conv_w[l], dn_a_log[l], dn_dt_bias[l], dn_norm_g[l], B, R, T)
        xn, h2, logits_t = _merge(xa, modtab, g1, g2, w["w_gate"], o_a, o_b, o_c, w["wb"], w["wo"],
                                  w["rwt"], w["rb"], nbp)
        y = _moe(h2, logits_t, exp_w_gu, w["bgu"], exp_w_dn, w["bdn"], l)
        is_ctx = lax.broadcasted_iota(jnp.int32, (1, R, 1), 1) < T
        g2rows = jnp.where(is_ctx, modtab[:, 0, 5][:, None, :], modtab[:, 1, 5][:, None, :])
        xa = (xn.reshape(B, R, D) + g2rows * y.reshape(B, R, D)).reshape(B * R, D)
    out = _final_norm(xa, final_norm_g.reshape(1, D)).reshape(B, R, D)
    return out[:, T:]
```

```python
import functools

import jax
import jax.numpy as jnp
import numpy as np
from jax import lax
from jax.experimental import pallas as pl
from jax.experimental.pallas import tpu as pltpu

F32 = jnp.float32
BF16 = jnp.bfloat16

D_MODEL = 1024
GRID_W = 64
ROPE_THETA = 10000.0
NORM_EPS = 1e-6
MLA_HEADS = 8
MLA_Q_RANK = 384
MLA_KV_RANK = 256
MLA_NOPE = 64
MLA_ROPE = 32
MLA_V = 64
SWA_HEADS = 8
SWA_KV_HEADS = 2
SWA_HEAD_DIM = 64
SWA_WINDOW = 128
DN_HEADS = 8
DN_HEAD_DIM = 64
DN_CONV = 5
DN_CHUNK = 64
N_BRANCH = 3
BRANCH_W = 512
N_EXPERTS = 32
TOP_K = 4
D_EXPERT = 1024
SWIGLU_LIMIT = 7.0
SWIGLU_ALPHA = 1.702
MOE_BLK = 512
MOE_CHUNKS = 4

LANES = 128
LOG2E = 1.4426950408889634
TM = 256
VMEM_LIMIT = 56 * 1024 * 1024

_C_CQ = (0, 384)
_C_CKV = (384, 640)
_C_SQ = (640, 1152)
_C_SK = (1152, 1280)
_C_SV = (1280, 1408)
_C_DQKV = (1408, 2944)
_C_DZ = (2944, 3456)
_C_KRT = (3456, 3584)
_C_DG = (3584, 3712)
_N_PROJ = 3712


def _cparams(n_axes=1):
    return pltpu.CompilerParams(dimension_semantics=("arbitrary",) * n_axes, vmem_limit_bytes=VMEM_LIMIT)


def _dot(a, b):
    return jnp.dot(a, b, preferred_element_type=F32)


def _dot_nt(a, b):
    return lax.dot_general(a, b, (((1,), (1,)), ((), ())), preferred_element_type=F32)


def _split_bf16(a):
    hi = a.astype(BF16)
    lo = (a - hi.astype(F32)).astype(BF16)
    return hi, lo


def _dot3(a, b):
    ah, al = _split_bf16(a)
    bh, bl = _split_bf16(b)
    return _dot(ah, bh) + (_dot(ah, bl) + _dot(al, bh))


def _dot3_nt(a, b):
    ah, al = _split_bf16(a)
    bh, bl = _split_bf16(b)
    return _dot_nt(ah, bh) + (_dot_nt(ah, bl) + _dot_nt(al, bh))


def _rms(x, g):
    return x * lax.rsqrt(jnp.mean(x * x, axis=-1, keepdims=True) + NORM_EPS) * g


def _rope_cols(x, tab_ref, half):
    c, sa, sb = tab_ref[0], tab_ref[1], tab_ref[2]
    return x * c + pltpu.roll(x, LANES - half, axis=1) * sa + pltpu.roll(x, half, axis=1) * sb


def _mod_body(c_ref, w_ref, b_ref, o_ref):
    c = c_ref[...]
    a = c * jax.nn.sigmoid(c)
    o_ref[...] = _dot3(a, w_ref[...]) + b_ref[...]


def _modulation(cvec, w_mod, b_mod):
    m = cvec.shape[0]
    n = w_mod.shape[1]
    tn = 512
    return pl.pallas_call(
        _mod_body,
        out_shape=jax.ShapeDtypeStruct((m, n), F32),
        grid=(n // tn,),
        in_specs=[pl.BlockSpec((m, D_MODEL), lambda j: (0, 0)),
                  pl.BlockSpec((D_MODEL, tn), lambda j: (0, j)),
                  pl.BlockSpec((1, tn), lambda j: (0, j))],
        out_specs=pl.BlockSpec((m, tn), lambda j: (0, j)),
        compiler_params=_cparams(),
        name="modulation",
    )(cvec, w_mod, b_mod.reshape(1, n))


def _inproj_body(x_ref, mod_ref, g_ref, w_ref, rope_ref,
                 cq_ref, ckv_ref, krt_ref, sq_ref, sk_ref, sv_ref, dqkv_ref, dz_ref, dg_ref):
    x = x_ref[...]
    h = (_rms(x, g_ref[...]) * (1.0 + mod_ref[0, 0, 1:2, :]) + mod_ref[0, 0, 0:1, :]).astype(BF16)

    def proj(cols):
        return _dot(h, w_ref[:, cols[0]:cols[1]])

    cq_ref[...] = proj(_C_CQ)
    ckv_ref[...] = proj(_C_CKV)
    krt_ref[...] = proj(_C_KRT)
    dqkv_ref[...] = proj(_C_DQKV)
    dz_ref[...] = proj(_C_DZ)
    dg_ref[...] = proj(_C_DG)
    sv_ref[...] = proj(_C_SV).astype(BF16)
    sk_ref[...] = _rope_cols(proj(_C_SK), rope_ref, SWA_HEAD_DIM // 2).astype(BF16)
    sq = proj(_C_SQ)
    lane = lax.broadcasted_iota(jnp.int32, (TM, LANES), 1)
    lo = lane < SWA_HEAD_DIM
    for c in range(4):
        xr = _rope_cols(sq[:, c * LANES:(c + 1) * LANES], rope_ref, SWA_HEAD_DIM // 2)
        xs = pltpu.roll(xr, SWA_HEAD_DIM, axis=1)
        if c < 2:
            a, b = jnp.where(lo, xr, 0.0), jnp.where(lo, xs, 0.0)
        else:
            a, b = jnp.where(lo, 0.0, xs), jnp.where(lo, 0.0, xr)
        sq_ref[:, (2 * c) * LANES:(2 * c + 1) * LANES] = a.astype(BF16)
        sq_ref[:, (2 * c + 1) * LANES:(2 * c + 2) * LANES] = b.astype(BF16)


def _inproj(x, modtab, g, w, rope_s, nb_per_batch):
    n = x.shape[0]
    row = lambda i: (i, 0)
    widths = [(384, F32), (256, F32), (128, F32), (1024, BF16), (128, BF16), (128, BF16),
              (1536, F32), (512, F32), (128, F32)]
    return pl.pallas_call(
        _inproj_body,
        out_shape=[jax.ShapeDtypeStruct((n, wd), dt) for wd, dt in widths],
        grid=(n // TM,),
        in_specs=[pl.BlockSpec((TM, D_MODEL), row),
                  pl.BlockSpec((1, 1, 6, D_MODEL),
                               lambda i: (i // nb_per_batch, jnp.minimum(i % nb_per_batch, 1), 0, 0)),
                  pl.BlockSpec((1, D_MODEL), lambda i: (0, 0)),
                  pl.BlockSpec((D_MODEL, _N_PROJ), lambda i: (0, 0)),
                  pl.BlockSpec((3, TM, LANES), lambda i: (0, i % nb_per_batch, 0))],
        out_specs=[pl.BlockSpec((TM, wd), row) for wd, _ in widths],
        compiler_params=_cparams(),
        name="inproj",
    )(x, modtab, g, w, rope_s)


def _mla_prep_body(cq_ref, ckv_ref, krt_ref, qg_ref, kvg_ref, wq_ref, wkv_ref, rope_ref,
                   q_ref, k_ref, v_ref):
    half = MLA_ROPE // 2
    qn = _rms(cq_ref[...], qg_ref[...]).astype(BF16)
    q = _dot(qn, wq_ref[...])
    q_ref[:, 0:512] = q[:, 0:512].astype(BF16)
    for c in range(2):
        lo_, hi_ = 512 + c * LANES, 512 + (c + 1) * LANES
        q_ref[:, lo_:hi_] = _rope_cols(q[:, lo_:hi_], rope_ref, half).astype(BF16)
    kvn = _rms(ckv_ref[...], kvg_ref[...]).astype(BF16)
    kv = _dot(kvn, wkv_ref[...])
    kr = _rope_cols(krt_ref[...], rope_ref, half).astype(BF16)
    for c in range(4):
        k_ref[:, (2 * c) * LANES:(2 * c + 1) * LANES] = kv[:, c * LANES:(c + 1) * LANES].astype(BF16)
        k_ref[:, (2 * c + 1) * LANES:(2 * c + 2) * LANES] = kr
    v_ref[...] = kv[:, 512:1024].astype(BF16)


def _mla_prep(cq, ckv, krt, qg, kvg, wq, wkv, rope_m, nb_per_batch):
    n = cq.shape[0]
    row = lambda i: (i, 0)
    full = lambda i: (0, 0)
    return pl.pallas_call(
        _mla_prep_body,
        out_shape=[jax.ShapeDtypeStruct((n, 768), BF16), jax.ShapeDtypeStruct((n, 1024), BF16),
                   jax.ShapeDtypeStruct((n, 512), BF16)],
        grid=(n // TM,),
        in_specs=[pl.BlockSpec((TM, 384), row), pl.BlockSpec((TM, 256), row), pl.BlockSpec((TM, 128), row),
                  pl.BlockSpec((1, 384), full), pl.BlockSpec((1, 256), full),
                  pl.BlockSpec((384, 768), full), pl.BlockSpec((256, 1024), full),
                  pl.BlockSpec((3, TM, LANES), lambda i: (0, i % nb_per_batch, 0))],
        out_specs=[pl.BlockSpec((TM, 768), row), pl.BlockSpec((TM, 1024), row), pl.BlockSpec((TM, 512), row)],
        compiler_params=_cparams(),
        name="mla_prep",
    )(cq, ckv, krt, qg, kvg, wq, wkv, rope_m)


def _mla_attn_body(q_ref, k_ref, v_ref, o_ref, *, t_ctx):
    c1 = (MLA_NOPE + MLA_ROPE) ** -0.5 * LOG2E
    lane = lax.broadcasted_iota(jnp.int32, (TM, LANES), 1)
    zero = jnp.zeros((TM, LANES), BF16)

    def attend(nk):
        for c in range(4):
            kc = k_ref[0:nk, c * 2 * LANES:(c + 1) * 2 * LANES]
            vc = v_ref[0:nk, c * LANES:(c + 1) * LANES]
            qn = q_ref[:, c * LANES:(c + 1) * LANES]
            qr = q_ref[:, 512 + (c // 2) * LANES:512 + (c // 2 + 1) * LANES]
            qa = [jnp.concatenate(
                [jnp.where((lane >= 64 * s) & (lane < 64 * (s + 1)), qn, zero),
                 jnp.where((lane >= 32 * ((2 * c + s) % 4)) & (lane < 32 * ((2 * c + s) % 4 + 1)), qr, zero)],
                axis=1) for s in range(2)]
            sc = [_dot_nt(q, kc) * c1 for q in qa]
            m = [jnp.max(t, axis=-1, keepdims=True) for t in sc]
            e = [jnp.exp2(t - mm) for t, mm in zip(sc, m)]
            l = [jnp.sum(t, axis=-1, keepdims=True) for t in e]
            outs = [_dot(t.astype(BF16), vc) / ll for t, ll in zip(e, l)]
            o_ref[:, c * LANES:(c + 1) * LANES] = jnp.where(lane < 64, outs[0], outs[1]).astype(BF16)

    j = pl.program_id(1)

    @pl.when(j == 0)
    def _():
        attend(t_ctx)

    @pl.when(j > 0)
    def _():
        attend(k_ref.shape[0])


def _mla_attention(q, k, v, n_batch, rows):
    nbp = rows // TM
    return pl.pallas_call(
        functools.partial(_mla_attn_body, t_ctx=TM),
        out_shape=jax.ShapeDtypeStruct((n_batch * rows, BRANCH_W), BF16),
        grid=(n_batch, nbp),
        in_specs=[pl.BlockSpec((TM, 768), lambda b, j: (b * nbp + j, 0)),
                  pl.BlockSpec((rows, 1024), lambda b, j: (b, 0)),
                  pl.BlockSpec((rows, 512), lambda b, j: (b, 0))],
        out_specs=pl.BlockSpec((TM, BRANCH_W), lambda b, j: (b * nbp + j, 0)),
        compiler_params=_cparams(2),
        name="mla_attention",
    )(q, k, v)


SWA_QB = 128
SWA_NQ = 2


def _swa_body(q_ref, k_ref, v_ref, sink_ref, o_ref, *, t_ctx, seq):
    c1 = SWA_HEAD_DIM ** -0.5 * LOG2E
    j = pl.program_id(1)
    n_ctx_steps = t_ctx // (SWA_QB * SWA_NQ)
    lane = lax.broadcasted_iota(jnp.int32, (SWA_QB, LANES), 1)
    lo = lane < SWA_HEAD_DIM
    win = 3 * SWA_QB

    def finish(rows, parts):
        for c in range(4):
            g = c // 2
            r0 = (2 * (c % 2)) * SWA_QB
            a = parts[g][r0:r0 + SWA_QB]
            b = parts[g][r0 + SWA_QB:r0 + 2 * SWA_QB]
            if g == 0:
                col = jnp.where(lo, a, pltpu.roll(b, SWA_HEAD_DIM, axis=1))
            else:
                col = jnp.where(lo, pltpu.roll(a, SWA_HEAD_DIM, axis=1), b)
            o_ref[rows, c * LANES:(c + 1) * LANES] = col.astype(BF16)

    def blocks(blks):
        kc, vc = k_ref[0:t_ctx, :], v_ref[0:t_ctx, :]
        windowed = blks[0] is not None
        kw, vw, band = [], [], []
        for blk in blks if windowed else ():
            start = jnp.clip((blk - 1) * SWA_QB, 0, seq - win)
            rs = pl.multiple_of(t_ctx + start, SWA_QB)
            kw.append(k_ref[pl.ds(rs, win), :])
            vw.append(v_ref[pl.ds(rs, win), :])
            qpos = blk * SWA_QB + lax.broadcasted_iota(jnp.int32, (SWA_QB, win), 0)
            kpos = start + lax.broadcasted_iota(jnp.int32, (SWA_QB, win), 1)
            band1 = jnp.abs(kpos - qpos) <= SWA_WINDOW
            band.append(jnp.concatenate([band1] * 4, axis=0))
        units = [(sub, g) for sub in range(SWA_NQ) for g in range(SWA_KV_HEADS)]
        rows = [slice(sub * SWA_QB, (sub + 1) * SWA_QB) for sub in range(SWA_NQ)]
        qg = [jnp.concatenate([q_ref[rows[sub], (4 * g + i) * LANES:(4 * g + i + 1) * LANES] for i in range(4)],
                              axis=0) for sub, g in units]
        sk = [c1 * jnp.concatenate(
            [jnp.broadcast_to(sink_ref[4 * g + i:4 * g + i + 1, 0:1], (SWA_QB, 1)) for i in range(4)], axis=0)
            for _, g in units]
        t_c = [_dot_nt(q, kc) * c1 for q in qg]
        m = [jnp.maximum(jnp.max(t, axis=-1, keepdims=True), s) for t, s in zip(t_c, sk)]
        if windowed:
            t_w = [jnp.where(band[sub], _dot_nt(q, kw[sub]) * c1, -jnp.inf) for (sub, _), q in zip(units, qg)]
            m = [jnp.maximum(mm, jnp.max(t, axis=-1, keepdims=True)) for mm, t in zip(m, t_w)]
        e_c = [jnp.exp2(t - mm) for t, mm in zip(t_c, m)]
        l = [jnp.sum(e, axis=-1, keepdims=True) + jnp.exp2(s - mm) for e, s, mm in zip(e_c, sk, m)]
        acc = [_dot(e.astype(BF16), vc) for e in e_c]
        if windowed:
            e_w = [jnp.exp2(t - mm) for t, mm in zip(t_w, m)]
            l = [ll + jnp.sum(e, axis=-1, keepdims=True) for ll, e in zip(l, e_w)]
            acc = [a + _dot(e.astype(BF16), vw[sub]) for a, e, (sub, _) in zip(acc, e_w, units)]
        parts = [a / ll for a, ll in zip(acc, l)]
        for sub in range(SWA_NQ):
            finish(rows[sub], parts[SWA_KV_HEADS * sub:SWA_KV_HEADS * (sub + 1)])

    @pl.when(j < n_ctx_steps)
    def _():
        blocks([None] * SWA_NQ)

    @pl.when(j >= n_ctx_steps)
    def _():
        blocks([(j - n_ctx_steps) * SWA_NQ + sub for sub in range(SWA_NQ)])


def _swa_attention(q, k, v, sink, n_batch, t_ctx, seq):
    rows = t_ctx + seq
    qrows = SWA_QB * SWA_NQ
    assert t_ctx % qrows == 0 and seq % qrows == 0
    nqb = rows // qrows
    sink_tab = jnp.broadcast_to(sink.astype(F32)[:, None], (SWA_HEADS, LANES))
    return pl.pallas_call(
        functools.partial(_swa_body, t_ctx=t_ctx, seq=seq),
        out_shape=jax.ShapeDtypeStruct((n_batch * rows, BRANCH_W), BF16),
        grid=(n_batch, nqb),
        in_specs=[pl.BlockSpec((qrows, 1024), lambda b, j: (b * nqb + j, 0)),
                  pl.BlockSpec((rows, LANES), lambda b, j: (b, 0)),
                  pl.BlockSpec((rows, LANES), lambda b, j: (b, 0)),
                  pl.BlockSpec((SWA_HEADS, LANES), lambda b, j: (0, 0))],
        out_specs=pl.BlockSpec((qrows, BRANCH_W), lambda b, j: (b * nqb + j, 0)),
        compiler_params=_cparams(2),
        name="swa_attention",
    )(q, k, v, sink_tab)


def _merge_body(x_ref, mod_ref, g1_ref, g2_ref, wg_ref, oa_ref, ob_ref, oc_ref, wb_ref, wo_ref,
                rw_ref, rb_ref, xo_ref, h2_ref, lg_ref):
    x = x_ref[...]
    h = (_rms(x, g1_ref[...]) * (1.0 + mod_ref[0, 0, 1:2, :]) + mod_ref[0, 0, 0:1, :]).astype(BF16)
    y = None
    for i, o_ref in enumerate((oa_ref, ob_ref, oc_ref)):
        gate = jax.nn.sigmoid(_dot(h, wg_ref[:, i * D_MODEL:(i + 1) * D_MODEL]))
        yi = gate * _dot(o_ref[...], wb_ref[i])
        y = yi if y is None else y + yi
    xn = x + mod_ref[0, 0, 2:3, :] * _dot(y.astype(BF16), wo_ref[...])
    xo_ref[...] = xn
    h2 = _rms(xn, g2_ref[...]) * (1.0 + mod_ref[0, 0, 4:5, :]) + mod_ref[0, 0, 3:4, :]
    h2_ref[...] = h2.astype(BF16)
    lg_ref[...] = _dot3_nt(rw_ref[...], h2) + rb_ref[...]


def _merge(x, modtab, g1, g2, wg, oa, ob, oc, wb, wo, rwt, rb, nb_per_batch, latent_only):
    if latent_only:
        nlat = nb_per_batch - 1
        n = x.shape[0] // nb_per_batch * nlat
        row = lambda i: ((i // nlat) * nb_per_batch + i % nlat + 1, 0)
        mod_idx = lambda i: (i // nlat, 1, 0, 0)
    else:
        n = x.shape[0]
        row = lambda i: (i, 0)
        mod_idx = lambda i: (i // nb_per_batch, jnp.minimum(i % nb_per_batch, 1), 0, 0)
    orow = lambda i: (i, 0)
    full2 = lambda i: (0, 0)
    return pl.pallas_call(
        _merge_body,
        out_shape=[jax.ShapeDtypeStruct((n, D_MODEL), F32), jax.ShapeDtypeStruct((n, D_MODEL), BF16),
                   jax.ShapeDtypeStruct((N_EXPERTS, n), F32)],
        grid=(n // TM,),
        in_specs=[pl.BlockSpec((TM, D_MODEL), row),
                  pl.BlockSpec((1, 1, 6, D_MODEL), mod_idx),
                  pl.BlockSpec((1, D_MODEL), full2), pl.BlockSpec((1, D_MODEL), full2),
                  pl.BlockSpec((D_MODEL, N_BRANCH * D_MODEL), full2),
                  pl.BlockSpec((TM, BRANCH_W), row), pl.BlockSpec((TM, BRANCH_W), row),
                  pl.BlockSpec((TM, BRANCH_W), row),
                  pl.BlockSpec((N_BRANCH, BRANCH_W, D_MODEL), lambda i: (0, 0, 0)),
                  pl.BlockSpec((D_MODEL, D_MODEL), full2),
                  pl.BlockSpec((N_EXPERTS, D_MODEL), full2), pl.BlockSpec((N_EXPERTS, 1), full2)],
        out_specs=[pl.BlockSpec((TM, D_MODEL), orow), pl.BlockSpec((TM, D_MODEL), orow),
                   pl.BlockSpec((N_EXPERTS, TM), lambda i: (0, i))],
        compiler_params=_cparams(),
        name="merge",
    )(x, modtab, g1, g2, wg, oa, ob, oc, wb, wo, rwt, rb)


ROUTE_TB = 1024


def _router_body(lg_ref, tri_ref, e_ref, gate_ref, pos_ref, cnt_ref, run_ref):
    @pl.when(pl.program_id(0) == 0)
    def _():
        run_ref[...] = jnp.zeros_like(run_ref)

    lg = lg_ref[...]
    eid = lax.broadcasted_iota(jnp.int32, lg.shape, 0)
    work = lg
    vals, idxs = [], []
    sel = jnp.zeros(lg.shape, F32)
    for _ in range(TOP_K):
        m = jnp.max(work, axis=0, keepdims=True)
        idx = jnp.min(jnp.where(work == m, eid, N_EXPERTS), axis=0, keepdims=True)
        hit = eid == idx
        sel = jnp.where(hit, 1.0, sel)
        work = jnp.where(hit, -jnp.inf, work)
        vals.append(m)
        idxs.append(idx)
    ex = [jnp.exp(v - vals[0]) for v in vals]
    den = ex[0] + ex[1] + ex[2] + ex[3]
    before = _dot(sel.astype(BF16), tri_ref[...]) + run_ref[:, 0:1]
    for k in range(TOP_K):
        e_ref[k:k + 1, :] = idxs[k]
        gate_ref[k:k + 1, :] = ex[k] / den
        pos_ref[k:k + 1, :] = jnp.sum(jnp.where(eid == idxs[k], before, 0.0), axis=0,
                                      keepdims=True).astype(jnp.int32)
    run_ref[...] = run_ref[...] + jnp.sum(sel, axis=1, keepdims=True)
    cnt_ref[...] = run_ref[...]


def _router(logits_t):
    n = logits_t.shape[1]
    tri = (jnp.arange(ROUTE_TB)[:, None] < jnp.arange(ROUTE_TB)[None, :]).astype(BF16)
    blk = lambda i: (0, i)
    return pl.pallas_call(
        _router_body,
        out_shape=[jax.ShapeDtypeStruct((TOP_K, n), jnp.int32), jax.ShapeDtypeStruct((TOP_K, n), F32),
                   jax.ShapeDtypeStruct((TOP_K, n), jnp.int32), jax.ShapeDtypeStruct((N_EXPERTS, LANES), F32)],
        grid=(n // ROUTE_TB,),
        in_specs=[pl.BlockSpec((N_EXPERTS, ROUTE_TB), blk), pl.BlockSpec((ROUTE_TB, ROUTE_TB), lambda i: (0, 0))],
        out_specs=[pl.BlockSpec((TOP_K, ROUTE_TB), blk), pl.BlockSpec((TOP_K, ROUTE_TB), blk),
                   pl.BlockSpec((TOP_K, ROUTE_TB), blk), pl.BlockSpec((N_EXPERTS, LANES), lambda i: (0, 0))],
        scratch_shapes=[pltpu.VMEM((N_EXPERTS, LANES), F32)],
        compiler_params=_cparams(),
        name="router",
    )(logits_t, tri)


def _expert_body(ib_ref, ie_ref, lo_ref, hi_ref, x_ref, wgu_ref, bgu_ref, wdn_ref, bdn_ref, perm_ref, *rest):
    o_ref, wgu_s, wdn_s = rest[-3:]
    i = pl.program_id(0)
    prev = jnp.maximum(i - 1, 0)

    @pl.when((i == 0) | (ie_ref[i] != ie_ref[prev]))
    def _():
        for m in range(D_EXPERT // LANES):
            t = _dot(wgu_ref[0, 0, :, m * 2 * LANES:(m + 1) * 2 * LANES].astype(BF16), perm_ref[...])
            wgu_s[:, m * LANES:(m + 1) * LANES] = t[:, 0:LANES].astype(BF16)
            wgu_s[:, D_EXPERT + m * LANES:D_EXPERT + (m + 1) * LANES] = t[:, LANES:2 * LANES].astype(BF16)
        wdn_s[...] = wdn_ref[0, 0].astype(BF16)

    lo, hi = lo_ref[i], hi_ref[i]

    @pl.when(hi > lo)
    def _():
        gu = _dot(x_ref[...], wgu_s[...]) + bgu_ref[0]
        g_ = jnp.minimum(gu[:, :D_EXPERT], SWIGLU_LIMIT)
        u_ = jnp.clip(gu[:, D_EXPERT:], -SWIGLU_LIMIT, SWIGLU_LIMIT)
        act = (u_ + 1.0) * (g_ * jax.nn.sigmoid(SWIGLU_ALPHA * g_))
        y = _dot(act.astype(BF16), wdn_s[...]) + bdn_ref[0]
        first = (i == 0) | (ib_ref[i] != ib_ref[prev])
        row = lax.broadcasted_iota(jnp.int32, (MOE_BLK, 1), 0)
        o_ref[...] = jnp.where(first | ((row >= lo) & (row < hi)), y, o_ref[...])


def _experts(items, xs, w_gu, bgu, w_dn, bdn, layer, block0, n_rows, yb_prev):
    n_items = items[0].shape[0]
    perm = np.zeros((2 * LANES, 2 * LANES), np.float32)
    j = np.arange(LANES)
    perm[2 * j, j] = 1.0
    perm[2 * j + 1, LANES + j] = 1.0
    in_specs = [pl.BlockSpec((MOE_BLK, D_MODEL), lambda i, ib, ie, lo, hi: (ib[i] - block0, 0)),
                pl.BlockSpec((1, 1, D_MODEL, 2 * D_EXPERT), lambda i, ib, ie, lo, hi: (layer, ie[i], 0, 0)),
                pl.BlockSpec((1, 1, 2 * D_EXPERT), lambda i, ib, ie, lo, hi: (ie[i], 0, 0)),
                pl.BlockSpec((1, 1, D_EXPERT, D_MODEL), lambda i, ib, ie, lo, hi: (layer, ie[i], 0, 0)),
                pl.BlockSpec((1, 1, D_MODEL), lambda i, ib, ie, lo, hi: (ie[i], 0, 0)),
                pl.BlockSpec((2 * LANES, 2 * LANES), lambda i, ib, ie, lo, hi: (0, 0))]
    args = [*items, xs, w_gu, bgu, w_dn, bdn, jnp.asarray(perm, BF16)]
    aliases = {}
    if yb_prev is not None:
        in_specs.append(pl.BlockSpec(memory_space=pl.ANY))
        aliases = {len(args): 0}
        args.append(yb_prev)
    return pl.pallas_call(
        _expert_body,
        out_shape=jax.ShapeDtypeStruct((n_rows, D_MODEL), F32),
        grid_spec=pltpu.PrefetchScalarGridSpec(
            num_scalar_prefetch=4,
            grid=(n_items,),
            in_specs=in_specs,
            out_specs=pl.BlockSpec((MOE_BLK, D_MODEL), lambda i, ib, ie, lo, hi: (ib[i], 0)),
            scratch_shapes=[pltpu.VMEM((D_MODEL, 2 * D_EXPERT), BF16), pltpu.VMEM((D_EXPERT, D_MODEL), BF16)]),
        input_output_aliases=aliases,
        compiler_params=_cparams(),
        name="experts",
    )(*args)


def _lookup(table, idx):
    ids = jnp.arange(table.shape[0], dtype=jnp.int32)
    return jnp.sum(jnp.where(idx[..., None] == ids, table, 0), axis=-1)


def _work_items(cnt_start, cnt_end, block0, n_blk):
    n_items = n_blk + N_EXPERTS - 1
    b1 = block0 + n_blk
    first = jnp.maximum(cnt_start // MOE_BLK, block0)
    last = jnp.minimum((cnt_end - 1) // MOE_BLK, b1 - 1)
    n_e = jnp.where(cnt_end > cnt_start, jnp.maximum(last - first + 1, 0), 0)
    item_end = jnp.cumsum(n_e)
    item_start = item_end - n_e
    total = item_end[-1]
    ii = jnp.arange(n_items, dtype=jnp.int32)
    valid = ii < total
    e_i = jnp.sum((item_end[None, :] <= jnp.minimum(ii, total - 1)[:, None]).astype(jnp.int32), axis=1)
    blk = jnp.where(valid, _lookup(first, e_i) + ii - _lookup(item_start, e_i), b1 - 1)
    lo = jnp.clip(_lookup(cnt_start, e_i) - blk * MOE_BLK, 0, MOE_BLK)
    hi = jnp.clip(_lookup(cnt_end, e_i) - blk * MOE_BLK, 0, MOE_BLK)
    lo = jnp.where(valid, lo, 0)
    hi = jnp.where(valid, hi, 0)
    return blk.astype(jnp.int32), e_i.astype(jnp.int32), lo.astype(jnp.int32), hi.astype(jnp.int32)


def _moe(h2, logits_t, w_gu, bgu, w_dn, bdn, layer):
    n = h2.shape[0]
    a = n * TOP_K
    assert a % (MOE_BLK * MOE_CHUNKS) == 0
    top_e, gate, pos, cnt = _router(logits_t)
    counts = cnt[:, 0].astype(jnp.int32)
    cnt_end = jnp.cumsum(counts)
    cnt_start = cnt_end - counts
    slot = pos + _lookup(cnt_start, top_e)
    tok = jnp.broadcast_to(jnp.arange(n, dtype=jnp.int32)[None, :], (TOP_K, n))
    _, slot_tok = lax.sort_key_val(slot.reshape(-1), tok.reshape(-1))
    n_blk = a // MOE_BLK // MOE_CHUNKS
    rows_c = n_blk * MOE_BLK
    yb = None
    for c in range(MOE_CHUNKS):
        xs = h2.at[slot_tok[c * rows_c:(c + 1) * rows_c]].get(mode="promise_in_bounds")
        items = _work_items(cnt_start, cnt_end, c * n_blk, n_blk)
        yb = _experts(items, xs, w_gu, bgu, w_dn, bdn, layer, c * n_blk, a, yb)
    y = None
    for k in range(TOP_K):
        yk = yb.at[slot[k]].get(mode="promise_in_bounds") * gate[k][:, None]
        y = yk if y is None else y + yk
    return y


DN_GW = 256
HALO = 8
DN_PAIR = 2
_M_EYE, _M_BLK16, _M_OFF32, _M_OFF64, _M_INCL, _M_STRICT = 0, 1, 2, 3, 4, 6


def _dn_masks():
    row = np.arange(DN_CHUNK)[:, None]
    col = np.arange(DN_GW)[None, :] % DN_CHUNK
    b16 = (row // 16) == (col // 16)
    b32 = (row // 32) == (col // 32)
    m = np.stack([row == col, b16, b32 & ~b16, ~b32, col <= row, col >= row, col < row, col > row])
    hid = np.arange(DN_GW) // DN_HEAD_DIM
    return jnp.asarray(m, F32), jnp.asarray(hid[:, None] == hid[None, :], BF16)


def _split3(a):
    hi = a.astype(BF16)
    r = a - hi.astype(F32)
    lo = r.astype(BF16)
    lo2 = (r - lo.astype(F32)).astype(BF16)
    return hi, lo, lo2


def _bd(x, bdm_ref):
    xb = x.astype(BF16)
    return jnp.concatenate([xb, xb, xb, xb], axis=0) * bdm_ref[...]


def _mm(a, wbd):
    return _dot(a.astype(BF16), wbd)


def _tri_inverse(l_mats, m_ref, bdm_ref):
    half = DN_CHUNK
    bd = lambda v: _bd(v, bdm_ref)
    lds = [l * m_ref[_M_BLK16] for l in l_mats]
    ps = [_mm(ld, bd(ld)) for ld in lds]
    xs = [m_ref[_M_EYE] - ld for ld in lds]
    for _ in range(2):
        rs = [_mm(jnp.concatenate([p, x], axis=0), bd(p)) for p, x in zip(ps, xs)]
        ps = [r[0:half] for r in rs]
        xs = [x + r[half:2 * half] for x, r in zip(xs, rs)]
    xs = [x + _mm(x, bd(p)) for x, p in zip(xs, ps)]
    ts = [_mm(x, bd(l * m_ref[_M_OFF32])) for x, l in zip(xs, l_mats)]
    xs = [x - _mm(t, bd(x)) for x, t in zip(xs, ts)]
    ts = [_mm(x, bd(l * m_ref[_M_OFF64])) for x, l in zip(xs, l_mats)]
    return [x - _mm(t, bd(x)) for x, t in zip(xs, ts)]


def _dn_prep_body(cur_ref, prev_ref, next_ref, cw_ref, dg_ref, gc_ref, g512_ref, e_ref, tri_ref, m_ref, bdm_ref,
                  u_ref, w_ref, kd_ref, qd_ref, aq_ref, gl_ref,
                  ext_ref, q_s, k_s, v_s, gx_s, bx_s):
    j = pl.program_id(1)
    nbp = pl.num_programs(1)
    C = DN_CHUNK
    ext_ref[HALO:HALO + TM, :] = cur_ref[...]
    ext_ref[0:HALO, :] = jnp.where(j >= 2, prev_ref[...], 0.0)
    ext_ref[HALO + TM:2 * HALO + TM, :] = jnp.where((j >= 1) & (j < nbp - 1), next_ref[...], 0.0)
    y = None
    for t in range(DN_CONV):
        o = HALO - DN_CONV // 2 + t
        term = cw_ref[t:t + 1, :] * ext_ref[o:o + TM, :]
        y = term if y is None else y + term
    y = y * jax.nn.sigmoid(y)

    g512 = g512_ref[...]

    def head_sumsq(x):
        hi, lo = _split_bf16(x * x)
        return _dot(hi, g512) + _dot(lo, g512)

    q = y[:, 0:512]
    k = y[:, 512:1024]
    q_s[...] = q * lax.rsqrt(head_sumsq(q) + NORM_EPS) * (DN_HEAD_DIM ** -0.5)
    k_s[...] = k * lax.rsqrt(head_sumsq(k) + NORM_EPS)
    v_s[...] = y[:, 1024:1536]

    dg = dg_ref[...]
    beta_all = jax.nn.sigmoid(dg)
    z = dg + gc_ref[1:2, :]
    g_all = gc_ref[0:1, :] * (jnp.maximum(z, 0.0) + jnp.log(1.0 + jnp.exp(-jnp.abs(z))))
    bh, bl = _split_bf16(beta_all)
    bcat = jnp.concatenate([bh, bl], axis=0)
    gparts = jnp.concatenate(_split3(g_all), axis=1)
    for d in range(2):
        cs = _dot(tri_ref[d], gparts)
        gcum = cs[:, 0:128] + cs[:, 128:256] + cs[:, 256:384]
        gcat = jnp.concatenate(_split3(gcum), axis=0)
        for grp in range(2):
            eg = _dot(gcat, e_ref[4 + 2 * d + grp])
            gx_s[d, grp] = eg[0:TM] + eg[TM:2 * TM] + eg[2 * TM:3 * TM]
            eb = _dot(bcat, e_ref[2 * d + grp])
            bx_s[d, grp] = eb[0:TM] + eb[TM:2 * TM]

    def chunk_pair(it, carry):
        units = []
        l_mats = []
        for ci in range(DN_PAIR):
            cc = it * DN_PAIR + ci
            rows = pl.ds(pl.multiple_of(cc * C, C), C)
            for grp in range(2):
                cols = slice(grp * DN_GW, (grp + 1) * DN_GW)
                kg, qg = k_s[rows, cols], q_s[rows, cols]
                kb = [kg * bx_s[d, grp, rows, :] for d in range(2)]
                raw = _dot_nt(jnp.concatenate([kb[0], kb[1], qg], axis=0).astype(BF16), _bd(kg, bdm_ref))
                for d in range(2):
                    gx = gx_s[d, grp, rows, :]
                    rvec = jnp.sum(gx * m_ref[_M_EYE], axis=0, keepdims=True)
                    dec = jnp.exp(jnp.minimum(gx - rvec, 0.0)) * m_ref[_M_INCL + d]
                    l_mats.append(raw[C * d:C * (d + 1)] * dec * m_ref[_M_STRICT + d])
                    glast = gx[C - 1:C, :] if d == 0 else gx[0:1, :]
                    kd_ref[d, rows, cols] = (kg * jnp.exp(glast - gx)).astype(BF16)
                    qd_ref[d, rows, cols] = (qg * jnp.exp(gx)).astype(BF16)
                    aq_ref[d, rows, cols] = (raw[2 * C:3 * C] * dec).astype(BF16)
                    gl_ref[d, pl.ds(cc, 1), :, cols] = jnp.exp(glast).reshape(1, 1, DN_GW)
                    units.append((rows, cols, d, grp))
        tinvs = _tri_inverse(l_mats, m_ref, bdm_ref)
        for (rows, cols, d, grp), tinv in zip(units, tinvs):
            bx = bx_s[d, grp, rows, :]
            u_ref[d, rows, cols] = _mm(tinv, _bd(v_s[rows, cols] * bx, bdm_ref))
            w_ref[d, rows, cols] = _mm(tinv, _bd(k_s[rows, cols] * bx * jnp.exp(gx_s[d, grp, rows, :]), bdm_ref)
                                       ).astype(BF16)
        return carry

    lax.fori_loop(0, TM // C // DN_PAIR, chunk_pair, 0)


def _dn_prep(dqkv, dg, conv_w, a_log, dt_bias, n_batch, rows):
    n = dqkv.shape[0]
    nbp = rows // TM
    hb = TM // HALO
    H = DN_HEADS
    cw = jnp.zeros((8, 1536), F32).at[0:DN_CONV].set(conv_w)
    gc = jnp.zeros((8, LANES), F32)
    gc = gc.at[0, 2 * H:4 * H].set(-jnp.exp(a_log.reshape(-1))).at[1, 2 * H:4 * H].set(dt_bias.reshape(-1))
    hid = np.arange(512) // DN_HEAD_DIM
    g512 = jnp.asarray(hid[:, None] == hid[None, :], BF16)
    e = np.zeros((8, LANES, DN_GW), np.float32)
    for kind in range(2):
        for d in range(2):
            for grp in range(2):
                for h in range(4):
                    e[4 * kind + 2 * d + grp, 16 * kind + 8 * d + 4 * grp + h, 64 * h:64 * (h + 1)] = 1.0
    e = jnp.asarray(e, BF16)
    t = np.arange(TM)
    same = (t[:, None] // DN_CHUNK) == (t[None, :] // DN_CHUNK)
    tri = jnp.asarray(np.stack([same & (t[None, :] <= t[:, None]), same & (t[None, :] >= t[:, None])]), BF16)
    masks, bdm = _dn_masks()
    blk = lambda b, j: (b * nbp + j, 0)
    full2 = lambda b, j: (0, 0)
    full3 = lambda b, j: (0, 0, 0)
    dblk = lambda b, j: (0, b * nbp + j, 0)
    nlast = n // HALO - 1
    outs = pl.pallas_call(
        _dn_prep_body,
        out_shape=[jax.ShapeDtypeStruct((2, n, 512), F32)] + [jax.ShapeDtypeStruct((2, n, 512), BF16)] * 4
        + [jax.ShapeDtypeStruct((2, n // DN_CHUNK, 1, 512), F32)],
        grid=(n_batch, nbp),
        in_specs=[pl.BlockSpec((TM, 1536), blk),
                  pl.BlockSpec((HALO, 1536), lambda b, j: (jnp.maximum((b * nbp + j) * hb - 1, 0), 0)),
                  pl.BlockSpec((HALO, 1536), lambda b, j: (jnp.minimum((b * nbp + j + 1) * hb, nlast), 0)),
                  pl.BlockSpec((8, 1536), full2), pl.BlockSpec((TM, LANES), blk), pl.BlockSpec((8, LANES), full2),
                  pl.BlockSpec((512, 512), full2), pl.BlockSpec((8, LANES, DN_GW), full3),
                  pl.BlockSpec((2, TM, TM), full3), pl.BlockSpec((8, DN_CHUNK, DN_GW), full3),
                  pl.BlockSpec((DN_GW, DN_GW), full2)],
        out_specs=[pl.BlockSpec((2, TM, 512), dblk)] * 5
        + [pl.BlockSpec((2, TM // DN_CHUNK, 1, 512), lambda b, j: (0, b * nbp + j, 0, 0))],
        scratch_shapes=[pltpu.VMEM((TM + 2 * HALO, 1536), F32), pltpu.VMEM((TM, 512), F32),
                        pltpu.VMEM((TM, 512), F32), pltpu.VMEM((TM, 512), F32),
                        pltpu.VMEM((2, 2, TM, DN_GW), F32), pltpu.VMEM((2, 2, TM, DN_GW), F32)],
        compiler_params=_cparams(2),
        name="dn_prep",
    )(dqkv, dqkv, dqkv, cw, dg, gc, g512, e, tri, masks, bdm)
    return outs, g512


DN_SCAN_NB = 4


def _dn_scan_body(uf, wf, kf, qf, af, gf, ub, wb, kb, qb, ab, gb, bdm_ref, of_ref, ob_ref, s_ref):
    @pl.when(pl.program_id(1) == 0)
    def _():
        s_ref[...] = jnp.zeros_like(s_ref)

    same = bdm_ref[...] != 0
    dirs = ((uf, wf, kf, qf, af, gf, of_ref), (ub, wb, kb, qb, ab, gb, ob_ref))
    units = [(bb, d, grp) for bb in range(DN_SCAN_NB) for d in range(2) for grp in range(2)]
    cols = [slice(grp * DN_GW, (grp + 1) * DN_GW) for grp in range(2)]
    s = [s_ref[bb, d, grp] for bb, d, grp in units]
    sq = [_dot(jnp.concatenate([dirs[d][1][0, bb, :, cols[grp]], dirs[d][3][0, bb, :, cols[grp]]], axis=0),
               st.astype(BF16)) for (bb, d, grp), st in zip(units, s)]
    vnew = [dirs[d][0][0, bb, :, cols[grp]] - r[0:DN_CHUNK] for (bb, d, grp), r in zip(units, sq)]
    intra = [_dot(dirs[d][4][0, bb, :, cols[grp]], _bd(v, bdm_ref)) for (bb, d, grp), v in zip(units, vnew)]
    upd = [lax.dot_general(dirs[d][2][0, bb, :, cols[grp]], v.astype(BF16), (((0,), (0,)), ((), ())),
                           preferred_element_type=F32) for (bb, d, grp), v in zip(units, vnew)]
    for (bb, d, grp), st, r, a, up in zip(units, s, sq, intra, upd):
        dirs[d][6][bb, :, cols[grp]] = r[DN_CHUNK:2 * DN_CHUNK] + a
        s_ref[bb, d, grp] = st * dirs[d][5][0, bb, 0, :, cols[grp]] + jnp.where(same, up, 0.0)


def _dn_scan(u, w, kd, qd, aq, gl, n_batch, rows, t_ctx):
    nch = rows // DN_CHUNK
    nctx = t_ctx // DN_CHUNK
    assert n_batch % DN_SCAN_NB == 0

    def cb(c):
        return jnp.where(c < nctx, nctx - 1 - c, nch - 1 - (c - nctx))

    per_batch = lambda a: a.reshape(2, n_batch, rows, 512)
    u, w, kd, qd, aq = map(per_batch, (u, w, kd, qd, aq))
    gl = gl.reshape(2, n_batch, nch, 1, 512)
    blk = (1, DN_SCAN_NB, DN_CHUNK, 512)
    gblk = (1, DN_SCAN_NB, 1, 1, 512)
    specs = []
    for d, ch in ((0, lambda c: c), (1, cb)):
        specs += [pl.BlockSpec(blk, lambda b, c, d=d, ch=ch: (d, b, ch(c), 0))] * 5
        specs += [pl.BlockSpec(gblk, lambda b, c, d=d, ch=ch: (d, b, ch(c), 0, 0))]
    specs.append(pl.BlockSpec((DN_GW, DN_GW), lambda b, c: (0, 0)))
    o_f, o_b = pl.pallas_call(
        _dn_scan_body,
        out_shape=[jax.ShapeDtypeStruct((n_batch, rows, 512), F32)] * 2,
        grid=(n_batch // DN_SCAN_NB, nch),
        in_specs=specs,
        out_specs=[pl.BlockSpec((DN_SCAN_NB, DN_CHUNK, 512), lambda b, c: (b, c, 0)),
                   pl.BlockSpec((DN_SCAN_NB, DN_CHUNK, 512), lambda b, c: (b, cb(c), 0))],
        scratch_shapes=[pltpu.VMEM((DN_SCAN_NB, 2, 2, DN_GW, DN_GW), F32)],
        compiler_params=_cparams(2),
        name="dn_scan",
    )(u, w, kd, qd, aq, gl, u, w, kd, qd, aq, gl, _dn_masks()[1])
    return o_f.reshape(n_batch * rows, 512), o_b.reshape(n_batch * rows, 512)


def _dn_out_body(of_ref, ob_ref, z_ref, g_ref, g512_ref, o_ref):
    o = of_ref[...] + ob_ref[...]
    hi, lo = _split_bf16(o * o)
    ms = (_dot(hi, g512_ref[...]) + _dot(lo, g512_ref[...])) * (1.0 / DN_HEAD_DIM)
    z = z_ref[...]
    o_ref[...] = (o * lax.rsqrt(ms + NORM_EPS) * g_ref[...] * (z * jax.nn.sigmoid(z))).astype(BF16)


def _dn_out(o_f, o_b, dz, norm_g, g512):
    n = o_f.shape[0]
    row = lambda i: (i, 0)
    g = jnp.tile(norm_g, DN_HEADS).reshape(1, 512)
    return pl.pallas_call(
        _dn_out_body,
        out_shape=jax.ShapeDtypeStruct((n, BRANCH_W), BF16),
        grid=(n // TM,),
        in_specs=[pl.BlockSpec((TM, 512), row)] * 3 + [pl.BlockSpec((1, 512), lambda i: (0, 0)),
                                                       pl.BlockSpec((512, 512), lambda i: (0, 0))],
        out_specs=pl.BlockSpec((TM, BRANCH_W), row),
        compiler_params=_cparams(),
        name="dn_out",
    )(o_f, o_b, dz, g, g512)


def _deltanet(dqkv, dz, dg, conv_w, a_log, dt_bias, norm_g, n_batch, rows, t_ctx):
    (u, w, kd, qd, aq, gl), g512 = _dn_prep(dqkv, dg, conv_w, a_log, dt_bias, n_batch, rows)
    o_f, o_b = _dn_scan(u, w, kd, qd, aq, gl, n_batch, rows, t_ctx)
    return _dn_out(o_f, o_b, dz, norm_g, g512)


def _final_norm_body(x_ref, g_ref, o_ref):
    o_ref[...] = _rms(x_ref[...], g_ref[...])


def _final_norm(x, g):
    n = x.shape[0]
    return pl.pallas_call(
        _final_norm_body,
        out_shape=jax.ShapeDtypeStruct(x.shape, F32),
        grid=(n // TM,),
        in_specs=[pl.BlockSpec((TM, D_MODEL), lambda i: (i, 0)), pl.BlockSpec((1, D_MODEL), lambda i: (0, 0))],
        out_specs=pl.BlockSpec((TM, D_MODEL), lambda i: (i, 0)),
        compiler_params=_cparams(),
        name="final_norm",
    )(x, g)


def _rope_tables(seq, t_ctx, rot_dim):
    rows = seq // GRID_W
    row = jnp.broadcast_to(jnp.arange(rows)[:, None], (rows, GRID_W)).reshape(-1).astype(F32)
    col = jnp.broadcast_to(jnp.arange(GRID_W)[None, :], (rows, GRID_W)).reshape(-1).astype(F32)
    n_freq = rot_dim // 4
    inv_freq = ROPE_THETA ** (-jnp.arange(n_freq, dtype=F32) / n_freq)
    ang = jnp.concatenate([row[:, None] * inv_freq, col[:, None] * inv_freq], axis=-1)
    cos, sin = jnp.cos(ang), jnp.sin(ang)
    zero = jnp.zeros_like(sin)
    c = jnp.concatenate([cos, cos], axis=-1)
    sa = jnp.concatenate([-sin, zero], axis=-1)
    sb = jnp.concatenate([zero, sin], axis=-1)
    ident = jnp.stack([jnp.ones((t_ctx, rot_dim), F32), jnp.zeros((t_ctx, rot_dim), F32),
                       jnp.zeros((t_ctx, rot_dim), F32)])
    tab = jnp.concatenate([ident, jnp.stack([c, sa, sb])], axis=1)
    return jnp.tile(tab, (1, 1, LANES // rot_dim))


def _layer_weights(w_in, w_q_up, w_kv_up, w_branch, w_out, router_w, router_b, b_gu, b_dn):
    sizes = (MLA_Q_RANK, MLA_KV_RANK, MLA_ROPE, 512, 128, 128, 1536, 512, 16, 16, N_BRANCH * D_MODEL)
    offs = np.cumsum((0,) + sizes)
    part = lambda i: w_in[:, offs[i]:offs[i + 1]]
    pad = jnp.zeros((D_MODEL, LANES - 32), F32)
    w_proj = jnp.concatenate(
        [part(0), part(1), part(3), part(4), part(5), part(6), part(7),
         jnp.tile(part(2), (1, LANES // MLA_ROPE)), part(8), part(9), pad], axis=1).astype(BF16)
    w_gate = part(10).astype(BF16)
    hq = MLA_NOPE + MLA_ROPE
    wq = w_q_up.reshape(MLA_Q_RANK, MLA_HEADS, hq)
    wq = jnp.concatenate([wq[:, :, :MLA_NOPE].reshape(MLA_Q_RANK, -1),
                          wq[:, :, MLA_NOPE:].reshape(MLA_Q_RANK, -1)], axis=1).astype(BF16)
    wkv = w_kv_up.reshape(MLA_KV_RANK, MLA_HEADS, MLA_NOPE + MLA_V)
    wkv = jnp.concatenate([wkv[:, :, :MLA_NOPE].reshape(MLA_KV_RANK, -1),
                           wkv[:, :, MLA_NOPE:].reshape(MLA_KV_RANK, -1)], axis=1).astype(BF16)
    bgu = jnp.concatenate([b_gu[:, 0::2], b_gu[:, 1::2]], axis=-1).reshape(N_EXPERTS, 1, 2 * D_EXPERT)
    return dict(w_proj=w_proj, w_gate=w_gate, wq=wq, wkv=wkv, wb=w_branch.astype(BF16),
                wo=w_out.astype(BF16), rwt=router_w.T, rb=router_b.reshape(N_EXPERTS, 1),
                bgu=bgu, bdn=b_dn.reshape(N_EXPERTS, 1, D_MODEL))


def kernel(x, c, ctx, c_ctx, w_mod, b_mod, norm1_g, norm2_g, w_in, mla_q_norm_g, mla_w_q_up,
           mla_kv_norm_g, mla_w_kv_up, swa_sink, dn_conv_w, dn_a_log, dn_dt_bias, dn_norm_g,
           w_branch, w_out, router_w, router_b, exp_w_gu, exp_b_gu, exp_w_dn, exp_b_dn, final_norm_g):
    B, S, D = x.shape
    T = ctx.shape[1]
    R = T + S
    depth = w_mod.shape[0]
    assert D == D_MODEL and T == TM and S % TM == 0 and (B * R) % ROUTE_TB == 0
    nbp = R // TM
    xa = jnp.concatenate([ctx, x], axis=1).reshape(B * R, D)
    rope_s = _rope_tables(S, T, SWA_HEAD_DIM)
    rope_m = _rope_tables(S, T, MLA_ROPE)
    cvec = jnp.concatenate([c, c_ctx[None, :]], axis=0)
    for l in range(depth):
        w = _layer_weights(w_in[l], mla_w_q_up[l], mla_w_kv_up[l], w_branch[l], w_out[l], router_w[l],
                           router_b[l], exp_b_gu[l], exp_b_dn[l])
        mod = _modulation(cvec, w_mod[l], b_mod[l]).reshape(B + 1, 6, D)
        modtab = jnp.stack([jnp.broadcast_to(mod[B][None], (B, 6, D)), mod[:B]], axis=1)
        g1 = norm1_g[l].reshape(1, D)
        g2 = norm2_g[l].reshape(1, D)
        cq, ckv, krt, sq, sk, sv, dqkv, dz, dg = _inproj(xa, modtab, g1, w["w_proj"], rope_s, nbp)
        q_m, k_m, v_m = _mla_prep(cq, ckv, krt, mla_q_norm_g[l].reshape(1, -1), mla_kv_norm_g[l].reshape(1, -1),
                                  w["wq"], w["wkv"], rope_m, nbp)
        o_a = _mla_attention(q_m, k_m, v_m, B, R)
        o_b = _swa_attention(sq, sk, sv, swa_sink[l], B, T, S)
        o_c = _deltanet(dqkv, dz, dg, dn_conv_w[l], dn_a_log[l], dn_dt_bias[l], dn_norm_g[l], B, R, T)
        last = l == depth - 1
        xn, h2, logits_t = _merge(xa, modtab, g1, g2, w["w_gate"], o_a, o_b, o_c, w["wb"], w["wo"],
                                  w["rwt"], w["rb"], nbp, last)
        y = _moe(h2, logits_t, exp_w_gu, w["bgu"], exp_w_dn, w["bdn"], l)
        if last:
            xa = (xn.reshape(B, S, D) + modtab[:, 1, 5][:, None, :] * y.reshape(B, S, D)).reshape(B * S, D)
        else:
            is_ctx = lax.broadcasted_iota(jnp.int32, (1, R, 1), 1) < T
            g2rows = jnp.where(is_ctx, modtab[:, 0, 5][:, None, :], modtab[:, 1, 5][:, None, :])
            xa = (xn.reshape(B, R, D) + g2rows * y.reshape(B, R, D)).reshape(B * R, D)
    return _final_norm(xa, final_norm_g.reshape(1, D)).reshape(B, S, D)
```

```python
import functools

import jax
import jax.numpy as jnp
import numpy as np
from jax import lax
from jax.experimental import pallas as pl
from jax.experimental.pallas import tpu as pltpu

F32 = jnp.float32
BF16 = jnp.bfloat16

D_MODEL = 1024
GRID_W = 64
ROPE_THETA = 10000.0
NORM_EPS = 1e-6
MLA_HEADS = 8
MLA_Q_RANK = 384
MLA_KV_RANK = 256
MLA_NOPE = 64
MLA_ROPE = 32
MLA_V = 64
SWA_HEADS = 8
SWA_KV_HEADS = 2
SWA_HEAD_DIM = 64
SWA_WINDOW = 128
DN_HEADS = 8
DN_HEAD_DIM = 64
DN_CONV = 5
DN_CHUNK = 64
N_BRANCH = 3
BRANCH_W = 512
N_EXPERTS = 32
TOP_K = 4
D_EXPERT = 1024
SWIGLU_LIMIT = 7.0
SWIGLU_ALPHA = 1.702
MOE_BLK = 512
MOE_CHUNKS = 4

LANES = 128
LOG2E = 1.4426950408889634
TM = 256
VMEM_LIMIT = 56 * 1024 * 1024

_C_CQ = (0, 384)
_C_CKV = (384, 640)
_C_SQ = (640, 1152)
_C_SK = (1152, 1280)
_C_SV = (1280, 1408)
_C_DQKV = (1408, 2944)
_C_DZ = (2944, 3456)
_C_KRT = (3456, 3584)
_C_DG = (3584, 3712)
_N_PROJ = 3712


def _cparams(n_axes=1):
    return pltpu.CompilerParams(dimension_semantics=("arbitrary",) * n_axes, vmem_limit_bytes=VMEM_LIMIT)


def _dot(a, b):
    return jnp.dot(a, b, preferred_element_type=F32)


def _dot_nt(a, b):
    return lax.dot_general(a, b, (((1,), (1,)), ((), ())), preferred_element_type=F32)


def _split_bf16(a):
    hi = a.astype(BF16)
    lo = (a - hi.astype(F32)).astype(BF16)
    return hi, lo


def _dot3(a, b):
    ah, al = _split_bf16(a)
    bh, bl = _split_bf16(b)
    return _dot(ah, bh) + (_dot(ah, bl) + _dot(al, bh))


def _dot3_nt(a, b):
    ah, al = _split_bf16(a)
    bh, bl = _split_bf16(b)
    return _dot_nt(ah, bh) + (_dot_nt(ah, bl) + _dot_nt(al, bh))


def _rms(x, g):
    return x * lax.rsqrt(jnp.mean(x * x, axis=-1, keepdims=True) + NORM_EPS) * g


def _rope_cols(x, tab_ref, half):
    c, sa, sb = tab_ref[0], tab_ref[1], tab_ref[2]
    return x * c + pltpu.roll(x, LANES - half, axis=1) * sa + pltpu.roll(x, half, axis=1) * sb


def _mod_body(c_ref, w_ref, b_ref, o_ref):
    c = c_ref[...]
    a = c * jax.nn.sigmoid(c)
    o_ref[...] = _dot3(a, w_ref[...]) + b_ref[...]


def _modulation(cvec, w_mod, b_mod):
    m = cvec.shape[0]
    n = w_mod.shape[1]
    tn = 512
    return pl.pallas_call(
        _mod_body,
        out_shape=jax.ShapeDtypeStruct((m, n), F32),
        grid=(n // tn,),
        in_specs=[pl.BlockSpec((m, D_MODEL), lambda j: (0, 0)),
                  pl.BlockSpec((D_MODEL, tn), lambda j: (0, j)),
                  pl.BlockSpec((1, tn), lambda j: (0, j))],
        out_specs=pl.BlockSpec((m, tn), lambda j: (0, j)),
        compiler_params=_cparams(),
        name="modulation",
    )(cvec, w_mod, b_mod.reshape(1, n))


def _inproj_body(x_ref, mod_ref, g_ref, w_ref, rope_ref,
                 cq_ref, ckv_ref, krt_ref, sq_ref, sk_ref, sv_ref, dqkv_ref, dz_ref, dg_ref):
    x = x_ref[...]
    h = (_rms(x, g_ref[...]) * (1.0 + mod_ref[0, 0, 1:2, :]) + mod_ref[0, 0, 0:1, :]).astype(BF16)

    def proj(cols):
        return _dot(h, w_ref[:, cols[0]:cols[1]])

    cq_ref[...] = proj(_C_CQ)
    ckv_ref[...] = proj(_C_CKV)
    krt_ref[...] = proj(_C_KRT)
    dqkv_ref[...] = proj(_C_DQKV)
    dz_ref[...] = proj(_C_DZ)
    dg_ref[...] = proj(_C_DG)
    sv_ref[...] = proj(_C_SV).astype(BF16)
    sk_ref[...] = _rope_cols(proj(_C_SK), rope_ref, SWA_HEAD_DIM // 2).astype(BF16)
    sq = proj(_C_SQ)
    lane = lax.broadcasted_iota(jnp.int32, (TM, LANES), 1)
    lo = lane < SWA_HEAD_DIM
    for c in range(4):
        xr = _rope_cols(sq[:, c * LANES:(c + 1) * LANES], rope_ref, SWA_HEAD_DIM // 2)
        xs = pltpu.roll(xr, SWA_HEAD_DIM, axis=1)
        if c < 2:
            a, b = jnp.where(lo, xr, 0.0), jnp.where(lo, xs, 0.0)
        else:
            a, b = jnp.where(lo, 0.0, xs), jnp.where(lo, 0.0, xr)
        sq_ref[:, (2 * c) * LANES:(2 * c + 1) * LANES] = a.astype(BF16)
        sq_ref[:, (2 * c + 1) * LANES:(2 * c + 2) * LANES] = b.astype(BF16)


def _inproj(x, modtab, g, w, rope_s, nb_per_batch):
    n = x.shape[0]
    row = lambda i: (i, 0)
    widths = [(384, F32), (256, F32), (128, F32), (1024, BF16), (128, BF16), (128, BF16),
              (1536, F32), (512, F32), (128, F32)]
    return pl.pallas_call(
        _inproj_body,
        out_shape=[jax.ShapeDtypeStruct((n, wd), dt) for wd, dt in widths],
        grid=(n // TM,),
        in_specs=[pl.BlockSpec((TM, D_MODEL), row),
                  pl.BlockSpec((1, 1, 6, D_MODEL),
                               lambda i: (i // nb_per_batch, jnp.minimum(i % nb_per_batch, 1), 0, 0)),
                  pl.BlockSpec((1, D_MODEL), lambda i: (0, 0)),
                  pl.BlockSpec((D_MODEL, _N_PROJ), lambda i: (0, 0)),
                  pl.BlockSpec((3, TM, LANES), lambda i: (0, i % nb_per_batch, 0))],
        out_specs=[pl.BlockSpec((TM, wd), row) for wd, _ in widths],
        compiler_params=_cparams(),
        name="inproj",
    )(x, modtab, g, w, rope_s)


def _mla_prep_body(cq_ref, ckv_ref, krt_ref, qg_ref, kvg_ref, wq_ref, wkv_ref, rope_ref,
                   q_ref, k_ref, v_ref):
    half = MLA_ROPE // 2
    qn = _rms(cq_ref[...], qg_ref[...]).astype(BF16)
    q = _dot(qn, wq_ref[...])
    q_ref[:, 0:512] = q[:, 0:512].astype(BF16)
    for c in range(2):
        lo_, hi_ = 512 + c * LANES, 512 + (c + 1) * LANES
        q_ref[:, lo_:hi_] = _rope_cols(q[:, lo_:hi_], rope_ref, half).astype(BF16)
    kvn = _rms(ckv_ref[...], kvg_ref[...]).astype(BF16)
    kv = _dot(kvn, wkv_ref[...])
    kr = _rope_cols(krt_ref[...], rope_ref, half).astype(BF16)
    for c in range(4):
        k_ref[:, (2 * c) * LANES:(2 * c + 1) * LANES] = kv[:, c * LANES:(c + 1) * LANES].astype(BF16)
        k_ref[:, (2 * c + 1) * LANES:(2 * c + 2) * LANES] = kr
    v_ref[...] = kv[:, 512:1024].astype(BF16)


def _mla_prep(cq, ckv, krt, qg, kvg, wq, wkv, rope_m, nb_per_batch):
    n = cq.shape[0]
    row = lambda i: (i, 0)
    full = lambda i: (0, 0)
    return pl.pallas_call(
        _mla_prep_body,
        out_shape=[jax.ShapeDtypeStruct((n, 768), BF16), jax.ShapeDtypeStruct((n, 1024), BF16),
                   jax.ShapeDtypeStruct((n, 512), BF16)],
        grid=(n // TM,),
        in_specs=[pl.BlockSpec((TM, 384), row), pl.BlockSpec((TM, 256), row), pl.BlockSpec((TM, 128), row),
                  pl.BlockSpec((1, 384), full), pl.BlockSpec((1, 256), full),
                  pl.BlockSpec((384, 768), full), pl.BlockSpec((256, 1024), full),
                  pl.BlockSpec((3, TM, LANES), lambda i: (0, i % nb_per_batch, 0))],
        out_specs=[pl.BlockSpec((TM, 768), row), pl.BlockSpec((TM, 1024), row), pl.BlockSpec((TM, 512), row)],
        compiler_params=_cparams(),
        name="mla_prep",
    )(cq, ckv, krt, qg, kvg, wq, wkv, rope_m)


def _mla_attn_body(q_ref, k_ref, v_ref, o_ref, *, t_ctx):
    c1 = (MLA_NOPE + MLA_ROPE) ** -0.5 * LOG2E
    lane = lax.broadcasted_iota(jnp.int32, (TM, LANES), 1)
    zero = jnp.zeros((TM, LANES), BF16)

    def attend(nk):
        for c in range(4):
            kc = k_ref[0:nk, c * 2 * LANES:(c + 1) * 2 * LANES]
            vc = v_ref[0:nk, c * LANES:(c + 1) * LANES]
            qn = q_ref[:, c * LANES:(c + 1) * LANES]
            qr = q_ref[:, 512 + (c // 2) * LANES:512 + (c // 2 + 1) * LANES]
            qa = [jnp.concatenate(
                [jnp.where((lane >= 64 * s) & (lane < 64 * (s + 1)), qn, zero),
                 jnp.where((lane >= 32 * ((2 * c + s) % 4)) & (lane < 32 * ((2 * c + s) % 4 + 1)), qr, zero)],
                axis=1) for s in range(2)]
            sc = [_dot_nt(q, kc) * c1 for q in qa]
            m = [jnp.max(t, axis=-1, keepdims=True) for t in sc]
            e = [jnp.exp2(t - mm) for t, mm in zip(sc, m)]
            l = [jnp.sum(t, axis=-1, keepdims=True) for t in e]
            outs = [_dot(t.astype(BF16), vc) / ll for t, ll in zip(e, l)]
            o_ref[:, c * LANES:(c + 1) * LANES] = jnp.where(lane < 64, outs[0], outs[1]).astype(BF16)

    j = pl.program_id(1)

    @pl.when(j == 0)
    def _():
        attend(t_ctx)

    @pl.when(j > 0)
    def _():
        attend(k_ref.shape[0])


def _mla_attention(q, k, v, n_batch, rows):
    nbp = rows // TM
    return pl.pallas_call(
        functools.partial(_mla_attn_body, t_ctx=TM),
        out_shape=jax.ShapeDtypeStruct((n_batch * rows, BRANCH_W), BF16),
        grid=(n_batch, nbp),
        in_specs=[pl.BlockSpec((TM, 768), lambda b, j: (b * nbp + j, 0)),
                  pl.BlockSpec((rows, 1024), lambda b, j: (b, 0)),
                  pl.BlockSpec((rows, 512), lambda b, j: (b, 0))],
        out_specs=pl.BlockSpec((TM, BRANCH_W), lambda b, j: (b * nbp + j, 0)),
        compiler_params=_cparams(2),
        name="mla_attention",
    )(q, k, v)


SWA_QB = 128
SWA_NQ = 2


def _swa_body(q_ref, k_ref, v_ref, sink_ref, o_ref, *, t_ctx, seq):
    c1 = SWA_HEAD_DIM ** -0.5 * LOG2E
    j = pl.program_id(1)
    n_ctx_steps = t_ctx // (SWA_QB * SWA_NQ)
    lane = lax.broadcasted_iota(jnp.int32, (SWA_QB, LANES), 1)
    lo = lane < SWA_HEAD_DIM
    win = 3 * SWA_QB

    def finish(rows, parts):
        for c in range(4):
            g = c // 2
            r0 = (2 * (c % 2)) * SWA_QB
            a = parts[g][r0:r0 + SWA_QB]
            b = parts[g][r0 + SWA_QB:r0 + 2 * SWA_QB]
            if g == 0:
                col = jnp.where(lo, a, pltpu.roll(b, SWA_HEAD_DIM, axis=1))
            else:
                col = jnp.where(lo, pltpu.roll(a, SWA_HEAD_DIM, axis=1), b)
            o_ref[rows, c * LANES:(c + 1) * LANES] = col.astype(BF16)

    def blocks(blks):
        kc, vc = k_ref[0:t_ctx, :], v_ref[0:t_ctx, :]
        windowed = blks[0] is not None
        kw, vw, band = [], [], []
        for blk in blks if windowed else ():
            start = jnp.clip((blk - 1) * SWA_QB, 0, seq - win)
            rs = pl.multiple_of(t_ctx + start, SWA_QB)
            kw.append(k_ref[pl.ds(rs, win), :])
            vw.append(v_ref[pl.ds(rs, win), :])
            qpos = blk * SWA_QB + lax.broadcasted_iota(jnp.int32, (SWA_QB, win), 0)
            kpos = start + lax.broadcasted_iota(jnp.int32, (SWA_QB, win), 1)
            band1 = jnp.abs(kpos - qpos) <= SWA_WINDOW
            band.append(jnp.concatenate([band1] * 4, axis=0))
        units = [(sub, g) for sub in range(SWA_NQ) for g in range(SWA_KV_HEADS)]
        rows = [slice(sub * SWA_QB, (sub + 1) * SWA_QB) for sub in range(SWA_NQ)]
        qg = [jnp.concatenate([q_ref[rows[sub], (4 * g + i) * LANES:(4 * g + i + 1) * LANES] for i in range(4)],
                              axis=0) for sub, g in units]
        sk = [c1 * jnp.concatenate(
            [jnp.broadcast_to(sink_ref[4 * g + i:4 * g + i + 1, 0:1], (SWA_QB, 1)) for i in range(4)], axis=0)
            for _, g in units]
        t_c = [_dot_nt(q, kc) * c1 for q in qg]
        m = [jnp.maximum(jnp.max(t, axis=-1, keepdims=True), s) for t, s in zip(t_c, sk)]
        if windowed:
            t_w = [jnp.where(band[sub], _dot_nt(q, kw[sub]) * c1, -jnp.inf) for (sub, _), q in zip(units, qg)]
            m = [jnp.maximum(mm, jnp.max(t, axis=-1, keepdims=True)) for mm, t in zip(m, t_w)]
        e_c = [jnp.exp2(t - mm) for t, mm in zip(t_c, m)]
        l = [jnp.sum(e, axis=-1, keepdims=True) + jnp.exp2(s - mm) for e, s, mm in zip(e_c, sk, m)]
        acc = [_dot(e.astype(BF16), vc) for e in e_c]
        if windowed:
            e_w = [jnp.exp2(t - mm) for t, mm in zip(t_w, m)]
            l = [ll + jnp.sum(e, axis=-1, keepdims=True) for ll, e in zip(l, e_w)]
            acc = [a + _dot(e.astype(BF16), vw[sub]) for a, e, (sub, _) in zip(acc, e_w, units)]
        parts = [a / ll for a, ll in zip(acc, l)]
        for sub in range(SWA_NQ):
            finish(rows[sub], parts[SWA_KV_HEADS * sub:SWA_KV_HEADS * (sub + 1)])

    @pl.when(j < n_ctx_steps)
    def _():
        blocks([None] * SWA_NQ)

    @pl.when(j >= n_ctx_steps)
    def _():
        blocks([(j - n_ctx_steps) * SWA_NQ + sub for sub in range(SWA_NQ)])


def _swa_attention(q, k, v, sink, n_batch, t_ctx, seq):
    rows = t_ctx + seq
    qrows = SWA_QB * SWA_NQ
    assert t_ctx % qrows == 0 and seq % qrows == 0
    nqb = rows // qrows
    sink_tab = jnp.broadcast_to(sink.astype(F32)[:, None], (SWA_HEADS, LANES))
    return pl.pallas_call(
        functools.partial(_swa_body, t_ctx=t_ctx, seq=seq),
        out_shape=jax.ShapeDtypeStruct((n_batch * rows, BRANCH_W), BF16),
        grid=(n_batch, nqb),
        in_specs=[pl.BlockSpec((qrows, 1024), lambda b, j: (b * nqb + j, 0)),
                  pl.BlockSpec((rows, LANES), lambda b, j: (b, 0)),
                  pl.BlockSpec((rows, LANES), lambda b, j: (b, 0)),
                  pl.BlockSpec((SWA_HEADS, LANES), lambda b, j: (0, 0))],
        out_specs=pl.BlockSpec((qrows, BRANCH_W), lambda b, j: (b * nqb + j, 0)),
        compiler_params=_cparams(2),
        name="swa_attention",
    )(q, k, v, sink_tab)


def _merge_body(x_ref, mod_ref, g1_ref, g2_ref, wg_ref, oa_ref, ob_ref, df_ref, db_ref, dz_ref, dng_ref, g512_ref,
                wb_ref, wo_ref, rw_ref, rb_ref, xo_ref, h2_ref, lg_ref):
    x = x_ref[...]
    h = (_rms(x, g1_ref[...]) * (1.0 + mod_ref[0, 0, 1:2, :]) + mod_ref[0, 0, 0:1, :]).astype(BF16)
    od = df_ref[...] + db_ref[...]
    hi, lo = _split_bf16(od * od)
    ms = (_dot(hi, g512_ref[...]) + _dot(lo, g512_ref[...])) * (1.0 / DN_HEAD_DIM)
    z = dz_ref[...]
    oc = (od * lax.rsqrt(ms + NORM_EPS) * dng_ref[...] * (z * jax.nn.sigmoid(z))).astype(BF16)
    y = None
    for i, o in enumerate((oa_ref[...], ob_ref[...], oc)):
        gate = jax.nn.sigmoid(_dot(h, wg_ref[:, i * D_MODEL:(i + 1) * D_MODEL]))
        yi = gate * _dot(o, wb_ref[i])
        y = yi if y is None else y + yi
    xn = x + mod_ref[0, 0, 2:3, :] * _dot(y.astype(BF16), wo_ref[...])
    xo_ref[...] = xn
    h2 = _rms(xn, g2_ref[...]) * (1.0 + mod_ref[0, 0, 4:5, :]) + mod_ref[0, 0, 3:4, :]
    h2_ref[...] = h2.astype(BF16)
    lg_ref[...] = _dot3_nt(rw_ref[...], h2) + rb_ref[...]


def _merge(x, modtab, g1, g2, wg, oa, ob, dn_f, dn_b, dz, dn_g, g512, wb, wo, rwt, rb, nb_per_batch, latent_only):
    if latent_only:
        nlat = nb_per_batch - 1
        n = x.shape[0] // nb_per_batch * nlat
        row = lambda i: ((i // nlat) * nb_per_batch + i % nlat + 1, 0)
        mod_idx = lambda i: (i // nlat, 1, 0, 0)
    else:
        n = x.shape[0]
        row = lambda i: (i, 0)
        mod_idx = lambda i: (i // nb_per_batch, jnp.minimum(i % nb_per_batch, 1), 0, 0)
    orow = lambda i: (i, 0)
    full2 = lambda i: (0, 0)
    return pl.pallas_call(
        _merge_body,
        out_shape=[jax.ShapeDtypeStruct((n, D_MODEL), F32), jax.ShapeDtypeStruct((n, D_MODEL), BF16),
                   jax.ShapeDtypeStruct((N_EXPERTS, n), F32)],
        grid=(n // TM,),
        in_specs=[pl.BlockSpec((TM, D_MODEL), row),
                  pl.BlockSpec((1, 1, 6, D_MODEL), mod_idx),
                  pl.BlockSpec((1, D_MODEL), full2), pl.BlockSpec((1, D_MODEL), full2),
                  pl.BlockSpec((D_MODEL, N_BRANCH * D_MODEL), full2),
                  pl.BlockSpec((TM, BRANCH_W), row), pl.BlockSpec((TM, BRANCH_W), row),
                  pl.BlockSpec((TM, BRANCH_W), row), pl.BlockSpec((TM, BRANCH_W), row),
                  pl.BlockSpec((TM, BRANCH_W), row), pl.BlockSpec((1, BRANCH_W), full2),
                  pl.BlockSpec((BRANCH_W, BRANCH_W), full2),
                  pl.BlockSpec((N_BRANCH, BRANCH_W, D_MODEL), lambda i: (0, 0, 0)),
                  pl.BlockSpec((D_MODEL, D_MODEL), full2),
                  pl.BlockSpec((N_EXPERTS, D_MODEL), full2), pl.BlockSpec((N_EXPERTS, 1), full2)],
        out_specs=[pl.BlockSpec((TM, D_MODEL), orow), pl.BlockSpec((TM, D_MODEL), orow),
                   pl.BlockSpec((N_EXPERTS, TM), lambda i: (0, i))],
        compiler_params=_cparams(),
        name="merge",
    )(x, modtab, g1, g2, wg, oa, ob, dn_f, dn_b, dz, dn_g, g512, wb, wo, rwt, rb)


ROUTE_TB = 1024


def _router_body(lg_ref, tri_ref, e_ref, gate_ref, pos_ref, cnt_ref, run_ref):
    @pl.when(pl.program_id(0) == 0)
    def _():
        run_ref[...] = jnp.zeros_like(run_ref)

    lg = lg_ref[...]
    eid = lax.broadcasted_iota(jnp.int32, lg.shape, 0)
    work = lg
    vals, idxs = [], []
    sel = jnp.zeros(lg.shape, F32)
    for _ in range(TOP_K):
        m = jnp.max(work, axis=0, keepdims=True)
        idx = jnp.min(jnp.where(work == m, eid, N_EXPERTS), axis=0, keepdims=True)
        hit = eid == idx
        sel = jnp.where(hit, 1.0, sel)
        work = jnp.where(hit, -jnp.inf, work)
        vals.append(m)
        idxs.append(idx)
    ex = [jnp.exp(v - vals[0]) for v in vals]
    den = ex[0] + ex[1] + ex[2] + ex[3]
    before = _dot(sel.astype(BF16), tri_ref[...]) + run_ref[:, 0:1]
    for k in range(TOP_K):
        e_ref[k:k + 1, :] = idxs[k]
        gate_ref[k:k + 1, :] = ex[k] / den
        pos_ref[k:k + 1, :] = jnp.sum(jnp.where(eid == idxs[k], before, 0.0), axis=0,
                                      keepdims=True).astype(jnp.int32)
    run_ref[...] = run_ref[...] + jnp.sum(sel, axis=1, keepdims=True)
    cnt_ref[...] = run_ref[...]


def _router(logits_t):
    n = logits_t.shape[1]
    tri = (jnp.arange(ROUTE_TB)[:, None] < jnp.arange(ROUTE_TB)[None, :]).astype(BF16)
    blk = lambda i: (0, i)
    return pl.pallas_call(
        _router_body,
        out_shape=[jax.ShapeDtypeStruct((TOP_K, n), jnp.int32), jax.ShapeDtypeStruct((TOP_K, n), F32),
                   jax.ShapeDtypeStruct((TOP_K, n), jnp.int32), jax.ShapeDtypeStruct((N_EXPERTS, LANES), F32)],
        grid=(n // ROUTE_TB,),
        in_specs=[pl.BlockSpec((N_EXPERTS, ROUTE_TB), blk), pl.BlockSpec((ROUTE_TB, ROUTE_TB), lambda i: (0, 0))],
        out_specs=[pl.BlockSpec((TOP_K, ROUTE_TB), blk), pl.BlockSpec((TOP_K, ROUTE_TB), blk),
                   pl.BlockSpec((TOP_K, ROUTE_TB), blk), pl.BlockSpec((N_EXPERTS, LANES), lambda i: (0, 0))],
        scratch_shapes=[pltpu.VMEM((N_EXPERTS, LANES), F32)],
        compiler_params=_cparams(),
        name="router",
    )(logits_t, tri)


def _expert_body(ib_ref, ie_ref, lo_ref, hi_ref, x_ref, wgu_ref, bgu_ref, wdn_ref, bdn_ref, perm_ref, *rest):
    o_ref, wgu_s, wdn_s = rest[-3:]
    i = pl.program_id(0)
    prev = jnp.maximum(i - 1, 0)

    @pl.when((i == 0) | (ie_ref[i] != ie_ref[prev]))
    def _():
        for m in range(D_EXPERT // LANES):
            t = _dot(wgu_ref[0, 0, :, m * 2 * LANES:(m + 1) * 2 * LANES].astype(BF16), perm_ref[...])
            wgu_s[:, m * LANES:(m + 1) * LANES] = t[:, 0:LANES].astype(BF16)
            wgu_s[:, D_EXPERT + m * LANES:D_EXPERT + (m + 1) * LANES] = t[:, LANES:2 * LANES].astype(BF16)
        wdn_s[...] = wdn_ref[0, 0].astype(BF16)

    lo, hi = lo_ref[i], hi_ref[i]

    @pl.when(hi > lo)
    def _():
        gu = _dot(x_ref[...], wgu_s[...]) + bgu_ref[0]
        g_ = jnp.minimum(gu[:, :D_EXPERT], SWIGLU_LIMIT)
        u_ = jnp.clip(gu[:, D_EXPERT:], -SWIGLU_LIMIT, SWIGLU_LIMIT)
        act = (u_ + 1.0) * (g_ * jax.nn.sigmoid(SWIGLU_ALPHA * g_))
        y = _dot(act.astype(BF16), wdn_s[...]) + bdn_ref[0]
        first = (i == 0) | (ib_ref[i] != ib_ref[prev])
        row = lax.broadcasted_iota(jnp.int32, (MOE_BLK, 1), 0)
        o_ref[...] = jnp.where(first | ((row >= lo) & (row < hi)), y, o_ref[...])


def _experts(items, xs, w_gu, bgu, w_dn, bdn, layer, block0, n_rows, yb_prev):
    n_items = items[0].shape[0]
    perm = np.zeros((2 * LANES, 2 * LANES), np.float32)
    j = np.arange(LANES)
    perm[2 * j, j] = 1.0
    perm[2 * j + 1, LANES + j] = 1.0
    in_specs = [pl.BlockSpec((MOE_BLK, D_MODEL), lambda i, ib, ie, lo, hi: (ib[i] - block0, 0)),
                pl.BlockSpec((1, 1, D_MODEL, 2 * D_EXPERT), lambda i, ib, ie, lo, hi: (layer, ie[i], 0, 0)),
                pl.BlockSpec((1, 1, 2 * D_EXPERT), lambda i, ib, ie, lo, hi: (ie[i], 0, 0)),
                pl.BlockSpec((1, 1, D_EXPERT, D_MODEL), lambda i, ib, ie, lo, hi: (layer, ie[i], 0, 0)),
                pl.BlockSpec((1, 1, D_MODEL), lambda i, ib, ie, lo, hi: (ie[i], 0, 0)),
                pl.BlockSpec((2 * LANES, 2 * LANES), lambda i, ib, ie, lo, hi: (0, 0))]
    args = [*items, xs, w_gu, bgu, w_dn, bdn, jnp.asarray(perm, BF16)]
    aliases = {}
    if yb_prev is not None:
        in_specs.append(pl.BlockSpec(memory_space=pl.ANY))
        aliases = {len(args): 0}
        args.append(yb_prev)
    return pl.pallas_call(
        _expert_body,
        out_shape=jax.ShapeDtypeStruct((n_rows, D_MODEL), F32),
        grid_spec=pltpu.PrefetchScalarGridSpec(
            num_scalar_prefetch=4,
            grid=(n_items,),
            in_specs=in_specs,
            out_specs=pl.BlockSpec((MOE_BLK, D_MODEL), lambda i, ib, ie, lo, hi: (ib[i], 0)),
            scratch_shapes=[pltpu.VMEM((D_MODEL, 2 * D_EXPERT), BF16), pltpu.VMEM((D_EXPERT, D_MODEL), BF16)]),
        input_output_aliases=aliases,
        compiler_params=_cparams(),
        name="experts",
    )(*args)


def _lookup(table, idx):
    ids = jnp.arange(table.shape[0], dtype=jnp.int32)
    return jnp.sum(jnp.where(idx[..., None] == ids, table, 0), axis=-1)


def _work_items(cnt_start, cnt_end, block0, n_blk):
    n_items = n_blk + N_EXPERTS - 1
    b1 = block0 + n_blk
    first = jnp.maximum(cnt_start // MOE_BLK, block0)
    last = jnp.minimum((cnt_end - 1) // MOE_BLK, b1 - 1)
    n_e = jnp.where(cnt_end > cnt_start, jnp.maximum(last - first + 1, 0), 0)
    item_end = jnp.cumsum(n_e)
    item_start = item_end - n_e
    total = item_end[-1]
    ii = jnp.arange(n_items, dtype=jnp.int32)
    valid = ii < total
    e_i = jnp.sum((item_end[None, :] <= jnp.minimum(ii, total - 1)[:, None]).astype(jnp.int32), axis=1)
    blk = jnp.where(valid, _lookup(first, e_i) + ii - _lookup(item_start, e_i), b1 - 1)
    lo = jnp.clip(_lookup(cnt_start, e_i) - blk * MOE_BLK, 0, MOE_BLK)
    hi = jnp.clip(_lookup(cnt_end, e_i) - blk * MOE_BLK, 0, MOE_BLK)
    lo = jnp.where(valid, lo, 0)
    hi = jnp.where(valid, hi, 0)
    return blk.astype(jnp.int32), e_i.astype(jnp.int32), lo.astype(jnp.int32), hi.astype(jnp.int32)


def _moe(h2, logits_t, w_gu, bgu, w_dn, bdn, layer):
    n = h2.shape[0]
    a = n * TOP_K
    assert a % (MOE_BLK * MOE_CHUNKS) == 0
    top_e, gate, pos, cnt = _router(logits_t)
    counts = cnt[:, 0].astype(jnp.int32)
    cnt_end = jnp.cumsum(counts)
    cnt_start = cnt_end - counts
    slot = pos + _lookup(cnt_start, top_e)
    tok = jnp.broadcast_to(jnp.arange(n, dtype=jnp.int32)[None, :], (TOP_K, n))
    _, slot_tok = lax.sort_key_val(slot.reshape(-1), tok.reshape(-1))
    n_blk = a // MOE_BLK // MOE_CHUNKS
    rows_c = n_blk * MOE_BLK
    yb = None
    for c in range(MOE_CHUNKS):
        xs = h2.at[slot_tok[c * rows_c:(c + 1) * rows_c]].get(mode="promise_in_bounds")
        items = _work_items(cnt_start, cnt_end, c * n_blk, n_blk)
        yb = _experts(items, xs, w_gu, bgu, w_dn, bdn, layer, c * n_blk, a, yb)
    ys = [yb.at[slot[k]].get(mode="promise_in_bounds") for k in range(TOP_K)]
    return ys, gate.T


DN_GW = 256
HALO = 8
DN_PAIR = 4
_M_EYE, _M_BLK16, _M_OFF32, _M_OFF64, _M_INCL, _M_STRICT = 0, 1, 2, 3, 4, 6


def _dn_masks():
    row = np.arange(DN_CHUNK)[:, None]
    col = np.arange(DN_GW)[None, :] % DN_CHUNK
    b16 = (row // 16) == (col // 16)
    b32 = (row // 32) == (col // 32)
    m = np.stack([row == col, b16, b32 & ~b16, ~b32, col <= row, col >= row, col < row, col > row])
    hid = np.arange(DN_GW) // DN_HEAD_DIM
    return jnp.asarray(m, F32), jnp.asarray(hid[:, None] == hid[None, :], BF16)


def _split3(a):
    hi = a.astype(BF16)
    r = a - hi.astype(F32)
    lo = r.astype(BF16)
    lo2 = (r - lo.astype(F32)).astype(BF16)
    return hi, lo, lo2


def _bd(x, bdm_ref):
    xb = x.astype(BF16)
    return jnp.concatenate([xb, xb, xb, xb], axis=0) * bdm_ref[...]


def _mm(a, wbd):
    return _dot(a.astype(BF16), wbd)


def _tri_inverse(l_mats, m_ref, bdm_ref):
    half = DN_CHUNK
    bd = lambda v: _bd(v, bdm_ref)
    lds = [l * m_ref[_M_BLK16] for l in l_mats]
    ps = [_mm(ld, bd(ld)) for ld in lds]
    xs = [m_ref[_M_EYE] - ld for ld in lds]
    for _ in range(2):
        rs = [_mm(jnp.concatenate([p, x], axis=0), bd(p)) for p, x in zip(ps, xs)]
        ps = [r[0:half] for r in rs]
        xs = [x + r[half:2 * half] for x, r in zip(xs, rs)]
    xs = [x + _mm(x, bd(p)) for x, p in zip(xs, ps)]
    ts = [_mm(x, bd(l * m_ref[_M_OFF32])) for x, l in zip(xs, l_mats)]
    xs = [x - _mm(t, bd(x)) for x, t in zip(xs, ts)]
    ts = [_mm(x, bd(l * m_ref[_M_OFF64])) for x, l in zip(xs, l_mats)]
    return [x - _mm(t, bd(x)) for x, t in zip(xs, ts)]


def _dn_prep_body(cur_ref, prev_ref, next_ref, cw_ref, dg_ref, gc_ref, g512_ref, e_ref, tri_ref, m_ref, bdm_ref,
                  u_ref, w_ref, kd_ref, qd_ref, aq_ref, gl_ref,
                  ext_ref, q_s, k_s, v_s, gx_s, bx_s):
    j = pl.program_id(1)
    nbp = pl.num_programs(1)
    C = DN_CHUNK
    ext_ref[HALO:HALO + TM, :] = cur_ref[...]
    ext_ref[0:HALO, :] = jnp.where(j >= 2, prev_ref[...], 0.0)
    ext_ref[HALO + TM:2 * HALO + TM, :] = jnp.where((j >= 1) & (j < nbp - 1), next_ref[...], 0.0)
    y = None
    for t in range(DN_CONV):
        o = HALO - DN_CONV // 2 + t
        term = cw_ref[t:t + 1, :] * ext_ref[o:o + TM, :]
        y = term if y is None else y + term
    y = y * jax.nn.sigmoid(y)

    g512 = g512_ref[...]

    def head_sumsq(x):
        hi, lo = _split_bf16(x * x)
        return _dot(hi, g512) + _dot(lo, g512)

    q = y[:, 0:512]
    k = y[:, 512:1024]
    q_s[...] = q * lax.rsqrt(head_sumsq(q) + NORM_EPS) * (DN_HEAD_DIM ** -0.5)
    k_s[...] = k * lax.rsqrt(head_sumsq(k) + NORM_EPS)
    v_s[...] = y[:, 1024:1536]

    dg = dg_ref[...]
    beta_all = jax.nn.sigmoid(dg)
    z = dg + gc_ref[1:2, :]
    g_all = gc_ref[0:1, :] * (jnp.maximum(z, 0.0) + jnp.log(1.0 + jnp.exp(-jnp.abs(z))))
    bh, bl = _split_bf16(beta_all)
    bcat = jnp.concatenate([bh, bl], axis=0)
    gparts = jnp.concatenate(_split3(g_all), axis=1)
    for d in range(2):
        cs = _dot(tri_ref[d], gparts)
        gcum = cs[:, 0:128] + cs[:, 128:256] + cs[:, 256:384]
        gcat = jnp.concatenate(_split3(gcum), axis=0)
        for grp in range(2):
            eg = _dot(gcat, e_ref[4 + 2 * d + grp])
            gx_s[d, grp] = eg[0:TM] + eg[TM:2 * TM] + eg[2 * TM:3 * TM]
            eb = _dot(bcat, e_ref[2 * d + grp])
            bx_s[d, grp] = eb[0:TM] + eb[TM:2 * TM]

    def chunk_pair(it, carry):
        units = []
        l_mats = []
        for ci in range(DN_PAIR):
            cc = it * DN_PAIR + ci
            rows = pl.ds(pl.multiple_of(cc * C, C), C)
            for grp in range(2):
                cols = slice(grp * DN_GW, (grp + 1) * DN_GW)
                kg, qg = k_s[rows, cols], q_s[rows, cols]
                kb = [kg * bx_s[d, grp, rows, :] for d in range(2)]
                raw = _dot_nt(jnp.concatenate([kb[0], kb[1], qg], axis=0).astype(BF16), _bd(kg, bdm_ref))
                for d in range(2):
                    gx = gx_s[d, grp, rows, :]
                    rvec = jnp.sum(gx * m_ref[_M_EYE], axis=0, keepdims=True)
                    dec = jnp.exp(jnp.minimum(gx - rvec, 0.0)) * m_ref[_M_INCL + d]
                    l_mats.append(raw[C * d:C * (d + 1)] * dec * m_ref[_M_STRICT + d])
                    glast = gx[C - 1:C, :] if d == 0 else gx[0:1, :]
                    kd_ref[d, rows, cols] = (kg * jnp.exp(glast - gx)).astype(BF16)
                    qd_ref[d, rows, cols] = (qg * jnp.exp(gx)).astype(BF16)
                    aq_ref[d, rows, cols] = (raw[2 * C:3 * C] * dec).astype(BF16)
                    gl_ref[d, pl.ds(cc, 1), :, cols] = jnp.exp(glast).reshape(1, 1, DN_GW)
                    units.append((rows, cols, d, grp))
        tinvs = _tri_inverse(l_mats, m_ref, bdm_ref)
        for (rows, cols, d, grp), tinv in zip(units, tinvs):
            bx = bx_s[d, grp, rows, :]
            u_ref[d, rows, cols] = _mm(tinv, _bd(v_s[rows, cols] * bx, bdm_ref))
            w_ref[d, rows, cols] = _mm(tinv, _bd(k_s[rows, cols] * bx * jnp.exp(gx_s[d, grp, rows, :]), bdm_ref)
                                       ).astype(BF16)
        return carry

    lax.fori_loop(0, TM // C // DN_PAIR, chunk_pair, 0)


def _dn_prep(dqkv, dg, conv_w, a_log, dt_bias, n_batch, rows):
    n = dqkv.shape[0]
    nbp = rows // TM
    hb = TM // HALO
    H = DN_HEADS
    cw = jnp.zeros((8, 1536), F32).at[0:DN_CONV].set(conv_w)
    gc = jnp.zeros((8, LANES), F32)
    gc = gc.at[0, 2 * H:4 * H].set(-jnp.exp(a_log.reshape(-1))).at[1, 2 * H:4 * H].set(dt_bias.reshape(-1))
    hid = np.arange(512) // DN_HEAD_DIM
    g512 = jnp.asarray(hid[:, None] == hid[None, :], BF16)
    e = np.zeros((8, LANES, DN_GW), np.float32)
    for kind in range(2):
        for d in range(2):
            for grp in range(2):
                for h in range(4):
                    e[4 * kind + 2 * d + grp, 16 * kind + 8 * d + 4 * grp + h, 64 * h:64 * (h + 1)] = 1.0
    e = jnp.asarray(e, BF16)
    t = np.arange(TM)
    same = (t[:, None] // DN_CHUNK) == (t[None, :] // DN_CHUNK)
    tri = jnp.asarray(np.stack([same & (t[None, :] <= t[:, None]), same & (t[None, :] >= t[:, None])]), BF16)
    masks, bdm = _dn_masks()
    blk = lambda b, j: (b * nbp + j, 0)
    full2 = lambda b, j: (0, 0)
    full3 = lambda b, j: (0, 0, 0)
    dblk = lambda b, j: (0, b * nbp + j, 0)
    nlast = n // HALO - 1
    outs = pl.pallas_call(
        _dn_prep_body,
        out_shape=[jax.ShapeDtypeStruct((2, n, 512), F32)] + [jax.ShapeDtypeStruct((2, n, 512), BF16)] * 4
        + [jax.ShapeDtypeStruct((2, n // DN_CHUNK, 1, 512), F32)],
        grid=(n_batch, nbp),
        in_specs=[pl.BlockSpec((TM, 1536), blk),
                  pl.BlockSpec((HALO, 1536), lambda b, j: (jnp.maximum((b * nbp + j) * hb - 1, 0), 0)),
                  pl.BlockSpec((HALO, 1536), lambda b, j: (jnp.minimum((b * nbp + j + 1) * hb, nlast), 0)),
                  pl.BlockSpec((8, 1536), full2), pl.BlockSpec((TM, LANES), blk), pl.BlockSpec((8, LANES), full2),
                  pl.BlockSpec((512, 512), full2), pl.BlockSpec((8, LANES, DN_GW), full3),
                  pl.BlockSpec((2, TM, TM), full3), pl.BlockSpec((8, DN_CHUNK, DN_GW), full3),
                  pl.BlockSpec((DN_GW, DN_GW), full2)],
        out_specs=[pl.BlockSpec((2, TM, 512), dblk)] * 5
        + [pl.BlockSpec((2, TM // DN_CHUNK, 1, 512), lambda b, j: (0, b * nbp + j, 0, 0))],
        scratch_shapes=[pltpu.VMEM((TM + 2 * HALO, 1536), F32), pltpu.VMEM((TM, 512), F32),
                        pltpu.VMEM((TM, 512), F32), pltpu.VMEM((TM, 512), F32),
                        pltpu.VMEM((2, 2, TM, DN_GW), F32), pltpu.VMEM((2, 2, TM, DN_GW), F32)],
        compiler_params=_cparams(2),
        name="dn_prep",
    )(dqkv, dqkv, dqkv, cw, dg, gc, g512, e, tri, masks, bdm)
    return outs, g512


DN_SCAN_NB = 4


def _dn_scan_body(uf, wf, kf, qf, af, gf, ub, wb, kb, qb, ab, gb, bdm_ref, of_ref, ob_ref, s_ref):
    @pl.when(pl.program_id(1) == 0)
    def _():
        s_ref[...] = jnp.zeros_like(s_ref)

    same = bdm_ref[...] != 0
    dirs = ((uf, wf, kf, qf, af, gf, of_ref), (ub, wb, kb, qb, ab, gb, ob_ref))
    units = [(bb, d, grp) for bb in range(DN_SCAN_NB) for d in range(2) for grp in range(2)]
    cols = [slice(grp * DN_GW, (grp + 1) * DN_GW) for grp in range(2)]
    s = [s_ref[bb, d, grp] for bb, d, grp in units]
    sq = [_dot(jnp.concatenate([dirs[d][1][0, bb, :, cols[grp]], dirs[d][3][0, bb, :, cols[grp]]], axis=0),
               st.astype(BF16)) for (bb, d, grp), st in zip(units, s)]
    vnew = [dirs[d][0][0, bb, :, cols[grp]] - r[0:DN_CHUNK] for (bb, d, grp), r in zip(units, sq)]
    intra = [_dot(dirs[d][4][0, bb, :, cols[grp]], _bd(v, bdm_ref)) for (bb, d, grp), v in zip(units, vnew)]
    upd = [lax.dot_general(dirs[d][2][0, bb, :, cols[grp]], v.astype(BF16), (((0,), (0,)), ((), ())),
                           preferred_element_type=F32) for (bb, d, grp), v in zip(units, vnew)]
    for (bb, d, grp), st, r, a, up in zip(units, s, sq, intra, upd):
        dirs[d][6][bb, :, cols[grp]] = r[DN_CHUNK:2 * DN_CHUNK] + a
        s_ref[bb, d, grp] = st * dirs[d][5][0, bb, 0, :, cols[grp]] + jnp.where(same, up, 0.0)


def _dn_scan(u, w, kd, qd, aq, gl, n_batch, rows, t_ctx):
    nch = rows // DN_CHUNK
    nctx = t_ctx // DN_CHUNK
    assert n_batch % DN_SCAN_NB == 0

    def cb(c):
        return jnp.where(c < nctx, nctx - 1 - c, nch - 1 - (c - nctx))

    per_batch = lambda a: a.reshape(2, n_batch, rows, 512)
    u, w, kd, qd, aq = map(per_batch, (u, w, kd, qd, aq))
    gl = gl.reshape(2, n_batch, nch, 1, 512)
    blk = (1, DN_SCAN_NB, DN_CHUNK, 512)
    gblk = (1, DN_SCAN_NB, 1, 1, 512)
    specs = []
    for d, ch in ((0, lambda c: c), (1, cb)):
        specs += [pl.BlockSpec(blk, lambda b, c, d=d, ch=ch: (d, b, ch(c), 0))] * 5
        specs += [pl.BlockSpec(gblk, lambda b, c, d=d, ch=ch: (d, b, ch(c), 0, 0))]
    specs.append(pl.BlockSpec((DN_GW, DN_GW), lambda b, c: (0, 0)))
    o_f, o_b = pl.pallas_call(
        _dn_scan_body,
        out_shape=[jax.ShapeDtypeStruct((n_batch, rows, 512), F32)] * 2,
        grid=(n_batch // DN_SCAN_NB, nch),
        in_specs=specs,
        out_specs=[pl.BlockSpec((DN_SCAN_NB, DN_CHUNK, 512), lambda b, c: (b, c, 0)),
                   pl.BlockSpec((DN_SCAN_NB, DN_CHUNK, 512), lambda b, c: (b, cb(c), 0))],
        scratch_shapes=[pltpu.VMEM((DN_SCAN_NB, 2, 2, DN_GW, DN_GW), F32)],
        compiler_params=_cparams(2),
        name="dn_scan",
    )(u, w, kd, qd, aq, gl, u, w, kd, qd, aq, gl, _dn_masks()[1])
    return o_f.reshape(n_batch * rows, 512), o_b.reshape(n_batch * rows, 512)


def _deltanet(dqkv, dg, conv_w, a_log, dt_bias, n_batch, rows, t_ctx):
    (u, w, kd, qd, aq, gl), g512 = _dn_prep(dqkv, dg, conv_w, a_log, dt_bias, n_batch, rows)
    o_f, o_b = _dn_scan(u, w, kd, qd, aq, gl, n_batch, rows, t_ctx)
    return o_f, o_b, g512


def _combine_body(x_ref, y0_ref, y1_ref, y2_ref, y3_ref, gate_ref, mod_ref, fg_ref, o_ref, *, final):
    gate = gate_ref[...]
    y = None
    for k, y_ref in enumerate((y0_ref, y1_ref, y2_ref, y3_ref)):
        yk = y_ref[...] * gate[:, k:k + 1]
        y = yk if y is None else y + yk
    xo = x_ref[...] + mod_ref[0, 0, 5:6, :] * y
    o_ref[...] = _rms(xo, fg_ref[...]) if final else xo


def _combine(x, ys, gate_rows, modtab, final_g, nb_per_batch, latent_only):
    n = x.shape[0]
    if latent_only:
        nlat = nb_per_batch - 1
        mod_idx = lambda i: (i // nlat, 1, 0, 0)
    else:
        mod_idx = lambda i: (i // nb_per_batch, jnp.minimum(i % nb_per_batch, 1), 0, 0)
    row = lambda i: (i, 0)
    return pl.pallas_call(
        functools.partial(_combine_body, final=latent_only),
        out_shape=jax.ShapeDtypeStruct((n, D_MODEL), F32),
        grid=(n // TM,),
        in_specs=[pl.BlockSpec((TM, D_MODEL), row)] * 5
        + [pl.BlockSpec((TM, TOP_K), row), pl.BlockSpec((1, 1, 6, D_MODEL), mod_idx),
           pl.BlockSpec((1, D_MODEL), lambda i: (0, 0))],
        out_specs=pl.BlockSpec((TM, D_MODEL), row),
        compiler_params=_cparams(),
        name="combine",
    )(x, *ys, gate_rows, modtab, final_g)


def _rope_tables(seq, t_ctx, rot_dim):
    rows = seq // GRID_W
    row = jnp.broadcast_to(jnp.arange(rows)[:, None], (rows, GRID_W)).reshape(-1).astype(F32)
    col = jnp.broadcast_to(jnp.arange(GRID_W)[None, :], (rows, GRID_W)).reshape(-1).astype(F32)
    n_freq = rot_dim // 4
    inv_freq = ROPE_THETA ** (-jnp.arange(n_freq, dtype=F32) / n_freq)
    ang = jnp.concatenate([row[:, None] * inv_freq, col[:, None] * inv_freq], axis=-1)
    cos, sin = jnp.cos(ang), jnp.sin(ang)
    zero = jnp.zeros_like(sin)
    c = jnp.concatenate([cos, cos], axis=-1)
    sa = jnp.concatenate([-sin, zero], axis=-1)
    sb = jnp.concatenate([zero, sin], axis=-1)
    ident = jnp.stack([jnp.ones((t_ctx, rot_dim), F32), jnp.zeros((t_ctx, rot_dim), F32),
                       jnp.zeros((t_ctx, rot_dim), F32)])
    tab = jnp.concatenate([ident, jnp.stack([c, sa, sb])], axis=1)
    return jnp.tile(tab, (1, 1, LANES // rot_dim))


def _layer_weights(w_in, w_q_up, w_kv_up, w_branch, w_out, router_w, router_b, b_gu, b_dn):
    sizes = (MLA_Q_RANK, MLA_KV_RANK, MLA_ROPE, 512, 128, 128, 1536, 512, 16, 16, N_BRANCH * D_MODEL)
    offs = np.cumsum((0,) + sizes)
    part = lambda i: w_in[:, offs[i]:offs[i + 1]]
    pad = jnp.zeros((D_MODEL, LANES - 32), F32)
    w_proj = jnp.concatenate(
        [part(0), part(1), part(3), part(4), part(5), part(6), part(7),
         jnp.tile(part(2), (1, LANES // MLA_ROPE)), part(8), part(9), pad], axis=1).astype(BF16)
    w_gate = part(10).astype(BF16)
    hq = MLA_NOPE + MLA_ROPE
    wq = w_q_up.reshape(MLA_Q_RANK, MLA_HEADS, hq)
    wq = jnp.concatenate([wq[:, :, :MLA_NOPE].reshape(MLA_Q_RANK, -1),
                          wq[:, :, MLA_NOPE:].reshape(MLA_Q_RANK, -1)], axis=1).astype(BF16)
    wkv = w_kv_up.reshape(MLA_KV_RANK, MLA_HEADS, MLA_NOPE + MLA_V)
    wkv = jnp.concatenate([wkv[:, :, :MLA_NOPE].reshape(MLA_KV_RANK, -1),
                           wkv[:, :, MLA_NOPE:].reshape(MLA_KV_RANK, -1)], axis=1).astype(BF16)
    bgu = jnp.concatenate([b_gu[:, 0::2], b_gu[:, 1::2]], axis=-1).reshape(N_EXPERTS, 1, 2 * D_EXPERT)
    return dict(w_proj=w_proj, w_gate=w_gate, wq=wq, wkv=wkv, wb=w_branch.astype(BF16),
                wo=w_out.astype(BF16), rwt=router_w.T, rb=router_b.reshape(N_EXPERTS, 1),
                bgu=bgu, bdn=b_dn.reshape(N_EXPERTS, 1, D_MODEL))


def kernel(x, c, ctx, c_ctx, w_mod, b_mod, norm1_g, norm2_g, w_in, mla_q_norm_g, mla_w_q_up,
           mla_kv_norm_g, mla_w_kv_up, swa_sink, dn_conv_w, dn_a_log, dn_dt_bias, dn_norm_g,
           w_branch, w_out, router_w, router_b, exp_w_gu, exp_b_gu, exp_w_dn, exp_b_dn, final_norm_g):
    B, S, D = x.shape
    T = ctx.shape[1]
    R = T + S
    depth = w_mod.shape[0]
    assert D == D_MODEL and T == TM and S % TM == 0 and (B * R) % ROUTE_TB == 0
    nbp = R // TM
    xa = jnp.concatenate([ctx, x], axis=1).reshape(B * R, D)
    rope_s = _rope_tables(S, T, SWA_HEAD_DIM)
    rope_m = _rope_tables(S, T, MLA_ROPE)
    cvec = jnp.concatenate([c, c_ctx[None, :]], axis=0)
    for l in range(depth):
        w = _layer_weights(w_in[l], mla_w_q_up[l], mla_w_kv_up[l], w_branch[l], w_out[l], router_w[l],
                           router_b[l], exp_b_gu[l], exp_b_dn[l])
        mod = _modulation(cvec, w_mod[l], b_mod[l]).reshape(B + 1, 6, D)
        modtab = jnp.stack([jnp.broadcast_to(mod[B][None], (B, 6, D)), mod[:B]], axis=1)
        g1 = norm1_g[l].reshape(1, D)
        g2 = norm2_g[l].reshape(1, D)
        cq, ckv, krt, sq, sk, sv, dqkv, dz, dg = _inproj(xa, modtab, g1, w["w_proj"], rope_s, nbp)
        q_m, k_m, v_m = _mla_prep(cq, ckv, krt, mla_q_norm_g[l].reshape(1, -1), mla_kv_norm_g[l].reshape(1, -1),
                                  w["wq"], w["wkv"], rope_m, nbp)
        o_a = _mla_attention(q_m, k_m, v_m, B, R)
        o_b = _swa_attention(sq, sk, sv, swa_sink[l], B, T, S)
        dn_f, dn_b, g512 = _deltanet(dqkv, dg, dn_conv_w[l], dn_a_log[l], dn_dt_bias[l], B, R, T)
        last = l == depth - 1
        xn, h2, logits_t = _merge(xa, modtab, g1, g2, w["w_gate"], o_a, o_b, dn_f, dn_b, dz,
                                  jnp.tile(dn_norm_g[l], DN_HEADS).reshape(1, BRANCH_W), g512, w["wb"], w["wo"],
                                  w["rwt"], w["rb"], nbp, last)
        ys, gate_rows = _moe(h2, logits_t, exp_w_gu, w["bgu"], exp_w_dn, w["bdn"], l)
        xa = _combine(xn, ys, gate_rows, modtab, final_norm_g.reshape(1, D), nbp, last)
    return xa.reshape(B, S, D)
```

```python
import functools

import jax
import jax.numpy as jnp
import numpy as np
from jax import lax
from jax.experimental import pallas as pl
from jax.experimental.pallas import tpu as pltpu

F32 = jnp.float32
BF16 = jnp.bfloat16

D_MODEL = 1024
GRID_W = 64
ROPE_THETA = 10000.0
NORM_EPS = 1e-6
MLA_HEADS = 8
MLA_Q_RANK = 384
MLA_KV_RANK = 256
MLA_NOPE = 64
MLA_ROPE = 32
MLA_V = 64
SWA_HEADS = 8
SWA_KV_HEADS = 2
SWA_HEAD_DIM = 64
SWA_WINDOW = 128
DN_HEADS = 8
DN_HEAD_DIM = 64
DN_CONV = 5
DN_CHUNK = 64
N_BRANCH = 3
BRANCH_W = 512
N_EXPERTS = 32
TOP_K = 4
D_EXPERT = 1024
SWIGLU_LIMIT = 7.0
SWIGLU_ALPHA = 1.702
MOE_BLK = 512
MOE_CHUNKS = 4

LANES = 128
LOG2E = 1.4426950408889634
TM = 256
VMEM_LIMIT = 56 * 1024 * 1024

_C_CQ = (0, 384)
_C_CKV = (384, 640)
_C_SQ = (640, 1152)
_C_SK = (1152, 1280)
_C_SV = (1280, 1408)
_C_DQKV = (1408, 2944)
_C_DZ = (2944, 3456)
_C_KRT = (3456, 3584)
_C_DG = (3584, 3712)
_N_PROJ = 3712


def _cparams(n_axes=1):
    return pltpu.CompilerParams(dimension_semantics=("arbitrary",) * n_axes, vmem_limit_bytes=VMEM_LIMIT)


def _dot(a, b):
    return jnp.dot(a, b, preferred_element_type=F32)


def _dot_nt(a, b):
    return lax.dot_general(a, b, (((1,), (1,)), ((), ())), preferred_element_type=F32)


def _split_bf16(a):
    hi = a.astype(BF16)
    lo = (a - hi.astype(F32)).astype(BF16)
    return hi, lo


def _dot3(a, b):
    ah, al = _split_bf16(a)
    bh, bl = _split_bf16(b)
    return _dot(ah, bh) + (_dot(ah, bl) + _dot(al, bh))


def _dot3_nt(a, b):
    ah, al = _split_bf16(a)
    bh, bl = _split_bf16(b)
    return _dot_nt(ah, bh) + (_dot_nt(ah, bl) + _dot_nt(al, bh))


def _rms(x, g):
    return x * lax.rsqrt(jnp.mean(x * x, axis=-1, keepdims=True) + NORM_EPS) * g


def _rope_cols(x, tab_ref, half):
    c, sa, sb = tab_ref[0], tab_ref[1], tab_ref[2]
    return x * c + pltpu.roll(x, LANES - half, axis=1) * sa + pltpu.roll(x, half, axis=1) * sb


def _mod_body(c_ref, w_ref, b_ref, o_ref):
    c = c_ref[...]
    a = c * jax.nn.sigmoid(c)
    o_ref[...] = _dot3(a, w_ref[...]) + b_ref[...]


def _modulation(cvec, w_mod, b_mod):
    m = cvec.shape[0]
    n = w_mod.shape[1]
    tn = 512
    return pl.pallas_call(
        _mod_body,
        out_shape=jax.ShapeDtypeStruct((m, n), F32),
        grid=(n // tn,),
        in_specs=[pl.BlockSpec((m, D_MODEL), lambda j: (0, 0)),
                  pl.BlockSpec((D_MODEL, tn), lambda j: (0, j)),
                  pl.BlockSpec((1, tn), lambda j: (0, j))],
        out_specs=pl.BlockSpec((m, tn), lambda j: (0, j)),
        compiler_params=_cparams(),
        name="modulation",
    )(cvec, w_mod, b_mod.reshape(1, n))


IN_UNITS = 2


def _inproj_body(*refs):
    x_ref = refs[0]
    mod_refs = refs[1:1 + IN_UNITS]
    g_ref, w_ref = refs[1 + IN_UNITS:3 + IN_UNITS]
    rope_refs = refs[3 + IN_UNITS:3 + 2 * IN_UNITS]
    cq_ref, ckv_ref, krt_ref, sq_ref, sk_ref, sv_ref, dqkv_ref, dz_ref, dg_ref = refs[3 + 2 * IN_UNITS:]
    rows = [slice(u * TM, (u + 1) * TM) for u in range(IN_UNITS)]
    hs = [(_rms(x_ref[r, :], g_ref[...]) * (1.0 + m[0, 0, 1:2, :]) + m[0, 0, 0:1, :]).astype(BF16)
          for r, m in zip(rows, mod_refs)]

    def proj(cols):
        return [_dot(h, w_ref[:, cols[0]:cols[1]]) for h in hs]

    for ref, cols in ((cq_ref, _C_CQ), (ckv_ref, _C_CKV), (krt_ref, _C_KRT), (dqkv_ref, _C_DQKV),
                      (dz_ref, _C_DZ), (dg_ref, _C_DG)):
        for r, p in zip(rows, proj(cols)):
            ref[r, :] = p
    for r, p in zip(rows, proj(_C_SV)):
        sv_ref[r, :] = p.astype(BF16)
    for r, p, rope_ref in zip(rows, proj(_C_SK), rope_refs):
        sk_ref[r, :] = _rope_cols(p, rope_ref, SWA_HEAD_DIM // 2).astype(BF16)
    lane = lax.broadcasted_iota(jnp.int32, (TM, LANES), 1)
    lo = lane < SWA_HEAD_DIM
    for r, sq, rope_ref in zip(rows, proj(_C_SQ), rope_refs):
        for c in range(4):
            xr = _rope_cols(sq[:, c * LANES:(c + 1) * LANES], rope_ref, SWA_HEAD_DIM // 2)
            xs = pltpu.roll(xr, SWA_HEAD_DIM, axis=1)
            if c < 2:
                a, b = jnp.where(lo, xr, 0.0), jnp.where(lo, xs, 0.0)
            else:
                a, b = jnp.where(lo, 0.0, xs), jnp.where(lo, 0.0, xr)
            sq_ref[r, (2 * c) * LANES:(2 * c + 1) * LANES] = a.astype(BF16)
            sq_ref[r, (2 * c + 1) * LANES:(2 * c + 2) * LANES] = b.astype(BF16)


def _inproj(x, modtab, g, w, rope_s, nb_per_batch):
    n = x.shape[0]
    step = TM * IN_UNITS
    assert n % step == 0
    row = lambda i: (i, 0)
    unit = lambda i, u: i * IN_UNITS + u
    widths = [(384, F32), (256, F32), (128, F32), (1024, BF16), (128, BF16), (128, BF16),
              (1536, F32), (512, F32), (128, F32)]
    mod_specs = [pl.BlockSpec((1, 1, 6, D_MODEL),
                              lambda i, u=u: (unit(i, u) // nb_per_batch,
                                              jnp.minimum(unit(i, u) % nb_per_batch, 1), 0, 0))
                 for u in range(IN_UNITS)]
    rope_specs = [pl.BlockSpec((3, TM, LANES), lambda i, u=u: (0, unit(i, u) % nb_per_batch, 0))
                  for u in range(IN_UNITS)]
    return pl.pallas_call(
        _inproj_body,
        out_shape=[jax.ShapeDtypeStruct((n, wd), dt) for wd, dt in widths],
        grid=(n // step,),
        in_specs=[pl.BlockSpec((step, D_MODEL), row)] + mod_specs
        + [pl.BlockSpec((1, D_MODEL), lambda i: (0, 0)), pl.BlockSpec((D_MODEL, _N_PROJ), lambda i: (0, 0))]
        + rope_specs,
        out_specs=[pl.BlockSpec((step, wd), row) for wd, _ in widths],
        compiler_params=_cparams(),
        name="inproj",
    )(x, *([modtab] * IN_UNITS), g, w, *([rope_s] * IN_UNITS))


def _mla_prep_body(cq_ref, ckv_ref, krt_ref, qg_ref, kvg_ref, wq_ref, wkv_ref, rope_ref,
                   q_ref, k_ref, v_ref):
    half = MLA_ROPE // 2
    qn = _rms(cq_ref[...], qg_ref[...]).astype(BF16)
    q = _dot(qn, wq_ref[...])
    q_ref[:, 0:512] = q[:, 0:512].astype(BF16)
    for c in range(2):
        lo_, hi_ = 512 + c * LANES, 512 + (c + 1) * LANES
        q_ref[:, lo_:hi_] = _rope_cols(q[:, lo_:hi_], rope_ref, half).astype(BF16)
    kvn = _rms(ckv_ref[...], kvg_ref[...]).astype(BF16)
    kv = _dot(kvn, wkv_ref[...])
    kr = _rope_cols(krt_ref[...], rope_ref, half).astype(BF16)
    for c in range(4):
        k_ref[:, (2 * c) * LANES:(2 * c + 1) * LANES] = kv[:, c * LANES:(c + 1) * LANES].astype(BF16)
        k_ref[:, (2 * c + 1) * LANES:(2 * c + 2) * LANES] = kr
    v_ref[...] = kv[:, 512:1024].astype(BF16)


def _mla_prep(cq, ckv, krt, qg, kvg, wq, wkv, rope_m, nb_per_batch):
    n = cq.shape[0]
    row = lambda i: (i, 0)
    full = lambda i: (0, 0)
    return pl.pallas_call(
        _mla_prep_body,
        out_shape=[jax.ShapeDtypeStruct((n, 768), BF16), jax.ShapeDtypeStruct((n, 1024), BF16),
                   jax.ShapeDtypeStruct((n, 512), BF16)],
        grid=(n // TM,),
        in_specs=[pl.BlockSpec((TM, 384), row), pl.BlockSpec((TM, 256), row), pl.BlockSpec((TM, 128), row),
                  pl.BlockSpec((1, 384), full), pl.BlockSpec((1, 256), full),
                  pl.BlockSpec((384, 768), full), pl.BlockSpec((256, 1024), full),
                  pl.BlockSpec((3, TM, LANES), lambda i: (0, i % nb_per_batch, 0))],
        out_specs=[pl.BlockSpec((TM, 768), row), pl.BlockSpec((TM, 1024), row), pl.BlockSpec((TM, 512), row)],
        compiler_params=_cparams(),
        name="mla_prep",
    )(cq, ckv, krt, qg, kvg, wq, wkv, rope_m)


def _mla_attn_body(q_ref, k_ref, v_ref, o_ref, *, t_ctx):
    c1 = (MLA_NOPE + MLA_ROPE) ** -0.5 * LOG2E
    lane = lax.broadcasted_iota(jnp.int32, (TM, LANES), 1)
    zero = jnp.zeros((TM, LANES), BF16)

    def attend(nk):
        for c in range(4):
            kc = k_ref[0:nk, c * 2 * LANES:(c + 1) * 2 * LANES]
            vc = v_ref[0:nk, c * LANES:(c + 1) * LANES]
            qn = q_ref[:, c * LANES:(c + 1) * LANES]
            qr = q_ref[:, 512 + (c // 2) * LANES:512 + (c // 2 + 1) * LANES]
            qa = [jnp.concatenate(
                [jnp.where((lane >= 64 * s) & (lane < 64 * (s + 1)), qn, zero),
                 jnp.where((lane >= 32 * ((2 * c + s) % 4)) & (lane < 32 * ((2 * c + s) % 4 + 1)), qr, zero)],
                axis=1) for s in range(2)]
            sc = [_dot_nt(q, kc) * c1 for q in qa]
            m = [jnp.max(t, axis=-1, keepdims=True) for t in sc]
            e = [jnp.exp2(t - mm) for t, mm in zip(sc, m)]
            l = [jnp.sum(t, axis=-1, keepdims=True) for t in e]
            outs = [_dot(t.astype(BF16), vc) / ll for t, ll in zip(e, l)]
            o_ref[:, c * LANES:(c + 1) * LANES] = jnp.where(lane < 64, outs[0], outs[1]).astype(BF16)

    j = pl.program_id(1)

    @pl.when(j == 0)
    def _():
        attend(t_ctx)

    @pl.when(j > 0)
    def _():
        attend(k_ref.shape[0])


def _mla_attention(q, k, v, n_batch, rows):
    nbp = rows // TM
    return pl.pallas_call(
        functools.partial(_mla_attn_body, t_ctx=TM),
        out_shape=jax.ShapeDtypeStruct((n_batch * rows, BRANCH_W), BF16),
        grid=(n_batch, nbp),
        in_specs=[pl.BlockSpec((TM, 768), lambda b, j: (b * nbp + j, 0)),
                  pl.BlockSpec((rows, 1024), lambda b, j: (b, 0)),
                  pl.BlockSpec((rows, 512), lambda b, j: (b, 0))],
        out_specs=pl.BlockSpec((TM, BRANCH_W), lambda b, j: (b * nbp + j, 0)),
        compiler_params=_cparams(2),
        name="mla_attention",
    )(q, k, v)


SWA_QB = 128
SWA_NQ = 2


def _swa_body(q_ref, k_ref, v_ref, sink_ref, o_ref, *, t_ctx, seq):
    c1 = SWA_HEAD_DIM ** -0.5 * LOG2E
    j = pl.program_id(1)
    n_ctx_steps = t_ctx // (SWA_QB * SWA_NQ)
    lane = lax.broadcasted_iota(jnp.int32, (SWA_QB, LANES), 1)
    lo = lane < SWA_HEAD_DIM
    win = 3 * SWA_QB

    def finish(rows, parts):
        for c in range(4):
            g = c // 2
            r0 = (2 * (c % 2)) * SWA_QB
            a = parts[g][r0:r0 + SWA_QB]
            b = parts[g][r0 + SWA_QB:r0 + 2 * SWA_QB]
            if g == 0:
                col = jnp.where(lo, a, pltpu.roll(b, SWA_HEAD_DIM, axis=1))
            else:
                col = jnp.where(lo, pltpu.roll(a, SWA_HEAD_DIM, axis=1), b)
            o_ref[rows, c * LANES:(c + 1) * LANES] = col.astype(BF16)

    def blocks(blks):
        kc, vc = k_ref[0:t_ctx, :], v_ref[0:t_ctx, :]
        windowed = blks[0] is not None
        kw, vw, band = [], [], []
        for blk in blks if windowed else ():
            start = jnp.clip((blk - 1) * SWA_QB, 0, seq - win)
            rs = pl.multiple_of(t_ctx + start, SWA_QB)
            kw.append(k_ref[pl.ds(rs, win), :])
            vw.append(v_ref[pl.ds(rs, win), :])
            qpos = blk * SWA_QB + lax.broadcasted_iota(jnp.int32, (SWA_QB, win), 0)
            kpos = start + lax.broadcasted_iota(jnp.int32, (SWA_QB, win), 1)
            band1 = jnp.abs(kpos - qpos) <= SWA_WINDOW
            band.append(jnp.concatenate([band1] * 4, axis=0))
        units = [(sub, g) for sub in range(SWA_NQ) for g in range(SWA_KV_HEADS)]
        rows = [slice(sub * SWA_QB, (sub + 1) * SWA_QB) for sub in range(SWA_NQ)]
        qg = [jnp.concatenate([q_ref[rows[sub], (4 * g + i) * LANES:(4 * g + i + 1) * LANES] for i in range(4)],
                              axis=0) for sub, g in units]
        sk = [c1 * jnp.concatenate(
            [jnp.broadcast_to(sink_ref[4 * g + i:4 * g + i + 1, 0:1], (SWA_QB, 1)) for i in range(4)], axis=0)
            for _, g in units]
        t_c = [_dot_nt(q, kc) * c1 for q in qg]
        m = [jnp.maximum(jnp.max(t, axis=-1, keepdims=True), s) for t, s in zip(t_c, sk)]
        if windowed:
            t_w = [jnp.where(band[sub], _dot_nt(q, kw[sub]) * c1, -jnp.inf) for (sub, _), q in zip(units, qg)]
            m = [jnp.maximum(mm, jnp.max(t, axis=-1, keepdims=True)) for mm, t in zip(m, t_w)]
        e_c = [jnp.exp2(t - mm) for t, mm in zip(t_c, m)]
        l = [jnp.sum(e, axis=-1, keepdims=True) + jnp.exp2(s - mm) for e, s, mm in zip(e_c, sk, m)]
        acc = [_dot(e.astype(BF16), vc) for e in e_c]
        if windowed:
            e_w = [jnp.exp2(t - mm) for t, mm in zip(t_w, m)]
            l = [ll + jnp.sum(e, axis=-1, keepdims=True) for ll, e in zip(l, e_w)]
            acc = [a + _dot(e.astype(BF16), vw[sub]) for a, e, (sub, _) in zip(acc, e_w, units)]
        parts = [a / ll for a, ll in zip(acc, l)]
        for sub in range(SWA_NQ):
            finish(rows[sub], parts[SWA_KV_HEADS * sub:SWA_KV_HEADS * (sub + 1)])

    @pl.when(j < n_ctx_steps)
    def _():
        blocks([None] * SWA_NQ)

    @pl.when(j >= n_ctx_steps)
    def _():
        blocks([(j - n_ctx_steps) * SWA_NQ + sub for sub in range(SWA_NQ)])


def _swa_attention(q, k, v, sink, n_batch, t_ctx, seq):
    rows = t_ctx + seq
    qrows = SWA_QB * SWA_NQ
    assert t_ctx % qrows == 0 and seq % qrows == 0
    nqb = rows // qrows
    sink_tab = jnp.broadcast_to(sink.astype(F32)[:, None], (SWA_HEADS, LANES))
    return pl.pallas_call(
        functools.partial(_swa_body, t_ctx=t_ctx, seq=seq),
        out_shape=jax.ShapeDtypeStruct((n_batch * rows, BRANCH_W), BF16),
        grid=(n_batch, nqb),
        in_specs=[pl.BlockSpec((qrows, 1024), lambda b, j: (b * nqb + j, 0)),
                  pl.BlockSpec((rows, LANES), lambda b, j: (b, 0)),
                  pl.BlockSpec((rows, LANES), lambda b, j: (b, 0)),
                  pl.BlockSpec((SWA_HEADS, LANES), lambda b, j: (0, 0))],
        out_specs=pl.BlockSpec((qrows, BRANCH_W), lambda b, j: (b * nqb + j, 0)),
        compiler_params=_cparams(2),
        name="swa_attention",
    )(q, k, v, sink_tab)


MERGE_UNITS = 2


def _merge_body(*refs):
    U = MERGE_UNITS
    row_refs = [refs[6 * u:6 * (u + 1)] for u in range(U)]
    mod_refs = refs[6 * U:7 * U]
    g1_ref, g2_ref, wg_ref, dng_ref, g512_ref, wb_ref, wo_ref, rw_ref, rb_ref = refs[7 * U:7 * U + 9]
    xo_ref, h2_ref, lg_ref = refs[7 * U + 9:]
    rows = [slice(u * TM, (u + 1) * TM) for u in range(U)]
    xs = [r[0][...] for r in row_refs]
    hs = [(_rms(x, g1_ref[...]) * (1.0 + m[0, 0, 1:2, :]) + m[0, 0, 0:1, :]).astype(BF16)
          for x, m in zip(xs, mod_refs)]
    ods = [r[3][...] + r[4][...] for r in row_refs]
    sqs = [_split_bf16(od * od) for od in ods]
    mss = [(_dot(hi, g512_ref[...]) + _dot(lo, g512_ref[...])) * (1.0 / DN_HEAD_DIM) for hi, lo in sqs]
    ocs = [(od * lax.rsqrt(ms + NORM_EPS) * dng_ref[...] * (r[5][...] * jax.nn.sigmoid(r[5][...]))).astype(BF16)
           for od, ms, r in zip(ods, mss, row_refs)]
    ys = [None] * U
    for i in range(N_BRANCH):
        gates = [jax.nn.sigmoid(_dot(h, wg_ref[:, i * D_MODEL:(i + 1) * D_MODEL])) for h in hs]
        branch = [(r[1][...], r[2][...], oc)[i] for r, oc in zip(row_refs, ocs)]
        yis = [g * _dot(o, wb_ref[i]) for g, o in zip(gates, branch)]
        ys = [yi if y is None else y + yi for y, yi in zip(ys, yis)]
    xns = [x + m[0, 0, 2:3, :] * _dot(y.astype(BF16), wo_ref[...]) for x, m, y in zip(xs, mod_refs, ys)]
    h2s = [_rms(xn, g2_ref[...]) * (1.0 + m[0, 0, 4:5, :]) + m[0, 0, 3:4, :] for xn, m in zip(xns, mod_refs)]
    lgs = [_dot3_nt(rw_ref[...], h2) + rb_ref[...] for h2 in h2s]
    for r, xn, h2, lg in zip(rows, xns, h2s, lgs):
        xo_ref[r, :] = xn
        h2_ref[r, :] = h2.astype(BF16)
        lg_ref[:, r] = lg


def _merge(x, modtab, g1, g2, wg, oa, ob, dn_f, dn_b, dz, dn_g, g512, wb, wo, rwt, rb, nb_per_batch, latent_only):
    U = MERGE_UNITS
    if latent_only:
        nlat = nb_per_batch - 1
        n = x.shape[0] // nb_per_batch * nlat
        src = lambda u: (u // nlat) * nb_per_batch + u % nlat + 1
        mod_of = lambda u: (u // nlat, 1, 0, 0)
    else:
        n = x.shape[0]
        src = lambda u: u
        mod_of = lambda u: (u // nb_per_batch, jnp.minimum(u % nb_per_batch, 1), 0, 0)
    step = TM * U
    assert n % step == 0
    orow = lambda i: (i, 0)
    full2 = lambda i: (0, 0)
    row_specs, row_args = [], []
    for u in range(U):
        rowu = lambda i, u=u: (src(i * U + u), 0)
        row_specs += [pl.BlockSpec((TM, D_MODEL), rowu)] + [pl.BlockSpec((TM, BRANCH_W), rowu)] * 5
        row_args += [x, oa, ob, dn_f, dn_b, dz]
    mod_specs = [pl.BlockSpec((1, 1, 6, D_MODEL), lambda i, u=u: mod_of(i * U + u)) for u in range(U)]
    return pl.pallas_call(
        _merge_body,
        out_shape=[jax.ShapeDtypeStruct((n, D_MODEL), F32), jax.ShapeDtypeStruct((n, D_MODEL), BF16),
                   jax.ShapeDtypeStruct((N_EXPERTS, n), F32)],
        grid=(n // step,),
        in_specs=row_specs + mod_specs
        + [pl.BlockSpec((1, D_MODEL), full2), pl.BlockSpec((1, D_MODEL), full2),
           pl.BlockSpec((D_MODEL, N_BRANCH * D_MODEL), full2), pl.BlockSpec((1, BRANCH_W), full2),
           pl.BlockSpec((BRANCH_W, BRANCH_W), full2),
           pl.BlockSpec((N_BRANCH, BRANCH_W, D_MODEL), lambda i: (0, 0, 0)),
           pl.BlockSpec((D_MODEL, D_MODEL), full2),
           pl.BlockSpec((N_EXPERTS, D_MODEL), full2), pl.BlockSpec((N_EXPERTS, 1), full2)],
        out_specs=[pl.BlockSpec((step, D_MODEL), orow), pl.BlockSpec((step, D_MODEL), orow),
                   pl.BlockSpec((N_EXPERTS, step), lambda i: (0, i))],
        compiler_params=_cparams(),
        name="merge",
    )(*row_args, *([modtab] * U), g1, g2, wg, dn_g, g512, wb, wo, rwt, rb)


ROUTE_TB = 1024


def _router_body(lg_ref, tri_ref, e_ref, gate_ref, pos_ref, cnt_ref, run_ref):
    @pl.when(pl.program_id(0) == 0)
    def _():
        run_ref[...] = jnp.zeros_like(run_ref)

    lg = lg_ref[...]
    eid = lax.broadcasted_iota(jnp.int32, lg.shape, 0)
    work = lg
    vals, idxs = [], []
    sel = jnp.zeros(lg.shape, F32)
    for _ in range(TOP_K):
        m = jnp.max(work, axis=0, keepdims=True)
        idx = jnp.min(jnp.where(work == m, eid, N_EXPERTS), axis=0, keepdims=True)
        hit = eid == idx
        sel = jnp.where(hit, 1.0, sel)
        work = jnp.where(hit, -jnp.inf, work)
        vals.append(m)
        idxs.append(idx)
    ex = [jnp.exp(v - vals[0]) for v in vals]
    den = ex[0] + ex[1] + ex[2] + ex[3]
    before = _dot(sel.astype(BF16), tri_ref[...]) + run_ref[:, 0:1]
    for k in range(TOP_K):
        e_ref[k:k + 1, :] = idxs[k]
        gate_ref[k:k + 1, :] = ex[k] / den
        pos_ref[k:k + 1, :] = jnp.sum(jnp.where(eid == idxs[k], before, 0.0), axis=0,
                                      keepdims=True).astype(jnp.int32)
    run_ref[...] = run_ref[...] + jnp.sum(sel, axis=1, keepdims=True)
    cnt_ref[...] = run_ref[...]


def _router(logits_t):
    n = logits_t.shape[1]
    tri = (jnp.arange(ROUTE_TB)[:, None] < jnp.arange(ROUTE_TB)[None, :]).astype(BF16)
    blk = lambda i: (0, i)
    return pl.pallas_call(
        _router_body,
        out_shape=[jax.ShapeDtypeStruct((TOP_K, n), jnp.int32), jax.ShapeDtypeStruct((TOP_K, n), F32),
                   jax.ShapeDtypeStruct((TOP_K, n), jnp.int32), jax.ShapeDtypeStruct((N_EXPERTS, LANES), F32)],
        grid=(n // ROUTE_TB,),
        in_specs=[pl.BlockSpec((N_EXPERTS, ROUTE_TB), blk), pl.BlockSpec((ROUTE_TB, ROUTE_TB), lambda i: (0, 0))],
        out_specs=[pl.BlockSpec((TOP_K, ROUTE_TB), blk), pl.BlockSpec((TOP_K, ROUTE_TB), blk),
                   pl.BlockSpec((TOP_K, ROUTE_TB), blk), pl.BlockSpec((N_EXPERTS, LANES), lambda i: (0, 0))],
        scratch_shapes=[pltpu.VMEM((N_EXPERTS, LANES), F32)],
        compiler_params=_cparams(),
        name="router",
    )(logits_t, tri)


def _expert_body(ib_ref, ie_ref, lo_ref, hi_ref, x_ref, wgu_ref, bgu_ref, wdn_ref, bdn_ref, perm_ref, *rest):
    o_ref, wgu_s, wdn_s = rest[-3:]
    i = pl.program_id(0)
    prev = jnp.maximum(i - 1, 0)

    @pl.when((i == 0) | (ie_ref[i] != ie_ref[prev]))
    def _():
        for m in range(D_EXPERT // LANES):
            t = _dot(wgu_ref[0, 0, :, m * 2 * LANES:(m + 1) * 2 * LANES].astype(BF16), perm_ref[...])
            wgu_s[:, m * LANES:(m + 1) * LANES] = t[:, 0:LANES].astype(BF16)
            wgu_s[:, D_EXPERT + m * LANES:D_EXPERT + (m + 1) * LANES] = t[:, LANES:2 * LANES].astype(BF16)
        wdn_s[...] = wdn_ref[0, 0].astype(BF16)

    lo, hi = lo_ref[i], hi_ref[i]

    @pl.when(hi > lo)
    def _():
        gu = _dot(x_ref[...], wgu_s[...]) + bgu_ref[0]
        g_ = jnp.minimum(gu[:, :D_EXPERT], SWIGLU_LIMIT)
        u_ = jnp.clip(gu[:, D_EXPERT:], -SWIGLU_LIMIT, SWIGLU_LIMIT)
        act = (u_ + 1.0) * (g_ * jax.nn.sigmoid(SWIGLU_ALPHA * g_))
        y = _dot(act.astype(BF16), wdn_s[...]) + bdn_ref[0]
        first = (i == 0) | (ib_ref[i] != ib_ref[prev])
        row = lax.broadcasted_iota(jnp.int32, (MOE_BLK, 1), 0)
        o_ref[...] = jnp.where(first | ((row >= lo) & (row < hi)), y, o_ref[...])


def _experts(items, xs, w_gu, bgu, w_dn, bdn, layer, block0, n_rows, yb_prev):
    n_items = items[0].shape[0]
    perm = np.zeros((2 * LANES, 2 * LANES), np.float32)
    j = np.arange(LANES)
    perm[2 * j, j] = 1.0
    perm[2 * j + 1, LANES + j] = 1.0
    in_specs = [pl.BlockSpec((MOE_BLK, D_MODEL), lambda i, ib, ie, lo, hi: (ib[i] - block0, 0)),
                pl.BlockSpec((1, 1, D_MODEL, 2 * D_EXPERT), lambda i, ib, ie, lo, hi: (layer, ie[i], 0, 0)),
                pl.BlockSpec((1, 1, 2 * D_EXPERT), lambda i, ib, ie, lo, hi: (ie[i], 0, 0)),
                pl.BlockSpec((1, 1, D_EXPERT, D_MODEL), lambda i, ib, ie, lo, hi: (layer, ie[i], 0, 0)),
                pl.BlockSpec((1, 1, D_MODEL), lambda i, ib, ie, lo, hi: (ie[i], 0, 0)),
                pl.BlockSpec((2 * LANES, 2 * LANES), lambda i, ib, ie, lo, hi: (0, 0))]
    args = [*items, xs, w_gu, bgu, w_dn, bdn, jnp.asarray(perm, BF16)]
    aliases = {}
    if yb_prev is not None:
        in_specs.append(pl.BlockSpec(memory_space=pl.ANY))
        aliases = {len(args): 0}
        args.append(yb_prev)
    return pl.pallas_call(
        _expert_body,
        out_shape=jax.ShapeDtypeStruct((n_rows, D_MODEL), F32),
        grid_spec=pltpu.PrefetchScalarGridSpec(
            num_scalar_prefetch=4,
            grid=(n_items,),
            in_specs=in_specs,
            out_specs=pl.BlockSpec((MOE_BLK, D_MODEL), lambda i, ib, ie, lo, hi: (ib[i], 0)),
            scratch_shapes=[pltpu.VMEM((D_MODEL, 2 * D_EXPERT), BF16), pltpu.VMEM((D_EXPERT, D_MODEL), BF16)]),
        input_output_aliases=aliases,
        compiler_params=_cparams(),
        name="experts",
    )(*args)


def _lookup(table, idx):
    ids = jnp.arange(table.shape[0], dtype=jnp.int32)
    return jnp.sum(jnp.where(idx[..., None] == ids, table, 0), axis=-1)


def _work_items(cnt_start, cnt_end, block0, n_blk):
    n_items = n_blk + N_EXPERTS - 1
    b1 = block0 + n_blk
    first = jnp.maximum(cnt_start // MOE_BLK, block0)
    last = jnp.minimum((cnt_end - 1) // MOE_BLK, b1 - 1)
    n_e = jnp.where(cnt_end > cnt_start, jnp.maximum(last - first + 1, 0), 0)
    item_end = jnp.cumsum(n_e)
    item_start = item_end - n_e
    total = item_end[-1]
    ii = jnp.arange(n_items, dtype=jnp.int32)
    valid = ii < total
    e_i = jnp.sum((item_end[None, :] <= jnp.minimum(ii, total - 1)[:, None]).astype(jnp.int32), axis=1)
    blk = jnp.where(valid, _lookup(first, e_i) + ii - _lookup(item_start, e_i), b1 - 1)
    lo = jnp.clip(_lookup(cnt_start, e_i) - blk * MOE_BLK, 0, MOE_BLK)
    hi = jnp.clip(_lookup(cnt_end, e_i) - blk * MOE_BLK, 0, MOE_BLK)
    lo = jnp.where(valid, lo, 0)
    hi = jnp.where(valid, hi, 0)
    return blk.astype(jnp.int32), e_i.astype(jnp.int32), lo.astype(jnp.int32), hi.astype(jnp.int32)


def _moe(h2, logits_t, w_gu, bgu, w_dn, bdn, layer):
    n = h2.shape[0]
    a = n * TOP_K
    assert a % (MOE_BLK * MOE_CHUNKS) == 0
    top_e, gate, pos, cnt = _router(logits_t)
    counts = cnt[:, 0].astype(jnp.int32)
    cnt_end = jnp.cumsum(counts)
    cnt_start = cnt_end - counts
    slot = pos + _lookup(cnt_start, top_e)
    tok = jnp.broadcast_to(jnp.arange(n, dtype=jnp.int32)[None, :], (TOP_K, n))
    _, slot_tok = lax.sort_key_val(slot.reshape(-1), tok.reshape(-1))
    n_blk = a // MOE_BLK // MOE_CHUNKS
    rows_c = n_blk * MOE_BLK
    yb = None
    for c in range(MOE_CHUNKS):
        xs = h2.at[slot_tok[c * rows_c:(c + 1) * rows_c]].get(mode="promise_in_bounds")
        items = _work_items(cnt_start, cnt_end, c * n_blk, n_blk)
        yb = _experts(items, xs, w_gu, bgu, w_dn, bdn, layer, c * n_blk, a, yb)
    ys = [yb.at[slot[k]].get(mode="promise_in_bounds") for k in range(TOP_K)]
    return ys, gate.T


DN_GW = 256
HALO = 8
DN_PAIR = 4
_M_EYE, _M_BLK16, _M_OFF32, _M_OFF64, _M_INCL, _M_STRICT = 0, 1, 2, 3, 4, 6


def _dn_masks():
    row = np.arange(DN_CHUNK)[:, None]
    col = np.arange(DN_GW)[None, :] % DN_CHUNK
    b16 = (row // 16) == (col // 16)
    b32 = (row // 32) == (col // 32)
    m = np.stack([row == col, b16, b32 & ~b16, ~b32, col <= row, col >= row, col < row, col > row])
    hid = np.arange(DN_GW) // DN_HEAD_DIM
    return jnp.asarray(m, F32), jnp.asarray(hid[:, None] == hid[None, :], BF16)


def _split3(a):
    hi = a.astype(BF16)
    r = a - hi.astype(F32)
    lo = r.astype(BF16)
    lo2 = (r - lo.astype(F32)).astype(BF16)
    return hi, lo, lo2


def _bd(x, bdm_ref):
    xb = x.astype(BF16)
    return jnp.concatenate([xb, xb, xb, xb], axis=0) * bdm_ref[...]


def _mm(a, wbd):
    return _dot(a.astype(BF16), wbd)


def _tri_inverse(l_mats, m_ref, bdm_ref):
    half = DN_CHUNK
    bd = lambda v: _bd(v, bdm_ref)
    lds = [l * m_ref[_M_BLK16] for l in l_mats]
    ps = [_mm(ld, bd(ld)) for ld in lds]
    xs = [m_ref[_M_EYE] - ld for ld in lds]
    for _ in range(2):
        rs = [_mm(jnp.concatenate([p, x], axis=0), bd(p)) for p, x in zip(ps, xs)]
        ps = [r[0:half] for r in rs]
        xs = [x + r[half:2 * half] for x, r in zip(xs, rs)]
    xs = [x + _mm(x, bd(p)) for x, p in zip(xs, ps)]
    ts = [_mm(x, bd(l * m_ref[_M_OFF32])) for x, l in zip(xs, l_mats)]
    xs = [x - _mm(t, bd(x)) for x, t in zip(xs, ts)]
    ts = [_mm(x, bd(l * m_ref[_M_OFF64])) for x, l in zip(xs, l_mats)]
    return [x - _mm(t, bd(x)) for x, t in zip(xs, ts)]


def _dn_prep_body(cur_ref, prev_ref, next_ref, cw_ref, dg_ref, gc_ref, g512_ref, e_ref, tri_ref, m_ref, bdm_ref,
                  u_ref, w_ref, kd_ref, qd_ref, aq_ref, gl_ref,
                  ext_ref, q_s, k_s, v_s, gx_s, bx_s):
    j = pl.program_id(1)
    nbp = pl.num_programs(1)
    C = DN_CHUNK
    ext_ref[HALO:HALO + TM, :] = cur_ref[...]
    ext_ref[0:HALO, :] = jnp.where(j >= 2, prev_ref[...], 0.0)
    ext_ref[HALO + TM:2 * HALO + TM, :] = jnp.where((j >= 1) & (j < nbp - 1), next_ref[...], 0.0)
    y = None
    for t in range(DN_CONV):
        o = HALO - DN_CONV // 2 + t
        term = cw_ref[t:t + 1, :] * ext_ref[o:o + TM, :]
        y = term if y is None else y + term
    y = y * jax.nn.sigmoid(y)

    g512 = g512_ref[...]

    def head_sumsq(x):
        hi, lo = _split_bf16(x * x)
        return _dot(hi, g512) + _dot(lo, g512)

    q = y[:, 0:512]
    k = y[:, 512:1024]
    q_s[...] = q * lax.rsqrt(head_sumsq(q) + NORM_EPS) * (DN_HEAD_DIM ** -0.5)
    k_s[...] = k * lax.rsqrt(head_sumsq(k) + NORM_EPS)
    v_s[...] = y[:, 1024:1536]

    dg = dg_ref[...]
    beta_all = jax.nn.sigmoid(dg)
    z = dg + gc_ref[1:2, :]
    g_all = gc_ref[0:1, :] * (jnp.maximum(z, 0.0) + jnp.log(1.0 + jnp.exp(-jnp.abs(z))))
    bh, bl = _split_bf16(beta_all)
    bcat = jnp.concatenate([bh, bl], axis=0)
    gparts = jnp.concatenate(_split3(g_all), axis=1)
    for d in range(2):
        cs = _dot(tri_ref[d], gparts)
        gcum = cs[:, 0:128] + cs[:, 128:256] + cs[:, 256:384]
        gcat = jnp.concatenate(_split3(gcum), axis=0)
        for grp in range(2):
            eg = _dot(gcat, e_ref[4 + 2 * d + grp])
            gx_s[d, grp] = eg[0:TM] + eg[TM:2 * TM] + eg[2 * TM:3 * TM]
            eb = _dot(bcat, e_ref[2 * d + grp])
            bx_s[d, grp] = eb[0:TM] + eb[TM:2 * TM]

    def chunk_pair(it, carry):
        units = []
        l_mats = []
        for ci in range(DN_PAIR):
            cc = it * DN_PAIR + ci
            rows = pl.ds(pl.multiple_of(cc * C, C), C)
            for grp in range(2):
                cols = slice(grp * DN_GW, (grp + 1) * DN_GW)
                kg, qg = k_s[rows, cols], q_s[rows, cols]
                kb = [kg * bx_s[d, grp, rows, :] for d in range(2)]
                raw = _dot_nt(jnp.concatenate([kb[0], kb[1], qg], axis=0).astype(BF16), _bd(kg, bdm_ref))
                for d in range(2):
                    gx = gx_s[d, grp, rows, :]
                    rvec = jnp.sum(gx * m_ref[_M_EYE], axis=0, keepdims=True)
                    dec = jnp.exp(jnp.minimum(gx - rvec, 0.0)) * m_ref[_M_INCL + d]
                    l_mats.append(raw[C * d:C * (d + 1)] * dec * m_ref[_M_STRICT + d])
                    glast = gx[C - 1:C, :] if d == 0 else gx[0:1, :]
                    kd_ref[d, rows, cols] = (kg * jnp.exp(glast - gx)).astype(BF16)
                    qd_ref[d, rows, cols] = (qg * jnp.exp(gx)).astype(BF16)
                    aq_ref[d, rows, cols] = (raw[2 * C:3 * C] * dec).astype(BF16)
                    gl_ref[d, pl.ds(cc, 1), :, cols] = jnp.exp(glast).reshape(1, 1, DN_GW)
                    units.append((rows, cols, d, grp))
        tinvs = _tri_inverse(l_mats, m_ref, bdm_ref)
        for (rows, cols, d, grp), tinv in zip(units, tinvs):
            bx = bx_s[d, grp, rows, :]
            u_ref[d, rows, cols] = _mm(tinv, _bd(v_s[rows, cols] * bx, bdm_ref))
            w_ref[d, rows, cols] = _mm(tinv, _bd(k_s[rows, cols] * bx * jnp.exp(gx_s[d, grp, rows, :]), bdm_ref)
                                       ).astype(BF16)
        return carry

    lax.fori_loop(0, TM // C // DN_PAIR, chunk_pair, 0)


def _dn_prep(dqkv, dg, conv_w, a_log, dt_bias, n_batch, rows):
    n = dqkv.shape[0]
    nbp = rows // TM
    hb = TM // HALO
    H = DN_HEADS
    cw = jnp.zeros((8, 1536), F32).at[0:DN_CONV].set(conv_w)
    gc = jnp.zeros((8, LANES), F32)
    gc = gc.at[0, 2 * H:4 * H].set(-jnp.exp(a_log.reshape(-1))).at[1, 2 * H:4 * H].set(dt_bias.reshape(-1))
    hid = np.arange(512) // DN_HEAD_DIM
    g512 = jnp.asarray(hid[:, None] == hid[None, :], BF16)
    e = np.zeros((8, LANES, DN_GW), np.float32)
    for kind in range(2):
        for d in range(2):
            for grp in range(2):
                for h in range(4):
                    e[4 * kind + 2 * d + grp, 16 * kind + 8 * d + 4 * grp + h, 64 * h:64 * (h + 1)] = 1.0
    e = jnp.asarray(e, BF16)
    t = np.arange(TM)
    same = (t[:, None] // DN_CHUNK) == (t[None, :] // DN_CHUNK)
    tri = jnp.asarray(np.stack([same & (t[None, :] <= t[:, None]), same & (t[None, :] >= t[:, None])]), BF16)
    masks, bdm = _dn_masks()
    blk = lambda b, j: (b * nbp + j, 0)
    full2 = lambda b, j: (0, 0)
    full3 = lambda b, j: (0, 0, 0)
    dblk = lambda b, j: (0, b * nbp + j, 0)
    nlast = n // HALO - 1
    outs = pl.pallas_call(
        _dn_prep_body,
        out_shape=[jax.ShapeDtypeStruct((2, n, 512), F32)] + [jax.ShapeDtypeStruct((2, n, 512), BF16)] * 4
        + [jax.ShapeDtypeStruct((2, n // DN_CHUNK, 1, 512), F32)],
        grid=(n_batch, nbp),
        in_specs=[pl.BlockSpec((TM, 1536), blk),
                  pl.BlockSpec((HALO, 1536), lambda b, j: (jnp.maximum((b * nbp + j) * hb - 1, 0), 0)),
                  pl.BlockSpec((HALO, 1536), lambda b, j: (jnp.minimum((b * nbp + j + 1) * hb, nlast), 0)),
                  pl.BlockSpec((8, 1536), full2), pl.BlockSpec((TM, LANES), blk), pl.BlockSpec((8, LANES), full2),
                  pl.BlockSpec((512, 512), full2), pl.BlockSpec((8, LANES, DN_GW), full3),
                  pl.BlockSpec((2, TM, TM), full3), pl.BlockSpec((8, DN_CHUNK, DN_GW), full3),
                  pl.BlockSpec((DN_GW, DN_GW), full2)],
        out_specs=[pl.BlockSpec((2, TM, 512), dblk)] * 5
        + [pl.BlockSpec((2, TM // DN_CHUNK, 1, 512), lambda b, j: (0, b * nbp + j, 0, 0))],
        scratch_shapes=[pltpu.VMEM((TM + 2 * HALO, 1536), F32), pltpu.VMEM((TM, 512), F32),
                        pltpu.VMEM((TM, 512), F32), pltpu.VMEM((TM, 512), F32),
                        pltpu.VMEM((2, 2, TM, DN_GW), F32), pltpu.VMEM((2, 2, TM, DN_GW), F32)],
        compiler_params=_cparams(2),
        name="dn_prep",
    )(dqkv, dqkv, dqkv, cw, dg, gc, g512, e, tri, masks, bdm)
    return outs, g512


DN_SCAN_NB = 8


def _dn_scan_body(uf, wf, kf, qf, af, gf, ub, wb, kb, qb, ab, gb, bdm_ref, of_ref, ob_ref, s_ref):
    @pl.when(pl.program_id(1) == 0)
    def _():
        s_ref[...] = jnp.zeros_like(s_ref)

    same = bdm_ref[...] != 0
    dirs = ((uf, wf, kf, qf, af, gf, of_ref), (ub, wb, kb, qb, ab, gb, ob_ref))
    units = [(bb, d, grp) for bb in range(DN_SCAN_NB) for d in range(2) for grp in range(2)]
    cols = [slice(grp * DN_GW, (grp + 1) * DN_GW) for grp in range(2)]
    s = [s_ref[bb, d, grp] for bb, d, grp in units]
    sq = [_dot(jnp.concatenate([dirs[d][1][0, bb, :, cols[grp]], dirs[d][3][0, bb, :, cols[grp]]], axis=0),
               st.astype(BF16)) for (bb, d, grp), st in zip(units, s)]
    vnew = [dirs[d][0][0, bb, :, cols[grp]] - r[0:DN_CHUNK] for (bb, d, grp), r in zip(units, sq)]
    intra = [_dot(dirs[d][4][0, bb, :, cols[grp]], _bd(v, bdm_ref)) for (bb, d, grp), v in zip(units, vnew)]
    upd = [lax.dot_general(dirs[d][2][0, bb, :, cols[grp]], v.astype(BF16), (((0,), (0,)), ((), ())),
                           preferred_element_type=F32) for (bb, d, grp), v in zip(units, vnew)]
    for (bb, d, grp), st, r, a, up in zip(units, s, sq, intra, upd):
        dirs[d][6][bb, :, cols[grp]] = r[DN_CHUNK:2 * DN_CHUNK] + a
        s_ref[bb, d, grp] = st * dirs[d][5][0, bb, 0, :, cols[grp]] + jnp.where(same, up, 0.0)


def _dn_scan(u, w, kd, qd, aq, gl, n_batch, rows, t_ctx):
    nch = rows // DN_CHUNK
    nctx = t_ctx // DN_CHUNK
    assert n_batch % DN_SCAN_NB == 0

    def cb(c):
        return jnp.where(c < nctx, nctx - 1 - c, nch - 1 - (c - nctx))

    per_batch = lambda a: a.reshape(2, n_batch, rows, 512)
    u, w, kd, qd, aq = map(per_batch, (u, w, kd, qd, aq))
    gl = gl.reshape(2, n_batch, nch, 1, 512)
    blk = (1, DN_SCAN_NB, DN_CHUNK, 512)
    gblk = (1, DN_SCAN_NB, 1, 1, 512)
    specs = []
    for d, ch in ((0, lambda c: c), (1, cb)):
        specs += [pl.BlockSpec(blk, lambda b, c, d=d, ch=ch: (d, b, ch(c), 0))] * 5
        specs += [pl.BlockSpec(gblk, lambda b, c, d=d, ch=ch: (d, b, ch(c), 0, 0))]
    specs.append(pl.BlockSpec((DN_GW, DN_GW), lambda b, c: (0, 0)))
    o_f, o_b = pl.pallas_call(
        _dn_scan_body,
        out_shape=[jax.ShapeDtypeStruct((n_batch, rows, 512), F32)] * 2,
        grid=(n_batch // DN_SCAN_NB, nch),
        in_specs=specs,
        out_specs=[pl.BlockSpec((DN_SCAN_NB, DN_CHUNK, 512), lambda b, c: (b, c, 0)),
                   pl.BlockSpec((DN_SCAN_NB, DN_CHUNK, 512), lambda b, c: (b, cb(c), 0))],
        scratch_shapes=[pltpu.VMEM((DN_SCAN_NB, 2, 2, DN_GW, DN_GW), F32)],
        compiler_params=_cparams(2),
        name="dn_scan",
    )(u, w, kd, qd, aq, gl, u, w, kd, qd, aq, gl, _dn_masks()[1])
    return o_f.reshape(n_batch * rows, 512), o_b.reshape(n_batch * rows, 512)


def _deltanet(dqkv, dg, conv_w, a_log, dt_bias, n_batch, rows, t_ctx):
    (u, w, kd, qd, aq, gl), g512 = _dn_prep(dqkv, dg, conv_w, a_log, dt_bias, n_batch, rows)
    o_f, o_b = _dn_scan(u, w, kd, qd, aq, gl, n_batch, rows, t_ctx)
    return o_f, o_b, g512


def _combine_body(x_ref, y0_ref, y1_ref, y2_ref, y3_ref, gate_ref, mod_ref, fg_ref, o_ref, *, final):
    gate = gate_ref[...]
    y = None
    for k, y_ref in enumerate((y0_ref, y1_ref, y2_ref, y3_ref)):
        yk = y_ref[...] * gate[:, k:k + 1]
        y = yk if y is None else y + yk
    xo = x_ref[...] + mod_ref[0, 0, 5:6, :] * y
    o_ref[...] = _rms(xo, fg_ref[...]) if final else xo


def _combine(x, ys, gate_rows, modtab, final_g, nb_per_batch, latent_only):
    n = x.shape[0]
    if latent_only:
        nlat = nb_per_batch - 1
        mod_idx = lambda i: (i // nlat, 1, 0, 0)
    else:
        mod_idx = lambda i: (i // nb_per_batch, jnp.minimum(i % nb_per_batch, 1), 0, 0)
    row = lambda i: (i, 0)
    return pl.pallas_call(
        functools.partial(_combine_body, final=latent_only),
        out_shape=jax.ShapeDtypeStruct((n, D_MODEL), F32),
        grid=(n // TM,),
        in_specs=[pl.BlockSpec((TM, D_MODEL), row)] * 5
        + [pl.BlockSpec((TM, TOP_K), row), pl.BlockSpec((1, 1, 6, D_MODEL), mod_idx),
           pl.BlockSpec((1, D_MODEL), lambda i: (0, 0))],
        out_specs=pl.BlockSpec((TM, D_MODEL), row),
        compiler_params=_cparams(),
        name="combine",
    )(x, *ys, gate_rows, modtab, final_g)


def _rope_tables(seq, t_ctx, rot_dim):
    rows = seq // GRID_W
    row = jnp.broadcast_to(jnp.arange(rows)[:, None], (rows, GRID_W)).reshape(-1).astype(F32)
    col = jnp.broadcast_to(jnp.arange(GRID_W)[None, :], (rows, GRID_W)).reshape(-1).astype(F32)
    n_freq = rot_dim // 4
    inv_freq = ROPE_THETA ** (-jnp.arange(n_freq, dtype=F32) / n_freq)
    ang = jnp.concatenate([row[:, None] * inv_freq, col[:, None] * inv_freq], axis=-1)
    cos, sin = jnp.cos(ang), jnp.sin(ang)
    zero = jnp.zeros_like(sin)
    c = jnp.concatenate([cos, cos], axis=-1)
    sa = jnp.concatenate([-sin, zero], axis=-1)
    sb = jnp.concatenate([zero, sin], axis=-1)
    ident = jnp.stack([jnp.ones((t_ctx, rot_dim), F32), jnp.zeros((t_ctx, rot_dim), F32),
                       jnp.zeros((t_ctx, rot_dim), F32)])
    tab = jnp.concatenate([ident, jnp.stack([c, sa, sb])], axis=1)
    return jnp.tile(tab, (1, 1, LANES // rot_dim))


def _layer_weights(w_in, w_q_up, w_kv_up, w_branch, w_out, router_w, router_b, b_gu, b_dn):
    sizes = (MLA_Q_RANK, MLA_KV_RANK, MLA_ROPE, 512, 128, 128, 1536, 512, 16, 16, N_BRANCH * D_MODEL)
    offs = np.cumsum((0,) + sizes)
    part = lambda i: w_in[:, offs[i]:offs[i + 1]]
    pad = jnp.zeros((D_MODEL, LANES - 32), F32)
    w_proj = jnp.concatenate(
        [part(0), part(1), part(3), part(4), part(5), part(6), part(7),
         jnp.tile(part(2), (1, LANES // MLA_ROPE)), part(8), part(9), pad], axis=1).astype(BF16)
    w_gate = part(10).astype(BF16)
    hq = MLA_NOPE + MLA_ROPE
    wq = w_q_up.reshape(MLA_Q_RANK, MLA_HEADS, hq)
    wq = jnp.concatenate([wq[:, :, :MLA_NOPE].reshape(MLA_Q_RANK, -1),
                          wq[:, :, MLA_NOPE:].reshape(MLA_Q_RANK, -1)], axis=1).astype(BF16)
    wkv = w_kv_up.reshape(MLA_KV_RANK, MLA_HEADS, MLA_NOPE + MLA_V)
    wkv = jnp.concatenate([wkv[:, :, :MLA_NOPE].reshape(MLA_KV_RANK, -1),
                           wkv[:, :, MLA_NOPE:].reshape(MLA_KV_RANK, -1)], axis=1).astype(BF16)
    bgu = jnp.concatenate([b_gu[:, 0::2], b_gu[:, 1::2]], axis=-1).reshape(N_EXPERTS, 1, 2 * D_EXPERT)
    return dict(w_proj=w_proj, w_gate=w_gate, wq=wq, wkv=wkv, wb=w_branch.astype(BF16),
                wo=w_out.astype(BF16), rwt=router_w.T, rb=router_b.reshape(N_EXPERTS, 1),
                bgu=bgu, bdn=b_dn.reshape(N_EXPERTS, 1, D_MODEL))


def kernel(x, c, ctx, c_ctx, w_mod, b_mod, norm1_g, norm2_g, w_in, mla_q_norm_g, mla_w_q_up,
           mla_kv_norm_g, mla_w_kv_up, swa_sink, dn_conv_w, dn_a_log, dn_dt_bias, dn_norm_g,
           w_branch, w_out, router_w, router_b, exp_w_gu, exp_b_gu, exp_w_dn, exp_b_dn, final_norm_g):
    B, S, D = x.shape
    T = ctx.shape[1]
    R = T + S
    depth = w_mod.shape[0]
    assert D == D_MODEL and T == TM and S % TM == 0 and (B * R) % ROUTE_TB == 0
    nbp = R // TM
    xa = jnp.concatenate([ctx, x], axis=1).reshape(B * R, D)
    rope_s = _rope_tables(S, T, SWA_HEAD_DIM)
    rope_m = _rope_tables(S, T, MLA_ROPE)
    cvec = jnp.concatenate([c, c_ctx[None, :]], axis=0)
    for l in range(depth):
        w = _layer_weights(w_in[l], mla_w_q_up[l], mla_w_kv_up[l], w_branch[l], w_out[l], router_w[l],
                           router_b[l], exp_b_gu[l], exp_b_dn[l])
        mod = _modulation(cvec, w_mod[l], b_mod[l]).reshape(B + 1, 6, D)
        modtab = jnp.stack([jnp.broadcast_to(mod[B][None], (B, 6, D)), mod[:B]], axis=1)
        g1 = norm1_g[l].reshape(1, D)
        g2 = norm2_g[l].reshape(1, D)
        cq, ckv, krt, sq, sk, sv, dqkv, dz, dg = _inproj(xa, modtab, g1, w["w_proj"], rope_s, nbp)
        q_m, k_m, v_m = _mla_prep(cq, ckv, krt, mla_q_norm_g[l].reshape(1, -1), mla_kv_norm_g[l].reshape(1, -1),
                                  w["wq"], w["wkv"], rope_m, nbp)
        o_a = _mla_attention(q_m, k_m, v_m, B, R)
        o_b = _swa_attention(sq, sk, sv, swa_sink[l], B, T, S)
        dn_f, dn_b, g512 = _deltanet(dqkv, dg, dn_conv_w[l], dn_a_log[l], dn_dt_bias[l], B, R, T)
        last = l == depth - 1
        xn, h2, logits_t = _merge(xa, modtab, g1, g2, w["w_gate"], o_a, o_b, dn_f, dn_b, dz,
                                  jnp.tile(dn_norm_g[l], DN_HEADS).reshape(1, BRANCH_W), g512, w["wb"], w["wo"],
                                  w["rwt"], w["rb"], nbp, last)
        ys, gate_rows = _moe(h2, logits_t, exp_w_gu, w["bgu"], exp_w_dn, w["bdn"], l)
        xa = _combine(xn, ys, gate_rows, modtab, final_norm_g.reshape(1, D), nbp, last)
    return xa.reshape(B, S, D)
```

```python
import functools

import jax
import jax.numpy as jnp
import numpy as np
from jax import lax
from jax.experimental import pallas as pl
from jax.experimental.pallas import tpu as pltpu

F32 = jnp.float32
BF16 = jnp.bfloat16

D_MODEL = 1024
GRID_W = 64
ROPE_THETA = 10000.0
NORM_EPS = 1e-6
MLA_HEADS = 8
MLA_Q_RANK = 384
MLA_KV_RANK = 256
MLA_NOPE = 64
MLA_ROPE = 32
MLA_V = 64
SWA_HEADS = 8
SWA_KV_HEADS = 2
SWA_HEAD_DIM = 64
SWA_WINDOW = 128
DN_HEADS = 8
DN_HEAD_DIM = 64
DN_CONV = 5
DN_CHUNK = 64
N_BRANCH = 3
BRANCH_W = 512
N_EXPERTS = 32
TOP_K = 4
D_EXPERT = 1024
SWIGLU_LIMIT = 7.0
SWIGLU_ALPHA = 1.702
MOE_BLK = 512
MOE_CHUNKS = 4

LANES = 128
LOG2E = 1.4426950408889634
TM = 256
VMEM_LIMIT = 56 * 1024 * 1024

_C_CQ = (0, 384)
_C_CKV = (384, 640)
_C_SQ = (640, 1152)
_C_SK = (1152, 1280)
_C_SV = (1280, 1408)
_C_DQKV = (1408, 2944)
_C_DZ = (2944, 3456)
_C_KRT = (3456, 3584)
_C_DG = (3584, 3712)
_N_PROJ = 3712


def _cparams(n_axes=1):
    return pltpu.CompilerParams(dimension_semantics=("arbitrary",) * n_axes, vmem_limit_bytes=VMEM_LIMIT)


def _dot(a, b):
    return jnp.dot(a, b, preferred_element_type=F32)


def _dot_nt(a, b):
    return lax.dot_general(a, b, (((1,), (1,)), ((), ())), preferred_element_type=F32)


def _split_bf16(a):
    hi = a.astype(BF16)
    lo = (a - hi.astype(F32)).astype(BF16)
    return hi, lo


def _dot3(a, b):
    ah, al = _split_bf16(a)
    bh, bl = _split_bf16(b)
    return _dot(ah, bh) + (_dot(ah, bl) + _dot(al, bh))


def _dot3_nt(a, b):
    ah, al = _split_bf16(a)
    bh, bl = _split_bf16(b)
    return _dot_nt(ah, bh) + (_dot_nt(ah, bl) + _dot_nt(al, bh))


def _rms(x, g):
    return x * lax.rsqrt(jnp.mean(x * x, axis=-1, keepdims=True) + NORM_EPS) * g


def _rope_cols(x, tab_ref, half):
    c, sa, sb = tab_ref[0], tab_ref[1], tab_ref[2]
    return x * c + pltpu.roll(x, LANES - half, axis=1) * sa + pltpu.roll(x, half, axis=1) * sb


def _mod_body(c_ref, w_ref, b_ref, o_ref):
    c = c_ref[...]
    a = c * jax.nn.sigmoid(c)
    o_ref[...] = _dot3(a, w_ref[...]) + b_ref[...]


def _modulation(cvec, w_mod, b_mod):
    m = cvec.shape[0]
    n = w_mod.shape[1]
    tn = 512
    return pl.pallas_call(
        _mod_body,
        out_shape=jax.ShapeDtypeStruct((m, n), F32),
        grid=(n // tn,),
        in_specs=[pl.BlockSpec((m, D_MODEL), lambda j: (0, 0)),
                  pl.BlockSpec((D_MODEL, tn), lambda j: (0, j)),
                  pl.BlockSpec((1, tn), lambda j: (0, j))],
        out_specs=pl.BlockSpec((m, tn), lambda j: (0, j)),
        compiler_params=_cparams(),
        name="modulation",
    )(cvec, w_mod, b_mod.reshape(1, n))


IN_UNITS = 2


def _inproj_body(*refs):
    U = IN_UNITS
    x_ref = refs[0]
    mod_refs = refs[1:1 + U]
    g_ref, w_ref, qg_ref, kvg_ref, wq_ref, wkv_ref = refs[1 + U:7 + U]
    rope_s_refs = refs[7 + U:7 + 2 * U]
    rope_m_refs = refs[7 + 2 * U:7 + 3 * U]
    q_ref, k_ref, v_ref, sq_ref, sk_ref, sv_ref, dqkv_ref, dz_ref, dg_ref = refs[7 + 3 * U:]
    rows = [slice(u * TM, (u + 1) * TM) for u in range(U)]
    hs = [(_rms(x_ref[r, :], g_ref[...]) * (1.0 + m[0, 0, 1:2, :]) + m[0, 0, 0:1, :]).astype(BF16)
          for r, m in zip(rows, mod_refs)]

    def proj(cols):
        return [_dot(h, w_ref[:, cols[0]:cols[1]]) for h in hs]

    for ref, cols in ((dqkv_ref, _C_DQKV), (dz_ref, _C_DZ), (dg_ref, _C_DG)):
        for r, p in zip(rows, proj(cols)):
            ref[r, :] = p
    for r, p in zip(rows, proj(_C_SV)):
        sv_ref[r, :] = p.astype(BF16)
    for r, p, rope_ref in zip(rows, proj(_C_SK), rope_s_refs):
        sk_ref[r, :] = _rope_cols(p, rope_ref, SWA_HEAD_DIM // 2).astype(BF16)
    lane = lax.broadcasted_iota(jnp.int32, (TM, LANES), 1)
    lo = lane < SWA_HEAD_DIM
    for r, sq, rope_ref in zip(rows, proj(_C_SQ), rope_s_refs):
        for c in range(4):
            xr = _rope_cols(sq[:, c * LANES:(c + 1) * LANES], rope_ref, SWA_HEAD_DIM // 2)
            xs = pltpu.roll(xr, SWA_HEAD_DIM, axis=1)
            if c < 2:
                a, b = jnp.where(lo, xr, 0.0), jnp.where(lo, xs, 0.0)
            else:
                a, b = jnp.where(lo, 0.0, xs), jnp.where(lo, 0.0, xr)
            sq_ref[r, (2 * c) * LANES:(2 * c + 1) * LANES] = a.astype(BF16)
            sq_ref[r, (2 * c + 1) * LANES:(2 * c + 2) * LANES] = b.astype(BF16)
    half = MLA_ROPE // 2
    qns = [_rms(p, qg_ref[...]).astype(BF16) for p in proj(_C_CQ)]
    qs = [_dot(qn, wq_ref[...]) for qn in qns]
    kvns = [_rms(p, kvg_ref[...]).astype(BF16) for p in proj(_C_CKV)]
    kvs = [_dot(kvn, wkv_ref[...]) for kvn in kvns]
    krs = [_rope_cols(p, rope_ref, half).astype(BF16)
           for p, rope_ref in zip(proj(_C_KRT), rope_m_refs)]
    for r, q, kv, kr, rope_ref in zip(rows, qs, kvs, krs, rope_m_refs):
        q_ref[r, 0:512] = q[:, 0:512].astype(BF16)
        for c in range(2):
            lo_, hi_ = 512 + c * LANES, 512 + (c + 1) * LANES
            q_ref[r, lo_:hi_] = _rope_cols(q[:, lo_:hi_], rope_ref, half).astype(BF16)
        for c in range(4):
            k_ref[r, (2 * c) * LANES:(2 * c + 1) * LANES] = kv[:, c * LANES:(c + 1) * LANES].astype(BF16)
            k_ref[r, (2 * c + 1) * LANES:(2 * c + 2) * LANES] = kr
        v_ref[r, :] = kv[:, 512:1024].astype(BF16)


def _inproj(x, modtab, g, w, qg, kvg, wq, wkv, rope_s, rope_m, nb_per_batch):
    n = x.shape[0]
    step = TM * IN_UNITS
    assert n % step == 0
    row = lambda i: (i, 0)
    full = lambda i: (0, 0)
    unit = lambda i, u: i * IN_UNITS + u
    widths = [(768, BF16), (1024, BF16), (512, BF16), (1024, BF16), (128, BF16), (128, BF16),
              (1536, F32), (512, F32), (128, F32)]
    mod_specs = [pl.BlockSpec((1, 1, 6, D_MODEL),
                              lambda i, u=u: (unit(i, u) // nb_per_batch,
                                              jnp.minimum(unit(i, u) % nb_per_batch, 1), 0, 0))
                 for u in range(IN_UNITS)]
    rope_specs = [pl.BlockSpec((3, TM, LANES), lambda i, u=u: (0, unit(i, u) % nb_per_batch, 0))
                  for u in range(IN_UNITS)]
    return pl.pallas_call(
        _inproj_body,
        out_shape=[jax.ShapeDtypeStruct((n, wd), dt) for wd, dt in widths],
        grid=(n // step,),
        in_specs=[pl.BlockSpec((step, D_MODEL), row)] + mod_specs
        + [pl.BlockSpec((1, D_MODEL), full), pl.BlockSpec((D_MODEL, _N_PROJ), full),
           pl.BlockSpec((1, MLA_Q_RANK), full), pl.BlockSpec((1, MLA_KV_RANK), full),
           pl.BlockSpec((MLA_Q_RANK, 768), full), pl.BlockSpec((MLA_KV_RANK, 1024), full)]
        + rope_specs + rope_specs,
        out_specs=[pl.BlockSpec((step, wd), row) for wd, _ in widths],
        compiler_params=_cparams(),
        name="inproj",
    )(x, *([modtab] * IN_UNITS), g, w, qg, kvg, wq, wkv, *([rope_s] * IN_UNITS), *([rope_m] * IN_UNITS))


def _mla_attn_body(q_ref, k_ref, v_ref, o_ref, *, t_ctx):
    c1 = (MLA_NOPE + MLA_ROPE) ** -0.5 * LOG2E
    lane = lax.broadcasted_iota(jnp.int32, (TM, LANES), 1)
    zero = jnp.zeros((TM, LANES), BF16)

    def attend(nk):
        for c in range(4):
            kc = k_ref[0:nk, c * 2 * LANES:(c + 1) * 2 * LANES]
            vc = v_ref[0:nk, c * LANES:(c + 1) * LANES]
            qn = q_ref[:, c * LANES:(c + 1) * LANES]
            qr = q_ref[:, 512 + (c // 2) * LANES:512 + (c // 2 + 1) * LANES]
            qa = [jnp.concatenate(
                [jnp.where((lane >= 64 * s) & (lane < 64 * (s + 1)), qn, zero),
                 jnp.where((lane >= 32 * ((2 * c + s) % 4)) & (lane < 32 * ((2 * c + s) % 4 + 1)), qr, zero)],
                axis=1) for s in range(2)]
            sc = [_dot_nt(q, kc) * c1 for q in qa]
            m = [jnp.max(t, axis=-1, keepdims=True) for t in sc]
            e = [jnp.exp2(t - mm) for t, mm in zip(sc, m)]
            l = [jnp.sum(t, axis=-1, keepdims=True) for t in e]
            outs = [_dot(t.astype(BF16), vc) / ll for t, ll in zip(e, l)]
            o_ref[:, c * LANES:(c + 1) * LANES] = jnp.where(lane < 64, outs[0], outs[1]).astype(BF16)

    j = pl.program_id(1)

    @pl.when(j == 0)
    def _():
        attend(t_ctx)

    @pl.when(j > 0)
    def _():
        attend(k_ref.shape[0])


def _mla_attention(q, k, v, n_batch, rows):
    nbp = rows // TM
    return pl.pallas_call(
        functools.partial(_mla_attn_body, t_ctx=TM),
        out_shape=jax.ShapeDtypeStruct((n_batch * rows, BRANCH_W), BF16),
        grid=(n_batch, nbp),
        in_specs=[pl.BlockSpec((TM, 768), lambda b, j: (b * nbp + j, 0)),
                  pl.BlockSpec((rows, 1024), lambda b, j: (b, 0)),
                  pl.BlockSpec((rows, 512), lambda b, j: (b, 0))],
        out_specs=pl.BlockSpec((TM, BRANCH_W), lambda b, j: (b * nbp + j, 0)),
        compiler_params=_cparams(2),
        name="mla_attention",
    )(q, k, v)


SWA_QB = 128
SWA_NQ = 2


def _swa_body(q_ref, k_ref, v_ref, sink_ref, o_ref, *, t_ctx, seq):
    c1 = SWA_HEAD_DIM ** -0.5 * LOG2E
    j = pl.program_id(1)
    n_ctx_steps = t_ctx // (SWA_QB * SWA_NQ)
    lane = lax.broadcasted_iota(jnp.int32, (SWA_QB, LANES), 1)
    lo = lane < SWA_HEAD_DIM
    win = 3 * SWA_QB

    def finish(rows, parts):
        for c in range(4):
            g = c // 2
            r0 = (2 * (c % 2)) * SWA_QB
            a = parts[g][r0:r0 + SWA_QB]
            b = parts[g][r0 + SWA_QB:r0 + 2 * SWA_QB]
            if g == 0:
                col = jnp.where(lo, a, pltpu.roll(b, SWA_HEAD_DIM, axis=1))
            else:
                col = jnp.where(lo, pltpu.roll(a, SWA_HEAD_DIM, axis=1), b)
            o_ref[rows, c * LANES:(c + 1) * LANES] = col.astype(BF16)

    def blocks(blks):
        kc, vc = k_ref[0:t_ctx, :], v_ref[0:t_ctx, :]
        windowed = blks[0] is not None
        kw, vw, band = [], [], []
        for blk in blks if windowed else ():
            start = jnp.clip((blk - 1) * SWA_QB, 0, seq - win)
            rs = pl.multiple_of(t_ctx + start, SWA_QB)
            kw.append(k_ref[pl.ds(rs, win), :])
            vw.append(v_ref[pl.ds(rs, win), :])
            qpos = blk * SWA_QB + lax.broadcasted_iota(jnp.int32, (SWA_QB, win), 0)
            kpos = start + lax.broadcasted_iota(jnp.int32, (SWA_QB, win), 1)
            band1 = jnp.abs(kpos - qpos) <= SWA_WINDOW
            band.append(jnp.concatenate([band1] * 4, axis=0))
        units = [(sub, g) for sub in range(SWA_NQ) for g in range(SWA_KV_HEADS)]
        rows = [slice(sub * SWA_QB, (sub + 1) * SWA_QB) for sub in range(SWA_NQ)]
        qg = [jnp.concatenate([q_ref[rows[sub], (4 * g + i) * LANES:(4 * g + i + 1) * LANES] for i in range(4)],
                              axis=0) for sub, g in units]
        sk = [c1 * jnp.concatenate(
            [jnp.broadcast_to(sink_ref[4 * g + i:4 * g + i + 1, 0:1], (SWA_QB, 1)) for i in range(4)], axis=0)
            for _, g in units]
        t_c = [_dot_nt(q, kc) * c1 for q in qg]
        m = [jnp.maximum(jnp.max(t, axis=-1, keepdims=True), s) for t, s in zip(t_c, sk)]
        if windowed:
            t_w = [jnp.where(band[sub], _dot_nt(q, kw[sub]) * c1, -jnp.inf) for (sub, _), q in zip(units, qg)]
            m = [jnp.maximum(mm, jnp.max(t, axis=-1, keepdims=True)) for mm, t in zip(m, t_w)]
        e_c = [jnp.exp2(t - mm) for t, mm in zip(t_c, m)]
        l = [jnp.sum(e, axis=-1, keepdims=True) + jnp.exp2(s - mm) for e, s, mm in zip(e_c, sk, m)]
        acc = [_dot(e.astype(BF16), vc) for e in e_c]
        if windowed:
            e_w = [jnp.exp2(t - mm) for t, mm in zip(t_w, m)]
            l = [ll + jnp.sum(e, axis=-1, keepdims=True) for ll, e in zip(l, e_w)]
            acc = [a + _dot(e.astype(BF16), vw[sub]) for a, e, (sub, _) in zip(acc, e_w, units)]
        parts = [a / ll for a, ll in zip(acc, l)]
        for sub in range(SWA_NQ):
            finish(rows[sub], parts[SWA_KV_HEADS * sub:SWA_KV_HEADS * (sub + 1)])

    @pl.when(j < n_ctx_steps)
    def _():
        blocks([None] * SWA_NQ)

    @pl.when(j >= n_ctx_steps)
    def _():
        blocks([(j - n_ctx_steps) * SWA_NQ + sub for sub in range(SWA_NQ)])


def _swa_attention(q, k, v, sink, n_batch, t_ctx, seq):
    rows = t_ctx + seq
    qrows = SWA_QB * SWA_NQ
    assert t_ctx % qrows == 0 and seq % qrows == 0
    nqb = rows // qrows
    sink_tab = jnp.broadcast_to(sink.astype(F32)[:, None], (SWA_HEADS, LANES))
    return pl.pallas_call(
        functools.partial(_swa_body, t_ctx=t_ctx, seq=seq),
        out_shape=jax.ShapeDtypeStruct((n_batch * rows, BRANCH_W), BF16),
        grid=(n_batch, nqb),
        in_specs=[pl.BlockSpec((qrows, 1024), lambda b, j: (b * nqb + j, 0)),
                  pl.BlockSpec((rows, LANES), lambda b, j: (b, 0)),
                  pl.BlockSpec((rows, LANES), lambda b, j: (b, 0)),
                  pl.BlockSpec((SWA_HEADS, LANES), lambda b, j: (0, 0))],
        out_specs=pl.BlockSpec((qrows, BRANCH_W), lambda b, j: (b * nqb + j, 0)),
        compiler_params=_cparams(2),
        name="swa_attention",
    )(q, k, v, sink_tab)


MERGE_UNITS = 2


def _merge_body(*refs):
    U = MERGE_UNITS
    row_refs = [refs[6 * u:6 * (u + 1)] for u in range(U)]
    mod_refs = refs[6 * U:7 * U]
    g1_ref, g2_ref, wg_ref, dng_ref, g512_ref, wb_ref, wo_ref, rw_ref, rb_ref = refs[7 * U:7 * U + 9]
    xo_ref, h2_ref, lg_ref = refs[7 * U + 9:]
    rows = [slice(u * TM, (u + 1) * TM) for u in range(U)]
    xs = [r[0][...] for r in row_refs]
    hs = [(_rms(x, g1_ref[...]) * (1.0 + m[0, 0, 1:2, :]) + m[0, 0, 0:1, :]).astype(BF16)
          for x, m in zip(xs, mod_refs)]
    ods = [r[3][...] + r[4][...] for r in row_refs]
    sqs = [_split_bf16(od * od) for od in ods]
    mss = [(_dot(hi, g512_ref[...]) + _dot(lo, g512_ref[...])) * (1.0 / DN_HEAD_DIM) for hi, lo in sqs]
    ocs = [(od * lax.rsqrt(ms + NORM_EPS) * dng_ref[...] * (r[5][...] * jax.nn.sigmoid(r[5][...]))).astype(BF16)
           for od, ms, r in zip(ods, mss, row_refs)]
    ys = [None] * U
    for i in range(N_BRANCH):
        gates = [jax.nn.sigmoid(_dot(h, wg_ref[:, i * D_MODEL:(i + 1) * D_MODEL])) for h in hs]
        branch = [(r[1][...], r[2][...], oc)[i] for r, oc in zip(row_refs, ocs)]
        yis = [g * _dot(o, wb_ref[i]) for g, o in zip(gates, branch)]
        ys = [yi if y is None else y + yi for y, yi in zip(ys, yis)]
    xns = [x + m[0, 0, 2:3, :] * _dot(y.astype(BF16), wo_ref[...]) for x, m, y in zip(xs, mod_refs, ys)]
    h2s = [_rms(xn, g2_ref[...]) * (1.0 + m[0, 0, 4:5, :]) + m[0, 0, 3:4, :] for xn, m in zip(xns, mod_refs)]
    lgs = [_dot3_nt(rw_ref[...], h2) + rb_ref[...] for h2 in h2s]
    for r, xn, h2, lg in zip(rows, xns, h2s, lgs):
        xo_ref[r, :] = xn
        h2_ref[r, :] = h2.astype(BF16)
        lg_ref[:, r] = lg


def _merge(x, modtab, g1, g2, wg, oa, ob, dn_f, dn_b, dz, dn_g, g512, wb, wo, rwt, rb, nb_per_batch, latent_only):
    U = MERGE_UNITS
    if latent_only:
        nlat = nb_per_batch - 1
        n = x.shape[0] // nb_per_batch * nlat
        src = lambda u: (u // nlat) * nb_per_batch + u % nlat + 1
        mod_of = lambda u: (u // nlat, 1, 0, 0)
    else:
        n = x.shape[0]
        src = lambda u: u
        mod_of = lambda u: (u // nb_per_batch, jnp.minimum(u % nb_per_batch, 1), 0, 0)
    step = TM * U
    assert n % step == 0
    orow = lambda i: (i, 0)
    full2 = lambda i: (0, 0)
    row_specs, row_args = [], []
    for u in range(U):
        rowu = lambda i, u=u: (src(i * U + u), 0)
        row_specs += [pl.BlockSpec((TM, D_MODEL), rowu)] + [pl.BlockSpec((TM, BRANCH_W), rowu)] * 5
        row_args += [x, oa, ob, dn_f, dn_b, dz]
    mod_specs = [pl.BlockSpec((1, 1, 6, D_MODEL), lambda i, u=u: mod_of(i * U + u)) for u in range(U)]
    return pl.pallas_call(
        _merge_body,
        out_shape=[jax.ShapeDtypeStruct((n, D_MODEL), F32), jax.ShapeDtypeStruct((n, D_MODEL), BF16),
                   jax.ShapeDtypeStruct((N_EXPERTS, n), F32)],
        grid=(n // step,),
        in_specs=row_specs + mod_specs
        + [pl.BlockSpec((1, D_MODEL), full2), pl.BlockSpec((1, D_MODEL), full2),
           pl.BlockSpec((D_MODEL, N_BRANCH * D_MODEL), full2), pl.BlockSpec((1, BRANCH_W), full2),
           pl.BlockSpec((BRANCH_W, BRANCH_W), full2),
           pl.BlockSpec((N_BRANCH, BRANCH_W, D_MODEL), lambda i: (0, 0, 0)),
           pl.BlockSpec((D_MODEL, D_MODEL), full2),
           pl.BlockSpec((N_EXPERTS, D_MODEL), full2), pl.BlockSpec((N_EXPERTS, 1), full2)],
        out_specs=[pl.BlockSpec((step, D_MODEL), orow), pl.BlockSpec((step, D_MODEL), orow),
                   pl.BlockSpec((N_EXPERTS, step), lambda i: (0, i))],
        compiler_params=_cparams(),
        name="merge",
    )(*row_args, *([modtab] * U), g1, g2, wg, dn_g, g512, wb, wo, rwt, rb)


ROUTE_TB = 1024


def _router_body(lg_ref, tri_ref, e_ref, gate_ref, pos_ref, cnt_ref, run_ref):
    @pl.when(pl.program_id(0) == 0)
    def _():
        run_ref[...] = jnp.zeros_like(run_ref)

    lg = lg_ref[...]
    eid = lax.broadcasted_iota(jnp.int32, lg.shape, 0)
    work = lg
    vals, idxs = [], []
    sel = jnp.zeros(lg.shape, F32)
    for _ in range(TOP_K):
        m = jnp.max(work, axis=0, keepdims=True)
        idx = jnp.min(jnp.where(work == m, eid, N_EXPERTS), axis=0, keepdims=True)
        hit = eid == idx
        sel = jnp.where(hit, 1.0, sel)
        work = jnp.where(hit, -jnp.inf, work)
        vals.append(m)
        idxs.append(idx)
    ex = [jnp.exp(v - vals[0]) for v in vals]
    den = ex[0] + ex[1] + ex[2] + ex[3]
    before = _dot(sel.astype(BF16), tri_ref[...]) + run_ref[:, 0:1]
    for k in range(TOP_K):
        e_ref[k:k + 1, :] = idxs[k]
        gate_ref[k:k + 1, :] = ex[k] / den
        pos_ref[k:k + 1, :] = jnp.sum(jnp.where(eid == idxs[k], before, 0.0), axis=0,
                                      keepdims=True).astype(jnp.int32)
    run_ref[...] = run_ref[...] + jnp.sum(sel, axis=1, keepdims=True)
    cnt_ref[...] = run_ref[...]


def _router(logits_t):
    n = logits_t.shape[1]
    tri = (jnp.arange(ROUTE_TB)[:, None] < jnp.arange(ROUTE_TB)[None, :]).astype(BF16)
    blk = lambda i: (0, i)
    return pl.pallas_call(
        _router_body,
        out_shape=[jax.ShapeDtypeStruct((TOP_K, n), jnp.int32), jax.ShapeDtypeStruct((TOP_K, n), F32),
                   jax.ShapeDtypeStruct((TOP_K, n), jnp.int32), jax.ShapeDtypeStruct((N_EXPERTS, LANES), F32)],
        grid=(n // ROUTE_TB,),
        in_specs=[pl.BlockSpec((N_EXPERTS, ROUTE_TB), blk), pl.BlockSpec((ROUTE_TB, ROUTE_TB), lambda i: (0, 0))],
        out_specs=[pl.BlockSpec((TOP_K, ROUTE_TB), blk), pl.BlockSpec((TOP_K, ROUTE_TB), blk),
                   pl.BlockSpec((TOP_K, ROUTE_TB), blk), pl.BlockSpec((N_EXPERTS, LANES), lambda i: (0, 0))],
        scratch_shapes=[pltpu.VMEM((N_EXPERTS, LANES), F32)],
        compiler_params=_cparams(),
        name="router",
    )(logits_t, tri)


def _expert_body(ib_ref, ie_ref, lo_ref, hi_ref, x_ref, wgu_ref, bgu_ref, wdn_ref, bdn_ref, perm_ref, *rest):
    o_ref, wgu_s, wdn_s = rest[-3:]
    i = pl.program_id(0)
    prev = jnp.maximum(i - 1, 0)

    @pl.when((i == 0) | (ie_ref[i] != ie_ref[prev]))
    def _():
        for m in range(D_EXPERT // LANES):
            t = _dot(wgu_ref[0, 0, :, m * 2 * LANES:(m + 1) * 2 * LANES].astype(BF16), perm_ref[...])
            wgu_s[:, m * LANES:(m + 1) * LANES] = t[:, 0:LANES].astype(BF16)
            wgu_s[:, D_EXPERT + m * LANES:D_EXPERT + (m + 1) * LANES] = t[:, LANES:2 * LANES].astype(BF16)
        wdn_s[...] = wdn_ref[0, 0].astype(BF16)

    lo, hi = lo_ref[i], hi_ref[i]

    @pl.when(hi > lo)
    def _():
        gu = _dot(x_ref[...], wgu_s[...]) + bgu_ref[0]
        g_ = jnp.minimum(gu[:, :D_EXPERT], SWIGLU_LIMIT)
        u_ = jnp.clip(gu[:, D_EXPERT:], -SWIGLU_LIMIT, SWIGLU_LIMIT)
        act = (u_ + 1.0) * (g_ * jax.nn.sigmoid(SWIGLU_ALPHA * g_))
        y = _dot(act.astype(BF16), wdn_s[...]) + bdn_ref[0]
        first = (i == 0) | (ib_ref[i] != ib_ref[prev])
        row = lax.broadcasted_iota(jnp.int32, (MOE_BLK, 1), 0)
        o_ref[...] = jnp.where(first | ((row >= lo) & (row < hi)), y, o_ref[...])


def _experts(items, xs, w_gu, bgu, w_dn, bdn, layer, block0, n_rows, yb_prev):
    n_items = items[0].shape[0]
    perm = np.zeros((2 * LANES, 2 * LANES), np.float32)
    j = np.arange(LANES)
    perm[2 * j, j] = 1.0
    perm[2 * j + 1, LANES + j] = 1.0
    in_specs = [pl.BlockSpec((MOE_BLK, D_MODEL), lambda i, ib, ie, lo, hi: (ib[i] - block0, 0)),
                pl.BlockSpec((1, 1, D_MODEL, 2 * D_EXPERT), lambda i, ib, ie, lo, hi: (layer, ie[i], 0, 0)),
                pl.BlockSpec((1, 1, 2 * D_EXPERT), lambda i, ib, ie, lo, hi: (ie[i], 0, 0)),
                pl.BlockSpec((1, 1, D_EXPERT, D_MODEL), lambda i, ib, ie, lo, hi: (layer, ie[i], 0, 0)),
                pl.BlockSpec((1, 1, D_MODEL), lambda i, ib, ie, lo, hi: (ie[i], 0, 0)),
                pl.BlockSpec((2 * LANES, 2 * LANES), lambda i, ib, ie, lo, hi: (0, 0))]
    args = [*items, xs, w_gu, bgu, w_dn, bdn, jnp.asarray(perm, BF16)]
    aliases = {}
    if yb_prev is not None:
        in_specs.append(pl.BlockSpec(memory_space=pl.ANY))
        aliases = {len(args): 0}
        args.append(yb_prev)
    return pl.pallas_call(
        _expert_body,
        out_shape=jax.ShapeDtypeStruct((n_rows, D_MODEL), F32),
        grid_spec=pltpu.PrefetchScalarGridSpec(
            num_scalar_prefetch=4,
            grid=(n_items,),
            in_specs=in_specs,
            out_specs=pl.BlockSpec((MOE_BLK, D_MODEL), lambda i, ib, ie, lo, hi: (ib[i], 0)),
            scratch_shapes=[pltpu.VMEM((D_MODEL, 2 * D_EXPERT), BF16), pltpu.VMEM((D_EXPERT, D_MODEL), BF16)]),
        input_output_aliases=aliases,
        compiler_params=_cparams(),
        name="experts",
    )(*args)


def _lookup(table, idx):
    ids = jnp.arange(table.shape[0], dtype=jnp.int32)
    return jnp.sum(jnp.where(idx[..., None] == ids, table, 0), axis=-1)


def _work_items(cnt_start, cnt_end, block0, n_blk):
    n_items = n_blk + N_EXPERTS - 1
    b1 = block0 + n_blk
    first = jnp.maximum(cnt_start // MOE_BLK, block0)
    last = jnp.minimum((cnt_end - 1) // MOE_BLK, b1 - 1)
    n_e = jnp.where(cnt_end > cnt_start, jnp.maximum(last - first + 1, 0), 0)
    item_end = jnp.cumsum(n_e)
    item_start = item_end - n_e
    total = item_end[-1]
    ii = jnp.arange(n_items, dtype=jnp.int32)
    valid = ii < total
    e_i = jnp.sum((item_end[None, :] <= jnp.minimum(ii, total - 1)[:, None]).astype(jnp.int32), axis=1)
    blk = jnp.where(valid, _lookup(first, e_i) + ii - _lookup(item_start, e_i), b1 - 1)
    lo = jnp.clip(_lookup(cnt_start, e_i) - blk * MOE_BLK, 0, MOE_BLK)
    hi = jnp.clip(_lookup(cnt_end, e_i) - blk * MOE_BLK, 0, MOE_BLK)
    lo = jnp.where(valid, lo, 0)
    hi = jnp.where(valid, hi, 0)
    return blk.astype(jnp.int32), e_i.astype(jnp.int32), lo.astype(jnp.int32), hi.astype(jnp.int32)


def _moe(h2, logits_t, w_gu, bgu, w_dn, bdn, layer):
    n = h2.shape[0]
    a = n * TOP_K
    assert a % (MOE_BLK * MOE_CHUNKS) == 0
    top_e, gate, pos, cnt = _router(logits_t)
    counts = cnt[:, 0].astype(jnp.int32)
    cnt_end = jnp.cumsum(counts)
    cnt_start = cnt_end - counts
    slot = pos + _lookup(cnt_start, top_e)
    tok = jnp.broadcast_to(jnp.arange(n, dtype=jnp.int32)[None, :], (TOP_K, n))
    _, slot_tok = lax.sort_key_val(slot.reshape(-1), tok.reshape(-1))
    n_blk = a // MOE_BLK // MOE_CHUNKS
    rows_c = n_blk * MOE_BLK
    yb = None
    for c in range(MOE_CHUNKS):
        xs = h2.at[slot_tok[c * rows_c:(c + 1) * rows_c]].get(mode="promise_in_bounds")
        items = _work_items(cnt_start, cnt_end, c * n_blk, n_blk)
        yb = _experts(items, xs, w_gu, bgu, w_dn, bdn, layer, c * n_blk, a, yb)
    ys = [yb.at[slot[k]].get(mode="promise_in_bounds") for k in range(TOP_K)]
    return ys, gate.T


DN_GW = 256
HALO = 8
DN_PAIR = 4
_M_EYE, _M_BLK16, _M_OFF32, _M_OFF64, _M_INCL, _M_STRICT = 0, 1, 2, 3, 4, 6


def _dn_masks():
    row = np.arange(DN_CHUNK)[:, None]
    col = np.arange(DN_GW)[None, :] % DN_CHUNK
    b16 = (row // 16) == (col // 16)
    b32 = (row // 32) == (col // 32)
    m = np.stack([row == col, b16, b32 & ~b16, ~b32, col <= row, col >= row, col < row, col > row])
    hid = np.arange(DN_GW) // DN_HEAD_DIM
    return jnp.asarray(m, F32), jnp.asarray(hid[:, None] == hid[None, :], BF16)


def _split3(a):
    hi = a.astype(BF16)
    r = a - hi.astype(F32)
    lo = r.astype(BF16)
    lo2 = (r - lo.astype(F32)).astype(BF16)
    return hi, lo, lo2


def _bd(x, bdm_ref):
    xb = x.astype(BF16)
    return jnp.concatenate([xb, xb, xb, xb], axis=0) * bdm_ref[...]


def _mm(a, wbd):
    return _dot(a.astype(BF16), wbd)


def _tri_inverse(l_mats, m_ref, bdm_ref):
    half = DN_CHUNK
    bd = lambda v: _bd(v, bdm_ref)
    lds = [l * m_ref[_M_BLK16] for l in l_mats]
    ps = [_mm(ld, bd(ld)) for ld in lds]
    xs = [m_ref[_M_EYE] - ld for ld in lds]
    for _ in range(2):
        rs = [_mm(jnp.concatenate([p, x], axis=0), bd(p)) for p, x in zip(ps, xs)]
        ps = [r[0:half] for r in rs]
        xs = [x + r[half:2 * half] for x, r in zip(xs, rs)]
    xs = [x + _mm(x, bd(p)) for x, p in zip(xs, ps)]
    ts = [_mm(x, bd(l * m_ref[_M_OFF32])) for x, l in zip(xs, l_mats)]
    xs = [x - _mm(t, bd(x)) for x, t in zip(xs, ts)]
    ts = [_mm(x, bd(l * m_ref[_M_OFF64])) for x, l in zip(xs, l_mats)]
    return [x - _mm(t, bd(x)) for x, t in zip(xs, ts)]


def _dn_prep_body(cur_ref, prev_ref, next_ref, cw_ref, dg_ref, gc_ref, g512_ref, e_ref, tri_ref, m_ref, bdm_ref,
                  u_ref, w_ref, kd_ref, qd_ref, aq_ref, gl_ref,
                  ext_ref, q_s, k_s, v_s, gx_s, bx_s):
    j = pl.program_id(1)
    nbp = pl.num_programs(1)
    C = DN_CHUNK
    ext_ref[HALO:HALO + TM, :] = cur_ref[...]
    ext_ref[0:HALO, :] = jnp.where(j >= 2, prev_ref[...], 0.0)
    ext_ref[HALO + TM:2 * HALO + TM, :] = jnp.where((j >= 1) & (j < nbp - 1), next_ref[...], 0.0)
    y = None
    for t in range(DN_CONV):
        o = HALO - DN_CONV // 2 + t
        term = cw_ref[t:t + 1, :] * ext_ref[o:o + TM, :]
        y = term if y is None else y + term
    y = y * jax.nn.sigmoid(y)

    g512 = g512_ref[...]

    def head_sumsq(x):
        hi, lo = _split_bf16(x * x)
        return _dot(hi, g512) + _dot(lo, g512)

    q = y[:, 0:512]
    k = y[:, 512:1024]
    q_s[...] = q * lax.rsqrt(head_sumsq(q) + NORM_EPS) * (DN_HEAD_DIM ** -0.5)
    k_s[...] = k * lax.rsqrt(head_sumsq(k) + NORM_EPS)
    v_s[...] = y[:, 1024:1536]

    dg = dg_ref[...]
    beta_all = jax.nn.sigmoid(dg)
    z = dg + gc_ref[1:2, :]
    g_all = gc_ref[0:1, :] * (jnp.maximum(z, 0.0) + jnp.log(1.0 + jnp.exp(-jnp.abs(z))))
    bh, bl = _split_bf16(beta_all)
    bcat = jnp.concatenate([bh, bl], axis=0)
    gparts = jnp.concatenate(_split3(g_all), axis=1)
    for d in range(2):
        cs = _dot(tri_ref[d], gparts)
        gcum = cs[:, 0:128] + cs[:, 128:256] + cs[:, 256:384]
        gcat = jnp.concatenate(_split3(gcum), axis=0)
        for grp in range(2):
            eg = _dot(gcat, e_ref[4 + 2 * d + grp])
            gx_s[d, grp] = eg[0:TM] + eg[TM:2 * TM] + eg[2 * TM:3 * TM]
            eb = _dot(bcat, e_ref[2 * d + grp])
            bx_s[d, grp] = eb[0:TM] + eb[TM:2 * TM]

    def chunk_pair(it, carry):
        units = []
        l_mats = []
        for ci in range(DN_PAIR):
            cc = it * DN_PAIR + ci
            rows = pl.ds(pl.multiple_of(cc * C, C), C)
            for grp in range(2):
                cols = slice(grp * DN_GW, (grp + 1) * DN_GW)
                kg, qg = k_s[rows, cols], q_s[rows, cols]
                kb = [kg * bx_s[d, grp, rows, :] for d in range(2)]
                raw = _dot_nt(jnp.concatenate([kb[0], kb[1], qg], axis=0).astype(BF16), _bd(kg, bdm_ref))
                for d in range(2):
                    gx = gx_s[d, grp, rows, :]
                    rvec = jnp.sum(gx * m_ref[_M_EYE], axis=0, keepdims=True)
                    dec = jnp.exp(jnp.minimum(gx - rvec, 0.0)) * m_ref[_M_INCL + d]
                    l_mats.append(raw[C * d:C * (d + 1)] * dec * m_ref[_M_STRICT + d])
                    glast = gx[C - 1:C, :] if d == 0 else gx[0:1, :]
                    kd_ref[d, rows, cols] = (kg * jnp.exp(glast - gx)).astype(BF16)
                    qd_ref[d, rows, cols] = (qg * jnp.exp(gx)).astype(BF16)
                    aq_ref[d, rows, cols] = (raw[2 * C:3 * C] * dec).astype(BF16)
                    gl_ref[d, pl.ds(cc, 1), :, cols] = jnp.exp(glast).reshape(1, 1, DN_GW)
                    units.append((rows, cols, d, grp))
        tinvs = _tri_inverse(l_mats, m_ref, bdm_ref)
        for (rows, cols, d, grp), tinv in zip(units, tinvs):
            bx = bx_s[d, grp, rows, :]
            u_ref[d, rows, cols] = _mm(tinv, _bd(v_s[rows, cols] * bx, bdm_ref))
            w_ref[d, rows, cols] = _mm(tinv, _bd(k_s[rows, cols] * bx * jnp.exp(gx_s[d, grp, rows, :]), bdm_ref)
                                       ).astype(BF16)
        return carry

    lax.fori_loop(0, TM // C // DN_PAIR, chunk_pair, 0)


def _dn_prep(dqkv, dg, conv_w, a_log, dt_bias, n_batch, rows):
    n = dqkv.shape[0]
    nbp = rows // TM
    hb = TM // HALO
    H = DN_HEADS
    cw = jnp.zeros((8, 1536), F32).at[0:DN_CONV].set(conv_w)
    gc = jnp.zeros((8, LANES), F32)
    gc = gc.at[0, 2 * H:4 * H].set(-jnp.exp(a_log.reshape(-1))).at[1, 2 * H:4 * H].set(dt_bias.reshape(-1))
    hid = np.arange(512) // DN_HEAD_DIM
    g512 = jnp.asarray(hid[:, None] == hid[None, :], BF16)
    e = np.zeros((8, LANES, DN_GW), np.float32)
    for kind in range(2):
        for d in range(2):
            for grp in range(2):
                for h in range(4):
                    e[4 * kind + 2 * d + grp, 16 * kind + 8 * d + 4 * grp + h, 64 * h:64 * (h + 1)] = 1.0
    e = jnp.asarray(e, BF16)
    t = np.arange(TM)
    same = (t[:, None] // DN_CHUNK) == (t[None, :] // DN_CHUNK)
    tri = jnp.asarray(np.stack([same & (t[None, :] <= t[:, None]), same & (t[None, :] >= t[:, None])]), BF16)
    masks, bdm = _dn_masks()
    blk = lambda b, j: (b * nbp + j, 0)
    full2 = lambda b, j: (0, 0)
    full3 = lambda b, j: (0, 0, 0)
    dblk = lambda b, j: (0, b * nbp + j, 0)
    nlast = n // HALO - 1
    outs = pl.pallas_call(
        _dn_prep_body,
        out_shape=[jax.ShapeDtypeStruct((2, n, 512), F32)] + [jax.ShapeDtypeStruct((2, n, 512), BF16)] * 4
        + [jax.ShapeDtypeStruct((2, n // DN_CHUNK, 1, 512), F32)],
        grid=(n_batch, nbp),
        in_specs=[pl.BlockSpec((TM, 1536), blk),
                  pl.BlockSpec((HALO, 1536), lambda b, j: (jnp.maximum((b * nbp + j) * hb - 1, 0), 0)),
                  pl.BlockSpec((HALO, 1536), lambda b, j: (jnp.minimum((b * nbp + j + 1) * hb, nlast), 0)),
                  pl.BlockSpec((8, 1536), full2), pl.BlockSpec((TM, LANES), blk), pl.BlockSpec((8, LANES), full2),
                  pl.BlockSpec((512, 512), full2), pl.BlockSpec((8, LANES, DN_GW), full3),
                  pl.BlockSpec((2, TM, TM), full3), pl.BlockSpec((8, DN_CHUNK, DN_GW), full3),
                  pl.BlockSpec((DN_GW, DN_GW), full2)],
        out_specs=[pl.BlockSpec((2, TM, 512), dblk)] * 5
        + [pl.BlockSpec((2, TM // DN_CHUNK, 1, 512), lambda b, j: (0, b * nbp + j, 0, 0))],
        scratch_shapes=[pltpu.VMEM((TM + 2 * HALO, 1536), F32), pltpu.VMEM((TM, 512), F32),
                        pltpu.VMEM((TM, 512), F32), pltpu.VMEM((TM, 512), F32),
                        pltpu.VMEM((2, 2, TM, DN_GW), F32), pltpu.VMEM((2, 2, TM, DN_GW), F32)],
        compiler_params=_cparams(2),
        name="dn_prep",
    )(dqkv, dqkv, dqkv, cw, dg, gc, g512, e, tri, masks, bdm)
    return outs, g512


DN_SCAN_NB = 8


def _dn_scan_body(uf, wf, kf, qf, af, gf, ub, wb, kb, qb, ab, gb, bdm_ref, of_ref, ob_ref, s_ref):
    @pl.when(pl.program_id(1) == 0)
    def _():
        s_ref[...] = jnp.zeros_like(s_ref)

    same = bdm_ref[...] != 0
    dirs = ((uf, wf, kf, qf, af, gf, of_ref), (ub, wb, kb, qb, ab, gb, ob_ref))
    units = [(bb, d, grp) for bb in range(DN_SCAN_NB) for d in range(2) for grp in range(2)]
    cols = [slice(grp * DN_GW, (grp + 1) * DN_GW) for grp in range(2)]
    s = [s_ref[bb, d, grp] for bb, d, grp in units]
    sq = [_dot(jnp.concatenate([dirs[d][1][0, bb, :, cols[grp]], dirs[d][3][0, bb, :, cols[grp]]], axis=0),
               st.astype(BF16)) for (bb, d, grp), st in zip(units, s)]
    vnew = [dirs[d][0][0, bb, :, cols[grp]] - r[0:DN_CHUNK] for (bb, d, grp), r in zip(units, sq)]
    intra = [_dot(dirs[d][4][0, bb, :, cols[grp]], _bd(v, bdm_ref)) for (bb, d, grp), v in zip(units, vnew)]
    upd = [lax.dot_general(dirs[d][2][0, bb, :, cols[grp]], v.astype(BF16), (((0,), (0,)), ((), ())),
                           preferred_element_type=F32) for (bb, d, grp), v in zip(units, vnew)]
    for (bb, d, grp), st, r, a, up in zip(units, s, sq, intra, upd):
        dirs[d][6][bb, :, cols[grp]] = r[DN_CHUNK:2 * DN_CHUNK] + a
        s_ref[bb, d, grp] = st * dirs[d][5][0, bb, 0, :, cols[grp]] + jnp.where(same, up, 0.0)


def _dn_scan(u, w, kd, qd, aq, gl, n_batch, rows, t_ctx):
    nch = rows // DN_CHUNK
    nctx = t_ctx // DN_CHUNK
    assert n_batch % DN_SCAN_NB == 0

    def cb(c):
        return jnp.where(c < nctx, nctx - 1 - c, nch - 1 - (c - nctx))

    per_batch = lambda a: a.reshape(2, n_batch, rows, 512)
    u, w, kd, qd, aq = map(per_batch, (u, w, kd, qd, aq))
    gl = gl.reshape(2, n_batch, nch, 1, 512)
    blk = (1, DN_SCAN_NB, DN_CHUNK, 512)
    gblk = (1, DN_SCAN_NB, 1, 1, 512)
    specs = []
    for d, ch in ((0, lambda c: c), (1, cb)):
        specs += [pl.BlockSpec(blk, lambda b, c, d=d, ch=ch: (d, b, ch(c), 0))] * 5
        specs += [pl.BlockSpec(gblk, lambda b, c, d=d, ch=ch: (d, b, ch(c), 0, 0))]
    specs.append(pl.BlockSpec((DN_GW, DN_GW), lambda b, c: (0, 0)))
    o_f, o_b = pl.pallas_call(
        _dn_scan_body,
        out_shape=[jax.ShapeDtypeStruct((n_batch, rows, 512), F32)] * 2,
        grid=(n_batch // DN_SCAN_NB, nch),
        in_specs=specs,
        out_specs=[pl.BlockSpec((DN_SCAN_NB, DN_CHUNK, 512), lambda b, c: (b, c, 0)),
                   pl.BlockSpec((DN_SCAN_NB, DN_CHUNK, 512), lambda b, c: (b, cb(c), 0))],
        scratch_shapes=[pltpu.VMEM((DN_SCAN_NB, 2, 2, DN_GW, DN_GW), F32)],
        compiler_params=_cparams(2),
        name="dn_scan",
    )(u, w, kd, qd, aq, gl, u, w, kd, qd, aq, gl, _dn_masks()[1])
    return o_f.reshape(n_batch * rows, 512), o_b.reshape(n_batch * rows, 512)


def _deltanet(dqkv, dg, conv_w, a_log, dt_bias, n_batch, rows, t_ctx):
    (u, w, kd, qd, aq, gl), g512 = _dn_prep(dqkv, dg, conv_w, a_log, dt_bias, n_batch, rows)
    o_f, o_b = _dn_scan(u, w, kd, qd, aq, gl, n_batch, rows, t_ctx)
    return o_f, o_b, g512


def _combine_body(x_ref, y0_ref, y1_ref, y2_ref, y3_ref, gate_ref, mod_ref, fg_ref, o_ref, *, final):
    gate = gate_ref[...]
    y = None
    for k, y_ref in enumerate((y0_ref, y1_ref, y2_ref, y3_ref)):
        yk = y_ref[...] * gate[:, k:k + 1]
        y = yk if y is None else y + yk
    xo = x_ref[...] + mod_ref[0, 0, 5:6, :] * y
    o_ref[...] = _rms(xo, fg_ref[...]) if final else xo


def _combine(x, ys, gate_rows, modtab, final_g, nb_per_batch, latent_only):
    n = x.shape[0]
    if latent_only:
        nlat = nb_per_batch - 1
        mod_idx = lambda i: (i // nlat, 1, 0, 0)
    else:
        mod_idx = lambda i: (i // nb_per_batch, jnp.minimum(i % nb_per_batch, 1), 0, 0)
    row = lambda i: (i, 0)
    return pl.pallas_call(
        functools.partial(_combine_body, final=latent_only),
        out_shape=jax.ShapeDtypeStruct((n, D_MODEL), F32),
        grid=(n // TM,),
        in_specs=[pl.BlockSpec((TM, D_MODEL), row)] * 5
        + [pl.BlockSpec((TM, TOP_K), row), pl.BlockSpec((1, 1, 6, D_MODEL), mod_idx),
           pl.BlockSpec((1, D_MODEL), lambda i: (0, 0))],
        out_specs=pl.BlockSpec((TM, D_MODEL), row),
        compiler_params=_cparams(),
        name="combine",
    )(x, *ys, gate_rows, modtab, final_g)


def _rope_tables(seq, t_ctx, rot_dim):
    rows = seq // GRID_W
    row = jnp.broadcast_to(jnp.arange(rows)[:, None], (rows, GRID_W)).reshape(-1).astype(F32)
    col = jnp.broadcast_to(jnp.arange(GRID_W)[None, :], (rows, GRID_W)).reshape(-1).astype(F32)
    n_freq = rot_dim // 4
    inv_freq = ROPE_THETA ** (-jnp.arange(n_freq, dtype=F32) / n_freq)
    ang = jnp.concatenate([row[:, None] * inv_freq, col[:, None] * inv_freq], axis=-1)
    cos, sin = jnp.cos(ang), jnp.sin(ang)
    zero = jnp.zeros_like(sin)
    c = jnp.concatenate([cos, cos], axis=-1)
    sa = jnp.concatenate([-sin, zero], axis=-1)
    sb = jnp.concatenate([zero, sin], axis=-1)
    ident = jnp.stack([jnp.ones((t_ctx, rot_dim), F32), jnp.zeros((t_ctx, rot_dim), F32),
                       jnp.zeros((t_ctx, rot_dim), F32)])
    tab = jnp.concatenate([ident, jnp.stack([c, sa, sb])], axis=1)
    return jnp.tile(tab, (1, 1, LANES // rot_dim))


def _layer_weights(w_in, w_q_up, w_kv_up, w_branch, w_out, router_w, router_b, b_gu, b_dn):
    sizes = (MLA_Q_RANK, MLA_KV_RANK, MLA_ROPE, 512, 128, 128, 1536, 512, 16, 16, N_BRANCH * D_MODEL)
    offs = np.cumsum((0,) + sizes)
    part = lambda i: w_in[:, offs[i]:offs[i + 1]]
    pad = jnp.zeros((D_MODEL, LANES - 32), F32)
    w_proj = jnp.concatenate(
        [part(0), part(1), part(3), part(4), part(5), part(6), part(7),
         jnp.tile(part(2), (1, LANES // MLA_ROPE)), part(8), part(9), pad], axis=1).astype(BF16)
    w_gate = part(10).astype(BF16)
    hq = MLA_NOPE + MLA_ROPE
    wq = w_q_up.reshape(MLA_Q_RANK, MLA_HEADS, hq)
    wq = jnp.concatenate([wq[:, :, :MLA_NOPE].reshape(MLA_Q_RANK, -1),
                          wq[:, :, MLA_NOPE:].reshape(MLA_Q_RANK, -1)], axis=1).astype(BF16)
    wkv = w_kv_up.reshape(MLA_KV_RANK, MLA_HEADS, MLA_NOPE + MLA_V)
    wkv = jnp.concatenate([wkv[:, :, :MLA_NOPE].reshape(MLA_KV_RANK, -1),
                           wkv[:, :, MLA_NOPE:].reshape(MLA_KV_RANK, -1)], axis=1).astype(BF16)
    bgu = jnp.concatenate([b_gu[:, 0::2], b_gu[:, 1::2]], axis=-1).reshape(N_EXPERTS, 1, 2 * D_EXPERT)
    return dict(w_proj=w_proj, w_gate=w_gate, wq=wq, wkv=wkv, wb=w_branch.astype(BF16),
                wo=w_out.astype(BF16), rwt=router_w.T, rb=router_b.reshape(N_EXPERTS, 1),
                bgu=bgu, bdn=b_dn.reshape(N_EXPERTS, 1, D_MODEL))


def kernel(x, c, ctx, c_ctx, w_mod, b_mod, norm1_g, norm2_g, w_in, mla_q_norm_g, mla_w_q_up,
           mla_kv_norm_g, mla_w_kv_up, swa_sink, dn_conv_w, dn_a_log, dn_dt_bias, dn_norm_g,
           w_branch, w_out, router_w, router_b, exp_w_gu, exp_b_gu, exp_w_dn, exp_b_dn, final_norm_g):
    B, S, D = x.shape
    T = ctx.shape[1]
    R = T + S
    depth = w_mod.shape[0]
    assert D == D_MODEL and T == TM and S % TM == 0 and (B * R) % ROUTE_TB == 0
    nbp = R // TM
    xa = jnp.concatenate([ctx, x], axis=1).reshape(B * R, D)
    rope_s = _rope_tables(S, T, SWA_HEAD_DIM)
    rope_m = _rope_tables(S, T, MLA_ROPE)
    cvec = jnp.concatenate([c, c_ctx[None, :]], axis=0)
    for l in range(depth):
        w = _layer_weights(w_in[l], mla_w_q_up[l], mla_w_kv_up[l], w_branch[l], w_out[l], router_w[l],
                           router_b[l], exp_b_gu[l], exp_b_dn[l])
        mod = _modulation(cvec, w_mod[l], b_mod[l]).reshape(B + 1, 6, D)
        modtab = jnp.stack([jnp.broadcast_to(mod[B][None], (B, 6, D)), mod[:B]], axis=1)
        g1 = norm1_g[l].reshape(1, D)
        g2 = norm2_g[l].reshape(1, D)
        q_m, k_m, v_m, sq, sk, sv, dqkv, dz, dg = _inproj(
            xa, modtab, g1, w["w_proj"], mla_q_norm_g[l].reshape(1, -1), mla_kv_norm_g[l].reshape(1, -1),
            w["wq"], w["wkv"], rope_s, rope_m, nbp)
        o_a = _mla_attention(q_m, k_m, v_m, B, R)
        o_b = _swa_attention(sq, sk, sv, swa_sink[l], B, T, S)
        dn_f, dn_b, g512 = _deltanet(dqkv, dg, dn_conv_w[l], dn_a_log[l], dn_dt_bias[l], B, R, T)
        last = l == depth - 1
        xn, h2, logits_t = _merge(xa, modtab, g1, g2, w["w_gate"], o_a, o_b, dn_f, dn_b, dz,
                                  jnp.tile(dn_norm_g[l], DN_HEADS).reshape(1, BRANCH_W), g512, w["wb"], w["wo"],
                                  w["rwt"], w["rb"], nbp, last)
        ys, gate_rows = _moe(h2, logits_t, exp_w_gu, w["bgu"], exp_w_dn, w["bdn"], l)
        xa = _combine(xn, ys, gate_rows, modtab, final_norm_g.reshape(1, D), nbp, last)
    return xa.reshape(B, S, D)
```

```python
import functools

import jax
import jax.numpy as jnp
import numpy as np
from jax import lax
from jax.experimental import pallas as pl
from jax.experimental.pallas import tpu as pltpu

F32 = jnp.float32
BF16 = jnp.bfloat16

D_MODEL = 1024
GRID_W = 64
ROPE_THETA = 10000.0
NORM_EPS = 1e-6
MLA_HEADS = 8
MLA_Q_RANK = 384
MLA_KV_RANK = 256
MLA_NOPE = 64
MLA_ROPE = 32
MLA_V = 64
SWA_HEADS = 8
SWA_KV_HEADS = 2
SWA_HEAD_DIM = 64
SWA_WINDOW = 128
DN_HEADS = 8
DN_HEAD_DIM = 64
DN_CONV = 5
DN_CHUNK = 64
N_BRANCH = 3
BRANCH_W = 512
N_EXPERTS = 32
TOP_K = 4
D_EXPERT = 1024
SWIGLU_LIMIT = 7.0
SWIGLU_ALPHA = 1.702
MOE_BLK = 512
MOE_CHUNKS = 4
COMBINE_CHUNKS = 4

LANES = 128
LOG2E = 1.4426950408889634
TM = 256
VMEM_LIMIT = 56 * 1024 * 1024

_C_CQ = (0, 384)
_C_CKV = (384, 640)
_C_SQ = (640, 1152)
_C_SK = (1152, 1280)
_C_SV = (1280, 1408)
_C_DQKV = (1408, 2944)
_C_DZ = (2944, 3456)
_C_KRT = (3456, 3584)
_C_DG = (3584, 3712)
_N_PROJ = 3712


def _cparams(n_axes=1):
    return pltpu.CompilerParams(dimension_semantics=("arbitrary",) * n_axes, vmem_limit_bytes=VMEM_LIMIT)


def _dot(a, b):
    return jnp.dot(a, b, preferred_element_type=F32)


def _dot_nt(a, b):
    return lax.dot_general(a, b, (((1,), (1,)), ((), ())), preferred_element_type=F32)


def _split_bf16(a):
    hi = a.astype(BF16)
    lo = (a - hi.astype(F32)).astype(BF16)
    return hi, lo


def _dot3(a, b):
    ah, al = _split_bf16(a)
    bh, bl = _split_bf16(b)
    return _dot(ah, bh) + (_dot(ah, bl) + _dot(al, bh))


def _dot3_nt(a, b):
    ah, al = _split_bf16(a)
    bh, bl = _split_bf16(b)
    return _dot_nt(ah, bh) + (_dot_nt(ah, bl) + _dot_nt(al, bh))


def _rms(x, g):
    return x * lax.rsqrt(jnp.mean(x * x, axis=-1, keepdims=True) + NORM_EPS) * g


def _rope_cols(x, tab_ref, half):
    c, sa, sb = tab_ref[0], tab_ref[1], tab_ref[2]
    return x * c + pltpu.roll(x, LANES - half, axis=1) * sa + pltpu.roll(x, half, axis=1) * sb


def _mod_body(c_ref, w_ref, b_ref, o_ref):
    c = c_ref[...]
    a = c * jax.nn.sigmoid(c)
    o_ref[...] = _dot3(a, w_ref[...]) + b_ref[...]


def _modulation(cvec, w_mod, b_mod):
    m = cvec.shape[0]
    n = w_mod.shape[1]
    tn = 512
    return pl.pallas_call(
        _mod_body,
        out_shape=jax.ShapeDtypeStruct((m, n), F32),
        grid=(n // tn,),
        in_specs=[pl.BlockSpec((m, D_MODEL), lambda j: (0, 0)),
                  pl.BlockSpec((D_MODEL, tn), lambda j: (0, j)),
                  pl.BlockSpec((1, tn), lambda j: (0, j))],
        out_specs=pl.BlockSpec((m, tn), lambda j: (0, j)),
        compiler_params=_cparams(),
        name="modulation",
    )(cvec, w_mod, b_mod.reshape(1, n))


IN_UNITS = 2


def _inproj_body(*refs):
    U = IN_UNITS
    x_ref = refs[0]
    mod_refs = refs[1:1 + U]
    g_ref, w_ref, qg_ref, kvg_ref, wq_ref, wkv_ref = refs[1 + U:7 + U]
    rope_s_refs = refs[7 + U:7 + 2 * U]
    rope_m_refs = refs[7 + 2 * U:7 + 3 * U]
    q_ref, k_ref, v_ref, sq_ref, sk_ref, sv_ref, dqkv_ref, dz_ref, dg_ref = refs[7 + 3 * U:]
    rows = [slice(u * TM, (u + 1) * TM) for u in range(U)]
    hs = [(_rms(x_ref[r, :], g_ref[...]) * (1.0 + m[0, 0, 1:2, :]) + m[0, 0, 0:1, :]).astype(BF16)
          for r, m in zip(rows, mod_refs)]

    def proj(cols):
        return [_dot(h, w_ref[:, cols[0]:cols[1]]) for h in hs]

    for ref, cols in ((dqkv_ref, _C_DQKV), (dz_ref, _C_DZ), (dg_ref, _C_DG)):
        for r, p in zip(rows, proj(cols)):
            ref[r, :] = p
    for r, p in zip(rows, proj(_C_SV)):
        sv_ref[r, :] = p.astype(BF16)
    for r, p, rope_ref in zip(rows, proj(_C_SK), rope_s_refs):
        sk_ref[r, :] = _rope_cols(p, rope_ref, SWA_HEAD_DIM // 2).astype(BF16)
    lane = lax.broadcasted_iota(jnp.int32, (TM, LANES), 1)
    lo = lane < SWA_HEAD_DIM
    for r, sq, rope_ref in zip(rows, proj(_C_SQ), rope_s_refs):
        for c in range(4):
            xr = _rope_cols(sq[:, c * LANES:(c + 1) * LANES], rope_ref, SWA_HEAD_DIM // 2)
            xs = pltpu.roll(xr, SWA_HEAD_DIM, axis=1)
            if c < 2:
                a, b = jnp.where(lo, xr, 0.0), jnp.where(lo, xs, 0.0)
            else:
                a, b = jnp.where(lo, 0.0, xs), jnp.where(lo, 0.0, xr)
            sq_ref[r, (2 * c) * LANES:(2 * c + 1) * LANES] = a.astype(BF16)
            sq_ref[r, (2 * c + 1) * LANES:(2 * c + 2) * LANES] = b.astype(BF16)
    half = MLA_ROPE // 2
    qns = [_rms(p, qg_ref[...]).astype(BF16) for p in proj(_C_CQ)]
    qs = [_dot(qn, wq_ref[...]) for qn in qns]
    kvns = [_rms(p, kvg_ref[...]).astype(BF16) for p in proj(_C_CKV)]
    kvs = [_dot(kvn, wkv_ref[...]) for kvn in kvns]
    krs = [_rope_cols(p, rope_ref, half).astype(BF16)
           for p, rope_ref in zip(proj(_C_KRT), rope_m_refs)]
    for r, q, kv, kr, rope_ref in zip(rows, qs, kvs, krs, rope_m_refs):
        q_ref[r, 0:512] = q[:, 0:512].astype(BF16)
        for c in range(2):
            lo_, hi_ = 512 + c * LANES, 512 + (c + 1) * LANES
            q_ref[r, lo_:hi_] = _rope_cols(q[:, lo_:hi_], rope_ref, half).astype(BF16)
        for c in range(4):
            k_ref[r, (2 * c) * LANES:(2 * c + 1) * LANES] = kv[:, c * LANES:(c + 1) * LANES].astype(BF16)
            k_ref[r, (2 * c + 1) * LANES:(2 * c + 2) * LANES] = kr
        v_ref[r, :] = kv[:, 512:1024].astype(BF16)


def _inproj(x, modtab, g, w, qg, kvg, wq, wkv, rope_s, rope_m, nb_per_batch):
    n = x.shape[0]
    step = TM * IN_UNITS
    assert n % step == 0
    row = lambda i: (i, 0)
    full = lambda i: (0, 0)
    unit = lambda i, u: i * IN_UNITS + u
    widths = [(768, BF16), (1024, BF16), (512, BF16), (1024, BF16), (128, BF16), (128, BF16),
              (1536, F32), (512, F32), (128, F32)]
    mod_specs = [pl.BlockSpec((1, 1, 6, D_MODEL),
                              lambda i, u=u: (unit(i, u) // nb_per_batch,
                                              jnp.minimum(unit(i, u) % nb_per_batch, 1), 0, 0))
                 for u in range(IN_UNITS)]
    rope_specs = [pl.BlockSpec((3, TM, LANES), lambda i, u=u: (0, unit(i, u) % nb_per_batch, 0))
                  for u in range(IN_UNITS)]
    return pl.pallas_call(
        _inproj_body,
        out_shape=[jax.ShapeDtypeStruct((n, wd), dt) for wd, dt in widths],
        grid=(n // step,),
        in_specs=[pl.BlockSpec((step, D_MODEL), row)] + mod_specs
        + [pl.BlockSpec((1, D_MODEL), full), pl.BlockSpec((D_MODEL, _N_PROJ), full),
           pl.BlockSpec((1, MLA_Q_RANK), full), pl.BlockSpec((1, MLA_KV_RANK), full),
           pl.BlockSpec((MLA_Q_RANK, 768), full), pl.BlockSpec((MLA_KV_RANK, 1024), full)]
        + rope_specs + rope_specs,
        out_specs=[pl.BlockSpec((step, wd), row) for wd, _ in widths],
        compiler_params=_cparams(),
        name="inproj",
    )(x, *([modtab] * IN_UNITS), g, w, qg, kvg, wq, wkv, *([rope_s] * IN_UNITS), *([rope_m] * IN_UNITS))


def _mla_attn_body(q_ref, k_ref, v_ref, o_ref, *, t_ctx):
    c1 = (MLA_NOPE + MLA_ROPE) ** -0.5 * LOG2E
    lane = lax.broadcasted_iota(jnp.int32, (TM, LANES), 1)
    zero = jnp.zeros((TM, LANES), BF16)

    def attend(nk):
        for c in range(4):
            kc = k_ref[0:nk, c * 2 * LANES:(c + 1) * 2 * LANES]
            vc = v_ref[0:nk, c * LANES:(c + 1) * LANES]
            qn = q_ref[:, c * LANES:(c + 1) * LANES]
            qr = q_ref[:, 512 + (c // 2) * LANES:512 + (c // 2 + 1) * LANES]
            qa = [jnp.concatenate(
                [jnp.where((lane >= 64 * s) & (lane < 64 * (s + 1)), qn, zero),
                 jnp.where((lane >= 32 * ((2 * c + s) % 4)) & (lane < 32 * ((2 * c + s) % 4 + 1)), qr, zero)],
                axis=1) for s in range(2)]
            sc = [_dot_nt(q, kc) * c1 for q in qa]
            m = [jnp.max(t, axis=-1, keepdims=True) for t in sc]
            e = [jnp.exp2(t - mm) for t, mm in zip(sc, m)]
            l = [jnp.sum(t, axis=-1, keepdims=True) for t in e]
            outs = [_dot(t.astype(BF16), vc) / ll for t, ll in zip(e, l)]
            o_ref[:, c * LANES:(c + 1) * LANES] = jnp.where(lane < 64, outs[0], outs[1]).astype(BF16)

    j = pl.program_id(1)

    @pl.when(j == 0)
    def _():
        attend(t_ctx)

    @pl.when(j > 0)
    def _():
        attend(k_ref.shape[0])


def _mla_attention(q, k, v, n_batch, rows):
    nbp = rows // TM
    return pl.pallas_call(
        functools.partial(_mla_attn_body, t_ctx=TM),
        out_shape=jax.ShapeDtypeStruct((n_batch * rows, BRANCH_W), BF16),
        grid=(n_batch, nbp),
        in_specs=[pl.BlockSpec((TM, 768), lambda b, j: (b * nbp + j, 0)),
                  pl.BlockSpec((rows, 1024), lambda b, j: (b, 0)),
                  pl.BlockSpec((rows, 512), lambda b, j: (b, 0))],
        out_specs=pl.BlockSpec((TM, BRANCH_W), lambda b, j: (b * nbp + j, 0)),
        compiler_params=_cparams(2),
        name="mla_attention",
    )(q, k, v)


SWA_QB = 128
SWA_NQ = 2


def _swa_body(q_ref, k_ref, v_ref, sink_ref, o_ref, *, t_ctx, seq):
    c1 = SWA_HEAD_DIM ** -0.5 * LOG2E
    j = pl.program_id(1)
    n_ctx_steps = t_ctx // (SWA_QB * SWA_NQ)
    lane = lax.broadcasted_iota(jnp.int32, (SWA_QB, LANES), 1)
    lo = lane < SWA_HEAD_DIM
    win = 3 * SWA_QB

    def finish(rows, parts):
        for c in range(4):
            g = c // 2
            r0 = (2 * (c % 2)) * SWA_QB
            a = parts[g][r0:r0 + SWA_QB]
            b = parts[g][r0 + SWA_QB:r0 + 2 * SWA_QB]
            if g == 0:
                col = jnp.where(lo, a, pltpu.roll(b, SWA_HEAD_DIM, axis=1))
            else:
                col = jnp.where(lo, pltpu.roll(a, SWA_HEAD_DIM, axis=1), b)
            o_ref[rows, c * LANES:(c + 1) * LANES] = col.astype(BF16)

    def blocks(blks):
        kc, vc = k_ref[0:t_ctx, :], v_ref[0:t_ctx, :]
        windowed = blks[0] is not None
        kw, vw, band = [], [], []
        for blk in blks if windowed else ():
            start = jnp.clip((blk - 1) * SWA_QB, 0, seq - win)
            rs = pl.multiple_of(t_ctx + start, SWA_QB)
            kw.append(k_ref[pl.ds(rs, win), :])
            vw.append(v_ref[pl.ds(rs, win), :])
            qpos = blk * SWA_QB + lax.broadcasted_iota(jnp.int32, (SWA_QB, win), 0)
            kpos = start + lax.broadcasted_iota(jnp.int32, (SWA_QB, win), 1)
            band1 = jnp.abs(kpos - qpos) <= SWA_WINDOW
            band.append(jnp.concatenate([band1] * 4, axis=0))
        units = [(sub, g) for sub in range(SWA_NQ) for g in range(SWA_KV_HEADS)]
        rows = [slice(sub * SWA_QB, (sub + 1) * SWA_QB) for sub in range(SWA_NQ)]
        qg = [jnp.concatenate([q_ref[rows[sub], (4 * g + i) * LANES:(4 * g + i + 1) * LANES] for i in range(4)],
                              axis=0) for sub, g in units]
        sk = [c1 * jnp.concatenate(
            [jnp.broadcast_to(sink_ref[4 * g + i:4 * g + i + 1, 0:1], (SWA_QB, 1)) for i in range(4)], axis=0)
            for _, g in units]
        t_c = [_dot_nt(q, kc) * c1 for q in qg]
        m = [jnp.maximum(jnp.max(t, axis=-1, keepdims=True), s) for t, s in zip(t_c, sk)]
        if windowed:
            t_w = [jnp.where(band[sub], _dot_nt(q, kw[sub]) * c1, -jnp.inf) for (sub, _), q in zip(units, qg)]
            m = [jnp.maximum(mm, jnp.max(t, axis=-1, keepdims=True)) for mm, t in zip(m, t_w)]
        e_c = [jnp.exp2(t - mm) for t, mm in zip(t_c, m)]
        l = [jnp.sum(e, axis=-1, keepdims=True) + jnp.exp2(s - mm) for e, s, mm in zip(e_c, sk, m)]
        acc = [_dot(e.astype(BF16), vc) for e in e_c]
        if windowed:
            e_w = [jnp.exp2(t - mm) for t, mm in zip(t_w, m)]
            l = [ll + jnp.sum(e, axis=-1, keepdims=True) for ll, e in zip(l, e_w)]
            acc = [a + _dot(e.astype(BF16), vw[sub]) for a, e, (sub, _) in zip(acc, e_w, units)]
        parts = [a / ll for a, ll in zip(acc, l)]
        for sub in range(SWA_NQ):
            finish(rows[sub], parts[SWA_KV_HEADS * sub:SWA_KV_HEADS * (sub + 1)])

    @pl.when(j < n_ctx_steps)
    def _():
        blocks([None] * SWA_NQ)

    @pl.when(j >= n_ctx_steps)
    def _():
        blocks([(j - n_ctx_steps) * SWA_NQ + sub for sub in range(SWA_NQ)])


def _swa_attention(q, k, v, sink, n_batch, t_ctx, seq):
    rows = t_ctx + seq
    qrows = SWA_QB * SWA_NQ
    assert t_ctx % qrows == 0 and seq % qrows == 0
    nqb = rows // qrows
    sink_tab = jnp.broadcast_to(sink.astype(F32)[:, None], (SWA_HEADS, LANES))
    return pl.pallas_call(
        functools.partial(_swa_body, t_ctx=t_ctx, seq=seq),
        out_shape=jax.ShapeDtypeStruct((n_batch * rows, BRANCH_W), BF16),
        grid=(n_batch, nqb),
        in_specs=[pl.BlockSpec((qrows, 1024), lambda b, j: (b * nqb + j, 0)),
                  pl.BlockSpec((rows, LANES), lambda b, j: (b, 0)),
                  pl.BlockSpec((rows, LANES), lambda b, j: (b, 0)),
                  pl.BlockSpec((SWA_HEADS, LANES), lambda b, j: (0, 0))],
        out_specs=pl.BlockSpec((qrows, BRANCH_W), lambda b, j: (b * nqb + j, 0)),
        compiler_params=_cparams(2),
        name="swa_attention",
    )(q, k, v, sink_tab)


MERGE_UNITS = 2


def _merge_body(*refs):
    U = MERGE_UNITS
    row_refs = [refs[6 * u:6 * (u + 1)] for u in range(U)]
    mod_refs = refs[6 * U:7 * U]
    g1_ref, g2_ref, wg_ref, dng_ref, g512_ref, wb_ref, wo_ref, rw_ref, rb_ref = refs[7 * U:7 * U + 9]
    xo_ref, h2_ref, lg_ref = refs[7 * U + 9:]
    rows = [slice(u * TM, (u + 1) * TM) for u in range(U)]
    xs = [r[0][...] for r in row_refs]
    hs = [(_rms(x, g1_ref[...]) * (1.0 + m[0, 0, 1:2, :]) + m[0, 0, 0:1, :]).astype(BF16)
          for x, m in zip(xs, mod_refs)]
    ods = [r[3][...] + r[4][...] for r in row_refs]
    sqs = [_split_bf16(od * od) for od in ods]
    mss = [(_dot(hi, g512_ref[...]) + _dot(lo, g512_ref[...])) * (1.0 / DN_HEAD_DIM) for hi, lo in sqs]
    ocs = [(od * lax.rsqrt(ms + NORM_EPS) * dng_ref[...] * (r[5][...] * jax.nn.sigmoid(r[5][...]))).astype(BF16)
           for od, ms, r in zip(ods, mss, row_refs)]
    ys = [None] * U
    for i in range(N_BRANCH):
        gates = [jax.nn.sigmoid(_dot(h, wg_ref[:, i * D_MODEL:(i + 1) * D_MODEL])) for h in hs]
        branch = [(r[1][...], r[2][...], oc)[i] for r, oc in zip(row_refs, ocs)]
        yis = [g * _dot(o, wb_ref[i]) for g, o in zip(gates, branch)]
        ys = [yi if y is None else y + yi for y, yi in zip(ys, yis)]
    xns = [x + m[0, 0, 2:3, :] * _dot(y.astype(BF16), wo_ref[...]) for x, m, y in zip(xs, mod_refs, ys)]
    h2s = [_rms(xn, g2_ref[...]) * (1.0 + m[0, 0, 4:5, :]) + m[0, 0, 3:4, :] for xn, m in zip(xns, mod_refs)]
    lgs = [_dot3_nt(rw_ref[...], h2) + rb_ref[...] for h2 in h2s]
    for r, xn, h2, lg in zip(rows, xns, h2s, lgs):
        xo_ref[r, :] = xn
        h2_ref[r, :] = h2.astype(BF16)
        lg_ref[:, r] = lg


def _merge(x, modtab, g1, g2, wg, oa, ob, dn_f, dn_b, dz, dn_g, g512, wb, wo, rwt, rb, nb_per_batch, latent_only):
    U = MERGE_UNITS
    if latent_only:
        nlat = nb_per_batch - 1
        n = x.shape[0] // nb_per_batch * nlat
        src = lambda u: (u // nlat) * nb_per_batch + u % nlat + 1
        mod_of = lambda u: (u // nlat, 1, 0, 0)
    else:
        n = x.shape[0]
        src = lambda u: u
        mod_of = lambda u: (u // nb_per_batch, jnp.minimum(u % nb_per_batch, 1), 0, 0)
    step = TM * U
    assert n % step == 0
    orow = lambda i: (i, 0)
    full2 = lambda i: (0, 0)
    row_specs, row_args = [], []
    for u in range(U):
        rowu = lambda i, u=u: (src(i * U + u), 0)
        row_specs += [pl.BlockSpec((TM, D_MODEL), rowu)] + [pl.BlockSpec((TM, BRANCH_W), rowu)] * 5
        row_args += [x, oa, ob, dn_f, dn_b, dz]
    mod_specs = [pl.BlockSpec((1, 1, 6, D_MODEL), lambda i, u=u: mod_of(i * U + u)) for u in range(U)]
    return pl.pallas_call(
        _merge_body,
        out_shape=[jax.ShapeDtypeStruct((n, D_MODEL), F32), jax.ShapeDtypeStruct((n, D_MODEL), BF16),
                   jax.ShapeDtypeStruct((N_EXPERTS, n), F32)],
        grid=(n // step,),
        in_specs=row_specs + mod_specs
        + [pl.BlockSpec((1, D_MODEL), full2), pl.BlockSpec((1, D_MODEL), full2),
           pl.BlockSpec((D_MODEL, N_BRANCH * D_MODEL), full2), pl.BlockSpec((1, BRANCH_W), full2),
           pl.BlockSpec((BRANCH_W, BRANCH_W), full2),
           pl.BlockSpec((N_BRANCH, BRANCH_W, D_MODEL), lambda i: (0, 0, 0)),
           pl.BlockSpec((D_MODEL, D_MODEL), full2),
           pl.BlockSpec((N_EXPERTS, D_MODEL), full2), pl.BlockSpec((N_EXPERTS, 1), full2)],
        out_specs=[pl.BlockSpec((step, D_MODEL), orow), pl.BlockSpec((step, D_MODEL), orow),
                   pl.BlockSpec((N_EXPERTS, step), lambda i: (0, i))],
        compiler_params=_cparams(),
        name="merge",
    )(*row_args, *([modtab] * U), g1, g2, wg, dn_g, g512, wb, wo, rwt, rb)


ROUTE_TB = 1024


def _router_body(lg_ref, tri_ref, e_ref, gate_ref, pos_ref, cnt_ref, run_ref):
    @pl.when(pl.program_id(0) == 0)
    def _():
        run_ref[...] = jnp.zeros_like(run_ref)

    lg = lg_ref[...]
    eid = lax.broadcasted_iota(jnp.int32, lg.shape, 0)
    work = lg
    vals, idxs = [], []
    sel = jnp.zeros(lg.shape, F32)
    for _ in range(TOP_K):
        m = jnp.max(work, axis=0, keepdims=True)
        idx = jnp.min(jnp.where(work == m, eid, N_EXPERTS), axis=0, keepdims=True)
        hit = eid == idx
        sel = jnp.where(hit, 1.0, sel)
        work = jnp.where(hit, -jnp.inf, work)
        vals.append(m)
        idxs.append(idx)
    ex = [jnp.exp(v - vals[0]) for v in vals]
    den = ex[0] + ex[1] + ex[2] + ex[3]
    before = _dot(sel.astype(BF16), tri_ref[...]) + run_ref[:, 0:1]
    for k in range(TOP_K):
        e_ref[k:k + 1, :] = idxs[k]
        gate_ref[k:k + 1, :] = ex[k] / den
        pos_ref[k:k + 1, :] = jnp.sum(jnp.where(eid == idxs[k], before, 0.0), axis=0,
                                      keepdims=True).astype(jnp.int32)
    run_ref[...] = run_ref[...] + jnp.sum(sel, axis=1, keepdims=True)
    cnt_ref[...] = run_ref[...]


def _router(logits_t):
    n = logits_t.shape[1]
    tri = (jnp.arange(ROUTE_TB)[:, None] < jnp.arange(ROUTE_TB)[None, :]).astype(BF16)
    blk = lambda i: (0, i)
    return pl.pallas_call(
        _router_body,
        out_shape=[jax.ShapeDtypeStruct((TOP_K, n), jnp.int32), jax.ShapeDtypeStruct((TOP_K, n), F32),
                   jax.ShapeDtypeStruct((TOP_K, n), jnp.int32), jax.ShapeDtypeStruct((N_EXPERTS, LANES), F32)],
        grid=(n // ROUTE_TB,),
        in_specs=[pl.BlockSpec((N_EXPERTS, ROUTE_TB), blk), pl.BlockSpec((ROUTE_TB, ROUTE_TB), lambda i: (0, 0))],
        out_specs=[pl.BlockSpec((TOP_K, ROUTE_TB), blk), pl.BlockSpec((TOP_K, ROUTE_TB), blk),
                   pl.BlockSpec((TOP_K, ROUTE_TB), blk), pl.BlockSpec((N_EXPERTS, LANES), lambda i: (0, 0))],
        scratch_shapes=[pltpu.VMEM((N_EXPERTS, LANES), F32)],
        compiler_params=_cparams(),
        name="router",
    )(logits_t, tri)


def _expert_body(ib_ref, ie_ref, lo_ref, hi_ref, x_ref, wgu_ref, bgu_ref, wdn_ref, bdn_ref, perm_ref, *rest):
    o_ref, wgu_s, wdn_s = rest[-3:]
    i = pl.program_id(0)
    prev = jnp.maximum(i - 1, 0)

    @pl.when((i == 0) | (ie_ref[i] != ie_ref[prev]))
    def _():
        for m in range(D_EXPERT // LANES):
            t = _dot(wgu_ref[0, 0, :, m * 2 * LANES:(m + 1) * 2 * LANES].astype(BF16), perm_ref[...])
            wgu_s[:, m * LANES:(m + 1) * LANES] = t[:, 0:LANES].astype(BF16)
            wgu_s[:, D_EXPERT + m * LANES:D_EXPERT + (m + 1) * LANES] = t[:, LANES:2 * LANES].astype(BF16)
        wdn_s[...] = wdn_ref[0, 0].astype(BF16)

    lo, hi = lo_ref[i], hi_ref[i]

    @pl.when(hi > lo)
    def _():
        gu = _dot(x_ref[...], wgu_s[...]) + bgu_ref[0]
        g_ = jnp.minimum(gu[:, :D_EXPERT], SWIGLU_LIMIT)
        u_ = jnp.clip(gu[:, D_EXPERT:], -SWIGLU_LIMIT, SWIGLU_LIMIT)
        act = (u_ + 1.0) * (g_ * jax.nn.sigmoid(SWIGLU_ALPHA * g_))
        y = _dot(act.astype(BF16), wdn_s[...]) + bdn_ref[0]
        first = (i == 0) | (ib_ref[i] != ib_ref[prev])
        row = lax.broadcasted_iota(jnp.int32, (MOE_BLK, 1), 0)
        o_ref[...] = jnp.where(first | ((row >= lo) & (row < hi)), y, o_ref[...])


def _experts(items, xs, w_gu, bgu, w_dn, bdn, layer, block0, n_rows, yb_prev):
    n_items = items[0].shape[0]
    perm = np.zeros((2 * LANES, 2 * LANES), np.float32)
    j = np.arange(LANES)
    perm[2 * j, j] = 1.0
    perm[2 * j + 1, LANES + j] = 1.0
    in_specs = [pl.BlockSpec((MOE_BLK, D_MODEL), lambda i, ib, ie, lo, hi: (ib[i] - block0, 0)),
                pl.BlockSpec((1, 1, D_MODEL, 2 * D_EXPERT), lambda i, ib, ie, lo, hi: (layer, ie[i], 0, 0)),
                pl.BlockSpec((1, 1, 2 * D_EXPERT), lambda i, ib, ie, lo, hi: (ie[i], 0, 0)),
                pl.BlockSpec((1, 1, D_EXPERT, D_MODEL), lambda i, ib, ie, lo, hi: (layer, ie[i], 0, 0)),
                pl.BlockSpec((1, 1, D_MODEL), lambda i, ib, ie, lo, hi: (ie[i], 0, 0)),
                pl.BlockSpec((2 * LANES, 2 * LANES), lambda i, ib, ie, lo, hi: (0, 0))]
    args = [*items, xs, w_gu, bgu, w_dn, bdn, jnp.asarray(perm, BF16)]
    aliases = {}
    if yb_prev is not None:
        in_specs.append(pl.BlockSpec(memory_space=pl.ANY))
        aliases = {len(args): 0}
        args.append(yb_prev)
    return pl.pallas_call(
        _expert_body,
        out_shape=jax.ShapeDtypeStruct((n_rows, D_MODEL), F32),
        grid_spec=pltpu.PrefetchScalarGridSpec(
            num_scalar_prefetch=4,
            grid=(n_items,),
            in_specs=in_specs,
            out_specs=pl.BlockSpec((MOE_BLK, D_MODEL), lambda i, ib, ie, lo, hi: (ib[i], 0)),
            scratch_shapes=[pltpu.VMEM((D_MODEL, 2 * D_EXPERT), BF16), pltpu.VMEM((D_EXPERT, D_MODEL), BF16)]),
        input_output_aliases=aliases,
        compiler_params=_cparams(),
        name="experts",
    )(*args)


def _lookup(table, idx):
    ids = jnp.arange(table.shape[0], dtype=jnp.int32)
    return jnp.sum(jnp.where(idx[..., None] == ids, table, 0), axis=-1)


def _work_items(cnt_start, cnt_end, block0, n_blk):
    n_items = n_blk + N_EXPERTS - 1
    b1 = block0 + n_blk
    first = jnp.maximum(cnt_start // MOE_BLK, block0)
    last = jnp.minimum((cnt_end - 1) // MOE_BLK, b1 - 1)
    n_e = jnp.where(cnt_end > cnt_start, jnp.maximum(last - first + 1, 0), 0)
    item_end = jnp.cumsum(n_e)
    item_start = item_end - n_e
    total = item_end[-1]
    ii = jnp.arange(n_items, dtype=jnp.int32)
    valid = ii < total
    e_i = jnp.sum((item_end[None, :] <= jnp.minimum(ii, total - 1)[:, None]).astype(jnp.int32), axis=1)
    blk = jnp.where(valid, _lookup(first, e_i) + ii - _lookup(item_start, e_i), b1 - 1)
    lo = jnp.clip(_lookup(cnt_start, e_i) - blk * MOE_BLK, 0, MOE_BLK)
    hi = jnp.clip(_lookup(cnt_end, e_i) - blk * MOE_BLK, 0, MOE_BLK)
    lo = jnp.where(valid, lo, 0)
    hi = jnp.where(valid, hi, 0)
    return blk.astype(jnp.int32), e_i.astype(jnp.int32), lo.astype(jnp.int32), hi.astype(jnp.int32)


def _moe(h2, logits_t, w_gu, bgu, w_dn, bdn, layer):
    n = h2.shape[0]
    a = n * TOP_K
    assert a % (MOE_BLK * MOE_CHUNKS) == 0
    top_e, gate, pos, cnt = _router(logits_t)
    counts = cnt[:, 0].astype(jnp.int32)
    cnt_end = jnp.cumsum(counts)
    cnt_start = cnt_end - counts
    slot = pos + _lookup(cnt_start, top_e)
    tok = jnp.broadcast_to(jnp.arange(n, dtype=jnp.int32)[None, :], (TOP_K, n))
    _, slot_tok = lax.sort_key_val(slot.reshape(-1), tok.reshape(-1))
    n_blk = a // MOE_BLK // MOE_CHUNKS
    rows_c = n_blk * MOE_BLK
    yb = None
    for c in range(MOE_CHUNKS):
        xs = h2.at[slot_tok[c * rows_c:(c + 1) * rows_c]].get(mode="promise_in_bounds")
        items = _work_items(cnt_start, cnt_end, c * n_blk, n_blk)
        yb = _experts(items, xs, w_gu, bgu, w_dn, bdn, layer, c * n_blk, a, yb)
    return yb, slot, gate.T


DN_GW = 256
HALO = 8
DN_PAIR = 4
_M_EYE, _M_BLK16, _M_OFF32, _M_OFF64, _M_INCL, _M_STRICT = 0, 1, 2, 3, 4, 6


def _dn_masks():
    row = np.arange(DN_CHUNK)[:, None]
    col = np.arange(DN_GW)[None, :] % DN_CHUNK
    b16 = (row // 16) == (col // 16)
    b32 = (row // 32) == (col // 32)
    m = np.stack([row == col, b16, b32 & ~b16, ~b32, col <= row, col >= row, col < row, col > row])
    hid = np.arange(DN_GW) // DN_HEAD_DIM
    return jnp.asarray(m, F32), jnp.asarray(hid[:, None] == hid[None, :], BF16)


def _split3(a):
    hi = a.astype(BF16)
    r = a - hi.astype(F32)
    lo = r.astype(BF16)
    lo2 = (r - lo.astype(F32)).astype(BF16)
    return hi, lo, lo2


def _bd(x, bdm_ref):
    xb = x.astype(BF16)
    return jnp.concatenate([xb, xb, xb, xb], axis=0) * bdm_ref[...]


def _mm(a, wbd):
    return _dot(a.astype(BF16), wbd)


def _tri_inverse(l_mats, m_ref, bdm_ref):
    half = DN_CHUNK
    bd = lambda v: _bd(v, bdm_ref)
    lds = [l * m_ref[_M_BLK16] for l in l_mats]
    ps = [_mm(ld, bd(ld)) for ld in lds]
    xs = [m_ref[_M_EYE] - ld for ld in lds]
    for _ in range(2):
        rs = [_mm(jnp.concatenate([p, x], axis=0), bd(p)) for p, x in zip(ps, xs)]
        ps = [r[0:half] for r in rs]
        xs = [x + r[half:2 * half] for x, r in zip(xs, rs)]
    xs = [x + _mm(x, bd(p)) for x, p in zip(xs, ps)]
    ts = [_mm(x, bd(l * m_ref[_M_OFF32])) for x, l in zip(xs, l_mats)]
    xs = [x - _mm(t, bd(x)) for x, t in zip(xs, ts)]
    ts = [_mm(x, bd(l * m_ref[_M_OFF64])) for x, l in zip(xs, l_mats)]
    return [x - _mm(t, bd(x)) for x, t in zip(xs, ts)]


def _dn_prep_body(cur_ref, prev_ref, next_ref, cw_ref, dg_ref, gc_ref, g512_ref, e_ref, tri_ref, m_ref, bdm_ref,
                  u_ref, w_ref, kd_ref, qd_ref, aq_ref, gl_ref,
                  ext_ref, q_s, k_s, v_s, gx_s, bx_s):
    j = pl.program_id(1)
    nbp = pl.num_programs(1)
    C = DN_CHUNK
    ext_ref[HALO:HALO + TM, :] = cur_ref[...]
    ext_ref[0:HALO, :] = jnp.where(j >= 2, prev_ref[...], 0.0)
    ext_ref[HALO + TM:2 * HALO + TM, :] = jnp.where((j >= 1) & (j < nbp - 1), next_ref[...], 0.0)
    y = None
    for t in range(DN_CONV):
        o = HALO - DN_CONV // 2 + t
        term = cw_ref[t:t + 1, :] * ext_ref[o:o + TM, :]
        y = term if y is None else y + term
    y = y * jax.nn.sigmoid(y)

    g512 = g512_ref[...]

    def head_sumsq(x):
        hi, lo = _split_bf16(x * x)
        return _dot(hi, g512) + _dot(lo, g512)

    q = y[:, 0:512]
    k = y[:, 512:1024]
    q_s[...] = q * lax.rsqrt(head_sumsq(q) + NORM_EPS) * (DN_HEAD_DIM ** -0.5)
    k_s[...] = k * lax.rsqrt(head_sumsq(k) + NORM_EPS)
    v_s[...] = y[:, 1024:1536]

    dg = dg_ref[...]
    beta_all = jax.nn.sigmoid(dg)
    z = dg + gc_ref[1:2, :]
    g_all = gc_ref[0:1, :] * (jnp.maximum(z, 0.0) + jnp.log(1.0 + jnp.exp(-jnp.abs(z))))
    bh, bl = _split_bf16(beta_all)
    bcat = jnp.concatenate([bh, bl], axis=0)
    gparts = jnp.concatenate(_split3(g_all), axis=1)
    for d in range(2):
        cs = _dot(tri_ref[d], gparts)
        gcum = cs[:, 0:128] + cs[:, 128:256] + cs[:, 256:384]
        gcat = jnp.concatenate(_split3(gcum), axis=0)
        for grp in range(2):
            eg = _dot(gcat, e_ref[4 + 2 * d + grp])
            gx_s[d, grp] = eg[0:TM] + eg[TM:2 * TM] + eg[2 * TM:3 * TM]
            eb = _dot(bcat, e_ref[2 * d + grp])
            bx_s[d, grp] = eb[0:TM] + eb[TM:2 * TM]

    def chunk_pair(it, carry):
        units = []
        l_mats = []
        for ci in range(DN_PAIR):
            cc = it * DN_PAIR + ci
            rows = pl.ds(pl.multiple_of(cc * C, C), C)
            for grp in range(2):
                cols = slice(grp * DN_GW, (grp + 1) * DN_GW)
                kg, qg = k_s[rows, cols], q_s[rows, cols]
                kb = [kg * bx_s[d, grp, rows, :] for d in range(2)]
                raw = _dot_nt(jnp.concatenate([kb[0], kb[1], qg], axis=0).astype(BF16), _bd(kg, bdm_ref))
                for d in range(2):
                    gx = gx_s[d, grp, rows, :]
                    rvec = jnp.sum(gx * m_ref[_M_EYE], axis=0, keepdims=True)
                    dec = jnp.exp(jnp.minimum(gx - rvec, 0.0)) * m_ref[_M_INCL + d]
                    l_mats.append(raw[C * d:C * (d + 1)] * dec * m_ref[_M_STRICT + d])
                    glast = gx[C - 1:C, :] if d == 0 else gx[0:1, :]
                    kd_ref[d, rows, cols] = (kg * jnp.exp(glast - gx)).astype(BF16)
                    qd_ref[d, rows, cols] = (qg * jnp.exp(gx)).astype(BF16)
                    aq_ref[d, rows, cols] = (raw[2 * C:3 * C] * dec).astype(BF16)
                    gl_ref[d, pl.ds(cc, 1), :, cols] = jnp.exp(glast).reshape(1, 1, DN_GW)
                    units.append((rows, cols, d, grp))
        tinvs = _tri_inverse(l_mats, m_ref, bdm_ref)
        for (rows, cols, d, grp), tinv in zip(units, tinvs):
            bx = bx_s[d, grp, rows, :]
            u_ref[d, rows, cols] = _mm(tinv, _bd(v_s[rows, cols] * bx, bdm_ref))
            w_ref[d, rows, cols] = _mm(tinv, _bd(k_s[rows, cols] * bx * jnp.exp(gx_s[d, grp, rows, :]), bdm_ref)
                                       ).astype(BF16)
        return carry

    lax.fori_loop(0, TM // C // DN_PAIR, chunk_pair, 0)


def _dn_prep(dqkv, dg, conv_w, a_log, dt_bias, n_batch, rows):
    n = dqkv.shape[0]
    nbp = rows // TM
    hb = TM // HALO
    H = DN_HEADS
    cw = jnp.zeros((8, 1536), F32).at[0:DN_CONV].set(conv_w)
    gc = jnp.zeros((8, LANES), F32)
    gc = gc.at[0, 2 * H:4 * H].set(-jnp.exp(a_log.reshape(-1))).at[1, 2 * H:4 * H].set(dt_bias.reshape(-1))
    hid = np.arange(512) // DN_HEAD_DIM
    g512 = jnp.asarray(hid[:, None] == hid[None, :], BF16)
    e = np.zeros((8, LANES, DN_GW), np.float32)
    for kind in range(2):
        for d in range(2):
            for grp in range(2):
                for h in range(4):
                    e[4 * kind + 2 * d + grp, 16 * kind + 8 * d + 4 * grp + h, 64 * h:64 * (h + 1)] = 1.0
    e = jnp.asarray(e, BF16)
    t = np.arange(TM)
    same = (t[:, None] // DN_CHUNK) == (t[None, :] // DN_CHUNK)
    tri = jnp.asarray(np.stack([same & (t[None, :] <= t[:, None]), same & (t[None, :] >= t[:, None])]), BF16)
    masks, bdm = _dn_masks()
    blk = lambda b, j: (b * nbp + j, 0)
    full2 = lambda b, j: (0, 0)
    full3 = lambda b, j: (0, 0, 0)
    dblk = lambda b, j: (0, b * nbp + j, 0)
    nlast = n // HALO - 1
    outs = pl.pallas_call(
        _dn_prep_body,
        out_shape=[jax.ShapeDtypeStruct((2, n, 512), F32)] + [jax.ShapeDtypeStruct((2, n, 512), BF16)] * 4
        + [jax.ShapeDtypeStruct((2, n // DN_CHUNK, 1, 512), F32)],
        grid=(n_batch, nbp),
        in_specs=[pl.BlockSpec((TM, 1536), blk),
                  pl.BlockSpec((HALO, 1536), lambda b, j: (jnp.maximum((b * nbp + j) * hb - 1, 0), 0)),
                  pl.BlockSpec((HALO, 1536), lambda b, j: (jnp.minimum((b * nbp + j + 1) * hb, nlast), 0)),
                  pl.BlockSpec((8, 1536), full2), pl.BlockSpec((TM, LANES), blk), pl.BlockSpec((8, LANES), full2),
                  pl.BlockSpec((512, 512), full2), pl.BlockSpec((8, LANES, DN_GW), full3),
                  pl.BlockSpec((2, TM, TM), full3), pl.BlockSpec((8, DN_CHUNK, DN_GW), full3),
                  pl.BlockSpec((DN_GW, DN_GW), full2)],
        out_specs=[pl.BlockSpec((2, TM, 512), dblk)] * 5
        + [pl.BlockSpec((2, TM // DN_CHUNK, 1, 512), lambda b, j: (0, b * nbp + j, 0, 0))],
        scratch_shapes=[pltpu.VMEM((TM + 2 * HALO, 1536), F32), pltpu.VMEM((TM, 512), F32),
                        pltpu.VMEM((TM, 512), F32), pltpu.VMEM((TM, 512), F32),
                        pltpu.VMEM((2, 2, TM, DN_GW), F32), pltpu.VMEM((2, 2, TM, DN_GW), F32)],
        compiler_params=_cparams(2),
        name="dn_prep",
    )(dqkv, dqkv, dqkv, cw, dg, gc, g512, e, tri, masks, bdm)
    return outs, g512


DN_SCAN_NB = 8


def _dn_scan_body(uf, wf, kf, qf, af, gf, ub, wb, kb, qb, ab, gb, bdm_ref, of_ref, ob_ref, s_ref):
    @pl.when(pl.program_id(1) == 0)
    def _():
        s_ref[...] = jnp.zeros_like(s_ref)

    same = bdm_ref[...] != 0
    dirs = ((uf, wf, kf, qf, af, gf, of_ref), (ub, wb, kb, qb, ab, gb, ob_ref))
    units = [(bb, d, grp) for bb in range(DN_SCAN_NB) for d in range(2) for grp in range(2)]
    cols = [slice(grp * DN_GW, (grp + 1) * DN_GW) for grp in range(2)]
    s = [s_ref[bb, d, grp] for bb, d, grp in units]
    sq = [_dot(jnp.concatenate([dirs[d][1][0, bb, :, cols[grp]], dirs[d][3][0, bb, :, cols[grp]]], axis=0),
               st.astype(BF16)) for (bb, d, grp), st in zip(units, s)]
    vnew = [dirs[d][0][0, bb, :, cols[grp]] - r[0:DN_CHUNK] for (bb, d, grp), r in zip(units, sq)]
    intra = [_dot(dirs[d][4][0, bb, :, cols[grp]], _bd(v, bdm_ref)) for (bb, d, grp), v in zip(units, vnew)]
    upd = [lax.dot_general(dirs[d][2][0, bb, :, cols[grp]], v.astype(BF16), (((0,), (0,)), ((), ())),
                           preferred_element_type=F32) for (bb, d, grp), v in zip(units, vnew)]
    for (bb, d, grp), st, r, a, up in zip(units, s, sq, intra, upd):
        dirs[d][6][bb, :, cols[grp]] = r[DN_CHUNK:2 * DN_CHUNK] + a
        s_ref[bb, d, grp] = st * dirs[d][5][0, bb, 0, :, cols[grp]] + jnp.where(same, up, 0.0)


def _dn_scan(u, w, kd, qd, aq, gl, n_batch, rows, t_ctx):
    nch = rows // DN_CHUNK
    nctx = t_ctx // DN_CHUNK
    assert n_batch % DN_SCAN_NB == 0

    def cb(c):
        return jnp.where(c < nctx, nctx - 1 - c, nch - 1 - (c - nctx))

    per_batch = lambda a: a.reshape(2, n_batch, rows, 512)
    u, w, kd, qd, aq = map(per_batch, (u, w, kd, qd, aq))
    gl = gl.reshape(2, n_batch, nch, 1, 512)
    blk = (1, DN_SCAN_NB, DN_CHUNK, 512)
    gblk = (1, DN_SCAN_NB, 1, 1, 512)
    specs = []
    for d, ch in ((0, lambda c: c), (1, cb)):
        specs += [pl.BlockSpec(blk, lambda b, c, d=d, ch=ch: (d, b, ch(c), 0))] * 5
        specs += [pl.BlockSpec(gblk, lambda b, c, d=d, ch=ch: (d, b, ch(c), 0, 0))]
    specs.append(pl.BlockSpec((DN_GW, DN_GW), lambda b, c: (0, 0)))
    o_f, o_b = pl.pallas_call(
        _dn_scan_body,
        out_shape=[jax.ShapeDtypeStruct((n_batch, rows, 512), F32)] * 2,
        grid=(n_batch // DN_SCAN_NB, nch),
        in_specs=specs,
        out_specs=[pl.BlockSpec((DN_SCAN_NB, DN_CHUNK, 512), lambda b, c: (b, c, 0)),
                   pl.BlockSpec((DN_SCAN_NB, DN_CHUNK, 512), lambda b, c: (b, cb(c), 0))],
        scratch_shapes=[pltpu.VMEM((DN_SCAN_NB, 2, 2, DN_GW, DN_GW), F32)],
        compiler_params=_cparams(2),
        name="dn_scan",
    )(u, w, kd, qd, aq, gl, u, w, kd, qd, aq, gl, _dn_masks()[1])
    return o_f.reshape(n_batch * rows, 512), o_b.reshape(n_batch * rows, 512)


def _deltanet(dqkv, dg, conv_w, a_log, dt_bias, n_batch, rows, t_ctx):
    (u, w, kd, qd, aq, gl), g512 = _dn_prep(dqkv, dg, conv_w, a_log, dt_bias, n_batch, rows)
    o_f, o_b = _dn_scan(u, w, kd, qd, aq, gl, n_batch, rows, t_ctx)
    return o_f, o_b, g512


def _combine_body(x_ref, y0_ref, y1_ref, y2_ref, y3_ref, gate_ref, mod_ref, fg_ref, *rest, final):
    o_ref = rest[-1]
    gate = gate_ref[...]
    y = None
    for k, y_ref in enumerate((y0_ref, y1_ref, y2_ref, y3_ref)):
        yk = y_ref[...] * gate[:, k:k + 1]
        y = yk if y is None else y + yk
    xo = x_ref[...] + mod_ref[0, 0, 5:6, :] * y
    o_ref[...] = _rms(xo, fg_ref[...]) if final else xo


def _combine(x, ys, gate_rows, modtab, final_g, nb_per_batch, latent_only, block0, prev):
    n = x.shape[0]
    if latent_only:
        nlat = nb_per_batch - 1
        mod_idx = lambda i: ((i + block0) // nlat, 1, 0, 0)
    else:
        mod_idx = lambda i: ((i + block0) // nb_per_batch, jnp.minimum((i + block0) % nb_per_batch, 1), 0, 0)
    row = lambda i: (i + block0, 0)
    loc = lambda i: (i, 0)
    in_specs = ([pl.BlockSpec((TM, D_MODEL), row)] + [pl.BlockSpec((TM, D_MODEL), loc)] * TOP_K
                + [pl.BlockSpec((TM, TOP_K), row), pl.BlockSpec((1, 1, 6, D_MODEL), mod_idx),
                   pl.BlockSpec((1, D_MODEL), lambda i: (0, 0))])
    args = [x, *ys, gate_rows, modtab, final_g]
    aliases = {}
    if prev is not None:
        in_specs.append(pl.BlockSpec(memory_space=pl.ANY))
        aliases = {len(args): 0}
        args.append(prev)
    return pl.pallas_call(
        functools.partial(_combine_body, final=latent_only),
        out_shape=jax.ShapeDtypeStruct((n, D_MODEL), F32),
        grid=(ys[0].shape[0] // TM,),
        in_specs=in_specs,
        out_specs=pl.BlockSpec((TM, D_MODEL), row),
        input_output_aliases=aliases,
        compiler_params=_cparams(),
        name="combine",
    )(*args)


def _rope_tables(seq, t_ctx, rot_dim):
    rows = seq // GRID_W
    row = jnp.broadcast_to(jnp.arange(rows)[:, None], (rows, GRID_W)).reshape(-1).astype(F32)
    col = jnp.broadcast_to(jnp.arange(GRID_W)[None, :], (rows, GRID_W)).reshape(-1).astype(F32)
    n_freq = rot_dim // 4
    inv_freq = ROPE_THETA ** (-jnp.arange(n_freq, dtype=F32) / n_freq)
    ang = jnp.concatenate([row[:, None] * inv_freq, col[:, None] * inv_freq], axis=-1)
    cos, sin = jnp.cos(ang), jnp.sin(ang)
    zero = jnp.zeros_like(sin)
    c = jnp.concatenate([cos, cos], axis=-1)
    sa = jnp.concatenate([-sin, zero], axis=-1)
    sb = jnp.concatenate([zero, sin], axis=-1)
    ident = jnp.stack([jnp.ones((t_ctx, rot_dim), F32), jnp.zeros((t_ctx, rot_dim), F32),
                       jnp.zeros((t_ctx, rot_dim), F32)])
    tab = jnp.concatenate([ident, jnp.stack([c, sa, sb])], axis=1)
    return jnp.tile(tab, (1, 1, LANES // rot_dim))


def _layer_weights(w_in, w_q_up, w_kv_up, w_branch, w_out, router_w, router_b, b_gu, b_dn):
    sizes = (MLA_Q_RANK, MLA_KV_RANK, MLA_ROPE, 512, 128, 128, 1536, 512, 16, 16, N_BRANCH * D_MODEL)
    offs = np.cumsum((0,) + sizes)
    part = lambda i: w_in[:, offs[i]:offs[i + 1]]
    pad = jnp.zeros((D_MODEL, LANES - 32), F32)
    w_proj = jnp.concatenate(
        [part(0), part(1), part(3), part(4), part(5), part(6), part(7),
         jnp.tile(part(2), (1, LANES // MLA_ROPE)), part(8), part(9), pad], axis=1).astype(BF16)
    w_gate = part(10).astype(BF16)
    hq = MLA_NOPE + MLA_ROPE
    wq = w_q_up.reshape(MLA_Q_RANK, MLA_HEADS, hq)
    wq = jnp.concatenate([wq[:, :, :MLA_NOPE].reshape(MLA_Q_RANK, -1),
                          wq[:, :, MLA_NOPE:].reshape(MLA_Q_RANK, -1)], axis=1).astype(BF16)
    wkv = w_kv_up.reshape(MLA_KV_RANK, MLA_HEADS, MLA_NOPE + MLA_V)
    wkv = jnp.concatenate([wkv[:, :, :MLA_NOPE].reshape(MLA_KV_RANK, -1),
                           wkv[:, :, MLA_NOPE:].reshape(MLA_KV_RANK, -1)], axis=1).astype(BF16)
    bgu = jnp.concatenate([b_gu[:, 0::2], b_gu[:, 1::2]], axis=-1).reshape(N_EXPERTS, 1, 2 * D_EXPERT)
    return dict(w_proj=w_proj, w_gate=w_gate, wq=wq, wkv=wkv, wb=w_branch.astype(BF16),
                wo=w_out.astype(BF16), rwt=router_w.T, rb=router_b.reshape(N_EXPERTS, 1),
                bgu=bgu, bdn=b_dn.reshape(N_EXPERTS, 1, D_MODEL))


def kernel(x, c, ctx, c_ctx, w_mod, b_mod, norm1_g, norm2_g, w_in, mla_q_norm_g, mla_w_q_up,
           mla_kv_norm_g, mla_w_kv_up, swa_sink, dn_conv_w, dn_a_log, dn_dt_bias, dn_norm_g,
           w_branch, w_out, router_w, router_b, exp_w_gu, exp_b_gu, exp_w_dn, exp_b_dn, final_norm_g):
    B, S, D = x.shape
    T = ctx.shape[1]
    R = T + S
    depth = w_mod.shape[0]
    assert D == D_MODEL and T == TM and S % TM == 0 and (B * R) % ROUTE_TB == 0
    nbp = R // TM
    xa = jnp.concatenate([ctx, x], axis=1).reshape(B * R, D)
    rope_s = _rope_tables(S, T, SWA_HEAD_DIM)
    rope_m = _rope_tables(S, T, MLA_ROPE)
    cvec = jnp.concatenate([c, c_ctx[None, :]], axis=0)
    for l in range(depth):
        w = _layer_weights(w_in[l], mla_w_q_up[l], mla_w_kv_up[l], w_branch[l], w_out[l], router_w[l],
                           router_b[l], exp_b_gu[l], exp_b_dn[l])
        mod = _modulation(cvec, w_mod[l], b_mod[l]).reshape(B + 1, 6, D)
        modtab = jnp.stack([jnp.broadcast_to(mod[B][None], (B, 6, D)), mod[:B]], axis=1)
        g1 = norm1_g[l].reshape(1, D)
        g2 = norm2_g[l].reshape(1, D)
        q_m, k_m, v_m, sq, sk, sv, dqkv, dz, dg = _inproj(
            xa, modtab, g1, w["w_proj"], mla_q_norm_g[l].reshape(1, -1), mla_kv_norm_g[l].reshape(1, -1),
            w["wq"], w["wkv"], rope_s, rope_m, nbp)
        o_a = _mla_attention(q_m, k_m, v_m, B, R)
        o_b = _swa_attention(sq, sk, sv, swa_sink[l], B, T, S)
        dn_f, dn_b, g512 = _deltanet(dqkv, dg, dn_conv_w[l], dn_a_log[l], dn_dt_bias[l], B, R, T)
        last = l == depth - 1
        xn, h2, logits_t = _merge(xa, modtab, g1, g2, w["w_gate"], o_a, o_b, dn_f, dn_b, dz,
                                  jnp.tile(dn_norm_g[l], DN_HEADS).reshape(1, BRANCH_W), g512, w["wb"], w["wo"],
                                  w["rwt"], w["rb"], nbp, last)
        yb, slot, gate_rows = _moe(h2, logits_t, exp_w_gu, w["bgu"], exp_w_dn, w["bdn"], l)
        n_tok = xn.shape[0]
        rows_c = n_tok // COMBINE_CHUNKS
        assert rows_c % TM == 0
        xa = None
        for cc in range(COMBINE_CHUNKS):
            ys = [yb.at[slot[k, cc * rows_c:(cc + 1) * rows_c]].get(mode="promise_in_bounds")
                  for k in range(TOP_K)]
            xa = _combine(xn, ys, gate_rows, modtab, final_norm_g.reshape(1, D), nbp, last,
                          cc * rows_c // TM, xa)
    return xa.reshape(B, S, D)
```

```python
import functools

import jax
import jax.numpy as jnp
import numpy as np
from jax import lax
from jax.experimental import pallas as pl
from jax.experimental.pallas import tpu as pltpu

F32 = jnp.float32
BF16 = jnp.bfloat16

D_MODEL = 1024
GRID_W = 64
ROPE_THETA = 10000.0
NORM_EPS = 1e-6
MLA_HEADS = 8
MLA_Q_RANK = 384
MLA_KV_RANK = 256
MLA_NOPE = 64
MLA_ROPE = 32
MLA_V = 64
SWA_HEADS = 8
SWA_KV_HEADS = 2
SWA_HEAD_DIM = 64
SWA_WINDOW = 128
DN_HEADS = 8
DN_HEAD_DIM = 64
DN_CONV = 5
DN_CHUNK = 64
N_BRANCH = 3
BRANCH_W = 512
N_EXPERTS = 32
TOP_K = 4
D_EXPERT = 1024
SWIGLU_LIMIT = 7.0
SWIGLU_ALPHA = 1.702
MOE_BLK = 512
MOE_CHUNKS = 4

LANES = 128
LOG2E = 1.4426950408889634
TM = 256
VMEM_LIMIT = 56 * 1024 * 1024

_C_CQ = (0, 384)
_C_CKV = (384, 640)
_C_SQ = (640, 1152)
_C_SK = (1152, 1280)
_C_SV = (1280, 1408)
_C_DQKV = (1408, 2944)
_C_DZ = (2944, 3456)
_C_KRT = (3456, 3584)
_C_DG = (3584, 3712)
_N_PROJ = 3712


def _cparams(n_axes=1):
    return pltpu.CompilerParams(dimension_semantics=("arbitrary",) * n_axes, vmem_limit_bytes=VMEM_LIMIT)


def _dot(a, b):
    return jnp.dot(a, b, preferred_element_type=F32)


def _dot_nt(a, b):
    return lax.dot_general(a, b, (((1,), (1,)), ((), ())), preferred_element_type=F32)


def _split_bf16(a):
    hi = a.astype(BF16)
    lo = (a - hi.astype(F32)).astype(BF16)
    return hi, lo


def _dot3(a, b):
    ah, al = _split_bf16(a)
    bh, bl = _split_bf16(b)
    return _dot(ah, bh) + (_dot(ah, bl) + _dot(al, bh))


def _dot3_nt(a, b):
    ah, al = _split_bf16(a)
    bh, bl = _split_bf16(b)
    return _dot_nt(ah, bh) + (_dot_nt(ah, bl) + _dot_nt(al, bh))


def _rms(x, g):
    return x * lax.rsqrt(jnp.mean(x * x, axis=-1, keepdims=True) + NORM_EPS) * g


def _rope_cols(x, tab_ref, half):
    c, sa, sb = tab_ref[0], tab_ref[1], tab_ref[2]
    return x * c + pltpu.roll(x, LANES - half, axis=1) * sa + pltpu.roll(x, half, axis=1) * sb


def _mod_body(c_ref, w_ref, b_ref, o_ref):
    c = c_ref[...]
    a = c * jax.nn.sigmoid(c)
    o_ref[...] = _dot3(a, w_ref[...]) + b_ref[...]


def _modulation(cvec, w_mod, b_mod):
    m = cvec.shape[0]
    n = w_mod.shape[1]
    tn = 512
    return pl.pallas_call(
        _mod_body,
        out_shape=jax.ShapeDtypeStruct((m, n), F32),
        grid=(n // tn,),
        in_specs=[pl.BlockSpec((m, D_MODEL), lambda j: (0, 0)),
                  pl.BlockSpec((D_MODEL, tn), lambda j: (0, j)),
                  pl.BlockSpec((1, tn), lambda j: (0, j))],
        out_specs=pl.BlockSpec((m, tn), lambda j: (0, j)),
        compiler_params=_cparams(),
        name="modulation",
    )(cvec, w_mod, b_mod.reshape(1, n))


IN_UNITS = 2


def _inproj_body(*refs):
    U = IN_UNITS
    x_ref = refs[0]
    mod_refs = refs[1:1 + U]
    g_ref, w_ref, qg_ref, kvg_ref, wq_ref, wkv_ref = refs[1 + U:7 + U]
    rope_s_refs = refs[7 + U:7 + 2 * U]
    rope_m_refs = refs[7 + 2 * U:7 + 3 * U]
    q_ref, k_ref, v_ref, sq_ref, sk_ref, sv_ref, dqkv_ref, dz_ref, dg_ref = refs[7 + 3 * U:]
    rows = [slice(u * TM, (u + 1) * TM) for u in range(U)]
    hs = [(_rms(x_ref[r, :], g_ref[...]) * (1.0 + m[0, 0, 1:2, :]) + m[0, 0, 0:1, :]).astype(BF16)
          for r, m in zip(rows, mod_refs)]

    def proj(cols):
        return [_dot(h, w_ref[:, cols[0]:cols[1]]) for h in hs]

    for ref, cols in ((dqkv_ref, _C_DQKV), (dz_ref, _C_DZ), (dg_ref, _C_DG)):
        for r, p in zip(rows, proj(cols)):
            ref[r, :] = p
    for r, p in zip(rows, proj(_C_SV)):
        sv_ref[r, :] = p.astype(BF16)
    for r, p, rope_ref in zip(rows, proj(_C_SK), rope_s_refs):
        sk_ref[r, :] = _rope_cols(p, rope_ref, SWA_HEAD_DIM // 2).astype(BF16)
    lane = lax.broadcasted_iota(jnp.int32, (TM, LANES), 1)
    lo = lane < SWA_HEAD_DIM
    for r, sq, rope_ref in zip(rows, proj(_C_SQ), rope_s_refs):
        for c in range(4):
            xr = _rope_cols(sq[:, c * LANES:(c + 1) * LANES], rope_ref, SWA_HEAD_DIM // 2)
            xs = pltpu.roll(xr, SWA_HEAD_DIM, axis=1)
            if c < 2:
                a, b = jnp.where(lo, xr, 0.0), jnp.where(lo, xs, 0.0)
            else:
                a, b = jnp.where(lo, 0.0, xs), jnp.where(lo, 0.0, xr)
            sq_ref[r, (2 * c) * LANES:(2 * c + 1) * LANES] = a.astype(BF16)
            sq_ref[r, (2 * c + 1) * LANES:(2 * c + 2) * LANES] = b.astype(BF16)
    half = MLA_ROPE // 2
    qns = [_rms(p, qg_ref[...]).astype(BF16) for p in proj(_C_CQ)]
    qs = [_dot(qn, wq_ref[...]) for qn in qns]
    kvns = [_rms(p, kvg_ref[...]).astype(BF16) for p in proj(_C_CKV)]
    kvs = [_dot(kvn, wkv_ref[...]) for kvn in kvns]
    krs = [_rope_cols(p, rope_ref, half).astype(BF16)
           for p, rope_ref in zip(proj(_C_KRT), rope_m_refs)]
    for r, q, kv, kr, rope_ref in zip(rows, qs, kvs, krs, rope_m_refs):
        q_ref[r, 0:512] = q[:, 0:512].astype(BF16)
        for c in range(2):
            lo_, hi_ = 512 + c * LANES, 512 + (c + 1) * LANES
            q_ref[r, lo_:hi_] = _rope_cols(q[:, lo_:hi_], rope_ref, half).astype(BF16)
        for c in range(4):
            k_ref[r, (2 * c) * LANES:(2 * c + 1) * LANES] = kv[:, c * LANES:(c + 1) * LANES].astype(BF16)
            k_ref[r, (2 * c + 1) * LANES:(2 * c + 2) * LANES] = kr
        v_ref[r, :] = kv[:, 512:1024].astype(BF16)


def _inproj(x, modtab, g, w, qg, kvg, wq, wkv, rope_s, rope_m, nb_per_batch):
    n = x.shape[0]
    step = TM * IN_UNITS
    assert n % step == 0
    row = lambda i: (i, 0)
    full = lambda i: (0, 0)
    unit = lambda i, u: i * IN_UNITS + u
    widths = [(768, BF16), (1024, BF16), (512, BF16), (1024, BF16), (128, BF16), (128, BF16),
              (1536, F32), (512, F32), (128, F32)]
    mod_specs = [pl.BlockSpec((1, 1, 6, D_MODEL),
                              lambda i, u=u: (unit(i, u) // nb_per_batch,
                                              jnp.minimum(unit(i, u) % nb_per_batch, 1), 0, 0))
                 for u in range(IN_UNITS)]
    rope_specs = [pl.BlockSpec((3, TM, LANES), lambda i, u=u: (0, unit(i, u) % nb_per_batch, 0))
                  for u in range(IN_UNITS)]
    return pl.pallas_call(
        _inproj_body,
        out_shape=[jax.ShapeDtypeStruct((n, wd), dt) for wd, dt in widths],
        grid=(n // step,),
        in_specs=[pl.BlockSpec((step, D_MODEL), row)] + mod_specs
        + [pl.BlockSpec((1, D_MODEL), full), pl.BlockSpec((D_MODEL, _N_PROJ), full),
           pl.BlockSpec((1, MLA_Q_RANK), full), pl.BlockSpec((1, MLA_KV_RANK), full),
           pl.BlockSpec((MLA_Q_RANK, 768), full), pl.BlockSpec((MLA_KV_RANK, 1024), full)]
        + rope_specs + rope_specs,
        out_specs=[pl.BlockSpec((step, wd), row) for wd, _ in widths],
        compiler_params=_cparams(),
        name="inproj",
    )(x, *([modtab] * IN_UNITS), g, w, qg, kvg, wq, wkv, *([rope_s] * IN_UNITS), *([rope_m] * IN_UNITS))


def _mla_attn_body(q_ref, k_ref, v_ref, o_ref, *, t_ctx):
    c1 = (MLA_NOPE + MLA_ROPE) ** -0.5 * LOG2E
    lane = lax.broadcasted_iota(jnp.int32, (TM, LANES), 1)
    zero = jnp.zeros((TM, LANES), BF16)

    def attend(nk):
        for c in range(4):
            kc = k_ref[0:nk, c * 2 * LANES:(c + 1) * 2 * LANES]
            vc = v_ref[0:nk, c * LANES:(c + 1) * LANES]
            qn = q_ref[:, c * LANES:(c + 1) * LANES]
            qr = q_ref[:, 512 + (c // 2) * LANES:512 + (c // 2 + 1) * LANES]
            qa = [jnp.concatenate(
                [jnp.where((lane >= 64 * s) & (lane < 64 * (s + 1)), qn, zero),
                 jnp.where((lane >= 32 * ((2 * c + s) % 4)) & (lane < 32 * ((2 * c + s) % 4 + 1)), qr, zero)],
                axis=1) for s in range(2)]
            sc = [_dot_nt(q, kc) * c1 for q in qa]
            m = [jnp.max(t, axis=-1, keepdims=True) for t in sc]
            e = [jnp.exp2(t - mm) for t, mm in zip(sc, m)]
            l = [jnp.sum(t, axis=-1, keepdims=True) for t in e]
            outs = [_dot(t.astype(BF16), vc) / ll for t, ll in zip(e, l)]
            o_ref[:, c * LANES:(c + 1) * LANES] = jnp.where(lane < 64, outs[0], outs[1]).astype(BF16)

    j = pl.program_id(1)

    @pl.when(j == 0)
    def _():
        attend(t_ctx)

    @pl.when(j > 0)
    def _():
        attend(k_ref.shape[0])


def _mla_attention(q, k, v, n_batch, rows):
    nbp = rows // TM
    return pl.pallas_call(
        functools.partial(_mla_attn_body, t_ctx=TM),
        out_shape=jax.ShapeDtypeStruct((n_batch * rows, BRANCH_W), BF16),
        grid=(n_batch, nbp),
        in_specs=[pl.BlockSpec((TM, 768), lambda b, j: (b * nbp + j, 0)),
                  pl.BlockSpec((rows, 1024), lambda b, j: (b, 0)),
                  pl.BlockSpec((rows, 512), lambda b, j: (b, 0))],
        out_specs=pl.BlockSpec((TM, BRANCH_W), lambda b, j: (b * nbp + j, 0)),
        compiler_params=_cparams(2),
        name="mla_attention",
    )(q, k, v)


SWA_QB = 128
SWA_NQ = 2


def _swa_body(q_ref, k_ref, v_ref, sink_ref, o_ref, *, t_ctx, seq):
    c1 = SWA_HEAD_DIM ** -0.5 * LOG2E
    j = pl.program_id(1)
    n_ctx_steps = t_ctx // (SWA_QB * SWA_NQ)
    lane = lax.broadcasted_iota(jnp.int32, (SWA_QB, LANES), 1)
    lo = lane < SWA_HEAD_DIM
    win = 3 * SWA_QB

    def finish(rows, parts):
        for c in range(4):
            g = c // 2
            r0 = (2 * (c % 2)) * SWA_QB
            a = parts[g][r0:r0 + SWA_QB]
            b = parts[g][r0 + SWA_QB:r0 + 2 * SWA_QB]
            if g == 0:
                col = jnp.where(lo, a, pltpu.roll(b, SWA_HEAD_DIM, axis=1))
            else:
                col = jnp.where(lo, pltpu.roll(a, SWA_HEAD_DIM, axis=1), b)
            o_ref[rows, c * LANES:(c + 1) * LANES] = col.astype(BF16)

    def blocks(blks):
        kc, vc = k_ref[0:t_ctx, :], v_ref[0:t_ctx, :]
        windowed = blks[0] is not None
        kw, vw, band = [], [], []
        for blk in blks if windowed else ():
            start = jnp.clip((blk - 1) * SWA_QB, 0, seq - win)
            rs = pl.multiple_of(t_ctx + start, SWA_QB)
            kw.append(k_ref[pl.ds(rs, win), :])
            vw.append(v_ref[pl.ds(rs, win), :])
            qpos = blk * SWA_QB + lax.broadcasted_iota(jnp.int32, (SWA_QB, win), 0)
            kpos = start + lax.broadcasted_iota(jnp.int32, (SWA_QB, win), 1)
            band1 = jnp.abs(kpos - qpos) <= SWA_WINDOW
            band.append(jnp.concatenate([band1] * 4, axis=0))
        units = [(sub, g) for sub in range(SWA_NQ) for g in range(SWA_KV_HEADS)]
        rows = [slice(sub * SWA_QB, (sub + 1) * SWA_QB) for sub in range(SWA_NQ)]
        qg = [jnp.concatenate([q_ref[rows[sub], (4 * g + i) * LANES:(4 * g + i + 1) * LANES] for i in range(4)],
                              axis=0) for sub, g in units]
        sk = [c1 * jnp.concatenate(
            [jnp.broadcast_to(sink_ref[4 * g + i:4 * g + i + 1, 0:1], (SWA_QB, 1)) for i in range(4)], axis=0)
            for _, g in units]
        t_c = [_dot_nt(q, kc) * c1 for q in qg]
        m = [jnp.maximum(jnp.max(t, axis=-1, keepdims=True), s) for t, s in zip(t_c, sk)]
        if windowed:
            t_w = [jnp.where(band[sub], _dot_nt(q, kw[sub]) * c1, -jnp.inf) for (sub, _), q in zip(units, qg)]
            m = [jnp.maximum(mm, jnp.max(t, axis=-1, keepdims=True)) for mm, t in zip(m, t_w)]
        e_c = [jnp.exp2(t - mm) for t, mm in zip(t_c, m)]
        l = [jnp.sum(e, axis=-1, keepdims=True) + jnp.exp2(s - mm) for e, s, mm in zip(e_c, sk, m)]
        acc = [_dot(e.astype(BF16), vc) for e in e_c]
        if windowed:
            e_w = [jnp.exp2(t - mm) for t, mm in zip(t_w, m)]
            l = [ll + jnp.sum(e, axis=-1, keepdims=True) for ll, e in zip(l, e_w)]
            acc = [a + _dot(e.astype(BF16), vw[sub]) for a, e, (sub, _) in zip(acc, e_w, units)]
        parts = [a / ll for a, ll in zip(acc, l)]
        for sub in range(SWA_NQ):
            finish(rows[sub], parts[SWA_KV_HEADS * sub:SWA_KV_HEADS * (sub + 1)])

    @pl.when(j < n_ctx_steps)
    def _():
        blocks([None] * SWA_NQ)

    @pl.when(j >= n_ctx_steps)
    def _():
        blocks([(j - n_ctx_steps) * SWA_NQ + sub for sub in range(SWA_NQ)])


def _swa_attention(q, k, v, sink, n_batch, t_ctx, seq):
    rows = t_ctx + seq
    qrows = SWA_QB * SWA_NQ
    assert t_ctx % qrows == 0 and seq % qrows == 0
    nqb = rows // qrows
    sink_tab = jnp.broadcast_to(sink.astype(F32)[:, None], (SWA_HEADS, LANES))
    return pl.pallas_call(
        functools.partial(_swa_body, t_ctx=t_ctx, seq=seq),
        out_shape=jax.ShapeDtypeStruct((n_batch * rows, BRANCH_W), BF16),
        grid=(n_batch, nqb),
        in_specs=[pl.BlockSpec((qrows, 1024), lambda b, j: (b * nqb + j, 0)),
                  pl.BlockSpec((rows, LANES), lambda b, j: (b, 0)),
                  pl.BlockSpec((rows, LANES), lambda b, j: (b, 0)),
                  pl.BlockSpec((SWA_HEADS, LANES), lambda b, j: (0, 0))],
        out_specs=pl.BlockSpec((qrows, BRANCH_W), lambda b, j: (b * nqb + j, 0)),
        compiler_params=_cparams(2),
        name="swa_attention",
    )(q, k, v, sink_tab)


MERGE_UNITS = 2


def _merge_body(*refs):
    U = MERGE_UNITS
    row_refs = [refs[6 * u:6 * (u + 1)] for u in range(U)]
    mod_refs = refs[6 * U:7 * U]
    g1_ref, g2_ref, wg_ref, dng_ref, g512_ref, wb_ref, wo_ref, rw_ref, rb_ref = refs[7 * U:7 * U + 9]
    xo_ref, h2_ref, lg_ref = refs[7 * U + 9:]
    rows = [slice(u * TM, (u + 1) * TM) for u in range(U)]
    xs = [r[0][...] for r in row_refs]
    hs = [(_rms(x, g1_ref[...]) * (1.0 + m[0, 0, 1:2, :]) + m[0, 0, 0:1, :]).astype(BF16)
          for x, m in zip(xs, mod_refs)]
    ods = [r[3][...] + r[4][...] for r in row_refs]
    sqs = [_split_bf16(od * od) for od in ods]
    mss = [(_dot(hi, g512_ref[...]) + _dot(lo, g512_ref[...])) * (1.0 / DN_HEAD_DIM) for hi, lo in sqs]
    ocs = [(od * lax.rsqrt(ms + NORM_EPS) * dng_ref[...] * (r[5][...] * jax.nn.sigmoid(r[5][...]))).astype(BF16)
           for od, ms, r in zip(ods, mss, row_refs)]
    ys = [None] * U
    for i in range(N_BRANCH):
        gates = [jax.nn.sigmoid(_dot(h, wg_ref[:, i * D_MODEL:(i + 1) * D_MODEL])) for h in hs]
        branch = [(r[1][...], r[2][...], oc)[i] for r, oc in zip(row_refs, ocs)]
        yis = [g * _dot(o, wb_ref[i]) for g, o in zip(gates, branch)]
        ys = [yi if y is None else y + yi for y, yi in zip(ys, yis)]
    xns = [x + m[0, 0, 2:3, :] * _dot(y.astype(BF16), wo_ref[...]) for x, m, y in zip(xs, mod_refs, ys)]
    h2s = [_rms(xn, g2_ref[...]) * (1.0 + m[0, 0, 4:5, :]) + m[0, 0, 3:4, :] for xn, m in zip(xns, mod_refs)]
    lgs = [_dot3_nt(rw_ref[...], h2) + rb_ref[...] for h2 in h2s]
    for r, xn, h2, lg in zip(rows, xns, h2s, lgs):
        xo_ref[r, :] = xn
        h2_ref[r, :] = h2.astype(BF16)
        lg_ref[:, r] = lg


def _merge(x, modtab, g1, g2, wg, oa, ob, dn_f, dn_b, dz, dn_g, g512, wb, wo, rwt, rb, nb_per_batch, latent_only):
    U = MERGE_UNITS
    if latent_only:
        nlat = nb_per_batch - 1
        n = x.shape[0] // nb_per_batch * nlat
        src = lambda u: (u // nlat) * nb_per_batch + u % nlat + 1
        mod_of = lambda u: (u // nlat, 1, 0, 0)
    else:
        n = x.shape[0]
        src = lambda u: u
        mod_of = lambda u: (u // nb_per_batch, jnp.minimum(u % nb_per_batch, 1), 0, 0)
    step = TM * U
    assert n % step == 0
    orow = lambda i: (i, 0)
    full2 = lambda i: (0, 0)
    row_specs, row_args = [], []
    for u in range(U):
        rowu = lambda i, u=u: (src(i * U + u), 0)
        row_specs += [pl.BlockSpec((TM, D_MODEL), rowu)] + [pl.BlockSpec((TM, BRANCH_W), rowu)] * 5
        row_args += [x, oa, ob, dn_f, dn_b, dz]
    mod_specs = [pl.BlockSpec((1, 1, 6, D_MODEL), lambda i, u=u: mod_of(i * U + u)) for u in range(U)]
    return pl.pallas_call(
        _merge_body,
        out_shape=[jax.ShapeDtypeStruct((n, D_MODEL), F32), jax.ShapeDtypeStruct((n, D_MODEL), BF16),
                   jax.ShapeDtypeStruct((N_EXPERTS, n), F32)],
        grid=(n // step,),
        in_specs=row_specs + mod_specs
        + [pl.BlockSpec((1, D_MODEL), full2), pl.BlockSpec((1, D_MODEL), full2),
           pl.BlockSpec((D_MODEL, N_BRANCH * D_MODEL), full2), pl.BlockSpec((1, BRANCH_W), full2),
           pl.BlockSpec((BRANCH_W, BRANCH_W), full2),
           pl.BlockSpec((N_BRANCH, BRANCH_W, D_MODEL), lambda i: (0, 0, 0)),
           pl.BlockSpec((D_MODEL, D_MODEL), full2),
           pl.BlockSpec((N_EXPERTS, D_MODEL), full2), pl.BlockSpec((N_EXPERTS, 1), full2)],
        out_specs=[pl.BlockSpec((step, D_MODEL), orow), pl.BlockSpec((step, D_MODEL), orow),
                   pl.BlockSpec((N_EXPERTS, step), lambda i: (0, i))],
        compiler_params=_cparams(),
        name="merge",
    )(*row_args, *([modtab] * U), g1, g2, wg, dn_g, g512, wb, wo, rwt, rb)


ROUTE_TB = 1024


def _router_body(lg_ref, tri_ref, e_ref, gate_ref, pos_ref, cnt_ref, run_ref):
    @pl.when(pl.program_id(0) == 0)
    def _():
        run_ref[...] = jnp.zeros_like(run_ref)

    lg = lg_ref[...]
    eid = lax.broadcasted_iota(jnp.int32, lg.shape, 0)
    work = lg
    vals, idxs = [], []
    sel = jnp.zeros(lg.shape, F32)
    for _ in range(TOP_K):
        m = jnp.max(work, axis=0, keepdims=True)
        idx = jnp.min(jnp.where(work == m, eid, N_EXPERTS), axis=0, keepdims=True)
        hit = eid == idx
        sel = jnp.where(hit, 1.0, sel)
        work = jnp.where(hit, -jnp.inf, work)
        vals.append(m)
        idxs.append(idx)
    ex = [jnp.exp(v - vals[0]) for v in vals]
    den = ex[0] + ex[1] + ex[2] + ex[3]
    before = _dot(sel.astype(BF16), tri_ref[...]) + run_ref[:, 0:1]
    for k in range(TOP_K):
        e_ref[k:k + 1, :] = idxs[k]
        gate_ref[k:k + 1, :] = ex[k] / den
        pos_ref[k:k + 1, :] = jnp.sum(jnp.where(eid == idxs[k], before, 0.0), axis=0,
                                      keepdims=True).astype(jnp.int32)
    run_ref[...] = run_ref[...] + jnp.sum(sel, axis=1, keepdims=True)
    cnt_ref[...] = run_ref[...]


def _router(logits_t):
    n = logits_t.shape[1]
    tri = (jnp.arange(ROUTE_TB)[:, None] < jnp.arange(ROUTE_TB)[None, :]).astype(BF16)
    blk = lambda i: (0, i)
    return pl.pallas_call(
        _router_body,
        out_shape=[jax.ShapeDtypeStruct((TOP_K, n), jnp.int32), jax.ShapeDtypeStruct((TOP_K, n), F32),
                   jax.ShapeDtypeStruct((TOP_K, n), jnp.int32), jax.ShapeDtypeStruct((N_EXPERTS, LANES), F32)],
        grid=(n // ROUTE_TB,),
        in_specs=[pl.BlockSpec((N_EXPERTS, ROUTE_TB), blk), pl.BlockSpec((ROUTE_TB, ROUTE_TB), lambda i: (0, 0))],
        out_specs=[pl.BlockSpec((TOP_K, ROUTE_TB), blk), pl.BlockSpec((TOP_K, ROUTE_TB), blk),
                   pl.BlockSpec((TOP_K, ROUTE_TB), blk), pl.BlockSpec((N_EXPERTS, LANES), lambda i: (0, 0))],
        scratch_shapes=[pltpu.VMEM((N_EXPERTS, LANES), F32)],
        compiler_params=_cparams(),
        name="router",
    )(logits_t, tri)


def _expert_body(ib_ref, ie_ref, lo_ref, hi_ref, x_ref, wgu_ref, bgu_ref, wdn_ref, bdn_ref, perm_ref, *rest):
    o_ref, wgu_s, wdn_s = rest[-3:]
    i = pl.program_id(0)
    prev = jnp.maximum(i - 1, 0)

    @pl.when((i == 0) | (ie_ref[i] != ie_ref[prev]))
    def _():
        for m in range(D_EXPERT // LANES):
            t = _dot(wgu_ref[0, 0, :, m * 2 * LANES:(m + 1) * 2 * LANES].astype(BF16), perm_ref[...])
            wgu_s[:, m * LANES:(m + 1) * LANES] = t[:, 0:LANES].astype(BF16)
            wgu_s[:, D_EXPERT + m * LANES:D_EXPERT + (m + 1) * LANES] = t[:, LANES:2 * LANES].astype(BF16)
        wdn_s[...] = wdn_ref[0, 0].astype(BF16)

    lo, hi = lo_ref[i], hi_ref[i]

    @pl.when(hi > lo)
    def _():
        gu = _dot(x_ref[...], wgu_s[...]) + bgu_ref[0]
        g_ = jnp.minimum(gu[:, :D_EXPERT], SWIGLU_LIMIT)
        u_ = jnp.clip(gu[:, D_EXPERT:], -SWIGLU_LIMIT, SWIGLU_LIMIT)
        act = (u_ + 1.0) * (g_ * jax.nn.sigmoid(SWIGLU_ALPHA * g_))
        y = _dot(act.astype(BF16), wdn_s[...]) + bdn_ref[0]
        first = (i == 0) | (ib_ref[i] != ib_ref[prev])
        row = lax.broadcasted_iota(jnp.int32, (MOE_BLK, 1), 0)
        o_ref[...] = jnp.where(first | ((row >= lo) & (row < hi)), y, o_ref[...])


def _experts(items, xs, w_gu, bgu, w_dn, bdn, layer, block0, yb_prev):
    n_items = items[0].shape[0]
    perm = np.zeros((2 * LANES, 2 * LANES), np.float32)
    j = np.arange(LANES)
    perm[2 * j, j] = 1.0
    perm[2 * j + 1, LANES + j] = 1.0
    in_specs = [pl.BlockSpec((MOE_BLK, D_MODEL), lambda i, ib, ie, lo, hi: (ib[i] - block0, 0)),
                pl.BlockSpec((1, 1, D_MODEL, 2 * D_EXPERT), lambda i, ib, ie, lo, hi: (layer, ie[i], 0, 0)),
                pl.BlockSpec((1, 1, 2 * D_EXPERT), lambda i, ib, ie, lo, hi: (ie[i], 0, 0)),
                pl.BlockSpec((1, 1, D_EXPERT, D_MODEL), lambda i, ib, ie, lo, hi: (layer, ie[i], 0, 0)),
                pl.BlockSpec((1, 1, D_MODEL), lambda i, ib, ie, lo, hi: (ie[i], 0, 0)),
                pl.BlockSpec((2 * LANES, 2 * LANES), lambda i, ib, ie, lo, hi: (0, 0)),
                pl.BlockSpec(memory_space=pl.ANY)]
    args = [*items, xs, w_gu, bgu, w_dn, bdn, jnp.asarray(perm, BF16), yb_prev]
    aliases = {len(args) - 1: 0}
    return pl.pallas_call(
        _expert_body,
        out_shape=jax.ShapeDtypeStruct(yb_prev.shape, F32),
        grid_spec=pltpu.PrefetchScalarGridSpec(
            num_scalar_prefetch=4,
            grid=(n_items,),
            in_specs=in_specs,
            out_specs=pl.BlockSpec((MOE_BLK, D_MODEL), lambda i, ib, ie, lo, hi: (ib[i], 0)),
            scratch_shapes=[pltpu.VMEM((D_MODEL, 2 * D_EXPERT), BF16), pltpu.VMEM((D_EXPERT, D_MODEL), BF16)]),
        input_output_aliases=aliases,
        compiler_params=_cparams(),
        name="experts",
    )(*args)


def _lookup(table, idx):
    ids = jnp.arange(table.shape[0], dtype=jnp.int32)
    return jnp.sum(jnp.where(idx[..., None] == ids, table, 0), axis=-1)


def _work_items(cnt_start, cnt_end, block0, n_blk):
    n_items = n_blk + N_EXPERTS - 1
    b1 = block0 + n_blk
    first = jnp.maximum(cnt_start // MOE_BLK, block0)
    last = jnp.minimum((cnt_end - 1) // MOE_BLK, b1 - 1)
    n_e = jnp.where(cnt_end > cnt_start, jnp.maximum(last - first + 1, 0), 0)
    item_end = jnp.cumsum(n_e)
    item_start = item_end - n_e
    total = item_end[-1]
    ii = jnp.arange(n_items, dtype=jnp.int32)
    valid = ii < total
    e_i = jnp.sum((item_end[None, :] <= jnp.minimum(ii, total - 1)[:, None]).astype(jnp.int32), axis=1)
    blk = jnp.where(valid, _lookup(first, e_i) + ii - _lookup(item_start, e_i), b1 - 1)
    lo = jnp.clip(_lookup(cnt_start, e_i) - blk * MOE_BLK, 0, MOE_BLK)
    hi = jnp.clip(_lookup(cnt_end, e_i) - blk * MOE_BLK, 0, MOE_BLK)
    lo = jnp.where(valid, lo, 0)
    hi = jnp.where(valid, hi, 0)
    return blk.astype(jnp.int32), e_i.astype(jnp.int32), lo.astype(jnp.int32), hi.astype(jnp.int32)


def _moe(h2, logits_t, w_gu, bgu, w_dn, bdn, layer, yb):
    n = h2.shape[0]
    a = n * TOP_K
    assert a % (MOE_BLK * MOE_CHUNKS) == 0 and yb.shape[0] >= a
    top_e, gate, pos, cnt = _router(logits_t)
    counts = cnt[:, 0].astype(jnp.int32)
    cnt_end = jnp.cumsum(counts)
    cnt_start = cnt_end - counts
    slot = pos + _lookup(cnt_start, top_e)
    tok = jnp.broadcast_to(jnp.arange(n, dtype=jnp.int32)[None, :], (TOP_K, n))
    _, slot_tok = lax.sort_key_val(slot.reshape(-1), tok.reshape(-1))
    n_blk = a // MOE_BLK // MOE_CHUNKS
    rows_c = n_blk * MOE_BLK
    for c in range(MOE_CHUNKS):
        xs = h2.at[slot_tok[c * rows_c:(c + 1) * rows_c]].get(mode="promise_in_bounds")
        items = _work_items(cnt_start, cnt_end, c * n_blk, n_blk)
        yb = _experts(items, xs, w_gu, bgu, w_dn, bdn, layer, c * n_blk, yb)
    ys = [yb.at[slot[k]].get(mode="promise_in_bounds") for k in range(TOP_K)]
    return ys, gate.T, yb


DN_GW = 256
HALO = 8
DN_PAIR = 4
_M_EYE, _M_BLK16, _M_OFF32, _M_OFF64, _M_INCL, _M_STRICT = 0, 1, 2, 3, 4, 6


def _dn_masks():
    row = np.arange(DN_CHUNK)[:, None]
    col = np.arange(DN_GW)[None, :] % DN_CHUNK
    b16 = (row // 16) == (col // 16)
    b32 = (row // 32) == (col // 32)
    m = np.stack([row == col, b16, b32 & ~b16, ~b32, col <= row, col >= row, col < row, col > row])
    hid = np.arange(DN_GW) // DN_HEAD_DIM
    return jnp.asarray(m, F32), jnp.asarray(hid[:, None] == hid[None, :], BF16)


def _split3(a):
    hi = a.astype(BF16)
    r = a - hi.astype(F32)
    lo = r.astype(BF16)
    lo2 = (r - lo.astype(F32)).astype(BF16)
    return hi, lo, lo2


def _bd(x, bdm_ref):
    xb = x.astype(BF16)
    return jnp.concatenate([xb, xb, xb, xb], axis=0) * bdm_ref[...]


def _mm(a, wbd):
    return _dot(a.astype(BF16), wbd)


def _tri_inverse(l_mats, m_ref, bdm_ref):
    half = DN_CHUNK
    bd = lambda v: _bd(v, bdm_ref)
    lds = [l * m_ref[_M_BLK16] for l in l_mats]
    ps = [_mm(ld, bd(ld)) for ld in lds]
    xs = [m_ref[_M_EYE] - ld for ld in lds]
    for _ in range(2):
        rs = [_mm(jnp.concatenate([p, x], axis=0), bd(p)) for p, x in zip(ps, xs)]
        ps = [r[0:half] for r in rs]
        xs = [x + r[half:2 * half] for x, r in zip(xs, rs)]
    xs = [x + _mm(x, bd(p)) for x, p in zip(xs, ps)]
    ts = [_mm(x, bd(l * m_ref[_M_OFF32])) for x, l in zip(xs, l_mats)]
    xs = [x - _mm(t, bd(x)) for x, t in zip(xs, ts)]
    ts = [_mm(x, bd(l * m_ref[_M_OFF64])) for x, l in zip(xs, l_mats)]
    return [x - _mm(t, bd(x)) for x, t in zip(xs, ts)]


def _dn_prep_body(cur_ref, prev_ref, next_ref, cw_ref, dg_ref, gc_ref, g512_ref, e_ref, tri_ref, m_ref, bdm_ref,
                  u_ref, w_ref, kd_ref, qd_ref, aq_ref, gl_ref,
                  ext_ref, q_s, k_s, v_s, gx_s, bx_s):
    j = pl.program_id(1)
    nbp = pl.num_programs(1)
    C = DN_CHUNK
    ext_ref[HALO:HALO + TM, :] = cur_ref[...]
    ext_ref[0:HALO, :] = jnp.where(j >= 2, prev_ref[...], 0.0)
    ext_ref[HALO + TM:2 * HALO + TM, :] = jnp.where((j >= 1) & (j < nbp - 1), next_ref[...], 0.0)
    y = None
    for t in range(DN_CONV):
        o = HALO - DN_CONV // 2 + t
        term = cw_ref[t:t + 1, :] * ext_ref[o:o + TM, :]
        y = term if y is None else y + term
    y = y * jax.nn.sigmoid(y)

    g512 = g512_ref[...]

    def head_sumsq(x):
        hi, lo = _split_bf16(x * x)
        return _dot(hi, g512) + _dot(lo, g512)

    q = y[:, 0:512]
    k = y[:, 512:1024]
    q_s[...] = q * lax.rsqrt(head_sumsq(q) + NORM_EPS) * (DN_HEAD_DIM ** -0.5)
    k_s[...] = k * lax.rsqrt(head_sumsq(k) + NORM_EPS)
    v_s[...] = y[:, 1024:1536]

    dg = dg_ref[...]
    beta_all = jax.nn.sigmoid(dg)
    z = dg + gc_ref[1:2, :]
    g_all = gc_ref[0:1, :] * (jnp.maximum(z, 0.0) + jnp.log(1.0 + jnp.exp(-jnp.abs(z))))
    bh, bl = _split_bf16(beta_all)
    bcat = jnp.concatenate([bh, bl], axis=0)
    gparts = jnp.concatenate(_split3(g_all), axis=1)
    for d in range(2):
        cs = _dot(tri_ref[d], gparts)
        gcum = cs[:, 0:128] + cs[:, 128:256] + cs[:, 256:384]
        gcat = jnp.concatenate(_split3(gcum), axis=0)
        for grp in range(2):
            eg = _dot(gcat, e_ref[4 + 2 * d + grp])
            gx_s[d, grp] = eg[0:TM] + eg[TM:2 * TM] + eg[2 * TM:3 * TM]
            eb = _dot(bcat, e_ref[2 * d + grp])
            bx_s[d, grp] = eb[0:TM] + eb[TM:2 * TM]

    def chunk_pair(it, carry):
        units = []
        l_mats = []
        for ci in range(DN_PAIR):
            cc = it * DN_PAIR + ci
            rows = pl.ds(pl.multiple_of(cc * C, C), C)
            for grp in range(2):
                cols = slice(grp * DN_GW, (grp + 1) * DN_GW)
                kg, qg = k_s[rows, cols], q_s[rows, cols]
                kb = [kg * bx_s[d, grp, rows, :] for d in range(2)]
                raw = _dot_nt(jnp.concatenate([kb[0], kb[1], qg], axis=0).astype(BF16), _bd(kg, bdm_ref))
                for d in range(2):
                    gx = gx_s[d, grp, rows, :]
                    rvec = jnp.sum(gx * m_ref[_M_EYE], axis=0, keepdims=True)
                    dec = jnp.exp(jnp.minimum(gx - rvec, 0.0)) * m_ref[_M_INCL + d]
                    l_mats.append(raw[C * d:C * (d + 1)] * dec * m_ref[_M_STRICT + d])
                    glast = gx[C - 1:C, :] if d == 0 else gx[0:1, :]
                    kd_ref[d, rows, cols] = (kg * jnp.exp(glast - gx)).astype(BF16)
                    qd_ref[d, rows, cols] = (qg * jnp.exp(gx)).astype(BF16)
                    aq_ref[d, rows, cols] = (raw[2 * C:3 * C] * dec).astype(BF16)
                    gl_ref[d, pl.ds(cc, 1), :, cols] = jnp.exp(glast).reshape(1, 1, DN_GW)
                    units.append((rows, cols, d, grp))
        tinvs = _tri_inverse(l_mats, m_ref, bdm_ref)
        for (rows, cols, d, grp), tinv in zip(units, tinvs):
            bx = bx_s[d, grp, rows, :]
            u_ref[d, rows, cols] = _mm(tinv, _bd(v_s[rows, cols] * bx, bdm_ref))
            w_ref[d, rows, cols] = _mm(tinv, _bd(k_s[rows, cols] * bx * jnp.exp(gx_s[d, grp, rows, :]), bdm_ref)
                                       ).astype(BF16)
        return carry

    lax.fori_loop(0, TM // C // DN_PAIR, chunk_pair, 0)


def _dn_prep(dqkv, dg, conv_w, a_log, dt_bias, n_batch, rows):
    n = dqkv.shape[0]
    nbp = rows // TM
    hb = TM // HALO
    H = DN_HEADS
    cw = jnp.zeros((8, 1536), F32).at[0:DN_CONV].set(conv_w)
    gc = jnp.zeros((8, LANES), F32)
    gc = gc.at[0, 2 * H:4 * H].set(-jnp.exp(a_log.reshape(-1))).at[1, 2 * H:4 * H].set(dt_bias.reshape(-1))
    hid = np.arange(512) // DN_HEAD_DIM
    g512 = jnp.asarray(hid[:, None] == hid[None, :], BF16)
    e = np.zeros((8, LANES, DN_GW), np.float32)
    for kind in range(2):
        for d in range(2):
            for grp in range(2):
                for h in range(4):
                    e[4 * kind + 2 * d + grp, 16 * kind + 8 * d + 4 * grp + h, 64 * h:64 * (h + 1)] = 1.0
    e = jnp.asarray(e, BF16)
    t = np.arange(TM)
    same = (t[:, None] // DN_CHUNK) == (t[None, :] // DN_CHUNK)
    tri = jnp.asarray(np.stack([same & (t[None, :] <= t[:, None]), same & (t[None, :] >= t[:, None])]), BF16)
    masks, bdm = _dn_masks()
    blk = lambda b, j: (b * nbp + j, 0)
    full2 = lambda b, j: (0, 0)
    full3 = lambda b, j: (0, 0, 0)
    dblk = lambda b, j: (0, b * nbp + j, 0)
    nlast = n // HALO - 1
    outs = pl.pallas_call(
        _dn_prep_body,
        out_shape=[jax.ShapeDtypeStruct((2, n, 512), F32)] + [jax.ShapeDtypeStruct((2, n, 512), BF16)] * 4
        + [jax.ShapeDtypeStruct((2, n // DN_CHUNK, 1, 512), F32)],
        grid=(n_batch, nbp),
        in_specs=[pl.BlockSpec((TM, 1536), blk),
                  pl.BlockSpec((HALO, 1536), lambda b, j: (jnp.maximum((b * nbp + j) * hb - 1, 0), 0)),
                  pl.BlockSpec((HALO, 1536), lambda b, j: (jnp.minimum((b * nbp + j + 1) * hb, nlast), 0)),
                  pl.BlockSpec((8, 1536), full2), pl.BlockSpec((TM, LANES), blk), pl.BlockSpec((8, LANES), full2),
                  pl.BlockSpec((512, 512), full2), pl.BlockSpec((8, LANES, DN_GW), full3),
                  pl.BlockSpec((2, TM, TM), full3), pl.BlockSpec((8, DN_CHUNK, DN_GW), full3),
                  pl.BlockSpec((DN_GW, DN_GW), full2)],
        out_specs=[pl.BlockSpec((2, TM, 512), dblk)] * 5
        + [pl.BlockSpec((2, TM // DN_CHUNK, 1, 512), lambda b, j: (0, b * nbp + j, 0, 0))],
        scratch_shapes=[pltpu.VMEM((TM + 2 * HALO, 1536), F32), pltpu.VMEM((TM, 512), F32),
                        pltpu.VMEM((TM, 512), F32), pltpu.VMEM((TM, 512), F32),
                        pltpu.VMEM((2, 2, TM, DN_GW), F32), pltpu.VMEM((2, 2, TM, DN_GW), F32)],
        compiler_params=_cparams(2),
        name="dn_prep",
    )(dqkv, dqkv, dqkv, cw, dg, gc, g512, e, tri, masks, bdm)
    return outs, g512


DN_SCAN_NB = 8


def _dn_scan_body(uf, wf, kf, qf, af, gf, ub, wb, kb, qb, ab, gb, bdm_ref, of_ref, ob_ref, s_ref):
    @pl.when(pl.program_id(1) == 0)
    def _():
        s_ref[...] = jnp.zeros_like(s_ref)

    same = bdm_ref[...] != 0
    dirs = ((uf, wf, kf, qf, af, gf, of_ref), (ub, wb, kb, qb, ab, gb, ob_ref))
    units = [(bb, d, grp) for bb in range(DN_SCAN_NB) for d in range(2) for grp in range(2)]
    cols = [slice(grp * DN_GW, (grp + 1) * DN_GW) for grp in range(2)]
    s = [s_ref[bb, d, grp] for bb, d, grp in units]
    sq = [_dot(jnp.concatenate([dirs[d][1][0, bb, :, cols[grp]], dirs[d][3][0, bb, :, cols[grp]]], axis=0),
               st.astype(BF16)) for (bb, d, grp), st in zip(units, s)]
    vnew = [dirs[d][0][0, bb, :, cols[grp]] - r[0:DN_CHUNK] for (bb, d, grp), r in zip(units, sq)]
    intra = [_dot(dirs[d][4][0, bb, :, cols[grp]], _bd(v, bdm_ref)) for (bb, d, grp), v in zip(units, vnew)]
    upd = [lax.dot_general(dirs[d][2][0, bb, :, cols[grp]], v.astype(BF16), (((0,), (0,)), ((), ())),
                           preferred_element_type=F32) for (bb, d, grp), v in zip(units, vnew)]
    for (bb, d, grp), st, r, a, up in zip(units, s, sq, intra, upd):
        dirs[d][6][bb, :, cols[grp]] = r[DN_CHUNK:2 * DN_CHUNK] + a
        s_ref[bb, d, grp] = st * dirs[d][5][0, bb, 0, :, cols[grp]] + jnp.where(same, up, 0.0)


def _dn_scan(u, w, kd, qd, aq, gl, n_batch, rows, t_ctx):
    nch = rows // DN_CHUNK
    nctx = t_ctx // DN_CHUNK
    assert n_batch % DN_SCAN_NB == 0

    def cb(c):
        return jnp.where(c < nctx, nctx - 1 - c, nch - 1 - (c - nctx))

    per_batch = lambda a: a.reshape(2, n_batch, rows, 512)
    u, w, kd, qd, aq = map(per_batch, (u, w, kd, qd, aq))
    gl = gl.reshape(2, n_batch, nch, 1, 512)
    blk = (1, DN_SCAN_NB, DN_CHUNK, 512)
    gblk = (1, DN_SCAN_NB, 1, 1, 512)
    specs = []
    for d, ch in ((0, lambda c: c), (1, cb)):
        specs += [pl.BlockSpec(blk, lambda b, c, d=d, ch=ch: (d, b, ch(c), 0))] * 5
        specs += [pl.BlockSpec(gblk, lambda b, c, d=d, ch=ch: (d, b, ch(c), 0, 0))]
    specs.append(pl.BlockSpec((DN_GW, DN_GW), lambda b, c: (0, 0)))
    o_f, o_b = pl.pallas_call(
        _dn_scan_body,
        out_shape=[jax.ShapeDtypeStruct((n_batch, rows, 512), F32)] * 2,
        grid=(n_batch // DN_SCAN_NB, nch),
        in_specs=specs,
        out_specs=[pl.BlockSpec((DN_SCAN_NB, DN_CHUNK, 512), lambda b, c: (b, c, 0)),
                   pl.BlockSpec((DN_SCAN_NB, DN_CHUNK, 512), lambda b, c: (b, cb(c), 0))],
        scratch_shapes=[pltpu.VMEM((DN_SCAN_NB, 2, 2, DN_GW, DN_GW), F32)],
        compiler_params=_cparams(2),
        name="dn_scan",
    )(u, w, kd, qd, aq, gl, u, w, kd, qd, aq, gl, _dn_masks()[1])
    return o_f.reshape(n_batch * rows, 512), o_b.reshape(n_batch * rows, 512)


def _deltanet(dqkv, dg, conv_w, a_log, dt_bias, n_batch, rows, t_ctx):
    (u, w, kd, qd, aq, gl), g512 = _dn_prep(dqkv, dg, conv_w, a_log, dt_bias, n_batch, rows)
    o_f, o_b = _dn_scan(u, w, kd, qd, aq, gl, n_batch, rows, t_ctx)
    return o_f, o_b, g512


def _combine_body(x_ref, y0_ref, y1_ref, y2_ref, y3_ref, gate_ref, mod_ref, fg_ref, o_ref, *, final):
    gate = gate_ref[...]
    y = None
    for k, y_ref in enumerate((y0_ref, y1_ref, y2_ref, y3_ref)):
        yk = y_ref[...] * gate[:, k:k + 1]
        y = yk if y is None else y + yk
    xo = x_ref[...] + mod_ref[0, 0, 5:6, :] * y
    o_ref[...] = _rms(xo, fg_ref[...]) if final else xo


def _combine(x, ys, gate_rows, modtab, final_g, nb_per_batch, latent_only):
    n = x.shape[0]
    if latent_only:
        nlat = nb_per_batch - 1
        mod_idx = lambda i: (i // nlat, 1, 0, 0)
    else:
        mod_idx = lambda i: (i // nb_per_batch, jnp.minimum(i % nb_per_batch, 1), 0, 0)
    row = lambda i: (i, 0)
    return pl.pallas_call(
        functools.partial(_combine_body, final=latent_only),
        out_shape=jax.ShapeDtypeStruct((n, D_MODEL), F32),
        grid=(n // TM,),
        in_specs=[pl.BlockSpec((TM, D_MODEL), row)] * 5
        + [pl.BlockSpec((TM, TOP_K), row), pl.BlockSpec((1, 1, 6, D_MODEL), mod_idx),
           pl.BlockSpec((1, D_MODEL), lambda i: (0, 0))],
        out_specs=pl.BlockSpec((TM, D_MODEL), row),
        compiler_params=_cparams(),
        name="combine",
    )(x, *ys, gate_rows, modtab, final_g)


def _rope_tables(seq, t_ctx, rot_dim):
    rows = seq // GRID_W
    row = jnp.broadcast_to(jnp.arange(rows)[:, None], (rows, GRID_W)).reshape(-1).astype(F32)
    col = jnp.broadcast_to(jnp.arange(GRID_W)[None, :], (rows, GRID_W)).reshape(-1).astype(F32)
    n_freq = rot_dim // 4
    inv_freq = ROPE_THETA ** (-jnp.arange(n_freq, dtype=F32) / n_freq)
    ang = jnp.concatenate([row[:, None] * inv_freq, col[:, None] * inv_freq], axis=-1)
    cos, sin = jnp.cos(ang), jnp.sin(ang)
    zero = jnp.zeros_like(sin)
    c = jnp.concatenate([cos, cos], axis=-1)
    sa = jnp.concatenate([-sin, zero], axis=-1)
    sb = jnp.concatenate([zero, sin], axis=-1)
    ident = jnp.stack([jnp.ones((t_ctx, rot_dim), F32), jnp.zeros((t_ctx, rot_dim), F32),
                       jnp.zeros((t_ctx, rot_dim), F32)])
    tab = jnp.concatenate([ident, jnp.stack([c, sa, sb])], axis=1)
    return jnp.tile(tab, (1, 1, LANES // rot_dim))


def _layer_weights(w_in, w_q_up, w_kv_up, w_branch, w_out, router_w, router_b, b_gu, b_dn):
    sizes = (MLA_Q_RANK, MLA_KV_RANK, MLA_ROPE, 512, 128, 128, 1536, 512, 16, 16, N_BRANCH * D_MODEL)
    offs = np.cumsum((0,) + sizes)
    part = lambda i: w_in[:, offs[i]:offs[i + 1]]
    pad = jnp.zeros((D_MODEL, LANES - 32), F32)
    w_proj = jnp.concatenate(
        [part(0), part(1), part(3), part(4), part(5), part(6), part(7),
         jnp.tile(part(2), (1, LANES // MLA_ROPE)), part(8), part(9), pad], axis=1).astype(BF16)
    w_gate = part(10).astype(BF16)
    hq = MLA_NOPE + MLA_ROPE
    wq = w_q_up.reshape(MLA_Q_RANK, MLA_HEADS, hq)
    wq = jnp.concatenate([wq[:, :, :MLA_NOPE].reshape(MLA_Q_RANK, -1),
                          wq[:, :, MLA_NOPE:].reshape(MLA_Q_RANK, -1)], axis=1).astype(BF16)
    wkv = w_kv_up.reshape(MLA_KV_RANK, MLA_HEADS, MLA_NOPE + MLA_V)
    wkv = jnp.concatenate([wkv[:, :, :MLA_NOPE].reshape(MLA_KV_RANK, -1),
                           wkv[:, :, MLA_NOPE:].reshape(MLA_KV_RANK, -1)], axis=1).astype(BF16)
    bgu = jnp.concatenate([b_gu[:, 0::2], b_gu[:, 1::2]], axis=-1).reshape(N_EXPERTS, 1, 2 * D_EXPERT)
    return dict(w_proj=w_proj, w_gate=w_gate, wq=wq, wkv=wkv, wb=w_branch.astype(BF16),
                wo=w_out.astype(BF16), rwt=router_w.T, rb=router_b.reshape(N_EXPERTS, 1),
                bgu=bgu, bdn=b_dn.reshape(N_EXPERTS, 1, D_MODEL))


def kernel(x, c, ctx, c_ctx, w_mod, b_mod, norm1_g, norm2_g, w_in, mla_q_norm_g, mla_w_q_up,
           mla_kv_norm_g, mla_w_kv_up, swa_sink, dn_conv_w, dn_a_log, dn_dt_bias, dn_norm_g,
           w_branch, w_out, router_w, router_b, exp_w_gu, exp_b_gu, exp_w_dn, exp_b_dn, final_norm_g):
    B, S, D = x.shape
    T = ctx.shape[1]
    R = T + S
    depth = w_mod.shape[0]
    assert D == D_MODEL and T == TM and S % TM == 0 and (B * R) % ROUTE_TB == 0
    nbp = R // TM
    xa = jnp.concatenate([ctx, x], axis=1).reshape(B * R, D)
    rope_s = _rope_tables(S, T, SWA_HEAD_DIM)
    rope_m = _rope_tables(S, T, MLA_ROPE)
    cvec = jnp.concatenate([c, c_ctx[None, :]], axis=0)
    yb = jnp.zeros((B * R * TOP_K, D), F32)
    for l in range(depth):
        w = _layer_weights(w_in[l], mla_w_q_up[l], mla_w_kv_up[l], w_branch[l], w_out[l], router_w[l],
                           router_b[l], exp_b_gu[l], exp_b_dn[l])
        mod = _modulation(cvec, w_mod[l], b_mod[l]).reshape(B + 1, 6, D)
        modtab = jnp.stack([jnp.broadcast_to(mod[B][None], (B, 6, D)), mod[:B]], axis=1)
        g1 = norm1_g[l].reshape(1, D)
        g2 = norm2_g[l].reshape(1, D)
        q_m, k_m, v_m, sq, sk, sv, dqkv, dz, dg = _inproj(
            xa, modtab, g1, w["w_proj"], mla_q_norm_g[l].reshape(1, -1), mla_kv_norm_g[l].reshape(1, -1),
            w["wq"], w["wkv"], rope_s, rope_m, nbp)
        o_a = _mla_attention(q_m, k_m, v_m, B, R)
        o_b = _swa_attention(sq, sk, sv, swa_sink[l], B, T, S)
        dn_f, dn_b, g512 = _deltanet(dqkv, dg, dn_conv_w[l], dn_a_log[l], dn_dt_bias[l], B, R, T)
        last = l == depth - 1
        xn, h2, logits_t = _merge(xa, modtab, g1, g2, w["w_gate"], o_a, o_b, dn_f, dn_b, dz,
                                  jnp.tile(dn_norm_g[l], DN_HEADS).reshape(1, BRANCH_W), g512, w["wb"], w["wo"],
                                  w["rwt"], w["rb"], nbp, last)
        ys, gate_rows, yb = _moe(h2, logits_t, exp_w_gu, w["bgu"], exp_w_dn, w["bdn"], l, yb)
        xa = _combine(xn, ys, gate_rows, modtab, final_norm_g.reshape(1, D), nbp, last)
    return xa.reshape(B, S, D)
```

```python
import functools

import jax
import jax.numpy as jnp
import numpy as np
from jax import lax
from jax.experimental import pallas as pl
from jax.experimental.pallas import tpu as pltpu

F32 = jnp.float32
BF16 = jnp.bfloat16

D_MODEL = 1024
GRID_W = 64
ROPE_THETA = 10000.0
NORM_EPS = 1e-6
MLA_HEADS = 8
MLA_Q_RANK = 384
MLA_KV_RANK = 256
MLA_NOPE = 64
MLA_ROPE = 32
MLA_V = 64
SWA_HEADS = 8
SWA_KV_HEADS = 2
SWA_HEAD_DIM = 64
SWA_WINDOW = 128
DN_HEADS = 8
DN_HEAD_DIM = 64
DN_CONV = 5
DN_CHUNK = 64
N_BRANCH = 3
BRANCH_W = 512
N_EXPERTS = 32
TOP_K = 4
D_EXPERT = 1024
SWIGLU_LIMIT = 7.0
SWIGLU_ALPHA = 1.702
MOE_BLK = 512
MOE_CHUNKS = 4

LANES = 128
LOG2E = 1.4426950408889634
TM = 256
VMEM_LIMIT = 56 * 1024 * 1024

_C_CQ = (0, 384)
_C_CKV = (384, 640)
_C_SQ = (640, 1152)
_C_SK = (1152, 1280)
_C_SV = (1280, 1408)
_C_DQKV = (1408, 2944)
_C_DZ = (2944, 3456)
_C_KRT = (3456, 3584)
_C_DG = (3584, 3712)
_N_PROJ = 3712


def _cparams(n_axes=1):
    return pltpu.CompilerParams(dimension_semantics=("arbitrary",) * n_axes, vmem_limit_bytes=VMEM_LIMIT)


def _dot(a, b):
    return jnp.dot(a, b, preferred_element_type=F32)


def _dot_nt(a, b):
    return lax.dot_general(a, b, (((1,), (1,)), ((), ())), preferred_element_type=F32)


def _split_bf16(a):
    hi = a.astype(BF16)
    lo = (a - hi.astype(F32)).astype(BF16)
    return hi, lo


def _dot3(a, b):
    ah, al = _split_bf16(a)
    bh, bl = _split_bf16(b)
    return _dot(ah, bh) + (_dot(ah, bl) + _dot(al, bh))


def _dot3_nt(a, b):
    ah, al = _split_bf16(a)
    bh, bl = _split_bf16(b)
    return _dot_nt(ah, bh) + (_dot_nt(ah, bl) + _dot_nt(al, bh))


def _rms(x, g):
    return x * lax.rsqrt(jnp.mean(x * x, axis=-1, keepdims=True) + NORM_EPS) * g


def _rope_cols(x, tab_ref, half):
    c, sa, sb = tab_ref[0], tab_ref[1], tab_ref[2]
    return x * c + pltpu.roll(x, LANES - half, axis=1) * sa + pltpu.roll(x, half, axis=1) * sb


def _mod_body(c_ref, w_ref, b_ref, o_ref):
    c = c_ref[...]
    a = c * jax.nn.sigmoid(c)
    o_ref[...] = _dot3(a, w_ref[...]) + b_ref[...]


def _modulation(cvec, w_mod, b_mod):
    m = cvec.shape[0]
    n = w_mod.shape[1]
    tn = 512
    return pl.pallas_call(
        _mod_body,
        out_shape=jax.ShapeDtypeStruct((m, n), F32),
        grid=(n // tn,),
        in_specs=[pl.BlockSpec((m, D_MODEL), lambda j: (0, 0)),
                  pl.BlockSpec((D_MODEL, tn), lambda j: (0, j)),
                  pl.BlockSpec((1, tn), lambda j: (0, j))],
        out_specs=pl.BlockSpec((m, tn), lambda j: (0, j)),
        compiler_params=_cparams(),
        name="modulation",
    )(cvec, w_mod, b_mod.reshape(1, n))


IN_UNITS = 2


def _inproj_body(*refs):
    U = IN_UNITS
    x_ref = refs[0]
    mod_refs = refs[1:1 + U]
    g_ref, w_ref, qg_ref, kvg_ref, wq_ref, wkv_ref = refs[1 + U:7 + U]
    rope_s_refs = refs[7 + U:7 + 2 * U]
    rope_m_refs = refs[7 + 2 * U:7 + 3 * U]
    q_ref, k_ref, v_ref, sq_ref, sk_ref, sv_ref, dqkv_ref, dz_ref, dg_ref = refs[7 + 3 * U:]
    rows = [slice(u * TM, (u + 1) * TM) for u in range(U)]
    hs = [(_rms(x_ref[r, :], g_ref[...]) * (1.0 + m[0, 0, 1:2, :]) + m[0, 0, 0:1, :]).astype(BF16)
          for r, m in zip(rows, mod_refs)]

    def proj(cols):
        return [_dot(h, w_ref[:, cols[0]:cols[1]]) for h in hs]

    for ref, cols in ((dqkv_ref, _C_DQKV), (dz_ref, _C_DZ), (dg_ref, _C_DG)):
        for r, p in zip(rows, proj(cols)):
            ref[r, :] = p
    for r, p in zip(rows, proj(_C_SV)):
        sv_ref[r, :] = p.astype(BF16)
    for r, p, rope_ref in zip(rows, proj(_C_SK), rope_s_refs):
        sk_ref[r, :] = _rope_cols(p, rope_ref, SWA_HEAD_DIM // 2).astype(BF16)
    lane = lax.broadcasted_iota(jnp.int32, (TM, LANES), 1)
    lo = lane < SWA_HEAD_DIM
    for r, sq, rope_ref in zip(rows, proj(_C_SQ), rope_s_refs):
        for c in range(4):
            xr = _rope_cols(sq[:, c * LANES:(c + 1) * LANES], rope_ref, SWA_HEAD_DIM // 2)
            xs = pltpu.roll(xr, SWA_HEAD_DIM, axis=1)
            if c < 2:
                a, b = jnp.where(lo, xr, 0.0), jnp.where(lo, xs, 0.0)
            else:
                a, b = jnp.where(lo, 0.0, xs), jnp.where(lo, 0.0, xr)
            sq_ref[r, (2 * c) * LANES:(2 * c + 1) * LANES] = a.astype(BF16)
            sq_ref[r, (2 * c + 1) * LANES:(2 * c + 2) * LANES] = b.astype(BF16)
    half = MLA_ROPE // 2
    qns = [_rms(p, qg_ref[...]).astype(BF16) for p in proj(_C_CQ)]
    qs = [_dot(qn, wq_ref[...]) for qn in qns]
    kvns = [_rms(p, kvg_ref[...]).astype(BF16) for p in proj(_C_CKV)]
    kvs = [_dot(kvn, wkv_ref[...]) for kvn in kvns]
    krs = [_rope_cols(p, rope_ref, half).astype(BF16)
           for p, rope_ref in zip(proj(_C_KRT), rope_m_refs)]
    for r, q, kv, kr, rope_ref in zip(rows, qs, kvs, krs, rope_m_refs):
        q_ref[r, 0:512] = q[:, 0:512].astype(BF16)
        for c in range(2):
            lo_, hi_ = 512 + c * LANES, 512 + (c + 1) * LANES
            q_ref[r, lo_:hi_] = _rope_cols(q[:, lo_:hi_], rope_ref, half).astype(BF16)
        for c in range(4):
            k_ref[r, (2 * c) * LANES:(2 * c + 1) * LANES] = kv[:, c * LANES:(c + 1) * LANES].astype(BF16)
            k_ref[r, (2 * c + 1) * LANES:(2 * c + 2) * LANES] = kr
        v_ref[r, :] = kv[:, 512:1024].astype(BF16)


def _inproj(x, modtab, g, w, qg, kvg, wq, wkv, rope_s, rope_m, nb_per_batch):
    n = x.shape[0]
    step = TM * IN_UNITS
    assert n % step == 0
    row = lambda i: (i, 0)
    full = lambda i: (0, 0)
    unit = lambda i, u: i * IN_UNITS + u
    widths = [(768, BF16), (1024, BF16), (512, BF16), (1024, BF16), (128, BF16), (128, BF16),
              (1536, F32), (512, F32), (128, F32)]
    mod_specs = [pl.BlockSpec((1, 1, 6, D_MODEL),
                              lambda i, u=u: (unit(i, u) // nb_per_batch,
                                              jnp.minimum(unit(i, u) % nb_per_batch, 1), 0, 0))
                 for u in range(IN_UNITS)]
    rope_specs = [pl.BlockSpec((3, TM, LANES), lambda i, u=u: (0, unit(i, u) % nb_per_batch, 0))
                  for u in range(IN_UNITS)]
    return pl.pallas_call(
        _inproj_body,
        out_shape=[jax.ShapeDtypeStruct((n, wd), dt) for wd, dt in widths],
        grid=(n // step,),
        in_specs=[pl.BlockSpec((step, D_MODEL), row)] + mod_specs
        + [pl.BlockSpec((1, D_MODEL), full), pl.BlockSpec((D_MODEL, _N_PROJ), full),
           pl.BlockSpec((1, MLA_Q_RANK), full), pl.BlockSpec((1, MLA_KV_RANK), full),
           pl.BlockSpec((MLA_Q_RANK, 768), full), pl.BlockSpec((MLA_KV_RANK, 1024), full)]
        + rope_specs + rope_specs,
        out_specs=[pl.BlockSpec((step, wd), row) for wd, _ in widths],
        compiler_params=_cparams(),
        name="inproj",
    )(x, *([modtab] * IN_UNITS), g, w, qg, kvg, wq, wkv, *([rope_s] * IN_UNITS), *([rope_m] * IN_UNITS))


def _mla_attn_body(q_ref, k_ref, v_ref, o_ref, *, t_ctx):
    c1 = (MLA_NOPE + MLA_ROPE) ** -0.5 * LOG2E
    lane = lax.broadcasted_iota(jnp.int32, (TM, LANES), 1)
    zero = jnp.zeros((TM, LANES), BF16)

    def attend(nk):
        for c in range(4):
            kc = k_ref[0:nk, c * 2 * LANES:(c + 1) * 2 * LANES]
            vc = v_ref[0:nk, c * LANES:(c + 1) * LANES]
            qn = q_ref[:, c * LANES:(c + 1) * LANES]
            qr = q_ref[:, 512 + (c // 2) * LANES:512 + (c // 2 + 1) * LANES]
            qa = [jnp.concatenate(
                [jnp.where((lane >= 64 * s) & (lane < 64 * (s + 1)), qn, zero),
                 jnp.where((lane >= 32 * ((2 * c + s) % 4)) & (lane < 32 * ((2 * c + s) % 4 + 1)), qr, zero)],
                axis=1) for s in range(2)]
            sc = [_dot_nt(q, kc) * c1 for q in qa]
            m = [jnp.max(t, axis=-1, keepdims=True) for t in sc]
            e = [jnp.exp2(t - mm) for t, mm in zip(sc, m)]
            l = [jnp.sum(t, axis=-1, keepdims=True) for t in e]
            outs = [_dot(t.astype(BF16), vc) / ll for t, ll in zip(e, l)]
            o_ref[:, c * LANES:(c + 1) * LANES] = jnp.where(lane < 64, outs[0], outs[1]).astype(BF16)

    j = pl.program_id(1)

    @pl.when(j == 0)
    def _():
        attend(t_ctx)

    @pl.when(j > 0)
    def _():
        attend(k_ref.shape[0])


def _mla_attention(q, k, v, n_batch, rows):
    nbp = rows // TM
    return pl.pallas_call(
        functools.partial(_mla_attn_body, t_ctx=TM),
        out_shape=jax.ShapeDtypeStruct((n_batch * rows, BRANCH_W), BF16),
        grid=(n_batch, nbp),
        in_specs=[pl.BlockSpec((TM, 768), lambda b, j: (b * nbp + j, 0)),
                  pl.BlockSpec((rows, 1024), lambda b, j: (b, 0)),
                  pl.BlockSpec((rows, 512), lambda b, j: (b, 0))],
        out_specs=pl.BlockSpec((TM, BRANCH_W), lambda b, j: (b * nbp + j, 0)),
        compiler_params=_cparams(2),
        name="mla_attention",
    )(q, k, v)


SWA_QB = 128
SWA_NQ = 2


def _swa_body(q_ref, k_ref, v_ref, sink_ref, o_ref, *, t_ctx, seq):
    c1 = SWA_HEAD_DIM ** -0.5 * LOG2E
    j = pl.program_id(1)
    n_ctx_steps = t_ctx // (SWA_QB * SWA_NQ)
    lane = lax.broadcasted_iota(jnp.int32, (SWA_QB, LANES), 1)
    lo = lane < SWA_HEAD_DIM
    win = 3 * SWA_QB

    def finish(rows, parts):
        for c in range(4):
            g = c // 2
            r0 = (2 * (c % 2)) * SWA_QB
            a = parts[g][r0:r0 + SWA_QB]
            b = parts[g][r0 + SWA_QB:r0 + 2 * SWA_QB]
            if g == 0:
                col = jnp.where(lo, a, pltpu.roll(b, SWA_HEAD_DIM, axis=1))
            else:
                col = jnp.where(lo, pltpu.roll(a, SWA_HEAD_DIM, axis=1), b)
            o_ref[rows, c * LANES:(c + 1) * LANES] = col.astype(BF16)

    def blocks(blks):
        kc, vc = k_ref[0:t_ctx, :], v_ref[0:t_ctx, :]
        windowed = blks[0] is not None
        kw, vw, band = [], [], []
        for blk in blks if windowed else ():
            start = jnp.clip((blk - 1) * SWA_QB, 0, seq - win)
            rs = pl.multiple_of(t_ctx + start, SWA_QB)
            kw.append(k_ref[pl.ds(rs, win), :])
            vw.append(v_ref[pl.ds(rs, win), :])
            qpos = blk * SWA_QB + lax.broadcasted_iota(jnp.int32, (SWA_QB, win), 0)
            kpos = start + lax.broadcasted_iota(jnp.int32, (SWA_QB, win), 1)
            band1 = jnp.abs(kpos - qpos) <= SWA_WINDOW
            band.append(jnp.concatenate([band1] * 4, axis=0))
        units = [(sub, g) for sub in range(SWA_NQ) for g in range(SWA_KV_HEADS)]
        rows = [slice(sub * SWA_QB, (sub + 1) * SWA_QB) for sub in range(SWA_NQ)]
        qg = [jnp.concatenate([q_ref[rows[sub], (4 * g + i) * LANES:(4 * g + i + 1) * LANES] for i in range(4)],
                              axis=0) for sub, g in units]
        sk = [c1 * jnp.concatenate(
            [jnp.broadcast_to(sink_ref[4 * g + i:4 * g + i + 1, 0:1], (SWA_QB, 1)) for i in range(4)], axis=0)
            for _, g in units]
        t_c = [_dot_nt(q, kc) * c1 for q in qg]
        m = [jnp.maximum(jnp.max(t, axis=-1, keepdims=True), s) for t, s in zip(t_c, sk)]
        if windowed:
            t_w = [jnp.where(band[sub], _dot_nt(q, kw[sub]) * c1, -jnp.inf) for (sub, _), q in zip(units, qg)]
            m = [jnp.maximum(mm, jnp.max(t, axis=-1, keepdims=True)) for mm, t in zip(m, t_w)]
        e_c = [jnp.exp2(t - mm) for t, mm in zip(t_c, m)]
        l = [jnp.sum(e, axis=-1, keepdims=True) + jnp.exp2(s - mm) for e, s, mm in zip(e_c, sk, m)]
        acc = [_dot(e.astype(BF16), vc) for e in e_c]
        if windowed:
            e_w = [jnp.exp2(t - mm) for t, mm in zip(t_w, m)]
            l = [ll + jnp.sum(e, axis=-1, keepdims=True) for ll, e in zip(l, e_w)]
            acc = [a + _dot(e.astype(BF16), vw[sub]) for a, e, (sub, _) in zip(acc, e_w, units)]
        parts = [a / ll for a, ll in zip(acc, l)]
        for sub in range(SWA_NQ):
            finish(rows[sub], parts[SWA_KV_HEADS * sub:SWA_KV_HEADS * (sub + 1)])

    @pl.when(j < n_ctx_steps)
    def _():
        blocks([None] * SWA_NQ)

    @pl.when(j >= n_ctx_steps)
    def _():
        blocks([(j - n_ctx_steps) * SWA_NQ + sub for sub in range(SWA_NQ)])


def _swa_attention(q, k, v, sink, n_batch, t_ctx, seq):
    rows = t_ctx + seq
    qrows = SWA_QB * SWA_NQ
    assert t_ctx % qrows == 0 and seq % qrows == 0
    nqb = rows // qrows
    sink_tab = jnp.broadcast_to(sink.astype(F32)[:, None], (SWA_HEADS, LANES))
    return pl.pallas_call(
        functools.partial(_swa_body, t_ctx=t_ctx, seq=seq),
        out_shape=jax.ShapeDtypeStruct((n_batch * rows, BRANCH_W), BF16),
        grid=(n_batch, nqb),
        in_specs=[pl.BlockSpec((qrows, 1024), lambda b, j: (b * nqb + j, 0)),
                  pl.BlockSpec((rows, LANES), lambda b, j: (b, 0)),
                  pl.BlockSpec((rows, LANES), lambda b, j: (b, 0)),
                  pl.BlockSpec((SWA_HEADS, LANES), lambda b, j: (0, 0))],
        out_specs=pl.BlockSpec((qrows, BRANCH_W), lambda b, j: (b * nqb + j, 0)),
        compiler_params=_cparams(2),
        name="swa_attention",
    )(q, k, v, sink_tab)


MERGE_UNITS = 2


def _merge_body(*refs):
    U = MERGE_UNITS
    row_refs = [refs[6 * u:6 * (u + 1)] for u in range(U)]
    mod_refs = refs[6 * U:7 * U]
    g1_ref, g2_ref, wg_ref, dng_ref, g512_ref, wb_ref, wo_ref, rw_ref, rb_ref = refs[7 * U:7 * U + 9]
    xo_ref, h2_ref, lg_ref = refs[7 * U + 9:]
    rows = [slice(u * TM, (u + 1) * TM) for u in range(U)]
    xs = [r[0][...] for r in row_refs]
    hs = [(_rms(x, g1_ref[...]) * (1.0 + m[0, 0, 1:2, :]) + m[0, 0, 0:1, :]).astype(BF16)
          for x, m in zip(xs, mod_refs)]
    ods = [r[3][...] + r[4][...] for r in row_refs]
    sqs = [_split_bf16(od * od) for od in ods]
    mss = [(_dot(hi, g512_ref[...]) + _dot(lo, g512_ref[...])) * (1.0 / DN_HEAD_DIM) for hi, lo in sqs]
    ocs = [(od * lax.rsqrt(ms + NORM_EPS) * dng_ref[...] * (r[5][...] * jax.nn.sigmoid(r[5][...]))).astype(BF16)
           for od, ms, r in zip(ods, mss, row_refs)]
    ys = [None] * U
    for i in range(N_BRANCH):
        gates = [jax.nn.sigmoid(_dot(h, wg_ref[:, i * D_MODEL:(i + 1) * D_MODEL])) for h in hs]
        branch = [(r[1][...], r[2][...], oc)[i] for r, oc in zip(row_refs, ocs)]
        yis = [g * _dot(o, wb_ref[i]) for g, o in zip(gates, branch)]
        ys = [yi if y is None else y + yi for y, yi in zip(ys, yis)]
    xns = [x + m[0, 0, 2:3, :] * _dot(y.astype(BF16), wo_ref[...]) for x, m, y in zip(xs, mod_refs, ys)]
    h2s = [_rms(xn, g2_ref[...]) * (1.0 + m[0, 0, 4:5, :]) + m[0, 0, 3:4, :] for xn, m in zip(xns, mod_refs)]
    lgs = [_dot3_nt(rw_ref[...], h2) + rb_ref[...] for h2 in h2s]
    for r, xn, h2, lg in zip(rows, xns, h2s, lgs):
        xo_ref[r, :] = xn
        h2_ref[r, :] = h2.astype(BF16)
        lg_ref[:, r] = lg


def _merge(x, modtab, g1, g2, wg, oa, ob, dn_f, dn_b, dz, dn_g, g512, wb, wo, rwt, rb, nb_per_batch, latent_only):
    U = MERGE_UNITS
    if latent_only:
        nlat = nb_per_batch - 1
        n = x.shape[0] // nb_per_batch * nlat
        src = lambda u: (u // nlat) * nb_per_batch + u % nlat + 1
        mod_of = lambda u: (u // nlat, 1, 0, 0)
    else:
        n = x.shape[0]
        src = lambda u: u
        mod_of = lambda u: (u // nb_per_batch, jnp.minimum(u % nb_per_batch, 1), 0, 0)
    step = TM * U
    assert n % step == 0
    orow = lambda i: (i, 0)
    full2 = lambda i: (0, 0)
    row_specs, row_args = [], []
    for u in range(U):
        rowu = lambda i, u=u: (src(i * U + u), 0)
        row_specs += [pl.BlockSpec((TM, D_MODEL), rowu)] + [pl.BlockSpec((TM, BRANCH_W), rowu)] * 5
        row_args += [x, oa, ob, dn_f, dn_b, dz]
    mod_specs = [pl.BlockSpec((1, 1, 6, D_MODEL), lambda i, u=u: mod_of(i * U + u)) for u in range(U)]
    return pl.pallas_call(
        _merge_body,
        out_shape=[jax.ShapeDtypeStruct((n, D_MODEL), F32), jax.ShapeDtypeStruct((n, D_MODEL), BF16),
                   jax.ShapeDtypeStruct((N_EXPERTS, n), F32)],
        grid=(n // step,),
        in_specs=row_specs + mod_specs
        + [pl.BlockSpec((1, D_MODEL), full2), pl.BlockSpec((1, D_MODEL), full2),
           pl.BlockSpec((D_MODEL, N_BRANCH * D_MODEL), full2), pl.BlockSpec((1, BRANCH_W), full2),
           pl.BlockSpec((BRANCH_W, BRANCH_W), full2),
           pl.BlockSpec((N_BRANCH, BRANCH_W, D_MODEL), lambda i: (0, 0, 0)),
           pl.BlockSpec((D_MODEL, D_MODEL), full2),
           pl.BlockSpec((N_EXPERTS, D_MODEL), full2), pl.BlockSpec((N_EXPERTS, 1), full2)],
        out_specs=[pl.BlockSpec((step, D_MODEL), orow), pl.BlockSpec((step, D_MODEL), orow),
                   pl.BlockSpec((N_EXPERTS, step), lambda i: (0, i))],
        compiler_params=_cparams(),
        name="merge",
    )(*row_args, *([modtab] * U), g1, g2, wg, dn_g, g512, wb, wo, rwt, rb)


ROUTE_TB = 1024


def _router_body(lg_ref, tri_ref, e_ref, gate_ref, pos_ref, cnt_ref, run_ref):
    @pl.when(pl.program_id(0) == 0)
    def _():
        run_ref[...] = jnp.zeros_like(run_ref)

    lg = lg_ref[...]
    eid = lax.broadcasted_iota(jnp.int32, lg.shape, 0)
    work = lg
    vals, idxs = [], []
    sel = jnp.zeros(lg.shape, F32)
    for _ in range(TOP_K):
        m = jnp.max(work, axis=0, keepdims=True)
        idx = jnp.min(jnp.where(work == m, eid, N_EXPERTS), axis=0, keepdims=True)
        hit = eid == idx
        sel = jnp.where(hit, 1.0, sel)
        work = jnp.where(hit, -jnp.inf, work)
        vals.append(m)
        idxs.append(idx)
    ex = [jnp.exp(v - vals[0]) for v in vals]
    den = ex[0] + ex[1] + ex[2] + ex[3]
    before = _dot(sel.astype(BF16), tri_ref[...]) + run_ref[:, 0:1]
    for k in range(TOP_K):
        e_ref[k:k + 1, :] = idxs[k]
        gate_ref[k:k + 1, :] = ex[k] / den
        pos_ref[k:k + 1, :] = jnp.sum(jnp.where(eid == idxs[k], before, 0.0), axis=0,
                                      keepdims=True).astype(jnp.int32)
    run_ref[...] = run_ref[...] + jnp.sum(sel, axis=1, keepdims=True)
    cnt_ref[...] = run_ref[...]


def _router(logits_t):
    n = logits_t.shape[1]
    tri = (jnp.arange(ROUTE_TB)[:, None] < jnp.arange(ROUTE_TB)[None, :]).astype(BF16)
    blk = lambda i: (0, i)
    return pl.pallas_call(
        _router_body,
        out_shape=[jax.ShapeDtypeStruct((TOP_K, n), jnp.int32), jax.ShapeDtypeStruct((TOP_K, n), F32),
                   jax.ShapeDtypeStruct((TOP_K, n), jnp.int32), jax.ShapeDtypeStruct((N_EXPERTS, LANES), F32)],
        grid=(n // ROUTE_TB,),
        in_specs=[pl.BlockSpec((N_EXPERTS, ROUTE_TB), blk), pl.BlockSpec((ROUTE_TB, ROUTE_TB), lambda i: (0, 0))],
        out_specs=[pl.BlockSpec((TOP_K, ROUTE_TB), blk), pl.BlockSpec((TOP_K, ROUTE_TB), blk),
                   pl.BlockSpec((TOP_K, ROUTE_TB), blk), pl.BlockSpec((N_EXPERTS, LANES), lambda i: (0, 0))],
        scratch_shapes=[pltpu.VMEM((N_EXPERTS, LANES), F32)],
        compiler_params=_cparams(),
        name="router",
    )(logits_t, tri)


def _expert_body(ib_ref, ie_ref, lo_ref, hi_ref, x_ref, wgu_ref, bgu_ref, wdn_ref, bdn_ref, perm_ref, *rest):
    o_ref, wgu_s, wdn_s = rest[-3:]
    i = pl.program_id(0)
    prev = jnp.maximum(i - 1, 0)

    @pl.when((i == 0) | (ie_ref[i] != ie_ref[prev]))
    def _():
        for m in range(D_EXPERT // LANES):
            t = _dot(wgu_ref[0, 0, :, m * 2 * LANES:(m + 1) * 2 * LANES].astype(BF16), perm_ref[...])
            wgu_s[:, m * LANES:(m + 1) * LANES] = t[:, 0:LANES].astype(BF16)
            wgu_s[:, D_EXPERT + m * LANES:D_EXPERT + (m + 1) * LANES] = t[:, LANES:2 * LANES].astype(BF16)
        wdn_s[...] = wdn_ref[0, 0].astype(BF16)

    lo, hi = lo_ref[i], hi_ref[i]

    @pl.when(hi > lo)
    def _():
        gu = _dot(x_ref[...], wgu_s[...]) + bgu_ref[0]
        g_ = jnp.minimum(gu[:, :D_EXPERT], SWIGLU_LIMIT)
        u_ = jnp.clip(gu[:, D_EXPERT:], -SWIGLU_LIMIT, SWIGLU_LIMIT)
        act = (u_ + 1.0) * (g_ * jax.nn.sigmoid(SWIGLU_ALPHA * g_))
        y = _dot(act.astype(BF16), wdn_s[...]) + bdn_ref[0]
        first = (i == 0) | (ib_ref[i] != ib_ref[prev])
        row = lax.broadcasted_iota(jnp.int32, (MOE_BLK, 1), 0)
        o_ref[...] = jnp.where(first | ((row >= lo) & (row < hi)), y, o_ref[...])


def _experts(items, xs, w_gu, bgu, w_dn, bdn, layer, block0, yb_prev):
    n_items = items[0].shape[0]
    perm = np.zeros((2 * LANES, 2 * LANES), np.float32)
    j = np.arange(LANES)
    perm[2 * j, j] = 1.0
    perm[2 * j + 1, LANES + j] = 1.0
    in_specs = [pl.BlockSpec((MOE_BLK, D_MODEL), lambda i, ib, ie, lo, hi: (ib[i] - block0, 0)),
                pl.BlockSpec((1, 1, D_MODEL, 2 * D_EXPERT), lambda i, ib, ie, lo, hi: (layer, ie[i], 0, 0)),
                pl.BlockSpec((1, 1, 2 * D_EXPERT), lambda i, ib, ie, lo, hi: (ie[i], 0, 0)),
                pl.BlockSpec((1, 1, D_EXPERT, D_MODEL), lambda i, ib, ie, lo, hi: (layer, ie[i], 0, 0)),
                pl.BlockSpec((1, 1, D_MODEL), lambda i, ib, ie, lo, hi: (ie[i], 0, 0)),
                pl.BlockSpec((2 * LANES, 2 * LANES), lambda i, ib, ie, lo, hi: (0, 0)),
                pl.BlockSpec(memory_space=pl.ANY)]
    args = [*items, xs, w_gu, bgu, w_dn, bdn, jnp.asarray(perm, BF16), yb_prev]
    aliases = {len(args) - 1: 0}
    return pl.pallas_call(
        _expert_body,
        out_shape=jax.ShapeDtypeStruct(yb_prev.shape, F32),
        grid_spec=pltpu.PrefetchScalarGridSpec(
            num_scalar_prefetch=4,
            grid=(n_items,),
            in_specs=in_specs,
            out_specs=pl.BlockSpec((MOE_BLK, D_MODEL), lambda i, ib, ie, lo, hi: (ib[i], 0)),
            scratch_shapes=[pltpu.VMEM((D_MODEL, 2 * D_EXPERT), BF16), pltpu.VMEM((D_EXPERT, D_MODEL), BF16)]),
        input_output_aliases=aliases,
        compiler_params=_cparams(),
        name="experts",
    )(*args)


FILL_ROWS = 2048


def _zero_fill_body(o_ref):
    o_ref[...] = jnp.zeros_like(o_ref)


def _zero_rows(n_rows):
    assert n_rows % FILL_ROWS == 0
    return pl.pallas_call(
        _zero_fill_body,
        out_shape=jax.ShapeDtypeStruct((n_rows, D_MODEL), F32),
        grid=(n_rows // FILL_ROWS,),
        out_specs=pl.BlockSpec((FILL_ROWS, D_MODEL), lambda i: (i, 0)),
        compiler_params=_cparams(),
        name="zero_fill",
    )()


def _lookup(table, idx):
    ids = jnp.arange(table.shape[0], dtype=jnp.int32)
    return jnp.sum(jnp.where(idx[..., None] == ids, table, 0), axis=-1)


def _work_items(cnt_start, cnt_end, block0, n_blk):
    n_items = n_blk + N_EXPERTS - 1
    b1 = block0 + n_blk
    first = jnp.maximum(cnt_start // MOE_BLK, block0)
    last = jnp.minimum((cnt_end - 1) // MOE_BLK, b1 - 1)
    n_e = jnp.where(cnt_end > cnt_start, jnp.maximum(last - first + 1, 0), 0)
    item_end = jnp.cumsum(n_e)
    item_start = item_end - n_e
    total = item_end[-1]
    ii = jnp.arange(n_items, dtype=jnp.int32)
    valid = ii < total
    e_i = jnp.sum((item_end[None, :] <= jnp.minimum(ii, total - 1)[:, None]).astype(jnp.int32), axis=1)
    blk = jnp.where(valid, _lookup(first, e_i) + ii - _lookup(item_start, e_i), b1 - 1)
    lo = jnp.clip(_lookup(cnt_start, e_i) - blk * MOE_BLK, 0, MOE_BLK)
    hi = jnp.clip(_lookup(cnt_end, e_i) - blk * MOE_BLK, 0, MOE_BLK)
    lo = jnp.where(valid, lo, 0)
    hi = jnp.where(valid, hi, 0)
    return blk.astype(jnp.int32), e_i.astype(jnp.int32), lo.astype(jnp.int32), hi.astype(jnp.int32)


def _moe(h2, logits_t, w_gu, bgu, w_dn, bdn, layer, yb):
    n = h2.shape[0]
    a = n * TOP_K
    assert a % (MOE_BLK * MOE_CHUNKS) == 0 and yb.shape[0] >= a
    top_e, gate, pos, cnt = _router(logits_t)
    counts = cnt[:, 0].astype(jnp.int32)
    cnt_end = jnp.cumsum(counts)
    cnt_start = cnt_end - counts
    slot = pos + _lookup(cnt_start, top_e)
    tok = jnp.broadcast_to(jnp.arange(n, dtype=jnp.int32)[None, :], (TOP_K, n))
    _, slot_tok = lax.sort_key_val(slot.reshape(-1), tok.reshape(-1))
    n_blk = a // MOE_BLK // MOE_CHUNKS
    rows_c = n_blk * MOE_BLK
    for c in range(MOE_CHUNKS):
        xs = h2.at[slot_tok[c * rows_c:(c + 1) * rows_c]].get(mode="promise_in_bounds")
        items = _work_items(cnt_start, cnt_end, c * n_blk, n_blk)
        yb = _experts(items, xs, w_gu, bgu, w_dn, bdn, layer, c * n_blk, yb)
    ys = [yb.at[slot[k]].get(mode="promise_in_bounds") for k in range(TOP_K)]
    return ys, gate.T, yb


DN_GW = 256
HALO = 8
DN_PAIR = 4
_M_EYE, _M_BLK16, _M_OFF32, _M_OFF64, _M_INCL, _M_STRICT = 0, 1, 2, 3, 4, 6


def _dn_masks():
    row = np.arange(DN_CHUNK)[:, None]
    col = np.arange(DN_GW)[None, :] % DN_CHUNK
    b16 = (row // 16) == (col // 16)
    b32 = (row // 32) == (col // 32)
    m = np.stack([row == col, b16, b32 & ~b16, ~b32, col <= row, col >= row, col < row, col > row])
    hid = np.arange(DN_GW) // DN_HEAD_DIM
    return jnp.asarray(m, F32), jnp.asarray(hid[:, None] == hid[None, :], BF16)


def _split3(a):
    hi = a.astype(BF16)
    r = a - hi.astype(F32)
    lo = r.astype(BF16)
    lo2 = (r - lo.astype(F32)).astype(BF16)
    return hi, lo, lo2


def _bd(x, bdm_ref):
    xb = x.astype(BF16)
    return jnp.concatenate([xb, xb, xb, xb], axis=0) * bdm_ref[...]


def _mm(a, wbd):
    return _dot(a.astype(BF16), wbd)


def _tri_inverse(l_mats, m_ref, bdm_ref):
    half = DN_CHUNK
    bd = lambda v: _bd(v, bdm_ref)
    lds = [l * m_ref[_M_BLK16] for l in l_mats]
    ps = [_mm(ld, bd(ld)) for ld in lds]
    xs = [m_ref[_M_EYE] - ld for ld in lds]
    for _ in range(2):
        rs = [_mm(jnp.concatenate([p, x], axis=0), bd(p)) for p, x in zip(ps, xs)]
        ps = [r[0:half] for r in rs]
        xs = [x + r[half:2 * half] for x, r in zip(xs, rs)]
    xs = [x + _mm(x, bd(p)) for x, p in zip(xs, ps)]
    ts = [_mm(x, bd(l * m_ref[_M_OFF32])) for x, l in zip(xs, l_mats)]
    xs = [x - _mm(t, bd(x)) for x, t in zip(xs, ts)]
    ts = [_mm(x, bd(l * m_ref[_M_OFF64])) for x, l in zip(xs, l_mats)]
    return [x - _mm(t, bd(x)) for x, t in zip(xs, ts)]


def _dn_prep_body(cur_ref, prev_ref, next_ref, cw_ref, dg_ref, gc_ref, g512_ref, e_ref, tri_ref, m_ref, bdm_ref,
                  u_ref, w_ref, kd_ref, qd_ref, aq_ref, gl_ref,
                  ext_ref, q_s, k_s, v_s, gx_s, bx_s):
    j = pl.program_id(1)
    nbp = pl.num_programs(1)
    C = DN_CHUNK
    ext_ref[HALO:HALO + TM, :] = cur_ref[...]
    ext_ref[0:HALO, :] = jnp.where(j >= 2, prev_ref[...], 0.0)
    ext_ref[HALO + TM:2 * HALO + TM, :] = jnp.where((j >= 1) & (j < nbp - 1), next_ref[...], 0.0)
    y = None
    for t in range(DN_CONV):
        o = HALO - DN_CONV // 2 + t
        term = cw_ref[t:t + 1, :] * ext_ref[o:o + TM, :]
        y = term if y is None else y + term
    y = y * jax.nn.sigmoid(y)

    g512 = g512_ref[...]

    def head_sumsq(x):
        hi, lo = _split_bf16(x * x)
        return _dot(hi, g512) + _dot(lo, g512)

    q = y[:, 0:512]
    k = y[:, 512:1024]
    q_s[...] = q * lax.rsqrt(head_sumsq(q) + NORM_EPS) * (DN_HEAD_DIM ** -0.5)
    k_s[...] = k * lax.rsqrt(head_sumsq(k) + NORM_EPS)
    v_s[...] = y[:, 1024:1536]

    dg = dg_ref[...]
    beta_all = jax.nn.sigmoid(dg)
    z = dg + gc_ref[1:2, :]
    g_all = gc_ref[0:1, :] * (jnp.maximum(z, 0.0) + jnp.log(1.0 + jnp.exp(-jnp.abs(z))))
    bh, bl = _split_bf16(beta_all)
    bcat = jnp.concatenate([bh, bl], axis=0)
    gparts = jnp.concatenate(_split3(g_all), axis=1)
    for d in range(2):
        cs = _dot(tri_ref[d], gparts)
        gcum = cs[:, 0:128] + cs[:, 128:256] + cs[:, 256:384]
        gcat = jnp.concatenate(_split3(gcum), axis=0)
        for grp in range(2):
            eg = _dot(gcat, e_ref[4 + 2 * d + grp])
            gx_s[d, grp] = eg[0:TM] + eg[TM:2 * TM] + eg[2 * TM:3 * TM]
            eb = _dot(bcat, e_ref[2 * d + grp])
            bx_s[d, grp] = eb[0:TM] + eb[TM:2 * TM]

    def chunk_pair(it, carry):
        units = []
        l_mats = []
        for ci in range(DN_PAIR):
            cc = it * DN_PAIR + ci
            rows = pl.ds(pl.multiple_of(cc * C, C), C)
            for grp in range(2):
                cols = slice(grp * DN_GW, (grp + 1) * DN_GW)
                kg, qg = k_s[rows, cols], q_s[rows, cols]
                kb = [kg * bx_s[d, grp, rows, :] for d in range(2)]
                raw = _dot_nt(jnp.concatenate([kb[0], kb[1], qg], axis=0).astype(BF16), _bd(kg, bdm_ref))
                for d in range(2):
                    gx = gx_s[d, grp, rows, :]
                    rvec = jnp.sum(gx * m_ref[_M_EYE], axis=0, keepdims=True)
                    dec = jnp.exp(jnp.minimum(gx - rvec, 0.0)) * m_ref[_M_INCL + d]
                    l_mats.append(raw[C * d:C * (d + 1)] * dec * m_ref[_M_STRICT + d])
                    glast = gx[C - 1:C, :] if d == 0 else gx[0:1, :]
                    kd_ref[d, rows, cols] = (kg * jnp.exp(glast - gx)).astype(BF16)
                    qd_ref[d, rows, cols] = (qg * jnp.exp(gx)).astype(BF16)
                    aq_ref[d, rows, cols] = (raw[2 * C:3 * C] * dec).astype(BF16)
                    gl_ref[d, pl.ds(cc, 1), :, cols] = jnp.exp(glast).reshape(1, 1, DN_GW)
                    units.append((rows, cols, d, grp))
        tinvs = _tri_inverse(l_mats, m_ref, bdm_ref)
        for (rows, cols, d, grp), tinv in zip(units, tinvs):
            bx = bx_s[d, grp, rows, :]
            u_ref[d, rows, cols] = _mm(tinv, _bd(v_s[rows, cols] * bx, bdm_ref))
            w_ref[d, rows, cols] = _mm(tinv, _bd(k_s[rows, cols] * bx * jnp.exp(gx_s[d, grp, rows, :]), bdm_ref)
                                       ).astype(BF16)
        return carry

    lax.fori_loop(0, TM // C // DN_PAIR, chunk_pair, 0)


def _dn_prep(dqkv, dg, conv_w, a_log, dt_bias, n_batch, rows):
    n = dqkv.shape[0]
    nbp = rows // TM
    hb = TM // HALO
    H = DN_HEADS
    cw = jnp.zeros((8, 1536), F32).at[0:DN_CONV].set(conv_w)
    gc = jnp.zeros((8, LANES), F32)
    gc = gc.at[0, 2 * H:4 * H].set(-jnp.exp(a_log.reshape(-1))).at[1, 2 * H:4 * H].set(dt_bias.reshape(-1))
    hid = np.arange(512) // DN_HEAD_DIM
    g512 = jnp.asarray(hid[:, None] == hid[None, :], BF16)
    e = np.zeros((8, LANES, DN_GW), np.float32)
    for kind in range(2):
        for d in range(2):
            for grp in range(2):
                for h in range(4):
                    e[4 * kind + 2 * d + grp, 16 * kind + 8 * d + 4 * grp + h, 64 * h:64 * (h + 1)] = 1.0
    e = jnp.asarray(e, BF16)
    t = np.arange(TM)
    same = (t[:, None] // DN_CHUNK) == (t[None, :] // DN_CHUNK)
    tri = jnp.asarray(np.stack([same & (t[None, :] <= t[:, None]), same & (t[None, :] >= t[:, None])]), BF16)
    masks, bdm = _dn_masks()
    blk = lambda b, j: (b * nbp + j, 0)
    full2 = lambda b, j: (0, 0)
    full3 = lambda b, j: (0, 0, 0)
    dblk = lambda b, j: (0, b * nbp + j, 0)
    nlast = n // HALO - 1
    outs = pl.pallas_call(
        _dn_prep_body,
        out_shape=[jax.ShapeDtypeStruct((2, n, 512), F32)] + [jax.ShapeDtypeStruct((2, n, 512), BF16)] * 4
        + [jax.ShapeDtypeStruct((2, n // DN_CHUNK, 1, 512), F32)],
        grid=(n_batch, nbp),
        in_specs=[pl.BlockSpec((TM, 1536), blk),
                  pl.BlockSpec((HALO, 1536), lambda b, j: (jnp.maximum((b * nbp + j) * hb - 1, 0), 0)),
                  pl.BlockSpec((HALO, 1536), lambda b, j: (jnp.minimum((b * nbp + j + 1) * hb, nlast), 0)),
                  pl.BlockSpec((8, 1536), full2), pl.BlockSpec((TM, LANES), blk), pl.BlockSpec((8, LANES), full2),
                  pl.BlockSpec((512, 512), full2), pl.BlockSpec((8, LANES, DN_GW), full3),
                  pl.BlockSpec((2, TM, TM), full3), pl.BlockSpec((8, DN_CHUNK, DN_GW), full3),
                  pl.BlockSpec((DN_GW, DN_GW), full2)],
        out_specs=[pl.BlockSpec((2, TM, 512), dblk)] * 5
        + [pl.BlockSpec((2, TM // DN_CHUNK, 1, 512), lambda b, j: (0, b * nbp + j, 0, 0))],
        scratch_shapes=[pltpu.VMEM((TM + 2 * HALO, 1536), F32), pltpu.VMEM((TM, 512), F32),
                        pltpu.VMEM((TM, 512), F32), pltpu.VMEM((TM, 512), F32),
                        pltpu.VMEM((2, 2, TM, DN_GW), F32), pltpu.VMEM((2, 2, TM, DN_GW), F32)],
        compiler_params=_cparams(2),
        name="dn_prep",
    )(dqkv, dqkv, dqkv, cw, dg, gc, g512, e, tri, masks, bdm)
    return outs, g512


DN_SCAN_NB = 8


def _dn_scan_body(uf, wf, kf, qf, af, gf, ub, wb, kb, qb, ab, gb, bdm_ref, of_ref, ob_ref, s_ref):
    @pl.when(pl.program_id(1) == 0)
    def _():
        s_ref[...] = jnp.zeros_like(s_ref)

    same = bdm_ref[...] != 0
    dirs = ((uf, wf, kf, qf, af, gf, of_ref), (ub, wb, kb, qb, ab, gb, ob_ref))
    units = [(bb, d, grp) for bb in range(DN_SCAN_NB) for d in range(2) for grp in range(2)]
    cols = [slice(grp * DN_GW, (grp + 1) * DN_GW) for grp in range(2)]
    s = [s_ref[bb, d, grp] for bb, d, grp in units]
    sq = [_dot(jnp.concatenate([dirs[d][1][0, bb, :, cols[grp]], dirs[d][3][0, bb, :, cols[grp]]], axis=0),
               st.astype(BF16)) for (bb, d, grp), st in zip(units, s)]
    vnew = [dirs[d][0][0, bb, :, cols[grp]] - r[0:DN_CHUNK] for (bb, d, grp), r in zip(units, sq)]
    intra = [_dot(dirs[d][4][0, bb, :, cols[grp]], _bd(v, bdm_ref)) for (bb, d, grp), v in zip(units, vnew)]
    upd = [lax.dot_general(dirs[d][2][0, bb, :, cols[grp]], v.astype(BF16), (((0,), (0,)), ((), ())),
                           preferred_element_type=F32) for (bb, d, grp), v in zip(units, vnew)]
    for (bb, d, grp), st, r, a, up in zip(units, s, sq, intra, upd):
        dirs[d][6][bb, :, cols[grp]] = r[DN_CHUNK:2 * DN_CHUNK] + a
        s_ref[bb, d, grp] = st * dirs[d][5][0, bb, 0, :, cols[grp]] + jnp.where(same, up, 0.0)


def _dn_scan(u, w, kd, qd, aq, gl, n_batch, rows, t_ctx):
    nch = rows // DN_CHUNK
    nctx = t_ctx // DN_CHUNK
    assert n_batch % DN_SCAN_NB == 0

    def cb(c):
        return jnp.where(c < nctx, nctx - 1 - c, nch - 1 - (c - nctx))

    per_batch = lambda a: a.reshape(2, n_batch, rows, 512)
    u, w, kd, qd, aq = map(per_batch, (u, w, kd, qd, aq))
    gl = gl.reshape(2, n_batch, nch, 1, 512)
    blk = (1, DN_SCAN_NB, DN_CHUNK, 512)
    gblk = (1, DN_SCAN_NB, 1, 1, 512)
    specs = []
    for d, ch in ((0, lambda c: c), (1, cb)):
        specs += [pl.BlockSpec(blk, lambda b, c, d=d, ch=ch: (d, b, ch(c), 0))] * 5
        specs += [pl.BlockSpec(gblk, lambda b, c, d=d, ch=ch: (d, b, ch(c), 0, 0))]
    specs.append(pl.BlockSpec((DN_GW, DN_GW), lambda b, c: (0, 0)))
    o_f, o_b = pl.pallas_call(
        _dn_scan_body,
        out_shape=[jax.ShapeDtypeStruct((n_batch, rows, 512), F32)] * 2,
        grid=(n_batch // DN_SCAN_NB, nch),
        in_specs=specs,
        out_specs=[pl.BlockSpec((DN_SCAN_NB, DN_CHUNK, 512), lambda b, c: (b, c, 0)),
                   pl.BlockSpec((DN_SCAN_NB, DN_CHUNK, 512), lambda b, c: (b, cb(c), 0))],
        scratch_shapes=[pltpu.VMEM((DN_SCAN_NB, 2, 2, DN_GW, DN_GW), F32)],
        compiler_params=_cparams(2),
        name="dn_scan",
    )(u, w, kd, qd, aq, gl, u, w, kd, qd, aq, gl, _dn_masks()[1])
    return o_f.reshape(n_batch * rows, 512), o_b.reshape(n_batch * rows, 512)


def _deltanet(dqkv, dg, conv_w, a_log, dt_bias, n_batch, rows, t_ctx):
    (u, w, kd, qd, aq, gl), g512 = _dn_prep(dqkv, dg, conv_w, a_log, dt_bias, n_batch, rows)
    o_f, o_b = _dn_scan(u, w, kd, qd, aq, gl, n_batch, rows, t_ctx)
    return o_f, o_b, g512


def _combine_body(x_ref, y0_ref, y1_ref, y2_ref, y3_ref, gate_ref, mod_ref, fg_ref, o_ref, *, final):
    gate = gate_ref[...]
    y = None
    for k, y_ref in enumerate((y0_ref, y1_ref, y2_ref, y3_ref)):
        yk = y_ref[...] * gate[:, k:k + 1]
        y = yk if y is None else y + yk
    xo = x_ref[...] + mod_ref[0, 0, 5:6, :] * y
    o_ref[...] = _rms(xo, fg_ref[...]) if final else xo


def _combine(x, ys, gate_rows, modtab, final_g, nb_per_batch, latent_only):
    n = x.shape[0]
    if latent_only:
        nlat = nb_per_batch - 1
        mod_idx = lambda i: (i // nlat, 1, 0, 0)
    else:
        mod_idx = lambda i: (i // nb_per_batch, jnp.minimum(i % nb_per_batch, 1), 0, 0)
    row = lambda i: (i, 0)
    return pl.pallas_call(
        functools.partial(_combine_body, final=latent_only),
        out_shape=jax.ShapeDtypeStruct((n, D_MODEL), F32),
        grid=(n // TM,),
        in_specs=[pl.BlockSpec((TM, D_MODEL), row)] * 5
        + [pl.BlockSpec((TM, TOP_K), row), pl.BlockSpec((1, 1, 6, D_MODEL), mod_idx),
           pl.BlockSpec((1, D_MODEL), lambda i: (0, 0))],
        out_specs=pl.BlockSpec((TM, D_MODEL), row),
        compiler_params=_cparams(),
        name="combine",
    )(x, *ys, gate_rows, modtab, final_g)


def _rope_tables(seq, t_ctx, rot_dim):
    rows = seq // GRID_W
    row = jnp.broadcast_to(jnp.arange(rows)[:, None], (rows, GRID_W)).reshape(-1).astype(F32)
    col = jnp.broadcast_to(jnp.arange(GRID_W)[None, :], (rows, GRID_W)).reshape(-1).astype(F32)
    n_freq = rot_dim // 4
    inv_freq = ROPE_THETA ** (-jnp.arange(n_freq, dtype=F32) / n_freq)
    ang = jnp.concatenate([row[:, None] * inv_freq, col[:, None] * inv_freq], axis=-1)
    cos, sin = jnp.cos(ang), jnp.sin(ang)
    zero = jnp.zeros_like(sin)
    c = jnp.concatenate([cos, cos], axis=-1)
    sa = jnp.concatenate([-sin, zero], axis=-1)
    sb = jnp.concatenate([zero, sin], axis=-1)
    ident = jnp.stack([jnp.ones((t_ctx, rot_dim), F32), jnp.zeros((t_ctx, rot_dim), F32),
                       jnp.zeros((t_ctx, rot_dim), F32)])
    tab = jnp.concatenate([ident, jnp.stack([c, sa, sb])], axis=1)
    return jnp.tile(tab, (1, 1, LANES // rot_dim))


def _layer_weights(w_in, w_q_up, w_kv_up, w_branch, w_out, router_w, router_b, b_gu, b_dn):
    sizes = (MLA_Q_RANK, MLA_KV_RANK, MLA_ROPE, 512, 128, 128, 1536, 512, 16, 16, N_BRANCH * D_MODEL)
    offs = np.cumsum((0,) + sizes)
    part = lambda i: w_in[:, offs[i]:offs[i + 1]]
    pad = jnp.zeros((D_MODEL, LANES - 32), F32)
    w_proj = jnp.concatenate(
        [part(0), part(1), part(3), part(4), part(5), part(6), part(7),
         jnp.tile(part(2), (1, LANES // MLA_ROPE)), part(8), part(9), pad], axis=1).astype(BF16)
    w_gate = part(10).astype(BF16)
    hq = MLA_NOPE + MLA_ROPE
    wq = w_q_up.reshape(MLA_Q_RANK, MLA_HEADS, hq)
    wq = jnp.concatenate([wq[:, :, :MLA_NOPE].reshape(MLA_Q_RANK, -1),
                          wq[:, :, MLA_NOPE:].reshape(MLA_Q_RANK, -1)], axis=1).astype(BF16)
    wkv = w_kv_up.reshape(MLA_KV_RANK, MLA_HEADS, MLA_NOPE + MLA_V)
    wkv = jnp.concatenate([wkv[:, :, :MLA_NOPE].reshape(MLA_KV_RANK, -1),
                           wkv[:, :, MLA_NOPE:].reshape(MLA_KV_RANK, -1)], axis=1).astype(BF16)
    bgu = jnp.concatenate([b_gu[:, 0::2], b_gu[:, 1::2]], axis=-1).reshape(N_EXPERTS, 1, 2 * D_EXPERT)
    return dict(w_proj=w_proj, w_gate=w_gate, wq=wq, wkv=wkv, wb=w_branch.astype(BF16),
                wo=w_out.astype(BF16), rwt=router_w.T, rb=router_b.reshape(N_EXPERTS, 1),
                bgu=bgu, bdn=b_dn.reshape(N_EXPERTS, 1, D_MODEL))


def kernel(x, c, ctx, c_ctx, w_mod, b_mod, norm1_g, norm2_g, w_in, mla_q_norm_g, mla_w_q_up,
           mla_kv_norm_g, mla_w_kv_up, swa_sink, dn_conv_w, dn_a_log, dn_dt_bias, dn_norm_g,
           w_branch, w_out, router_w, router_b, exp_w_gu, exp_b_gu, exp_w_dn, exp_b_dn, final_norm_g):
    B, S, D = x.shape
    T = ctx.shape[1]
    R = T + S
    depth = w_mod.shape[0]
    assert D == D_MODEL and T == TM and S % TM == 0 and (B * R) % ROUTE_TB == 0
    nbp = R // TM
    xa = jnp.concatenate([ctx, x], axis=1).reshape(B * R, D)
    rope_s = _rope_tables(S, T, SWA_HEAD_DIM)
    rope_m = _rope_tables(S, T, MLA_ROPE)
    cvec = jnp.concatenate([c, c_ctx[None, :]], axis=0)
    yb = _zero_rows(B * R * TOP_K)
    for l in range(depth):
        w = _layer_weights(w_in[l], mla_w_q_up[l], mla_w_kv_up[l], w_branch[l], w_out[l], router_w[l],
                           router_b[l], exp_b_gu[l], exp_b_dn[l])
        mod = _modulation(cvec, w_mod[l], b_mod[l]).reshape(B + 1, 6, D)
        modtab = jnp.stack([jnp.broadcast_to(mod[B][None], (B, 6, D)), mod[:B]], axis=1)
        g1 = norm1_g[l].reshape(1, D)
        g2 = norm2_g[l].reshape(1, D)
        q_m, k_m, v_m, sq, sk, sv, dqkv, dz, dg = _inproj(
            xa, modtab, g1, w["w_proj"], mla_q_norm_g[l].reshape(1, -1), mla_kv_norm_g[l].reshape(1, -1),
            w["wq"], w["wkv"], rope_s, rope_m, nbp)
        o_a = _mla_attention(q_m, k_m, v_m, B, R)
        o_b = _swa_attention(sq, sk, sv, swa_sink[l], B, T, S)
        dn_f, dn_b, g512 = _deltanet(dqkv, dg, dn_conv_w[l], dn_a_log[l], dn_dt_bias[l], B, R, T)
        last = l == depth - 1
        xn, h2, logits_t = _merge(xa, modtab, g1, g2, w["w_gate"], o_a, o_b, dn_f, dn_b, dz,
                                  jnp.tile(dn_norm_g[l], DN_HEADS).reshape(1, BRANCH_W), g512, w["wb"], w["wo"],
                                  w["rwt"], w["rb"], nbp, last)
        ys, gate_rows, yb = _moe(h2, logits_t, exp_w_gu, w["bgu"], exp_w_dn, w["bdn"], l, yb)
        xa = _combine(xn, ys, gate_rows, modtab, final_norm_g.reshape(1, D), nbp, last)
    return xa.reshape(B, S, D)
```

```python
import functools

import jax
import jax.numpy as jnp
import numpy as np
from jax import lax
from jax.experimental import pallas as pl
from jax.experimental.pallas import tpu as pltpu

F32 = jnp.float32
BF16 = jnp.bfloat16

D_MODEL = 1024
GRID_W = 64
ROPE_THETA = 10000.0
NORM_EPS = 1e-6
MLA_HEADS = 8
MLA_Q_RANK = 384
MLA_KV_RANK = 256
MLA_NOPE = 64
MLA_ROPE = 32
MLA_V = 64
SWA_HEADS = 8
SWA_KV_HEADS = 2
SWA_HEAD_DIM = 64
SWA_WINDOW = 128
DN_HEADS = 8
DN_HEAD_DIM = 64
DN_CONV = 5
DN_CHUNK = 64
N_BRANCH = 3
BRANCH_W = 512
N_EXPERTS = 32
TOP_K = 4
D_EXPERT = 1024
SWIGLU_LIMIT = 7.0
SWIGLU_ALPHA = 1.702
MOE_BLK = 512
MOE_CHUNKS = 4

LANES = 128
LOG2E = 1.4426950408889634
TM = 256
VMEM_LIMIT = 56 * 1024 * 1024

_C_CQ = (0, 384)
_C_CKV = (384, 640)
_C_SQ = (640, 1152)
_C_SK = (1152, 1280)
_C_SV = (1280, 1408)
_C_DQKV = (1408, 2944)
_C_DZ = (2944, 3456)
_C_KRT = (3456, 3584)
_C_DG = (3584, 3712)
_N_PROJ = 3712


def _cparams(n_axes=1):
    return pltpu.CompilerParams(dimension_semantics=("arbitrary",) * n_axes, vmem_limit_bytes=VMEM_LIMIT)


def _dot(a, b):
    return jnp.dot(a, b, preferred_element_type=F32)


def _dot_nt(a, b):
    return lax.dot_general(a, b, (((1,), (1,)), ((), ())), preferred_element_type=F32)


def _split_bf16(a):
    hi = a.astype(BF16)
    lo = (a - hi.astype(F32)).astype(BF16)
    return hi, lo


def _dot3(a, b):
    ah, al = _split_bf16(a)
    bh, bl = _split_bf16(b)
    return _dot(ah, bh) + (_dot(ah, bl) + _dot(al, bh))


def _dot3_nt(a, b):
    ah, al = _split_bf16(a)
    bh, bl = _split_bf16(b)
    return _dot_nt(ah, bh) + (_dot_nt(ah, bl) + _dot_nt(al, bh))


def _rms(x, g):
    return x * lax.rsqrt(jnp.mean(x * x, axis=-1, keepdims=True) + NORM_EPS) * g


def _rope_cols(x, tab_ref, half):
    c, sa, sb = tab_ref[0], tab_ref[1], tab_ref[2]
    return x * c + pltpu.roll(x, LANES - half, axis=1) * sa + pltpu.roll(x, half, axis=1) * sb


def _mod_body(c_ref, w_ref, b_ref, o_ref):
    c = c_ref[...]
    a = c * jax.nn.sigmoid(c)
    o_ref[...] = _dot3(a, w_ref[...]) + b_ref[...]


def _modulation(cvec, w_mod, b_mod):
    m = cvec.shape[0]
    n = w_mod.shape[1]
    tn = 512
    return pl.pallas_call(
        _mod_body,
        out_shape=jax.ShapeDtypeStruct((m, n), F32),
        grid=(n // tn,),
        in_specs=[pl.BlockSpec((m, D_MODEL), lambda j: (0, 0)),
                  pl.BlockSpec((D_MODEL, tn), lambda j: (0, j)),
                  pl.BlockSpec((1, tn), lambda j: (0, j))],
        out_specs=pl.BlockSpec((m, tn), lambda j: (0, j)),
        compiler_params=_cparams(),
        name="modulation",
    )(cvec, w_mod, b_mod.reshape(1, n))


IN_UNITS = 2


def _inproj_body(*refs):
    U = IN_UNITS
    x_ref = refs[0]
    mod_refs = refs[1:1 + U]
    g_ref, w_ref, qg_ref, kvg_ref, wq_ref, wkv_ref = refs[1 + U:7 + U]
    rope_s_refs = refs[7 + U:7 + 2 * U]
    rope_m_refs = refs[7 + 2 * U:7 + 3 * U]
    q_ref, k_ref, v_ref, sq_ref, sk_ref, sv_ref, dqkv_ref, dz_ref, dg_ref = refs[7 + 3 * U:]
    rows = [slice(u * TM, (u + 1) * TM) for u in range(U)]
    hs = [(_rms(x_ref[r, :], g_ref[...]) * (1.0 + m[0, 0, 1:2, :]) + m[0, 0, 0:1, :]).astype(BF16)
          for r, m in zip(rows, mod_refs)]

    def proj(cols):
        return [_dot(h, w_ref[:, cols[0]:cols[1]]) for h in hs]

    for ref, cols in ((dqkv_ref, _C_DQKV), (dz_ref, _C_DZ), (dg_ref, _C_DG)):
        for r, p in zip(rows, proj(cols)):
            ref[r, :] = p
    for r, p in zip(rows, proj(_C_SV)):
        sv_ref[r, :] = p.astype(BF16)
    for r, p, rope_ref in zip(rows, proj(_C_SK), rope_s_refs):
        sk_ref[r, :] = _rope_cols(p, rope_ref, SWA_HEAD_DIM // 2).astype(BF16)
    lane = lax.broadcasted_iota(jnp.int32, (TM, LANES), 1)
    lo = lane < SWA_HEAD_DIM
    for r, sq, rope_ref in zip(rows, proj(_C_SQ), rope_s_refs):
        for c in range(4):
            xr = _rope_cols(sq[:, c * LANES:(c + 1) * LANES], rope_ref, SWA_HEAD_DIM // 2)
            xs = pltpu.roll(xr, SWA_HEAD_DIM, axis=1)
            if c < 2:
                a, b = jnp.where(lo, xr, 0.0), jnp.where(lo, xs, 0.0)
            else:
                a, b = jnp.where(lo, 0.0, xs), jnp.where(lo, 0.0, xr)
            sq_ref[r, (2 * c) * LANES:(2 * c + 1) * LANES] = a.astype(BF16)
            sq_ref[r, (2 * c + 1) * LANES:(2 * c + 2) * LANES] = b.astype(BF16)
    half = MLA_ROPE // 2
    qns = [_rms(p, qg_ref[...]).astype(BF16) for p in proj(_C_CQ)]
    qs = [_dot(qn, wq_ref[...]) for qn in qns]
    kvns = [_rms(p, kvg_ref[...]).astype(BF16) for p in proj(_C_CKV)]
    kvs = [_dot(kvn, wkv_ref[...]) for kvn in kvns]
    krs = [_rope_cols(p, rope_ref, half).astype(BF16)
           for p, rope_ref in zip(proj(_C_KRT), rope_m_refs)]
    for r, q, kv, kr, rope_ref in zip(rows, qs, kvs, krs, rope_m_refs):
        q_ref[r, 0:512] = q[:, 0:512].astype(BF16)
        for c in range(2):
            lo_, hi_ = 512 + c * LANES, 512 + (c + 1) * LANES
            q_ref[r, lo_:hi_] = _rope_cols(q[:, lo_:hi_], rope_ref, half).astype(BF16)
        for c in range(4):
            k_ref[r, (2 * c) * LANES:(2 * c + 1) * LANES] = kv[:, c * LANES:(c + 1) * LANES].astype(BF16)
            k_ref[r, (2 * c + 1) * LANES:(2 * c + 2) * LANES] = kr
        v_ref[r, :] = kv[:, 512:1024].astype(BF16)


def _inproj(x, modtab, g, w, qg, kvg, wq, wkv, rope_s, rope_m, nb_per_batch):
    n = x.shape[0]
    step = TM * IN_UNITS
    assert n % step == 0
    row = lambda i: (i, 0)
    full = lambda i: (0, 0)
    unit = lambda i, u: i * IN_UNITS + u
    widths = [(768, BF16), (1024, BF16), (512, BF16), (1024, BF16), (128, BF16), (128, BF16),
              (1536, F32), (512, F32), (128, F32)]
    mod_specs = [pl.BlockSpec((1, 1, 6, D_MODEL),
                              lambda i, u=u: (unit(i, u) // nb_per_batch,
                                              jnp.minimum(unit(i, u) % nb_per_batch, 1), 0, 0))
                 for u in range(IN_UNITS)]
    rope_specs = [pl.BlockSpec((3, TM, LANES), lambda i, u=u: (0, unit(i, u) % nb_per_batch, 0))
                  for u in range(IN_UNITS)]
    return pl.pallas_call(
        _inproj_body,
        out_shape=[jax.ShapeDtypeStruct((n, wd), dt) for wd, dt in widths],
        grid=(n // step,),
        in_specs=[pl.BlockSpec((step, D_MODEL), row)] + mod_specs
        + [pl.BlockSpec((1, D_MODEL), full), pl.BlockSpec((D_MODEL, _N_PROJ), full),
           pl.BlockSpec((1, MLA_Q_RANK), full), pl.BlockSpec((1, MLA_KV_RANK), full),
           pl.BlockSpec((MLA_Q_RANK, 768), full), pl.BlockSpec((MLA_KV_RANK, 1024), full)]
        + rope_specs + rope_specs,
        out_specs=[pl.BlockSpec((step, wd), row) for wd, _ in widths],
        compiler_params=_cparams(),
        name="inproj",
    )(x, *([modtab] * IN_UNITS), g, w, qg, kvg, wq, wkv, *([rope_s] * IN_UNITS), *([rope_m] * IN_UNITS))


def _mla_attn_body(q_ref, k_ref, v_ref, o_ref, *fill_ref, t_ctx):
    for z_ref in fill_ref:
        z_ref[...] = jnp.zeros_like(z_ref)
    c1 = (MLA_NOPE + MLA_ROPE) ** -0.5 * LOG2E
    lane = lax.broadcasted_iota(jnp.int32, (TM, LANES), 1)
    zero = jnp.zeros((TM, LANES), BF16)

    def attend(nk):
        for c in range(4):
            kc = k_ref[0:nk, c * 2 * LANES:(c + 1) * 2 * LANES]
            vc = v_ref[0:nk, c * LANES:(c + 1) * LANES]
            qn = q_ref[:, c * LANES:(c + 1) * LANES]
            qr = q_ref[:, 512 + (c // 2) * LANES:512 + (c // 2 + 1) * LANES]
            qa = [jnp.concatenate(
                [jnp.where((lane >= 64 * s) & (lane < 64 * (s + 1)), qn, zero),
                 jnp.where((lane >= 32 * ((2 * c + s) % 4)) & (lane < 32 * ((2 * c + s) % 4 + 1)), qr, zero)],
                axis=1) for s in range(2)]
            sc = [_dot_nt(q, kc) * c1 for q in qa]
            m = [jnp.max(t, axis=-1, keepdims=True) for t in sc]
            e = [jnp.exp2(t - mm) for t, mm in zip(sc, m)]
            l = [jnp.sum(t, axis=-1, keepdims=True) for t in e]
            outs = [_dot(t.astype(BF16), vc) / ll for t, ll in zip(e, l)]
            o_ref[:, c * LANES:(c + 1) * LANES] = jnp.where(lane < 64, outs[0], outs[1]).astype(BF16)

    j = pl.program_id(1)

    @pl.when(j == 0)
    def _():
        attend(t_ctx)

    @pl.when(j > 0)
    def _():
        attend(k_ref.shape[0])


def _mla_attention(q, k, v, n_batch, rows, fill_rows=0):
    nbp = rows // TM
    blk = lambda b, j: (b * nbp + j, 0)
    out_shape = [jax.ShapeDtypeStruct((n_batch * rows, BRANCH_W), BF16)]
    out_specs = [pl.BlockSpec((TM, BRANCH_W), blk)]
    if fill_rows:
        assert fill_rows % (n_batch * nbp) == 0
        out_shape.append(jax.ShapeDtypeStruct((fill_rows, D_MODEL), F32))
        out_specs.append(pl.BlockSpec((fill_rows // (n_batch * nbp), D_MODEL), blk))
    return pl.pallas_call(
        functools.partial(_mla_attn_body, t_ctx=TM),
        out_shape=out_shape,
        grid=(n_batch, nbp),
        in_specs=[pl.BlockSpec((TM, 768), blk),
                  pl.BlockSpec((rows, 1024), lambda b, j: (b, 0)),
                  pl.BlockSpec((rows, 512), lambda b, j: (b, 0))],
        out_specs=out_specs,
        compiler_params=_cparams(2),
        name="mla_attention",
    )(q, k, v)


SWA_QB = 128
SWA_NQ = 2


def _swa_body(q_ref, k_ref, v_ref, sink_ref, o_ref, *, t_ctx, seq):
    c1 = SWA_HEAD_DIM ** -0.5 * LOG2E
    j = pl.program_id(1)
    n_ctx_steps = t_ctx // (SWA_QB * SWA_NQ)
    lane = lax.broadcasted_iota(jnp.int32, (SWA_QB, LANES), 1)
    lo = lane < SWA_HEAD_DIM
    win = 3 * SWA_QB

    def finish(rows, parts):
        for c in range(4):
            g = c // 2
            r0 = (2 * (c % 2)) * SWA_QB
            a = parts[g][r0:r0 + SWA_QB]
            b = parts[g][r0 + SWA_QB:r0 + 2 * SWA_QB]
            if g == 0:
                col = jnp.where(lo, a, pltpu.roll(b, SWA_HEAD_DIM, axis=1))
            else:
                col = jnp.where(lo, pltpu.roll(a, SWA_HEAD_DIM, axis=1), b)
            o_ref[rows, c * LANES:(c + 1) * LANES] = col.astype(BF16)

    def blocks(blks):
        kc, vc = k_ref[0:t_ctx, :], v_ref[0:t_ctx, :]
        windowed = blks[0] is not None
        kw, vw, band = [], [], []
        for blk in blks if windowed else ():
            start = jnp.clip((blk - 1) * SWA_QB, 0, seq - win)
            rs = pl.multiple_of(t_ctx + start, SWA_QB)
            kw.append(k_ref[pl.ds(rs, win), :])
            vw.append(v_ref[pl.ds(rs, win), :])
            qpos = blk * SWA_QB + lax.broadcasted_iota(jnp.int32, (SWA_QB, win), 0)
            kpos = start + lax.broadcasted_iota(jnp.int32, (SWA_QB, win), 1)
            band1 = jnp.abs(kpos - qpos) <= SWA_WINDOW
            band.append(jnp.concatenate([band1] * 4, axis=0))
        units = [(sub, g) for sub in range(SWA_NQ) for g in range(SWA_KV_HEADS)]
        rows = [slice(sub * SWA_QB, (sub + 1) * SWA_QB) for sub in range(SWA_NQ)]
        qg = [jnp.concatenate([q_ref[rows[sub], (4 * g + i) * LANES:(4 * g + i + 1) * LANES] for i in range(4)],
                              axis=0) for sub, g in units]
        sk = [c1 * jnp.concatenate(
            [jnp.broadcast_to(sink_ref[4 * g + i:4 * g + i + 1, 0:1], (SWA_QB, 1)) for i in range(4)], axis=0)
            for _, g in units]
        t_c = [_dot_nt(q, kc) * c1 for q in qg]
        m = [jnp.maximum(jnp.max(t, axis=-1, keepdims=True), s) for t, s in zip(t_c, sk)]
        if windowed:
            t_w = [jnp.where(band[sub], _dot_nt(q, kw[sub]) * c1, -jnp.inf) for (sub, _), q in zip(units, qg)]
            m = [jnp.maximum(mm, jnp.max(t, axis=-1, keepdims=True)) for mm, t in zip(m, t_w)]
        e_c = [jnp.exp2(t - mm) for t, mm in zip(t_c, m)]
        l = [jnp.sum(e, axis=-1, keepdims=True) + jnp.exp2(s - mm) for e, s, mm in zip(e_c, sk, m)]
        acc = [_dot(e.astype(BF16), vc) for e in e_c]
        if windowed:
            e_w = [jnp.exp2(t - mm) for t, mm in zip(t_w, m)]
            l = [ll + jnp.sum(e, axis=-1, keepdims=True) for ll, e in zip(l, e_w)]
            acc = [a + _dot(e.astype(BF16), vw[sub]) for a, e, (sub, _) in zip(acc, e_w, units)]
        parts = [a / ll for a, ll in zip(acc, l)]
        for sub in range(SWA_NQ):
            finish(rows[sub], parts[SWA_KV_HEADS * sub:SWA_KV_HEADS * (sub + 1)])

    @pl.when(j < n_ctx_steps)
    def _():
        blocks([None] * SWA_NQ)

    @pl.when(j >= n_ctx_steps)
    def _():
        blocks([(j - n_ctx_steps) * SWA_NQ + sub for sub in range(SWA_NQ)])


def _swa_attention(q, k, v, sink, n_batch, t_ctx, seq):
    rows = t_ctx + seq
    qrows = SWA_QB * SWA_NQ
    assert t_ctx % qrows == 0 and seq % qrows == 0
    nqb = rows // qrows
    sink_tab = jnp.broadcast_to(sink.astype(F32)[:, None], (SWA_HEADS, LANES))
    return pl.pallas_call(
        functools.partial(_swa_body, t_ctx=t_ctx, seq=seq),
        out_shape=jax.ShapeDtypeStruct((n_batch * rows, BRANCH_W), BF16),
        grid=(n_batch, nqb),
        in_specs=[pl.BlockSpec((qrows, 1024), lambda b, j: (b * nqb + j, 0)),
                  pl.BlockSpec((rows, LANES), lambda b, j: (b, 0)),
                  pl.BlockSpec((rows, LANES), lambda b, j: (b, 0)),
                  pl.BlockSpec((SWA_HEADS, LANES), lambda b, j: (0, 0))],
        out_specs=pl.BlockSpec((qrows, BRANCH_W), lambda b, j: (b * nqb + j, 0)),
        compiler_params=_cparams(2),
        name="swa_attention",
    )(q, k, v, sink_tab)


MERGE_UNITS = 2


def _merge_body(*refs):
    U = MERGE_UNITS
    row_refs = [refs[6 * u:6 * (u + 1)] for u in range(U)]
    mod_refs = refs[6 * U:7 * U]
    g1_ref, g2_ref, wg_ref, dng_ref, g512_ref, wb_ref, wo_ref, rw_ref, rb_ref = refs[7 * U:7 * U + 9]
    xo_ref, h2_ref, lg_ref = refs[7 * U + 9:]
    rows = [slice(u * TM, (u + 1) * TM) for u in range(U)]
    xs = [r[0][...] for r in row_refs]
    hs = [(_rms(x, g1_ref[...]) * (1.0 + m[0, 0, 1:2, :]) + m[0, 0, 0:1, :]).astype(BF16)
          for x, m in zip(xs, mod_refs)]
    ods = [r[3][...] + r[4][...] for r in row_refs]
    sqs = [_split_bf16(od * od) for od in ods]
    mss = [(_dot(hi, g512_ref[...]) + _dot(lo, g512_ref[...])) * (1.0 / DN_HEAD_DIM) for hi, lo in sqs]
    ocs = [(od * lax.rsqrt(ms + NORM_EPS) * dng_ref[...] * (r[5][...] * jax.nn.sigmoid(r[5][...]))).astype(BF16)
           for od, ms, r in zip(ods, mss, row_refs)]
    ys = [None] * U
    for i in range(N_BRANCH):
        gates = [jax.nn.sigmoid(_dot(h, wg_ref[:, i * D_MODEL:(i + 1) * D_MODEL])) for h in hs]
        branch = [(r[1][...], r[2][...], oc)[i] for r, oc in zip(row_refs, ocs)]
        yis = [g * _dot(o, wb_ref[i]) for g, o in zip(gates, branch)]
        ys = [yi if y is None else y + yi for y, yi in zip(ys, yis)]
    xns = [x + m[0, 0, 2:3, :] * _dot(y.astype(BF16), wo_ref[...]) for x, m, y in zip(xs, mod_refs, ys)]
    h2s = [_rms(xn, g2_ref[...]) * (1.0 + m[0, 0, 4:5, :]) + m[0, 0, 3:4, :] for xn, m in zip(xns, mod_refs)]
    lgs = [_dot3_nt(rw_ref[...], h2) + rb_ref[...] for h2 in h2s]
    for r, xn, h2, lg in zip(rows, xns, h2s, lgs):
        xo_ref[r, :] = xn
        h2_ref[r, :] = h2.astype(BF16)
        lg_ref[:, r] = lg


def _merge(x, modtab, g1, g2, wg, oa, ob, dn_f, dn_b, dz, dn_g, g512, wb, wo, rwt, rb, nb_per_batch, latent_only):
    U = MERGE_UNITS
    if latent_only:
        nlat = nb_per_batch - 1
        n = x.shape[0] // nb_per_batch * nlat
        src = lambda u: (u // nlat) * nb_per_batch + u % nlat + 1
        mod_of = lambda u: (u // nlat, 1, 0, 0)
    else:
        n = x.shape[0]
        src = lambda u: u
        mod_of = lambda u: (u // nb_per_batch, jnp.minimum(u % nb_per_batch, 1), 0, 0)
    step = TM * U
    assert n % step == 0
    orow = lambda i: (i, 0)
    full2 = lambda i: (0, 0)
    row_specs, row_args = [], []
    for u in range(U):
        rowu = lambda i, u=u: (src(i * U + u), 0)
        row_specs += [pl.BlockSpec((TM, D_MODEL), rowu)] + [pl.BlockSpec((TM, BRANCH_W), rowu)] * 5
        row_args += [x, oa, ob, dn_f, dn_b, dz]
    mod_specs = [pl.BlockSpec((1, 1, 6, D_MODEL), lambda i, u=u: mod_of(i * U + u)) for u in range(U)]
    return pl.pallas_call(
        _merge_body,
        out_shape=[jax.ShapeDtypeStruct((n, D_MODEL), F32), jax.ShapeDtypeStruct((n, D_MODEL), BF16),
                   jax.ShapeDtypeStruct((N_EXPERTS, n), F32)],
        grid=(n // step,),
        in_specs=row_specs + mod_specs
        + [pl.BlockSpec((1, D_MODEL), full2), pl.BlockSpec((1, D_MODEL), full2),
           pl.BlockSpec((D_MODEL, N_BRANCH * D_MODEL), full2), pl.BlockSpec((1, BRANCH_W), full2),
           pl.BlockSpec((BRANCH_W, BRANCH_W), full2),
           pl.BlockSpec((N_BRANCH, BRANCH_W, D_MODEL), lambda i: (0, 0, 0)),
           pl.BlockSpec((D_MODEL, D_MODEL), full2),
           pl.BlockSpec((N_EXPERTS, D_MODEL), full2), pl.BlockSpec((N_EXPERTS, 1), full2)],
        out_specs=[pl.BlockSpec((step, D_MODEL), orow), pl.BlockSpec((step, D_MODEL), orow),
                   pl.BlockSpec((N_EXPERTS, step), lambda i: (0, i))],
        compiler_params=_cparams(),
        name="merge",
    )(*row_args, *([modtab] * U), g1, g2, wg, dn_g, g512, wb, wo, rwt, rb)


ROUTE_TB = 1024


def _router_body(lg_ref, tri_ref, e_ref, gate_ref, pos_ref, cnt_ref, run_ref):
    @pl.when(pl.program_id(0) == 0)
    def _():
        run_ref[...] = jnp.zeros_like(run_ref)

    lg = lg_ref[...]
    eid = lax.broadcasted_iota(jnp.int32, lg.shape, 0)
    work = lg
    vals, idxs = [], []
    sel = jnp.zeros(lg.shape, F32)
    for _ in range(TOP_K):
        m = jnp.max(work, axis=0, keepdims=True)
        idx = jnp.min(jnp.where(work == m, eid, N_EXPERTS), axis=0, keepdims=True)
        hit = eid == idx
        sel = jnp.where(hit, 1.0, sel)
        work = jnp.where(hit, -jnp.inf, work)
        vals.append(m)
        idxs.append(idx)
    ex = [jnp.exp(v - vals[0]) for v in vals]
    den = ex[0] + ex[1] + ex[2] + ex[3]
    before = _dot(sel.astype(BF16), tri_ref[...]) + run_ref[:, 0:1]
    for k in range(TOP_K):
        e_ref[k:k + 1, :] = idxs[k]
        gate_ref[k:k + 1, :] = ex[k] / den
        pos_ref[k:k + 1, :] = jnp.sum(jnp.where(eid == idxs[k], before, 0.0), axis=0,
                                      keepdims=True).astype(jnp.int32)
    run_ref[...] = run_ref[...] + jnp.sum(sel, axis=1, keepdims=True)
    cnt_ref[...] = run_ref[...]


def _router(logits_t):
    n = logits_t.shape[1]
    tri = (jnp.arange(ROUTE_TB)[:, None] < jnp.arange(ROUTE_TB)[None, :]).astype(BF16)
    blk = lambda i: (0, i)
    return pl.pallas_call(
        _router_body,
        out_shape=[jax.ShapeDtypeStruct((TOP_K, n), jnp.int32), jax.ShapeDtypeStruct((TOP_K, n), F32),
                   jax.ShapeDtypeStruct((TOP_K, n), jnp.int32), jax.ShapeDtypeStruct((N_EXPERTS, LANES), F32)],
        grid=(n // ROUTE_TB,),
        in_specs=[pl.BlockSpec((N_EXPERTS, ROUTE_TB), blk), pl.BlockSpec((ROUTE_TB, ROUTE_TB), lambda i: (0, 0))],
        out_specs=[pl.BlockSpec((TOP_K, ROUTE_TB), blk), pl.BlockSpec((TOP_K, ROUTE_TB), blk),
                   pl.BlockSpec((TOP_K, ROUTE_TB), blk), pl.BlockSpec((N_EXPERTS, LANES), lambda i: (0, 0))],
        scratch_shapes=[pltpu.VMEM((N_EXPERTS, LANES), F32)],
        compiler_params=_cparams(),
        name="router",
    )(logits_t, tri)


def _expert_body(ib_ref, ie_ref, lo_ref, hi_ref, x_ref, wgu_ref, bgu_ref, wdn_ref, bdn_ref, perm_ref, *rest):
    o_ref, wgu_s, wdn_s = rest[-3:]
    i = pl.program_id(0)
    prev = jnp.maximum(i - 1, 0)

    @pl.when((i == 0) | (ie_ref[i] != ie_ref[prev]))
    def _():
        for m in range(D_EXPERT // LANES):
            t = _dot(wgu_ref[0, 0, :, m * 2 * LANES:(m + 1) * 2 * LANES].astype(BF16), perm_ref[...])
            wgu_s[:, m * LANES:(m + 1) * LANES] = t[:, 0:LANES].astype(BF16)
            wgu_s[:, D_EXPERT + m * LANES:D_EXPERT + (m + 1) * LANES] = t[:, LANES:2 * LANES].astype(BF16)
        wdn_s[...] = wdn_ref[0, 0].astype(BF16)

    lo, hi = lo_ref[i], hi_ref[i]

    @pl.when(hi > lo)
    def _():
        gu = _dot(x_ref[...], wgu_s[...]) + bgu_ref[0]
        g_ = jnp.minimum(gu[:, :D_EXPERT], SWIGLU_LIMIT)
        u_ = jnp.clip(gu[:, D_EXPERT:], -SWIGLU_LIMIT, SWIGLU_LIMIT)
        act = (u_ + 1.0) * (g_ * jax.nn.sigmoid(SWIGLU_ALPHA * g_))
        y = _dot(act.astype(BF16), wdn_s[...]) + bdn_ref[0]
        first = (i == 0) | (ib_ref[i] != ib_ref[prev])
        row = lax.broadcasted_iota(jnp.int32, (MOE_BLK, 1), 0)
        o_ref[...] = jnp.where(first | ((row >= lo) & (row < hi)), y, o_ref[...])


def _experts(items, xs, w_gu, bgu, w_dn, bdn, layer, block0, yb_prev):
    n_items = items[0].shape[0]
    perm = np.zeros((2 * LANES, 2 * LANES), np.float32)
    j = np.arange(LANES)
    perm[2 * j, j] = 1.0
    perm[2 * j + 1, LANES + j] = 1.0
    in_specs = [pl.BlockSpec((MOE_BLK, D_MODEL), lambda i, ib, ie, lo, hi: (ib[i] - block0, 0)),
                pl.BlockSpec((1, 1, D_MODEL, 2 * D_EXPERT), lambda i, ib, ie, lo, hi: (layer, ie[i], 0, 0)),
                pl.BlockSpec((1, 1, 2 * D_EXPERT), lambda i, ib, ie, lo, hi: (ie[i], 0, 0)),
                pl.BlockSpec((1, 1, D_EXPERT, D_MODEL), lambda i, ib, ie, lo, hi: (layer, ie[i], 0, 0)),
                pl.BlockSpec((1, 1, D_MODEL), lambda i, ib, ie, lo, hi: (ie[i], 0, 0)),
                pl.BlockSpec((2 * LANES, 2 * LANES), lambda i, ib, ie, lo, hi: (0, 0)),
                pl.BlockSpec(memory_space=pl.ANY)]
    args = [*items, xs, w_gu, bgu, w_dn, bdn, jnp.asarray(perm, BF16), yb_prev]
    aliases = {len(args) - 1: 0}
    return pl.pallas_call(
        _expert_body,
        out_shape=jax.ShapeDtypeStruct(yb_prev.shape, F32),
        grid_spec=pltpu.PrefetchScalarGridSpec(
            num_scalar_prefetch=4,
            grid=(n_items,),
            in_specs=in_specs,
            out_specs=pl.BlockSpec((MOE_BLK, D_MODEL), lambda i, ib, ie, lo, hi: (ib[i], 0)),
            scratch_shapes=[pltpu.VMEM((D_MODEL, 2 * D_EXPERT), BF16), pltpu.VMEM((D_EXPERT, D_MODEL), BF16)]),
        input_output_aliases=aliases,
        compiler_params=_cparams(),
        name="experts",
    )(*args)


def _lookup(table, idx):
    ids = jnp.arange(table.shape[0], dtype=jnp.int32)
    return jnp.sum(jnp.where(idx[..., None] == ids, table, 0), axis=-1)


def _work_items(cnt_start, cnt_end, block0, n_blk):
    n_items = n_blk + N_EXPERTS - 1
    b1 = block0 + n_blk
    first = jnp.maximum(cnt_start // MOE_BLK, block0)
    last = jnp.minimum((cnt_end - 1) // MOE_BLK, b1 - 1)
    n_e = jnp.where(cnt_end > cnt_start, jnp.maximum(last - first + 1, 0), 0)
    item_end = jnp.cumsum(n_e)
    item_start = item_end - n_e
    total = item_end[-1]
    ii = jnp.arange(n_items, dtype=jnp.int32)
    valid = ii < total
    e_i = jnp.sum((item_end[None, :] <= jnp.minimum(ii, total - 1)[:, None]).astype(jnp.int32), axis=1)
    blk = jnp.where(valid, _lookup(first, e_i) + ii - _lookup(item_start, e_i), b1 - 1)
    lo = jnp.clip(_lookup(cnt_start, e_i) - blk * MOE_BLK, 0, MOE_BLK)
    hi = jnp.clip(_lookup(cnt_end, e_i) - blk * MOE_BLK, 0, MOE_BLK)
    lo = jnp.where(valid, lo, 0)
    hi = jnp.where(valid, hi, 0)
    return blk.astype(jnp.int32), e_i.astype(jnp.int32), lo.astype(jnp.int32), hi.astype(jnp.int32)


def _moe(h2, logits_t, w_gu, bgu, w_dn, bdn, layer, yb):
    n = h2.shape[0]
    a = n * TOP_K
    assert a % (MOE_BLK * MOE_CHUNKS) == 0 and yb.shape[0] >= a
    top_e, gate, pos, cnt = _router(logits_t)
    counts = cnt[:, 0].astype(jnp.int32)
    cnt_end = jnp.cumsum(counts)
    cnt_start = cnt_end - counts
    slot = pos + _lookup(cnt_start, top_e)
    tok = jnp.broadcast_to(jnp.arange(n, dtype=jnp.int32)[None, :], (TOP_K, n))
    _, slot_tok = lax.sort_key_val(slot.reshape(-1), tok.reshape(-1))
    n_blk = a // MOE_BLK // MOE_CHUNKS
    rows_c = n_blk * MOE_BLK
    for c in range(MOE_CHUNKS):
        xs = h2.at[slot_tok[c * rows_c:(c + 1) * rows_c]].get(mode="promise_in_bounds")
        items = _work_items(cnt_start, cnt_end, c * n_blk, n_blk)
        yb = _experts(items, xs, w_gu, bgu, w_dn, bdn, layer, c * n_blk, yb)
    ys = [yb.at[slot[k]].get(mode="promise_in_bounds") for k in range(TOP_K)]
    return ys, gate.T, yb


DN_GW = 256
HALO = 8
DN_PAIR = 4
_M_EYE, _M_BLK16, _M_OFF32, _M_OFF64, _M_INCL, _M_STRICT = 0, 1, 2, 3, 4, 6


def _dn_masks():
    row = np.arange(DN_CHUNK)[:, None]
    col = np.arange(DN_GW)[None, :] % DN_CHUNK
    b16 = (row // 16) == (col // 16)
    b32 = (row // 32) == (col // 32)
    m = np.stack([row == col, b16, b32 & ~b16, ~b32, col <= row, col >= row, col < row, col > row])
    hid = np.arange(DN_GW) // DN_HEAD_DIM
    return jnp.asarray(m, F32), jnp.asarray(hid[:, None] == hid[None, :], BF16)


def _split3(a):
    hi = a.astype(BF16)
    r = a - hi.astype(F32)
    lo = r.astype(BF16)
    lo2 = (r - lo.astype(F32)).astype(BF16)
    return hi, lo, lo2


def _bd(x, bdm_ref):
    xb = x.astype(BF16)
    return jnp.concatenate([xb, xb, xb, xb], axis=0) * bdm_ref[...]


def _mm(a, wbd):
    return _dot(a.astype(BF16), wbd)


def _tri_inverse(l_mats, m_ref, bdm_ref):
    half = DN_CHUNK
    bd = lambda v: _bd(v, bdm_ref)
    lds = [l * m_ref[_M_BLK16] for l in l_mats]
    ps = [_mm(ld, bd(ld)) for ld in lds]
    xs = [m_ref[_M_EYE] - ld for ld in lds]
    for _ in range(2):
        rs = [_mm(jnp.concatenate([p, x], axis=0), bd(p)) for p, x in zip(ps, xs)]
        ps = [r[0:half] for r in rs]
        xs = [x + r[half:2 * half] for x, r in zip(xs, rs)]
    xs = [x + _mm(x, bd(p)) for x, p in zip(xs, ps)]
    ts = [_mm(x, bd(l * m_ref[_M_OFF32])) for x, l in zip(xs, l_mats)]
    xs = [x - _mm(t, bd(x)) for x, t in zip(xs, ts)]
    ts = [_mm(x, bd(l * m_ref[_M_OFF64])) for x, l in zip(xs, l_mats)]
    return [x - _mm(t, bd(x)) for x, t in zip(xs, ts)]


def _dn_prep_body(cur_ref, prev_ref, next_ref, cw_ref, dg_ref, gc_ref, g512_ref, e_ref, tri_ref, m_ref, bdm_ref,
                  u_ref, w_ref, kd_ref, qd_ref, aq_ref, gl_ref,
                  ext_ref, q_s, k_s, v_s, gx_s, bx_s):
    j = pl.program_id(1)
    nbp = pl.num_programs(1)
    C = DN_CHUNK
    ext_ref[HALO:HALO + TM, :] = cur_ref[...]
    ext_ref[0:HALO, :] = jnp.where(j >= 2, prev_ref[...], 0.0)
    ext_ref[HALO + TM:2 * HALO + TM, :] = jnp.where((j >= 1) & (j < nbp - 1), next_ref[...], 0.0)
    y = None
    for t in range(DN_CONV):
        o = HALO - DN_CONV // 2 + t
        term = cw_ref[t:t + 1, :] * ext_ref[o:o + TM, :]
        y = term if y is None else y + term
    y = y * jax.nn.sigmoid(y)

    g512 = g512_ref[...]

    def head_sumsq(x):
        hi, lo = _split_bf16(x * x)
        return _dot(hi, g512) + _dot(lo, g512)

    q = y[:, 0:512]
    k = y[:, 512:1024]
    q_s[...] = q * lax.rsqrt(head_sumsq(q) + NORM_EPS) * (DN_HEAD_DIM ** -0.5)
    k_s[...] = k * lax.rsqrt(head_sumsq(k) + NORM_EPS)
    v_s[...] = y[:, 1024:1536]

    dg = dg_ref[...]
    beta_all = jax.nn.sigmoid(dg)
    z = dg + gc_ref[1:2, :]
    g_all = gc_ref[0:1, :] * (jnp.maximum(z, 0.0) + jnp.log(1.0 + jnp.exp(-jnp.abs(z))))
    bh, bl = _split_bf16(beta_all)
    bcat = jnp.concatenate([bh, bl], axis=0)
    gparts = jnp.concatenate(_split3(g_all), axis=1)
    for d in range(2):
        cs = _dot(tri_ref[d], gparts)
        gcum = cs[:, 0:128] + cs[:, 128:256] + cs[:, 256:384]
        gcat = jnp.concatenate(_split3(gcum), axis=0)
        for grp in range(2):
            eg = _dot(gcat, e_ref[4 + 2 * d + grp])
            gx_s[d, grp] = eg[0:TM] + eg[TM:2 * TM] + eg[2 * TM:3 * TM]
            eb = _dot(bcat, e_ref[2 * d + grp])
            bx_s[d, grp] = eb[0:TM] + eb[TM:2 * TM]

    def chunk_pair(it, carry):
        units = []
        l_mats = []
        for ci in range(DN_PAIR):
            cc = it * DN_PAIR + ci
            rows = pl.ds(pl.multiple_of(cc * C, C), C)
            for grp in range(2):
                cols = slice(grp * DN_GW, (grp + 1) * DN_GW)
                kg, qg = k_s[rows, cols], q_s[rows, cols]
                kb = [kg * bx_s[d, grp, rows, :] for d in range(2)]
                raw = _dot_nt(jnp.concatenate([kb[0], kb[1], qg], axis=0).astype(BF16), _bd(kg, bdm_ref))
                for d in range(2):
                    gx = gx_s[d, grp, rows, :]
                    rvec = jnp.sum(gx * m_ref[_M_EYE], axis=0, keepdims=True)
                    dec = jnp.exp(jnp.minimum(gx - rvec, 0.0)) * m_ref[_M_INCL + d]
                    l_mats.append(raw[C * d:C * (d + 1)] * dec * m_ref[_M_STRICT + d])
                    glast = gx[C - 1:C, :] if d == 0 else gx[0:1, :]
                    kd_ref[d, rows, cols] = (kg * jnp.exp(glast - gx)).astype(BF16)
                    qd_ref[d, rows, cols] = (qg * jnp.exp(gx)).astype(BF16)
                    aq_ref[d, rows, cols] = (raw[2 * C:3 * C] * dec).astype(BF16)
                    gl_ref[d, pl.ds(cc, 1), :, cols] = jnp.exp(glast).reshape(1, 1, DN_GW)
                    units.append((rows, cols, d, grp))
        tinvs = _tri_inverse(l_mats, m_ref, bdm_ref)
        for (rows, cols, d, grp), tinv in zip(units, tinvs):
            bx = bx_s[d, grp, rows, :]
            u_ref[d, rows, cols] = _mm(tinv, _bd(v_s[rows, cols] * bx, bdm_ref))
            w_ref[d, rows, cols] = _mm(tinv, _bd(k_s[rows, cols] * bx * jnp.exp(gx_s[d, grp, rows, :]), bdm_ref)
                                       ).astype(BF16)
        return carry

    lax.fori_loop(0, TM // C // DN_PAIR, chunk_pair, 0)


def _dn_prep(dqkv, dg, conv_w, a_log, dt_bias, n_batch, rows):
    n = dqkv.shape[0]
    nbp = rows // TM
    hb = TM // HALO
    H = DN_HEADS
    cw = jnp.zeros((8, 1536), F32).at[0:DN_CONV].set(conv_w)
    gc = jnp.zeros((8, LANES), F32)
    gc = gc.at[0, 2 * H:4 * H].set(-jnp.exp(a_log.reshape(-1))).at[1, 2 * H:4 * H].set(dt_bias.reshape(-1))
    hid = np.arange(512) // DN_HEAD_DIM
    g512 = jnp.asarray(hid[:, None] == hid[None, :], BF16)
    e = np.zeros((8, LANES, DN_GW), np.float32)
    for kind in range(2):
        for d in range(2):
            for grp in range(2):
                for h in range(4):
                    e[4 * kind + 2 * d + grp, 16 * kind + 8 * d + 4 * grp + h, 64 * h:64 * (h + 1)] = 1.0
    e = jnp.asarray(e, BF16)
    t = np.arange(TM)
    same = (t[:, None] // DN_CHUNK) == (t[None, :] // DN_CHUNK)
    tri = jnp.asarray(np.stack([same & (t[None, :] <= t[:, None]), same & (t[None, :] >= t[:, None])]), BF16)
    masks, bdm = _dn_masks()
    blk = lambda b, j: (b * nbp + j, 0)
    full2 = lambda b, j: (0, 0)
    full3 = lambda b, j: (0, 0, 0)
    dblk = lambda b, j: (0, b * nbp + j, 0)
    nlast = n // HALO - 1
    outs = pl.pallas_call(
        _dn_prep_body,
        out_shape=[jax.ShapeDtypeStruct((2, n, 512), F32)] + [jax.ShapeDtypeStruct((2, n, 512), BF16)] * 4
        + [jax.ShapeDtypeStruct((2, n // DN_CHUNK, 1, 512), F32)],
        grid=(n_batch, nbp),
        in_specs=[pl.BlockSpec((TM, 1536), blk),
                  pl.BlockSpec((HALO, 1536), lambda b, j: (jnp.maximum((b * nbp + j) * hb - 1, 0), 0)),
                  pl.BlockSpec((HALO, 1536), lambda b, j: (jnp.minimum((b * nbp + j + 1) * hb, nlast), 0)),
                  pl.BlockSpec((8, 1536), full2), pl.BlockSpec((TM, LANES), blk), pl.BlockSpec((8, LANES), full2),
                  pl.BlockSpec((512, 512), full2), pl.BlockSpec((8, LANES, DN_GW), full3),
                  pl.BlockSpec((2, TM, TM), full3), pl.BlockSpec((8, DN_CHUNK, DN_GW), full3),
                  pl.BlockSpec((DN_GW, DN_GW), full2)],
        out_specs=[pl.BlockSpec((2, TM, 512), dblk)] * 5
        + [pl.BlockSpec((2, TM // DN_CHUNK, 1, 512), lambda b, j: (0, b * nbp + j, 0, 0))],
        scratch_shapes=[pltpu.VMEM((TM + 2 * HALO, 1536), F32), pltpu.VMEM((TM, 512), F32),
                        pltpu.VMEM((TM, 512), F32), pltpu.VMEM((TM, 512), F32),
                        pltpu.VMEM((2, 2, TM, DN_GW), F32), pltpu.VMEM((2, 2, TM, DN_GW), F32)],
        compiler_params=_cparams(2),
        name="dn_prep",
    )(dqkv, dqkv, dqkv, cw, dg, gc, g512, e, tri, masks, bdm)
    return outs, g512


DN_SCAN_NB = 8


def _dn_scan_body(uf, wf, kf, qf, af, gf, ub, wb, kb, qb, ab, gb, bdm_ref, of_ref, ob_ref, s_ref):
    @pl.when(pl.program_id(1) == 0)
    def _():
        s_ref[...] = jnp.zeros_like(s_ref)

    same = bdm_ref[...] != 0
    dirs = ((uf, wf, kf, qf, af, gf, of_ref), (ub, wb, kb, qb, ab, gb, ob_ref))
    units = [(bb, d, grp) for bb in range(DN_SCAN_NB) for d in range(2) for grp in range(2)]
    cols = [slice(grp * DN_GW, (grp + 1) * DN_GW) for grp in range(2)]
    s = [s_ref[bb, d, grp] for bb, d, grp in units]
    sq = [_dot(jnp.concatenate([dirs[d][1][0, bb, :, cols[grp]], dirs[d][3][0, bb, :, cols[grp]]], axis=0),
               st.astype(BF16)) for (bb, d, grp), st in zip(units, s)]
    vnew = [dirs[d][0][0, bb, :, cols[grp]] - r[0:DN_CHUNK] for (bb, d, grp), r in zip(units, sq)]
    intra = [_dot(dirs[d][4][0, bb, :, cols[grp]], _bd(v, bdm_ref)) for (bb, d, grp), v in zip(units, vnew)]
    upd = [lax.dot_general(dirs[d][2][0, bb, :, cols[grp]], v.astype(BF16), (((0,), (0,)), ((), ())),
                           preferred_element_type=F32) for (bb, d, grp), v in zip(units, vnew)]
    for (bb, d, grp), st, r, a, up in zip(units, s, sq, intra, upd):
        dirs[d][6][bb, :, cols[grp]] = r[DN_CHUNK:2 * DN_CHUNK] + a
        s_ref[bb, d, grp] = st * dirs[d][5][0, bb, 0, :, cols[grp]] + jnp.where(same, up, 0.0)


def _dn_scan(u, w, kd, qd, aq, gl, n_batch, rows, t_ctx):
    nch = rows // DN_CHUNK
    nctx = t_ctx // DN_CHUNK
    assert n_batch % DN_SCAN_NB == 0

    def cb(c):
        return jnp.where(c < nctx, nctx - 1 - c, nch - 1 - (c - nctx))

    per_batch = lambda a: a.reshape(2, n_batch, rows, 512)
    u, w, kd, qd, aq = map(per_batch, (u, w, kd, qd, aq))
    gl = gl.reshape(2, n_batch, nch, 1, 512)
    blk = (1, DN_SCAN_NB, DN_CHUNK, 512)
    gblk = (1, DN_SCAN_NB, 1, 1, 512)
    specs = []
    for d, ch in ((0, lambda c: c), (1, cb)):
        specs += [pl.BlockSpec(blk, lambda b, c, d=d, ch=ch: (d, b, ch(c), 0))] * 5
        specs += [pl.BlockSpec(gblk, lambda b, c, d=d, ch=ch: (d, b, ch(c), 0, 0))]
    specs.append(pl.BlockSpec((DN_GW, DN_GW), lambda b, c: (0, 0)))
    o_f, o_b = pl.pallas_call(
        _dn_scan_body,
        out_shape=[jax.ShapeDtypeStruct((n_batch, rows, 512), F32)] * 2,
        grid=(n_batch // DN_SCAN_NB, nch),
        in_specs=specs,
        out_specs=[pl.BlockSpec((DN_SCAN_NB, DN_CHUNK, 512), lambda b, c: (b, c, 0)),
                   pl.BlockSpec((DN_SCAN_NB, DN_CHUNK, 512), lambda b, c: (b, cb(c), 0))],
        scratch_shapes=[pltpu.VMEM((DN_SCAN_NB, 2, 2, DN_GW, DN_GW), F32)],
        compiler_params=_cparams(2),
        name="dn_scan",
    )(u, w, kd, qd, aq, gl, u, w, kd, qd, aq, gl, _dn_masks()[1])
    return o_f.reshape(n_batch * rows, 512), o_b.reshape(n_batch * rows, 512)


def _deltanet(dqkv, dg, conv_w, a_log, dt_bias, n_batch, rows, t_ctx):
    (u, w, kd, qd, aq, gl), g512 = _dn_prep(dqkv, dg, conv_w, a_log, dt_bias, n_batch, rows)
    o_f, o_b = _dn_scan(u, w, kd, qd, aq, gl, n_batch, rows, t_ctx)
    return o_f, o_b, g512


def _combine_body(x_ref, y0_ref, y1_ref, y2_ref, y3_ref, gate_ref, mod_ref, fg_ref, o_ref, *, final):
    gate = gate_ref[...]
    y = None
    for k, y_ref in enumerate((y0_ref, y1_ref, y2_ref, y3_ref)):
        yk = y_ref[...] * gate[:, k:k + 1]
        y = yk if y is None else y + yk
    xo = x_ref[...] + mod_ref[0, 0, 5:6, :] * y
    o_ref[...] = _rms(xo, fg_ref[...]) if final else xo


def _combine(x, ys, gate_rows, modtab, final_g, nb_per_batch, latent_only):
    n = x.shape[0]
    if latent_only:
        nlat = nb_per_batch - 1
        mod_idx = lambda i: (i // nlat, 1, 0, 0)
    else:
        mod_idx = lambda i: (i // nb_per_batch, jnp.minimum(i % nb_per_batch, 1), 0, 0)
    row = lambda i: (i, 0)
    return pl.pallas_call(
        functools.partial(_combine_body, final=latent_only),
        out_shape=jax.ShapeDtypeStruct((n, D_MODEL), F32),
        grid=(n // TM,),
        in_specs=[pl.BlockSpec((TM, D_MODEL), row)] * 5
        + [pl.BlockSpec((TM, TOP_K), row), pl.BlockSpec((1, 1, 6, D_MODEL), mod_idx),
           pl.BlockSpec((1, D_MODEL), lambda i: (0, 0))],
        out_specs=pl.BlockSpec((TM, D_MODEL), row),
        compiler_params=_cparams(),
        name="combine",
    )(x, *ys, gate_rows, modtab, final_g)


def _rope_tables(seq, t_ctx, rot_dim):
    rows = seq // GRID_W
    row = jnp.broadcast_to(jnp.arange(rows)[:, None], (rows, GRID_W)).reshape(-1).astype(F32)
    col = jnp.broadcast_to(jnp.arange(GRID_W)[None, :], (rows, GRID_W)).reshape(-1).astype(F32)
    n_freq = rot_dim // 4
    inv_freq = ROPE_THETA ** (-jnp.arange(n_freq, dtype=F32) / n_freq)
    ang = jnp.concatenate([row[:, None] * inv_freq, col[:, None] * inv_freq], axis=-1)
    cos, sin = jnp.cos(ang), jnp.sin(ang)
    zero = jnp.zeros_like(sin)
    c = jnp.concatenate([cos, cos], axis=-1)
    sa = jnp.concatenate([-sin, zero], axis=-1)
    sb = jnp.concatenate([zero, sin], axis=-1)
    ident = jnp.stack([jnp.ones((t_ctx, rot_dim), F32), jnp.zeros((t_ctx, rot_dim), F32),
                       jnp.zeros((t_ctx, rot_dim), F32)])
    tab = jnp.concatenate([ident, jnp.stack([c, sa, sb])], axis=1)
    return jnp.tile(tab, (1, 1, LANES // rot_dim))


def _layer_weights(w_in, w_q_up, w_kv_up, w_branch, w_out, router_w, router_b, b_gu, b_dn):
    sizes = (MLA_Q_RANK, MLA_KV_RANK, MLA_ROPE, 512, 128, 128, 1536, 512, 16, 16, N_BRANCH * D_MODEL)
    offs = np.cumsum((0,) + sizes)
    part = lambda i: w_in[:, offs[i]:offs[i + 1]]
    pad = jnp.zeros((D_MODEL, LANES - 32), F32)
    w_proj = jnp.concatenate(
        [part(0), part(1), part(3), part(4), part(5), part(6), part(7),
         jnp.tile(part(2), (1, LANES // MLA_ROPE)), part(8), part(9), pad], axis=1).astype(BF16)
    w_gate = part(10).astype(BF16)
    hq = MLA_NOPE + MLA_ROPE
    wq = w_q_up.reshape(MLA_Q_RANK, MLA_HEADS, hq)
    wq = jnp.concatenate([wq[:, :, :MLA_NOPE].reshape(MLA_Q_RANK, -1),
                          wq[:, :, MLA_NOPE:].reshape(MLA_Q_RANK, -1)], axis=1).astype(BF16)
    wkv = w_kv_up.reshape(MLA_KV_RANK, MLA_HEADS, MLA_NOPE + MLA_V)
    wkv = jnp.concatenate([wkv[:, :, :MLA_NOPE].reshape(MLA_KV_RANK, -1),
                           wkv[:, :, MLA_NOPE:].reshape(MLA_KV_RANK, -1)], axis=1).astype(BF16)
    bgu = jnp.concatenate([b_gu[:, 0::2], b_gu[:, 1::2]], axis=-1).reshape(N_EXPERTS, 1, 2 * D_EXPERT)
    return dict(w_proj=w_proj, w_gate=w_gate, wq=wq, wkv=wkv, wb=w_branch.astype(BF16),
                wo=w_out.astype(BF16), rwt=router_w.T, rb=router_b.reshape(N_EXPERTS, 1),
                bgu=bgu, bdn=b_dn.reshape(N_EXPERTS, 1, D_MODEL))


def kernel(x, c, ctx, c_ctx, w_mod, b_mod, norm1_g, norm2_g, w_in, mla_q_norm_g, mla_w_q_up,
           mla_kv_norm_g, mla_w_kv_up, swa_sink, dn_conv_w, dn_a_log, dn_dt_bias, dn_norm_g,
           w_branch, w_out, router_w, router_b, exp_w_gu, exp_b_gu, exp_w_dn, exp_b_dn, final_norm_g):
    B, S, D = x.shape
    T = ctx.shape[1]
    R = T + S
    depth = w_mod.shape[0]
    assert D == D_MODEL and T == TM and S % TM == 0 and (B * R) % ROUTE_TB == 0
    nbp = R // TM
    xa = jnp.concatenate([ctx, x], axis=1).reshape(B * R, D)
    rope_s = _rope_tables(S, T, SWA_HEAD_DIM)
    rope_m = _rope_tables(S, T, MLA_ROPE)
    cvec = jnp.concatenate([c, c_ctx[None, :]], axis=0)
    yb = None
    for l in range(depth):
        w = _layer_weights(w_in[l], mla_w_q_up[l], mla_w_kv_up[l], w_branch[l], w_out[l], router_w[l],
                           router_b[l], exp_b_gu[l], exp_b_dn[l])
        mod = _modulation(cvec, w_mod[l], b_mod[l]).reshape(B + 1, 6, D)
        modtab = jnp.stack([jnp.broadcast_to(mod[B][None], (B, 6, D)), mod[:B]], axis=1)
        g1 = norm1_g[l].reshape(1, D)
        g2 = norm2_g[l].reshape(1, D)
        q_m, k_m, v_m, sq, sk, sv, dqkv, dz, dg = _inproj(
            xa, modtab, g1, w["w_proj"], mla_q_norm_g[l].reshape(1, -1), mla_kv_norm_g[l].reshape(1, -1),
            w["wq"], w["wkv"], rope_s, rope_m, nbp)
        if yb is None:
            o_a, yb = _mla_attention(q_m, k_m, v_m, B, R, fill_rows=B * R * TOP_K)
        else:
            o_a, = _mla_attention(q_m, k_m, v_m, B, R)
        o_b = _swa_attention(sq, sk, sv, swa_sink[l], B, T, S)
        dn_f, dn_b, g512 = _deltanet(dqkv, dg, dn_conv_w[l], dn_a_log[l], dn_dt_bias[l], B, R, T)
        last = l == depth - 1
        xn, h2, logits_t = _merge(xa, modtab, g1, g2, w["w_gate"], o_a, o_b, dn_f, dn_b, dz,
                                  jnp.tile(dn_norm_g[l], DN_HEADS).reshape(1, BRANCH_W), g512, w["wb"], w["wo"],
                                  w["rwt"], w["rb"], nbp, last)
        ys, gate_rows, yb = _moe(h2, logits_t, exp_w_gu, w["bgu"], exp_w_dn, w["bdn"], l, yb)
        xa = _combine(xn, ys, gate_rows, modtab, final_norm_g.reshape(1, D), nbp, last)
    return xa.reshape(B, S, D)
```

```python
import functools

import jax
import jax.numpy as jnp
import numpy as np
from jax import lax
from jax.experimental import pallas as pl
from jax.experimental.pallas import tpu as pltpu

F32 = jnp.float32
BF16 = jnp.bfloat16

D_MODEL = 1024
GRID_W = 64
ROPE_THETA = 10000.0
NORM_EPS = 1e-6
MLA_HEADS = 8
MLA_Q_RANK = 384
MLA_KV_RANK = 256
MLA_NOPE = 64
MLA_ROPE = 32
MLA_V = 64
SWA_HEADS = 8
SWA_KV_HEADS = 2
SWA_HEAD_DIM = 64
SWA_WINDOW = 128
DN_HEADS = 8
DN_HEAD_DIM = 64
DN_CONV = 5
DN_CHUNK = 64
N_BRANCH = 3
BRANCH_W = 512
N_EXPERTS = 32
TOP_K = 4
D_EXPERT = 1024
SWIGLU_LIMIT = 7.0
SWIGLU_ALPHA = 1.702
MOE_BLK = 512
MOE_CHUNKS = 4

LANES = 128
LOG2E = 1.4426950408889634
MLA_QSCALE = (MLA_NOPE + MLA_ROPE) ** -0.5 * LOG2E
SWA_QSCALE = SWA_HEAD_DIM ** -0.5 * LOG2E
TM = 256
VMEM_LIMIT = 56 * 1024 * 1024

_C_CQ = (0, 384)
_C_CKV = (384, 640)
_C_SQ = (640, 1152)
_C_SK = (1152, 1280)
_C_SV = (1280, 1408)
_C_DQKV = (1408, 2944)
_C_DZ = (2944, 3456)
_C_KRT = (3456, 3584)
_C_DG = (3584, 3712)
_N_PROJ = 3712


def _cparams(n_axes=1):
    return pltpu.CompilerParams(dimension_semantics=("arbitrary",) * n_axes, vmem_limit_bytes=VMEM_LIMIT)


def _dot(a, b):
    return jnp.dot(a, b, preferred_element_type=F32)


def _dot_nt(a, b):
    return lax.dot_general(a, b, (((1,), (1,)), ((), ())), preferred_element_type=F32)


def _split_bf16(a):
    hi = a.astype(BF16)
    lo = (a - hi.astype(F32)).astype(BF16)
    return hi, lo


def _dot3(a, b):
    ah, al = _split_bf16(a)
    bh, bl = _split_bf16(b)
    return _dot(ah, bh) + (_dot(ah, bl) + _dot(al, bh))


def _dot3_nt(a, b):
    ah, al = _split_bf16(a)
    bh, bl = _split_bf16(b)
    return _dot_nt(ah, bh) + (_dot_nt(ah, bl) + _dot_nt(al, bh))


def _rms(x, g):
    return x * lax.rsqrt(jnp.mean(x * x, axis=-1, keepdims=True) + NORM_EPS) * g


def _rope_cols(x, tab_ref, half):
    c, sa, sb = tab_ref[0], tab_ref[1], tab_ref[2]
    return x * c + pltpu.roll(x, LANES - half, axis=1) * sa + pltpu.roll(x, half, axis=1) * sb


def _mod_body(c_ref, w_ref, b_ref, o_ref):
    c = c_ref[...]
    a = c * jax.nn.sigmoid(c)
    o_ref[...] = _dot3(a, w_ref[...]) + b_ref[...]


def _modulation(cvec, w_mod, b_mod):
    m = cvec.shape[0]
    n = w_mod.shape[1]
    tn = 512
    return pl.pallas_call(
        _mod_body,
        out_shape=jax.ShapeDtypeStruct((m, n), F32),
        grid=(n // tn,),
        in_specs=[pl.BlockSpec((m, D_MODEL), lambda j: (0, 0)),
                  pl.BlockSpec((D_MODEL, tn), lambda j: (0, j)),
                  pl.BlockSpec((1, tn), lambda j: (0, j))],
        out_specs=pl.BlockSpec((m, tn), lambda j: (0, j)),
        compiler_params=_cparams(),
        name="modulation",
    )(cvec, w_mod, b_mod.reshape(1, n))


IN_UNITS = 2


def _inproj_body(*refs):
    U = IN_UNITS
    x_ref = refs[0]
    mod_refs = refs[1:1 + U]
    g_ref, w_ref, qg_ref, kvg_ref, wq_ref, wkv_ref = refs[1 + U:7 + U]
    rope_s_refs = refs[7 + U:7 + 2 * U]
    rope_m_refs = refs[7 + 2 * U:7 + 3 * U]
    q_ref, k_ref, v_ref, sq_ref, sk_ref, sv_ref, dqkv_ref, dz_ref, dg_ref = refs[7 + 3 * U:]
    rows = [slice(u * TM, (u + 1) * TM) for u in range(U)]
    hs = [(_rms(x_ref[r, :], g_ref[...]) * (1.0 + m[0, 0, 1:2, :]) + m[0, 0, 0:1, :]).astype(BF16)
          for r, m in zip(rows, mod_refs)]

    def proj(cols):
        return [_dot(h, w_ref[:, cols[0]:cols[1]]) for h in hs]

    for ref, cols in ((dqkv_ref, _C_DQKV), (dz_ref, _C_DZ), (dg_ref, _C_DG)):
        for r, p in zip(rows, proj(cols)):
            ref[r, :] = p
    for r, p in zip(rows, proj(_C_SV)):
        sv_ref[r, :] = p.astype(BF16)
    for r, p, rope_ref in zip(rows, proj(_C_SK), rope_s_refs):
        sk_ref[r, :] = _rope_cols(p, rope_ref, SWA_HEAD_DIM // 2).astype(BF16)
    lane = lax.broadcasted_iota(jnp.int32, (TM, LANES), 1)
    lo = lane < SWA_HEAD_DIM
    for r, sq, rope_ref in zip(rows, [p * SWA_QSCALE for p in proj(_C_SQ)], rope_s_refs):
        for c in range(4):
            xr = _rope_cols(sq[:, c * LANES:(c + 1) * LANES], rope_ref, SWA_HEAD_DIM // 2)
            xs = pltpu.roll(xr, SWA_HEAD_DIM, axis=1)
            if c < 2:
                a, b = jnp.where(lo, xr, 0.0), jnp.where(lo, xs, 0.0)
            else:
                a, b = jnp.where(lo, 0.0, xs), jnp.where(lo, 0.0, xr)
            sq_ref[r, (2 * c) * LANES:(2 * c + 1) * LANES] = a.astype(BF16)
            sq_ref[r, (2 * c + 1) * LANES:(2 * c + 2) * LANES] = b.astype(BF16)
    half = MLA_ROPE // 2
    qns = [_rms(p, qg_ref[...]).astype(BF16) for p in proj(_C_CQ)]
    qs = [_dot(qn, wq_ref[...]) * MLA_QSCALE for qn in qns]
    kvns = [_rms(p, kvg_ref[...]).astype(BF16) for p in proj(_C_CKV)]
    kvs = [_dot(kvn, wkv_ref[...]) for kvn in kvns]
    krs = [_rope_cols(p, rope_ref, half).astype(BF16)
           for p, rope_ref in zip(proj(_C_KRT), rope_m_refs)]
    for r, q, kv, kr, rope_ref in zip(rows, qs, kvs, krs, rope_m_refs):
        q_ref[r, 0:512] = q[:, 0:512].astype(BF16)
        for c in range(2):
            lo_, hi_ = 512 + c * LANES, 512 + (c + 1) * LANES
            q_ref[r, lo_:hi_] = _rope_cols(q[:, lo_:hi_], rope_ref, half).astype(BF16)
        for c in range(4):
            k_ref[r, (2 * c) * LANES:(2 * c + 1) * LANES] = kv[:, c * LANES:(c + 1) * LANES].astype(BF16)
            k_ref[r, (2 * c + 1) * LANES:(2 * c + 2) * LANES] = kr
        v_ref[r, :] = kv[:, 512:1024].astype(BF16)


def _inproj(x, modtab, g, w, qg, kvg, wq, wkv, rope_s, rope_m, nb_per_batch):
    n = x.shape[0]
    step = TM * IN_UNITS
    assert n % step == 0
    row = lambda i: (i, 0)
    full = lambda i: (0, 0)
    unit = lambda i, u: i * IN_UNITS + u
    widths = [(768, BF16), (1024, BF16), (512, BF16), (1024, BF16), (128, BF16), (128, BF16),
              (1536, F32), (512, F32), (128, F32)]
    mod_specs = [pl.BlockSpec((1, 1, 6, D_MODEL),
                              lambda i, u=u: (unit(i, u) // nb_per_batch,
                                              jnp.minimum(unit(i, u) % nb_per_batch, 1), 0, 0))
                 for u in range(IN_UNITS)]
    rope_specs = [pl.BlockSpec((3, TM, LANES), lambda i, u=u: (0, unit(i, u) % nb_per_batch, 0))
                  for u in range(IN_UNITS)]
    return pl.pallas_call(
        _inproj_body,
        out_shape=[jax.ShapeDtypeStruct((n, wd), dt) for wd, dt in widths],
        grid=(n // step,),
        in_specs=[pl.BlockSpec((step, D_MODEL), row)] + mod_specs
        + [pl.BlockSpec((1, D_MODEL), full), pl.BlockSpec((D_MODEL, _N_PROJ), full),
           pl.BlockSpec((1, MLA_Q_RANK), full), pl.BlockSpec((1, MLA_KV_RANK), full),
           pl.BlockSpec((MLA_Q_RANK, 768), full), pl.BlockSpec((MLA_KV_RANK, 1024), full)]
        + rope_specs + rope_specs,
        out_specs=[pl.BlockSpec((step, wd), row) for wd, _ in widths],
        compiler_params=_cparams(),
        name="inproj",
    )(x, *([modtab] * IN_UNITS), g, w, qg, kvg, wq, wkv, *([rope_s] * IN_UNITS), *([rope_m] * IN_UNITS))


def _mla_attn_body(q_ref, k_ref, v_ref, o_ref, *fill_ref, t_ctx):
    for z_ref in fill_ref:
        z_ref[...] = jnp.zeros_like(z_ref)
    lane = lax.broadcasted_iota(jnp.int32, (TM, LANES), 1)
    zero = jnp.zeros((TM, LANES), BF16)

    def attend(nk):
        for c in range(4):
            kc = k_ref[0:nk, c * 2 * LANES:(c + 1) * 2 * LANES]
            vc = v_ref[0:nk, c * LANES:(c + 1) * LANES]
            qn = q_ref[:, c * LANES:(c + 1) * LANES]
            qr = q_ref[:, 512 + (c // 2) * LANES:512 + (c // 2 + 1) * LANES]
            qa = [jnp.concatenate(
                [jnp.where((lane >= 64 * s) & (lane < 64 * (s + 1)), qn, zero),
                 jnp.where((lane >= 32 * ((2 * c + s) % 4)) & (lane < 32 * ((2 * c + s) % 4 + 1)), qr, zero)],
                axis=1) for s in range(2)]
            sc = [_dot_nt(q, kc) for q in qa]
            m = [jnp.max(t, axis=-1, keepdims=True) for t in sc]
            e = [jnp.exp2(t - mm) for t, mm in zip(sc, m)]
            l = [jnp.sum(t, axis=-1, keepdims=True) for t in e]
            outs = [_dot(t.astype(BF16), vc) / ll for t, ll in zip(e, l)]
            o_ref[:, c * LANES:(c + 1) * LANES] = jnp.where(lane < 64, outs[0], outs[1]).astype(BF16)

    j = pl.program_id(1)

    @pl.when(j == 0)
    def _():
        attend(t_ctx)

    @pl.when(j > 0)
    def _():
        attend(k_ref.shape[0])


def _mla_attention(q, k, v, n_batch, rows, fill_rows=0):
    nbp = rows // TM
    blk = lambda b, j: (b * nbp + j, 0)
    out_shape = [jax.ShapeDtypeStruct((n_batch * rows, BRANCH_W), BF16)]
    out_specs = [pl.BlockSpec((TM, BRANCH_W), blk)]
    if fill_rows:
        assert fill_rows % (n_batch * nbp) == 0
        out_shape.append(jax.ShapeDtypeStruct((fill_rows, D_MODEL), F32))
        out_specs.append(pl.BlockSpec((fill_rows // (n_batch * nbp), D_MODEL), blk))
    return pl.pallas_call(
        functools.partial(_mla_attn_body, t_ctx=TM),
        out_shape=out_shape,
        grid=(n_batch, nbp),
        in_specs=[pl.BlockSpec((TM, 768), blk),
                  pl.BlockSpec((rows, 1024), lambda b, j: (b, 0)),
                  pl.BlockSpec((rows, 512), lambda b, j: (b, 0))],
        out_specs=out_specs,
        compiler_params=_cparams(2),
        name="mla_attention",
    )(q, k, v)


SWA_QB = 128
SWA_NQ = 2


def _swa_body(q_ref, k_ref, v_ref, sink_ref, o_ref, *, t_ctx, seq):
    j = pl.program_id(1)
    n_ctx_steps = t_ctx // (SWA_QB * SWA_NQ)
    lane = lax.broadcasted_iota(jnp.int32, (SWA_QB, LANES), 1)
    lo = lane < SWA_HEAD_DIM
    win = 3 * SWA_QB

    def finish(rows, parts):
        for c in range(4):
            g = c // 2
            r0 = (2 * (c % 2)) * SWA_QB
            a = parts[g][r0:r0 + SWA_QB]
            b = parts[g][r0 + SWA_QB:r0 + 2 * SWA_QB]
            if g == 0:
                col = jnp.where(lo, a, pltpu.roll(b, SWA_HEAD_DIM, axis=1))
            else:
                col = jnp.where(lo, pltpu.roll(a, SWA_HEAD_DIM, axis=1), b)
            o_ref[rows, c * LANES:(c + 1) * LANES] = col.astype(BF16)

    def blocks(blks):
        kc, vc = k_ref[0:t_ctx, :], v_ref[0:t_ctx, :]
        windowed = blks[0] is not None
        kw, vw, band = [], [], []
        for blk in blks if windowed else ():
            start = jnp.clip((blk - 1) * SWA_QB, 0, seq - win)
            rs = pl.multiple_of(t_ctx + start, SWA_QB)
            kw.append(k_ref[pl.ds(rs, win), :])
            vw.append(v_ref[pl.ds(rs, win), :])
            qpos = blk * SWA_QB + lax.broadcasted_iota(jnp.int32, (SWA_QB, win), 0)
            kpos = start + lax.broadcasted_iota(jnp.int32, (SWA_QB, win), 1)
            band1 = jnp.abs(kpos - qpos) <= SWA_WINDOW
            band.append(jnp.concatenate([band1] * 4, axis=0))
        units = [(sub, g) for sub in range(SWA_NQ) for g in range(SWA_KV_HEADS)]
        rows = [slice(sub * SWA_QB, (sub + 1) * SWA_QB) for sub in range(SWA_NQ)]
        qg = [jnp.concatenate([q_ref[rows[sub], (4 * g + i) * LANES:(4 * g + i + 1) * LANES] for i in range(4)],
                              axis=0) for sub, g in units]
        sk = [LOG2E * jnp.concatenate(
            [jnp.broadcast_to(sink_ref[4 * g + i:4 * g + i + 1, 0:1], (SWA_QB, 1)) for i in range(4)], axis=0)
            for _, g in units]
        t_c = [_dot_nt(q, kc) for q in qg]
        m = [jnp.maximum(jnp.max(t, axis=-1, keepdims=True), s) for t, s in zip(t_c, sk)]
        if windowed:
            t_w = [jnp.where(band[sub], _dot_nt(q, kw[sub]), -jnp.inf) for (sub, _), q in zip(units, qg)]
            m = [jnp.maximum(mm, jnp.max(t, axis=-1, keepdims=True)) for mm, t in zip(m, t_w)]
        e_c = [jnp.exp2(t - mm) for t, mm in zip(t_c, m)]
        l = [jnp.sum(e, axis=-1, keepdims=True) + jnp.exp2(s - mm) for e, s, mm in zip(e_c, sk, m)]
        acc = [_dot(e.astype(BF16), vc) for e in e_c]
        if windowed:
            e_w = [jnp.exp2(t - mm) for t, mm in zip(t_w, m)]
            l = [ll + jnp.sum(e, axis=-1, keepdims=True) for ll, e in zip(l, e_w)]
            acc = [a + _dot(e.astype(BF16), vw[sub]) for a, e, (sub, _) in zip(acc, e_w, units)]
        parts = [a / ll for a, ll in zip(acc, l)]
        for sub in range(SWA_NQ):
            finish(rows[sub], parts[SWA_KV_HEADS * sub:SWA_KV_HEADS * (sub + 1)])

    @pl.when(j < n_ctx_steps)
    def _():
        blocks([None] * SWA_NQ)

    @pl.when(j >= n_ctx_steps)
    def _():
        blocks([(j - n_ctx_steps) * SWA_NQ + sub for sub in range(SWA_NQ)])


def _swa_attention(q, k, v, sink, n_batch, t_ctx, seq):
    rows = t_ctx + seq
    qrows = SWA_QB * SWA_NQ
    assert t_ctx % qrows == 0 and seq % qrows == 0
    nqb = rows // qrows
    sink_tab = jnp.broadcast_to(sink.astype(F32)[:, None], (SWA_HEADS, LANES))
    return pl.pallas_call(
        functools.partial(_swa_body, t_ctx=t_ctx, seq=seq),
        out_shape=jax.ShapeDtypeStruct((n_batch * rows, BRANCH_W), BF16),
        grid=(n_batch, nqb),
        in_specs=[pl.BlockSpec((qrows, 1024), lambda b, j: (b * nqb + j, 0)),
                  pl.BlockSpec((rows, LANES), lambda b, j: (b, 0)),
                  pl.BlockSpec((rows, LANES), lambda b, j: (b, 0)),
                  pl.BlockSpec((SWA_HEADS, LANES), lambda b, j: (0, 0))],
        out_specs=pl.BlockSpec((qrows, BRANCH_W), lambda b, j: (b * nqb + j, 0)),
        compiler_params=_cparams(2),
        name="swa_attention",
    )(q, k, v, sink_tab)


MERGE_UNITS = 2


def _merge_body(*refs):
    U = MERGE_UNITS
    row_refs = [refs[6 * u:6 * (u + 1)] for u in range(U)]
    mod_refs = refs[6 * U:7 * U]
    g1_ref, g2_ref, wg_ref, dng_ref, g512_ref, wb_ref, wo_ref, rw_ref, rb_ref = refs[7 * U:7 * U + 9]
    xo_ref, h2_ref, lg_ref = refs[7 * U + 9:]
    rows = [slice(u * TM, (u + 1) * TM) for u in range(U)]
    xs = [r[0][...] for r in row_refs]
    hs = [(_rms(x, g1_ref[...]) * (1.0 + m[0, 0, 1:2, :]) + m[0, 0, 0:1, :]).astype(BF16)
          for x, m in zip(xs, mod_refs)]
    ods = [r[3][...] + r[4][...] for r in row_refs]
    sqs = [_split_bf16(od * od) for od in ods]
    mss = [(_dot(hi, g512_ref[...]) + _dot(lo, g512_ref[...])) * (1.0 / DN_HEAD_DIM) for hi, lo in sqs]
    ocs = [(od * lax.rsqrt(ms + NORM_EPS) * dng_ref[...] * (r[5][...] * jax.nn.sigmoid(r[5][...]))).astype(BF16)
           for od, ms, r in zip(ods, mss, row_refs)]
    ys = [None] * U
    for i in range(N_BRANCH):
        gates = [jax.nn.sigmoid(_dot(h, wg_ref[:, i * D_MODEL:(i + 1) * D_MODEL])) for h in hs]
        branch = [(r[1][...], r[2][...], oc)[i] for r, oc in zip(row_refs, ocs)]
        yis = [g * _dot(o, wb_ref[i]) for g, o in zip(gates, branch)]
        ys = [yi if y is None else y + yi for y, yi in zip(ys, yis)]
    xns = [x + m[0, 0, 2:3, :] * _dot(y.astype(BF16), wo_ref[...]) for x, m, y in zip(xs, mod_refs, ys)]
    h2s = [_rms(xn, g2_ref[...]) * (1.0 + m[0, 0, 4:5, :]) + m[0, 0, 3:4, :] for xn, m in zip(xns, mod_refs)]
    lgs = [_dot3_nt(rw_ref[...], h2) + rb_ref[...] for h2 in h2s]
    for r, xn, h2, lg in zip(rows, xns, h2s, lgs):
        xo_ref[r, :] = xn
        h2_ref[r, :] = h2.astype(BF16)
        lg_ref[:, r] = lg


def _merge(x, modtab, g1, g2, wg, oa, ob, dn_f, dn_b, dz, dn_g, g512, wb, wo, rwt, rb, nb_per_batch, latent_only):
    U = MERGE_UNITS
    if latent_only:
        nlat = nb_per_batch - 1
        n = x.shape[0] // nb_per_batch * nlat
        src = lambda u: (u // nlat) * nb_per_batch + u % nlat + 1
        mod_of = lambda u: (u // nlat, 1, 0, 0)
    else:
        n = x.shape[0]
        src = lambda u: u
        mod_of = lambda u: (u // nb_per_batch, jnp.minimum(u % nb_per_batch, 1), 0, 0)
    step = TM * U
    assert n % step == 0
    orow = lambda i: (i, 0)
    full2 = lambda i: (0, 0)
    row_specs, row_args = [], []
    for u in range(U):
        rowu = lambda i, u=u: (src(i * U + u), 0)
        row_specs += [pl.BlockSpec((TM, D_MODEL), rowu)] + [pl.BlockSpec((TM, BRANCH_W), rowu)] * 5
        row_args += [x, oa, ob, dn_f, dn_b, dz]
    mod_specs = [pl.BlockSpec((1, 1, 6, D_MODEL), lambda i, u=u: mod_of(i * U + u)) for u in range(U)]
    return pl.pallas_call(
        _merge_body,
        out_shape=[jax.ShapeDtypeStruct((n, D_MODEL), F32), jax.ShapeDtypeStruct((n, D_MODEL), BF16),
                   jax.ShapeDtypeStruct((N_EXPERTS, n), F32)],
        grid=(n // step,),
        in_specs=row_specs + mod_specs
        + [pl.BlockSpec((1, D_MODEL), full2), pl.BlockSpec((1, D_MODEL), full2),
           pl.BlockSpec((D_MODEL, N_BRANCH * D_MODEL), full2), pl.BlockSpec((1, BRANCH_W), full2),
           pl.BlockSpec((BRANCH_W, BRANCH_W), full2),
           pl.BlockSpec((N_BRANCH, BRANCH_W, D_MODEL), lambda i: (0, 0, 0)),
           pl.BlockSpec((D_MODEL, D_MODEL), full2),
           pl.BlockSpec((N_EXPERTS, D_MODEL), full2), pl.BlockSpec((N_EXPERTS, 1), full2)],
        out_specs=[pl.BlockSpec((step, D_MODEL), orow), pl.BlockSpec((step, D_MODEL), orow),
                   pl.BlockSpec((N_EXPERTS, step), lambda i: (0, i))],
        compiler_params=_cparams(),
        name="merge",
    )(*row_args, *([modtab] * U), g1, g2, wg, dn_g, g512, wb, wo, rwt, rb)


ROUTE_TB = 1024


def _router_body(lg_ref, tri_ref, e_ref, gate_ref, pos_ref, cnt_ref, run_ref):
    @pl.when(pl.program_id(0) == 0)
    def _():
        run_ref[...] = jnp.zeros_like(run_ref)

    lg = lg_ref[...]
    eid = lax.broadcasted_iota(jnp.int32, lg.shape, 0)
    work = lg
    vals, idxs = [], []
    sel = jnp.zeros(lg.shape, F32)
    for _ in range(TOP_K):
        m = jnp.max(work, axis=0, keepdims=True)
        idx = jnp.min(jnp.where(work == m, eid, N_EXPERTS), axis=0, keepdims=True)
        hit = eid == idx
        sel = jnp.where(hit, 1.0, sel)
        work = jnp.where(hit, -jnp.inf, work)
        vals.append(m)
        idxs.append(idx)
    ex = [jnp.exp(v - vals[0]) for v in vals]
    den = ex[0] + ex[1] + ex[2] + ex[3]
    before = _dot(sel.astype(BF16), tri_ref[...]) + run_ref[:, 0:1]
    for k in range(TOP_K):
        e_ref[k:k + 1, :] = idxs[k]
        gate_ref[k:k + 1, :] = ex[k] / den
        pos_ref[k:k + 1, :] = jnp.sum(jnp.where(eid == idxs[k], before, 0.0), axis=0,
                                      keepdims=True).astype(jnp.int32)
    run_ref[...] = run_ref[...] + jnp.sum(sel, axis=1, keepdims=True)
    cnt_ref[...] = run_ref[...]


def _router(logits_t):
    n = logits_t.shape[1]
    tri = (jnp.arange(ROUTE_TB)[:, None] < jnp.arange(ROUTE_TB)[None, :]).astype(BF16)
    blk = lambda i: (0, i)
    return pl.pallas_call(
        _router_body,
        out_shape=[jax.ShapeDtypeStruct((TOP_K, n), jnp.int32), jax.ShapeDtypeStruct((TOP_K, n), F32),
                   jax.ShapeDtypeStruct((TOP_K, n), jnp.int32), jax.ShapeDtypeStruct((N_EXPERTS, LANES), F32)],
        grid=(n // ROUTE_TB,),
        in_specs=[pl.BlockSpec((N_EXPERTS, ROUTE_TB), blk), pl.BlockSpec((ROUTE_TB, ROUTE_TB), lambda i: (0, 0))],
        out_specs=[pl.BlockSpec((TOP_K, ROUTE_TB), blk), pl.BlockSpec((TOP_K, ROUTE_TB), blk),
                   pl.BlockSpec((TOP_K, ROUTE_TB), blk), pl.BlockSpec((N_EXPERTS, LANES), lambda i: (0, 0))],
        scratch_shapes=[pltpu.VMEM((N_EXPERTS, LANES), F32)],
        compiler_params=_cparams(),
        name="router",
    )(logits_t, tri)


def _expert_body(ib_ref, ie_ref, lo_ref, hi_ref, x_ref, wgu_ref, bgu_ref, wdn_ref, bdn_ref, perm_ref, *rest):
    o_ref, wgu_s, wdn_s = rest[-3:]
    i = pl.program_id(0)
    prev = jnp.maximum(i - 1, 0)

    @pl.when((i == 0) | (ie_ref[i] != ie_ref[prev]))
    def _():
        for m in range(D_EXPERT // LANES):
            t = _dot(wgu_ref[0, 0, :, m * 2 * LANES:(m + 1) * 2 * LANES].astype(BF16), perm_ref[...])
            wgu_s[:, m * LANES:(m + 1) * LANES] = t[:, 0:LANES].astype(BF16)
            wgu_s[:, D_EXPERT + m * LANES:D_EXPERT + (m + 1) * LANES] = t[:, LANES:2 * LANES].astype(BF16)
        wdn_s[...] = wdn_ref[0, 0].astype(BF16)

    lo, hi = lo_ref[i], hi_ref[i]

    @pl.when(hi > lo)
    def _():
        gu = _dot(x_ref[...], wgu_s[...]) + bgu_ref[0]
        g_ = jnp.minimum(gu[:, :D_EXPERT], SWIGLU_LIMIT)
        u_ = jnp.clip(gu[:, D_EXPERT:], -SWIGLU_LIMIT, SWIGLU_LIMIT)
        act = (u_ + 1.0) * (g_ * jax.nn.sigmoid(SWIGLU_ALPHA * g_))
        y = _dot(act.astype(BF16), wdn_s[...]) + bdn_ref[0]
        first = (i == 0) | (ib_ref[i] != ib_ref[prev])
        row = lax.broadcasted_iota(jnp.int32, (MOE_BLK, 1), 0)
        o_ref[...] = jnp.where(first | ((row >= lo) & (row < hi)), y, o_ref[...])


def _experts(items, xs, w_gu, bgu, w_dn, bdn, layer, block0, yb_prev):
    n_items = items[0].shape[0]
    perm = np.zeros((2 * LANES, 2 * LANES), np.float32)
    j = np.arange(LANES)
    perm[2 * j, j] = 1.0
    perm[2 * j + 1, LANES + j] = 1.0
    in_specs = [pl.BlockSpec((MOE_BLK, D_MODEL), lambda i, ib, ie, lo, hi: (ib[i] - block0, 0)),
                pl.BlockSpec((1, 1, D_MODEL, 2 * D_EXPERT), lambda i, ib, ie, lo, hi: (layer, ie[i], 0, 0)),
                pl.BlockSpec((1, 1, 2 * D_EXPERT), lambda i, ib, ie, lo, hi: (ie[i], 0, 0)),
                pl.BlockSpec((1, 1, D_EXPERT, D_MODEL), lambda i, ib, ie, lo, hi: (layer, ie[i], 0, 0)),
                pl.BlockSpec((1, 1, D_MODEL), lambda i, ib, ie, lo, hi: (ie[i], 0, 0)),
                pl.BlockSpec((2 * LANES, 2 * LANES), lambda i, ib, ie, lo, hi: (0, 0)),
                pl.BlockSpec(memory_space=pl.ANY)]
    args = [*items, xs, w_gu, bgu, w_dn, bdn, jnp.asarray(perm, BF16), yb_prev]
    aliases = {len(args) - 1: 0}
    return pl.pallas_call(
        _expert_body,
        out_shape=jax.ShapeDtypeStruct(yb_prev.shape, F32),
        grid_spec=pltpu.PrefetchScalarGridSpec(
            num_scalar_prefetch=4,
            grid=(n_items,),
            in_specs=in_specs,
            out_specs=pl.BlockSpec((MOE_BLK, D_MODEL), lambda i, ib, ie, lo, hi: (ib[i], 0)),
            scratch_shapes=[pltpu.VMEM((D_MODEL, 2 * D_EXPERT), BF16), pltpu.VMEM((D_EXPERT, D_MODEL), BF16)]),
        input_output_aliases=aliases,
        compiler_params=_cparams(),
        name="experts",
    )(*args)


def _lookup(table, idx):
    ids = jnp.arange(table.shape[0], dtype=jnp.int32)
    return jnp.sum(jnp.where(idx[..., None] == ids, table, 0), axis=-1)


def _work_items(cnt_start, cnt_end, block0, n_blk):
    n_items = n_blk + N_EXPERTS - 1
    b1 = block0 + n_blk
    first = jnp.maximum(cnt_start // MOE_BLK, block0)
    last = jnp.minimum((cnt_end - 1) // MOE_BLK, b1 - 1)
    n_e = jnp.where(cnt_end > cnt_start, jnp.maximum(last - first + 1, 0), 0)
    item_end = jnp.cumsum(n_e)
    item_start = item_end - n_e
    total = item_end[-1]
    ii = jnp.arange(n_items, dtype=jnp.int32)
    valid = ii < total
    e_i = jnp.sum((item_end[None, :] <= jnp.minimum(ii, total - 1)[:, None]).astype(jnp.int32), axis=1)
    blk = jnp.where(valid, _lookup(first, e_i) + ii - _lookup(item_start, e_i), b1 - 1)
    lo = jnp.clip(_lookup(cnt_start, e_i) - blk * MOE_BLK, 0, MOE_BLK)
    hi = jnp.clip(_lookup(cnt_end, e_i) - blk * MOE_BLK, 0, MOE_BLK)
    lo = jnp.where(valid, lo, 0)
    hi = jnp.where(valid, hi, 0)
    return blk.astype(jnp.int32), e_i.astype(jnp.int32), lo.astype(jnp.int32), hi.astype(jnp.int32)


def _moe(h2, logits_t, w_gu, bgu, w_dn, bdn, layer, yb):
    n = h2.shape[0]
    a = n * TOP_K
    assert a % (MOE_BLK * MOE_CHUNKS) == 0 and yb.shape[0] >= a
    top_e, gate, pos, cnt = _router(logits_t)
    counts = cnt[:, 0].astype(jnp.int32)
    cnt_end = jnp.cumsum(counts)
    cnt_start = cnt_end - counts
    slot = pos + _lookup(cnt_start, top_e)
    tok = jnp.broadcast_to(jnp.arange(n, dtype=jnp.int32)[None, :], (TOP_K, n))
    _, slot_tok = lax.sort_key_val(slot.reshape(-1), tok.reshape(-1))
    n_blk = a // MOE_BLK // MOE_CHUNKS
    rows_c = n_blk * MOE_BLK
    for c in range(MOE_CHUNKS):
        xs = h2.at[slot_tok[c * rows_c:(c + 1) * rows_c]].get(mode="promise_in_bounds")
        items = _work_items(cnt_start, cnt_end, c * n_blk, n_blk)
        yb = _experts(items, xs, w_gu, bgu, w_dn, bdn, layer, c * n_blk, yb)
    ys = [yb.at[slot[k]].get(mode="promise_in_bounds") for k in range(TOP_K)]
    return ys, gate.T, yb


DN_GW = 256
HALO = 8
DN_PAIR = 4
_M_EYE, _M_BLK16, _M_OFF32, _M_OFF64, _M_INCL, _M_STRICT = 0, 1, 2, 3, 4, 6


def _dn_masks():
    row = np.arange(DN_CHUNK)[:, None]
    col = np.arange(DN_GW)[None, :] % DN_CHUNK
    b16 = (row // 16) == (col // 16)
    b32 = (row // 32) == (col // 32)
    m = np.stack([row == col, b16, b32 & ~b16, ~b32, col <= row, col >= row, col < row, col > row])
    hid = np.arange(DN_GW) // DN_HEAD_DIM
    return jnp.asarray(m, F32), jnp.asarray(hid[:, None] == hid[None, :], BF16)


def _split3(a):
    hi = a.astype(BF16)
    r = a - hi.astype(F32)
    lo = r.astype(BF16)
    lo2 = (r - lo.astype(F32)).astype(BF16)
    return hi, lo, lo2


def _bd(x, bdm_ref):
    xb = x.astype(BF16)
    return jnp.concatenate([xb, xb, xb, xb], axis=0) * bdm_ref[...]


def _mm(a, wbd):
    return _dot(a.astype(BF16), wbd)


def _tri_inverse(l_mats, m_ref, bdm_ref):
    half = DN_CHUNK
    bd = lambda v: _bd(v, bdm_ref)
    lds = [l * m_ref[_M_BLK16] for l in l_mats]
    ps = [_mm(ld, bd(ld)) for ld in lds]
    xs = [m_ref[_M_EYE] - ld for ld in lds]
    for _ in range(2):
        rs = [_mm(jnp.concatenate([p, x], axis=0), bd(p)) for p, x in zip(ps, xs)]
        ps = [r[0:half] for r in rs]
        xs = [x + r[half:2 * half] for x, r in zip(xs, rs)]
    xs = [x + _mm(x, bd(p)) for x, p in zip(xs, ps)]
    ts = [_mm(x, bd(l * m_ref[_M_OFF32])) for x, l in zip(xs, l_mats)]
    xs = [x - _mm(t, bd(x)) for x, t in zip(xs, ts)]
    ts = [_mm(x, bd(l * m_ref[_M_OFF64])) for x, l in zip(xs, l_mats)]
    return [x - _mm(t, bd(x)) for x, t in zip(xs, ts)]


def _dn_prep_body(cur_ref, prev_ref, next_ref, cw_ref, dg_ref, gc_ref, g512_ref, e_ref, tri_ref, m_ref, bdm_ref,
                  u_ref, w_ref, kd_ref, qd_ref, aq_ref, gl_ref,
                  ext_ref, q_s, k_s, v_s, gx_s, bx_s):
    j = pl.program_id(1)
    nbp = pl.num_programs(1)
    C = DN_CHUNK
    ext_ref[HALO:HALO + TM, :] = cur_ref[...]
    ext_ref[0:HALO, :] = jnp.where(j >= 2, prev_ref[...], 0.0)
    ext_ref[HALO + TM:2 * HALO + TM, :] = jnp.where((j >= 1) & (j < nbp - 1), next_ref[...], 0.0)
    y = None
    for t in range(DN_CONV):
        o = HALO - DN_CONV // 2 + t
        term = cw_ref[t:t + 1, :] * ext_ref[o:o + TM, :]
        y = term if y is None else y + term
    y = y * jax.nn.sigmoid(y)

    g512 = g512_ref[...]

    def head_sumsq(x):
        hi, lo = _split_bf16(x * x)
        return _dot(hi, g512) + _dot(lo, g512)

    q = y[:, 0:512]
    k = y[:, 512:1024]
    q_s[...] = q * lax.rsqrt(head_sumsq(q) + NORM_EPS) * (DN_HEAD_DIM ** -0.5)
    k_s[...] = k * lax.rsqrt(head_sumsq(k) + NORM_EPS)
    v_s[...] = y[:, 1024:1536]

    dg = dg_ref[...]
    beta_all = jax.nn.sigmoid(dg)
    z = dg + gc_ref[1:2, :]
    g_all = gc_ref[0:1, :] * (jnp.maximum(z, 0.0) + jnp.log(1.0 + jnp.exp(-jnp.abs(z))))
    bh, bl = _split_bf16(beta_all)
    bcat = jnp.concatenate([bh, bl], axis=0)
    gparts = jnp.concatenate(_split3(g_all), axis=1)
    for d in range(2):
        cs = _dot(tri_ref[d], gparts)
        gcum = cs[:, 0:128] + cs[:, 128:256] + cs[:, 256:384]
        gcat = jnp.concatenate(_split3(gcum), axis=0)
        for grp in range(2):
            eg = _dot(gcat, e_ref[4 + 2 * d + grp])
            gx_s[d, grp] = eg[0:TM] + eg[TM:2 * TM] + eg[2 * TM:3 * TM]
            eb = _dot(bcat, e_ref[2 * d + grp])
            bx_s[d, grp] = eb[0:TM] + eb[TM:2 * TM]

    def chunk_pair(it, carry):
        units = []
        l_mats = []
        for ci in range(DN_PAIR):
            cc = it * DN_PAIR + ci
            rows = pl.ds(pl.multiple_of(cc * C, C), C)
            for grp in range(2):
                cols = slice(grp * DN_GW, (grp + 1) * DN_GW)
                kg, qg = k_s[rows, cols], q_s[rows, cols]
                kb = [kg * bx_s[d, grp, rows, :] for d in range(2)]
                raw = _dot_nt(jnp.concatenate([kb[0], kb[1], qg], axis=0).astype(BF16), _bd(kg, bdm_ref))
                for d in range(2):
                    gx = gx_s[d, grp, rows, :]
                    rvec = jnp.sum(gx * m_ref[_M_EYE], axis=0, keepdims=True)
                    dec = jnp.exp(jnp.minimum(gx - rvec, 0.0)) * m_ref[_M_INCL + d]
                    l_mats.append(raw[C * d:C * (d + 1)] * dec * m_ref[_M_STRICT + d])
                    glast = gx[C - 1:C, :] if d == 0 else gx[0:1, :]
                    kd_ref[d, rows, cols] = (kg * jnp.exp(glast - gx)).astype(BF16)
                    qd_ref[d, rows, cols] = (qg * jnp.exp(gx)).astype(BF16)
                    aq_ref[d, rows, cols] = (raw[2 * C:3 * C] * dec).astype(BF16)
                    gl_ref[d, pl.ds(cc, 1), :, cols] = jnp.exp(glast).reshape(1, 1, DN_GW)
                    units.append((rows, cols, d, grp))
        tinvs = _tri_inverse(l_mats, m_ref, bdm_ref)
        for (rows, cols, d, grp), tinv in zip(units, tinvs):
            bx = bx_s[d, grp, rows, :]
            u_ref[d, rows, cols] = _mm(tinv, _bd(v_s[rows, cols] * bx, bdm_ref))
            w_ref[d, rows, cols] = _mm(tinv, _bd(k_s[rows, cols] * bx * jnp.exp(gx_s[d, grp, rows, :]), bdm_ref)
                                       ).astype(BF16)
        return carry

    lax.fori_loop(0, TM // C // DN_PAIR, chunk_pair, 0)


def _dn_prep(dqkv, dg, conv_w, a_log, dt_bias, n_batch, rows):
    n = dqkv.shape[0]
    nbp = rows // TM
    hb = TM // HALO
    H = DN_HEADS
    cw = jnp.zeros((8, 1536), F32).at[0:DN_CONV].set(conv_w)
    gc = jnp.zeros((8, LANES), F32)
    gc = gc.at[0, 2 * H:4 * H].set(-jnp.exp(a_log.reshape(-1))).at[1, 2 * H:4 * H].set(dt_bias.reshape(-1))
    hid = np.arange(512) // DN_HEAD_DIM
    g512 = jnp.asarray(hid[:, None] == hid[None, :], BF16)
    e = np.zeros((8, LANES, DN_GW), np.float32)
    for kind in range(2):
        for d in range(2):
            for grp in range(2):
                for h in range(4):
                    e[4 * kind + 2 * d + grp, 16 * kind + 8 * d + 4 * grp + h, 64 * h:64 * (h + 1)] = 1.0
    e = jnp.asarray(e, BF16)
    t = np.arange(TM)
    same = (t[:, None] // DN_CHUNK) == (t[None, :] // DN_CHUNK)
    tri = jnp.asarray(np.stack([same & (t[None, :] <= t[:, None]), same & (t[None, :] >= t[:, None])]), BF16)
    masks, bdm = _dn_masks()
    blk = lambda b, j: (b * nbp + j, 0)
    full2 = lambda b, j: (0, 0)
    full3 = lambda b, j: (0, 0, 0)
    dblk = lambda b, j: (0, b * nbp + j, 0)
    nlast = n // HALO - 1
    outs = pl.pallas_call(
        _dn_prep_body,
        out_shape=[jax.ShapeDtypeStruct((2, n, 512), F32)] + [jax.ShapeDtypeStruct((2, n, 512), BF16)] * 4
        + [jax.ShapeDtypeStruct((2, n // DN_CHUNK, 1, 512), F32)],
        grid=(n_batch, nbp),
        in_specs=[pl.BlockSpec((TM, 1536), blk),
                  pl.BlockSpec((HALO, 1536), lambda b, j: (jnp.maximum((b * nbp + j) * hb - 1, 0), 0)),
                  pl.BlockSpec((HALO, 1536), lambda b, j: (jnp.minimum((b * nbp + j + 1) * hb, nlast), 0)),
                  pl.BlockSpec((8, 1536), full2), pl.BlockSpec((TM, LANES), blk), pl.BlockSpec((8, LANES), full2),
                  pl.BlockSpec((512, 512), full2), pl.BlockSpec((8, LANES, DN_GW), full3),
                  pl.BlockSpec((2, TM, TM), full3), pl.BlockSpec((8, DN_CHUNK, DN_GW), full3),
                  pl.BlockSpec((DN_GW, DN_GW), full2)],
        out_specs=[pl.BlockSpec((2, TM, 512), dblk)] * 5
        + [pl.BlockSpec((2, TM // DN_CHUNK, 1, 512), lambda b, j: (0, b * nbp + j, 0, 0))],
        scratch_shapes=[pltpu.VMEM((TM + 2 * HALO, 1536), F32), pltpu.VMEM((TM, 512), F32),
                        pltpu.VMEM((TM, 512), F32), pltpu.VMEM((TM, 512), F32),
                        pltpu.VMEM((2, 2, TM, DN_GW), F32), pltpu.VMEM((2, 2, TM, DN_GW), F32)],
        compiler_params=_cparams(2),
        name="dn_prep",
    )(dqkv, dqkv, dqkv, cw, dg, gc, g512, e, tri, masks, bdm)
    return outs, g512


DN_SCAN_NB = 8


def _dn_scan_body(uf, wf, kf, qf, af, gf, ub, wb, kb, qb, ab, gb, bdm_ref, of_ref, ob_ref, s_ref):
    @pl.when(pl.program_id(1) == 0)
    def _():
        s_ref[...] = jnp.zeros_like(s_ref)

    same = bdm_ref[...] != 0
    dirs = ((uf, wf, kf, qf, af, gf, of_ref), (ub, wb, kb, qb, ab, gb, ob_ref))
    units = [(bb, d, grp) for bb in range(DN_SCAN_NB) for d in range(2) for grp in range(2)]
    cols = [slice(grp * DN_GW, (grp + 1) * DN_GW) for grp in range(2)]
    s = [s_ref[bb, d, grp] for bb, d, grp in units]
    sq = [_dot(jnp.concatenate([dirs[d][1][0, bb, :, cols[grp]], dirs[d][3][0, bb, :, cols[grp]]], axis=0),
               st.astype(BF16)) for (bb, d, grp), st in zip(units, s)]
    vnew = [dirs[d][0][0, bb, :, cols[grp]] - r[0:DN_CHUNK] for (bb, d, grp), r in zip(units, sq)]
    intra = [_dot(dirs[d][4][0, bb, :, cols[grp]], _bd(v, bdm_ref)) for (bb, d, grp), v in zip(units, vnew)]
    upd = [lax.dot_general(dirs[d][2][0, bb, :, cols[grp]], v.astype(BF16), (((0,), (0,)), ((), ())),
                           preferred_element_type=F32) for (bb, d, grp), v in zip(units, vnew)]
    for (bb, d, grp), st, r, a, up in zip(units, s, sq, intra, upd):
        dirs[d][6][bb, :, cols[grp]] = r[DN_CHUNK:2 * DN_CHUNK] + a
        s_ref[bb, d, grp] = st * dirs[d][5][0, bb, 0, :, cols[grp]] + jnp.where(same, up, 0.0)


def _dn_scan(u, w, kd, qd, aq, gl, n_batch, rows, t_ctx):
    nch = rows // DN_CHUNK
    nctx = t_ctx // DN_CHUNK
    assert n_batch % DN_SCAN_NB == 0

    def cb(c):
        return jnp.where(c < nctx, nctx - 1 - c, nch - 1 - (c - nctx))

    per_batch = lambda a: a.reshape(2, n_batch, rows, 512)
    u, w, kd, qd, aq = map(per_batch, (u, w, kd, qd, aq))
    gl = gl.reshape(2, n_batch, nch, 1, 512)
    blk = (1, DN_SCAN_NB, DN_CHUNK, 512)
    gblk = (1, DN_SCAN_NB, 1, 1, 512)
    specs = []
    for d, ch in ((0, lambda c: c), (1, cb)):
        specs += [pl.BlockSpec(blk, lambda b, c, d=d, ch=ch: (d, b, ch(c), 0))] * 5
        specs += [pl.BlockSpec(gblk, lambda b, c, d=d, ch=ch: (d, b, ch(c), 0, 0))]
    specs.append(pl.BlockSpec((DN_GW, DN_GW), lambda b, c: (0, 0)))
    o_f, o_b = pl.pallas_call(
        _dn_scan_body,
        out_shape=[jax.ShapeDtypeStruct((n_batch, rows, 512), F32)] * 2,
        grid=(n_batch // DN_SCAN_NB, nch),
        in_specs=specs,
        out_specs=[pl.BlockSpec((DN_SCAN_NB, DN_CHUNK, 512), lambda b, c: (b, c, 0)),
                   pl.BlockSpec((DN_SCAN_NB, DN_CHUNK, 512), lambda b, c: (b, cb(c), 0))],
        scratch_shapes=[pltpu.VMEM((DN_SCAN_NB, 2, 2, DN_GW, DN_GW), F32)],
        compiler_params=_cparams(2),
        name="dn_scan",
    )(u, w, kd, qd, aq, gl, u, w, kd, qd, aq, gl, _dn_masks()[1])
    return o_f.reshape(n_batch * rows, 512), o_b.reshape(n_batch * rows, 512)


def _deltanet(dqkv, dg, conv_w, a_log, dt_bias, n_batch, rows, t_ctx):
    (u, w, kd, qd, aq, gl), g512 = _dn_prep(dqkv, dg, conv_w, a_log, dt_bias, n_batch, rows)
    o_f, o_b = _dn_scan(u, w, kd, qd, aq, gl, n_batch, rows, t_ctx)
    return o_f, o_b, g512


def _combine_body(x_ref, y0_ref, y1_ref, y2_ref, y3_ref, gate_ref, mod_ref, fg_ref, o_ref, *, final):
    gate = gate_ref[...]
    y = None
    for k, y_ref in enumerate((y0_ref, y1_ref, y2_ref, y3_ref)):
        yk = y_ref[...] * gate[:, k:k + 1]
        y = yk if y is None else y + yk
    xo = x_ref[...] + mod_ref[0, 0, 5:6, :] * y
    o_ref[...] = _rms(xo, fg_ref[...]) if final else xo


def _combine(x, ys, gate_rows, modtab, final_g, nb_per_batch, latent_only):
    n = x.shape[0]
    if latent_only:
        nlat = nb_per_batch - 1
        mod_idx = lambda i: (i // nlat, 1, 0, 0)
    else:
        mod_idx = lambda i: (i // nb_per_batch, jnp.minimum(i % nb_per_batch, 1), 0, 0)
    row = lambda i: (i, 0)
    return pl.pallas_call(
        functools.partial(_combine_body, final=latent_only),
        out_shape=jax.ShapeDtypeStruct((n, D_MODEL), F32),
        grid=(n // TM,),
        in_specs=[pl.BlockSpec((TM, D_MODEL), row)] * 5
        + [pl.BlockSpec((TM, TOP_K), row), pl.BlockSpec((1, 1, 6, D_MODEL), mod_idx),
           pl.BlockSpec((1, D_MODEL), lambda i: (0, 0))],
        out_specs=pl.BlockSpec((TM, D_MODEL), row),
        compiler_params=_cparams(),
        name="combine",
    )(x, *ys, gate_rows, modtab, final_g)


def _rope_tables(seq, t_ctx, rot_dim):
    rows = seq // GRID_W
    row = jnp.broadcast_to(jnp.arange(rows)[:, None], (rows, GRID_W)).reshape(-1).astype(F32)
    col = jnp.broadcast_to(jnp.arange(GRID_W)[None, :], (rows, GRID_W)).reshape(-1).astype(F32)
    n_freq = rot_dim // 4
    inv_freq = ROPE_THETA ** (-jnp.arange(n_freq, dtype=F32) / n_freq)
    ang = jnp.concatenate([row[:, None] * inv_freq, col[:, None] * inv_freq], axis=-1)
    cos, sin = jnp.cos(ang), jnp.sin(ang)
    zero = jnp.zeros_like(sin)
    c = jnp.concatenate([cos, cos], axis=-1)
    sa = jnp.concatenate([-sin, zero], axis=-1)
    sb = jnp.concatenate([zero, sin], axis=-1)
    ident = jnp.stack([jnp.ones((t_ctx, rot_dim), F32), jnp.zeros((t_ctx, rot_dim), F32),
                       jnp.zeros((t_ctx, rot_dim), F32)])
    tab = jnp.concatenate([ident, jnp.stack([c, sa, sb])], axis=1)
    return jnp.tile(tab, (1, 1, LANES // rot_dim))


def _layer_weights(w_in, w_q_up, w_kv_up, w_branch, w_out, router_w, router_b, b_gu, b_dn):
    sizes = (MLA_Q_RANK, MLA_KV_RANK, MLA_ROPE, 512, 128, 128, 1536, 512, 16, 16, N_BRANCH * D_MODEL)
    offs = np.cumsum((0,) + sizes)
    part = lambda i: w_in[:, offs[i]:offs[i + 1]]
    pad = jnp.zeros((D_MODEL, LANES - 32), F32)
    w_proj = jnp.concatenate(
        [part(0), part(1), part(3), part(4), part(5), part(6), part(7),
         jnp.tile(part(2), (1, LANES // MLA_ROPE)), part(8), part(9), pad], axis=1).astype(BF16)
    w_gate = part(10).astype(BF16)
    hq = MLA_NOPE + MLA_ROPE
    wq = w_q_up.reshape(MLA_Q_RANK, MLA_HEADS, hq)
    wq = jnp.concatenate([wq[:, :, :MLA_NOPE].reshape(MLA_Q_RANK, -1),
                          wq[:, :, MLA_NOPE:].reshape(MLA_Q_RANK, -1)], axis=1).astype(BF16)
    wkv = w_kv_up.reshape(MLA_KV_RANK, MLA_HEADS, MLA_NOPE + MLA_V)
    wkv = jnp.concatenate([wkv[:, :, :MLA_NOPE].reshape(MLA_KV_RANK, -1),
                           wkv[:, :, MLA_NOPE:].reshape(MLA_KV_RANK, -1)], axis=1).astype(BF16)
    bgu = jnp.concatenate([b_gu[:, 0::2], b_gu[:, 1::2]], axis=-1).reshape(N_EXPERTS, 1, 2 * D_EXPERT)
    return dict(w_proj=w_proj, w_gate=w_gate, wq=wq, wkv=wkv, wb=w_branch.astype(BF16),
                wo=w_out.astype(BF16), rwt=router_w.T, rb=router_b.reshape(N_EXPERTS, 1),
                bgu=bgu, bdn=b_dn.reshape(N_EXPERTS, 1, D_MODEL))


def kernel(x, c, ctx, c_ctx, w_mod, b_mod, norm1_g, norm2_g, w_in, mla_q_norm_g, mla_w_q_up,
           mla_kv_norm_g, mla_w_kv_up, swa_sink, dn_conv_w, dn_a_log, dn_dt_bias, dn_norm_g,
           w_branch, w_out, router_w, router_b, exp_w_gu, exp_b_gu, exp_w_dn, exp_b_dn, final_norm_g):
    B, S, D = x.shape
    T = ctx.shape[1]
    R = T + S
    depth = w_mod.shape[0]
    assert D == D_MODEL and T == TM and S % TM == 0 and (B * R) % ROUTE_TB == 0
    nbp = R // TM
    xa = jnp.concatenate([ctx, x], axis=1).reshape(B * R, D)
    rope_s = _rope_tables(S, T, SWA_HEAD_DIM)
    rope_m = _rope_tables(S, T, MLA_ROPE)
    cvec = jnp.concatenate([c, c_ctx[None, :]], axis=0)
    yb = None
    for l in range(depth):
        w = _layer_weights(w_in[l], mla_w_q_up[l], mla_w_kv_up[l], w_branch[l], w_out[l], router_w[l],
                           router_b[l], exp_b_gu[l], exp_b_dn[l])
        mod = _modulation(cvec, w_mod[l], b_mod[l]).reshape(B + 1, 6, D)
        modtab = jnp.stack([jnp.broadcast_to(mod[B][None], (B, 6, D)), mod[:B]], axis=1)
        g1 = norm1_g[l].reshape(1, D)
        g2 = norm2_g[l].reshape(1, D)
        q_m, k_m, v_m, sq, sk, sv, dqkv, dz, dg = _inproj(
            xa, modtab, g1, w["w_proj"], mla_q_norm_g[l].reshape(1, -1), mla_kv_norm_g[l].reshape(1, -1),
            w["wq"], w["wkv"], rope_s, rope_m, nbp)
        if yb is None:
            o_a, yb = _mla_attention(q_m, k_m, v_m, B, R, fill_rows=B * R * TOP_K)
        else:
            o_a, = _mla_attention(q_m, k_m, v_m, B, R)
        o_b = _swa_attention(sq, sk, sv, swa_sink[l], B, T, S)
        dn_f, dn_b, g512 = _deltanet(dqkv, dg, dn_conv_w[l], dn_a_log[l], dn_dt_bias[l], B, R, T)
        last = l == depth - 1
        xn, h2, logits_t = _merge(xa, modtab, g1, g2, w["w_gate"], o_a, o_b, dn_f, dn_b, dz,
                                  jnp.tile(dn_norm_g[l], DN_HEADS).reshape(1, BRANCH_W), g512, w["wb"], w["wo"],
                                  w["rwt"], w["rb"], nbp, last)
        ys, gate_rows, yb = _moe(h2, logits_t, exp_w_gu, w["bgu"], exp_w_dn, w["bdn"], l, yb)
        xa = _combine(xn, ys, gate_rows, modtab, final_norm_g.reshape(1, D), nbp, last)
    return xa.reshape(B, S, D)
```
